```python
import math
import jax
import jax.numpy as jnp
from jax import lax
import numpy as np

D_MODEL = 1024
BATCH = 32
SEQ = 2048
DEPTH = 2

GRID_W = 64
CTX_LEN = 256
EPS = 1e-6
N_MOD = 6
S5_WIDTH = D_MODEL // 4
S5_GROUP = 16
S5_GROUPS = S5_WIDTH // S5_GROUP
S5_STATE = 64
S5_DT_MIN = 1e-3
S5_DT_MAX = 1e-1
RET_HEADS = 4
RET_WIDTH = 3 * D_MODEL // 8
RET_DV = RET_WIDTH // RET_HEADS
RET_DK = RET_DV // 2
RET_CHUNK = 128
ROPE_BASE = 10000.0
GLA_HEADS = 4
GLA_WIDTH = 3 * D_MODEL // 8
GLA_DV = GLA_WIDTH // GLA_HEADS
GLA_DK = GLA_DV // 2
GLA_RANK = 16
GLA_TAU = 16.0
GLA_CHUNK = 64
FFN_HIDDEN = -(-8 * D_MODEL // (3 * 256)) * 256
IN_SIZES = (S5_WIDTH,
            RET_HEADS * RET_DK, RET_HEADS * RET_DK, RET_WIDTH, RET_WIDTH,
            GLA_HEADS * GLA_DK, GLA_HEADS * GLA_DK, GLA_WIDTH, GLA_WIDTH, GLA_RANK, GLA_RANK,
            D_MODEL, D_MODEL, D_MODEL)
IN_DIM = sum(IN_SIZES)

kernel_name = 'hybrid_s5_retnet_gla_prefix_dit'


def _rms_norm(x, gain):
    xf = x.astype(jnp.float32)
    y = xf * lax.rsqrt(jnp.mean(xf * xf, axis=-1, keepdims=True) + EPS)
    return (y * gain.astype(jnp.float32)).astype(x.dtype)


def _modulate(h, shift, scale):
    return h * (1.0 + scale) + shift


def _split_in(p):
    return jnp.split(p, np.cumsum(IN_SIZES)[:-1].tolist(), axis=-1)


def _heads(t, n_heads):
    b, l, _ = t.shape
    return t.reshape(b, l, n_heads, -1).transpose(0, 2, 1, 3)


def _merge_heads(t):
    b, h, l, d = t.shape
    return t.transpose(0, 2, 1, 3).reshape(b, l, h * d)


def _flip(t, rev, axis=2):
    return jnp.flip(t, axis) if rev else t


def _to_chunks(t, size):
    b, h, l, d = t.shape
    return jnp.moveaxis(t.reshape(b, h, l // size, size, d), 2, 0)


def _from_chunks(t):
    n, b, h, c, d = t.shape
    return jnp.moveaxis(t, 0, 2).reshape(b, h, n * c, d)


def _group_norm(o):
    mu = jnp.mean(o, axis=-1, keepdims=True)
    var = jnp.mean(jnp.square(o - mu), axis=-1, keepdims=True)
    return (o - mu) * lax.rsqrt(var + EPS)


def _head_rms(o):
    return o * lax.rsqrt(jnp.mean(o * o, axis=-1, keepdims=True) + EPS)


def _rope_2d(t, rows):
    nf = t.shape[-1] // 4
    inv = 1.0 / (ROPE_BASE ** (jnp.arange(nf, dtype=jnp.float32) / nf))
    r = jnp.repeat(jnp.arange(rows, dtype=jnp.float32), GRID_W)
    col = jnp.tile(jnp.arange(GRID_W, dtype=jnp.float32), rows)
    ang = jnp.concatenate([r[:, None] * inv, col[:, None] * inv], axis=-1)
    cos, sin = jnp.cos(ang), jnp.sin(ang)
    t1, t2 = t[..., 0::2], t[..., 1::2]
    return jnp.stack([t1 * cos - t2 * sin, t1 * sin + t2 * cos], axis=-1).reshape(t.shape)


def _s5_discretize(lam_re, lam_im, log_dt, b_re, b_im):
    dt = jnp.exp(log_dt)[:, None]
    mag = jnp.exp(lam_re * dt)
    a_re, a_im = mag * jnp.cos(lam_im * dt), mag * jnp.sin(lam_im * dt)
    den = lam_re * lam_re + lam_im * lam_im
    f_re = ((a_re - 1.0) * lam_re + a_im * lam_im) / den
    f_im = (a_im * lam_re - (a_re - 1.0) * lam_im) / den
    bb_re = f_re[..., None] * b_re - f_im[..., None] * b_im
    bb_im = f_re[..., None] * b_im + f_im[..., None] * b_re
    return a_re, a_im, bb_re, bb_im


def _s5_scan(x_re, x_im, a_re, a_im, h0_re, h0_im):
    l = x_re.shape[1]
    ar = jnp.broadcast_to(a_re, (1, l) + a_re.shape)
    ai = jnp.broadcast_to(a_im, (1, l) + a_im.shape)

    def combine(e1, e2):
        a1r, a1i, x1r, x1i = e1
        a2r, a2i, x2r, x2i = e2
        return (a2r * a1r - a2i * a1i, a2r * a1i + a2i * a1r,
                a2r * x1r - a2i * x1i + x2r, a2r * x1i + a2i * x1r + x2i)

    pr, pi, hr, hi = lax.associative_scan(combine, (ar, ai, x_re, x_im), axis=1)
    h0r, h0i = h0_re[:, None], h0_im[:, None]
    return hr + pr * h0r - pi * h0i, hi + pr * h0i + pi * h0r


def _s5_branch(u_ctx, u_lat, lam_re, lam_im, log_dt, b_re, b_im, c_re, c_im, d_skip, glu_w, glu_b, need_ctx):
    f32 = jnp.float32
    lam_re, lam_im, log_dt, b_re, b_im, c_re, c_im, d_skip, glu_w, glu_b = (
        t.astype(f32) for t in (lam_re, lam_im, log_dt, b_re, b_im, c_re, c_im, d_skip, glu_w, glu_b))
    u_c, u_l = u_ctx.astype(f32), u_lat.astype(f32)
    bsz = u_l.shape[0]
    grp_c = u_c.reshape(bsz, u_c.shape[1], S5_GROUPS, S5_GROUP)
    grp_l = u_l.reshape(bsz, u_l.shape[1], S5_GROUPS, S5_GROUP)
    zeros = jnp.zeros((bsz, S5_GROUPS, S5_STATE), f32)
    hc_re = hc_im = hl_re = hl_im = 0.0
    for d in range(2):
        rev = d == 1
        a_re, a_im, bb_re, bb_im = _s5_discretize(lam_re[d], lam_im[d], log_dt[d], b_re, b_im)
        xc = [_flip(jnp.einsum('blgc,gpc->blgp', grp_c, bb), rev, 1) for bb in (bb_re, bb_im)]
        xl = [_flip(jnp.einsum('blgc,gpc->blgp', grp_l, bb), rev, 1) for bb in (bb_re, bb_im)]
        sc_re, sc_im = _s5_scan(xc[0], xc[1], a_re, a_im, zeros, zeros)
        sl_re, sl_im = _s5_scan(xl[0], xl[1], a_re, a_im, sc_re[:, -1], sc_im[:, -1])
        hc_re, hc_im = hc_re + _flip(sc_re, rev, 1), hc_im + _flip(sc_im, rev, 1)
        hl_re, hl_im = hl_re + _flip(sl_re, rev, 1), hl_im + _flip(sl_im, rev, 1)

    def readout(h_re, h_im, u):
        y = jnp.einsum('blgp,gcp->blgc', h_re, c_re) - jnp.einsum('blgp,gcp->blgc', h_im, c_im)
        y = jax.nn.gelu(y.reshape(u.shape) + d_skip * u)
        return y * jax.nn.sigmoid(y @ glu_w + glu_b)

    y_l = readout(hl_re, hl_im, u_l).astype(u_lat.dtype)
    y_c = readout(hc_re, hc_im, u_c).astype(u_ctx.dtype) if need_ctx else None
    return y_c, y_l


def _retention_chunked(q, k, v, log_gamma, s0):
    pos = jnp.arange(RET_CHUNK, dtype=jnp.float32)
    rel = pos[:, None] - pos[None, :]
    intra = jnp.where(rel >= 0, jnp.exp(jnp.maximum(rel, 0.0) * log_gamma[:, None, None]), 0.0)
    q_dec = jnp.exp((pos + 1.0) * log_gamma[:, None])[..., None]
    k_dec = jnp.exp((RET_CHUNK - 1.0 - pos) * log_gamma[:, None])[..., None]
    c_dec = jnp.exp(RET_CHUNK * log_gamma)[:, None, None]

    def step(s, blk):
        qc, kc, vc = blk
        att = jnp.einsum('bhid,bhjd->bhij', qc, kc) * intra
        o = jnp.einsum('bhij,bhjv->bhiv', att, vc) + jnp.einsum('bhid,bhdv->bhiv', qc * q_dec, s)
        s = c_dec * s + jnp.einsum('bhjd,bhjv->bhdv', kc * k_dec, vc)
        return s, o

    s_fin, o = lax.scan(step, s0, (_to_chunks(q, RET_CHUNK), _to_chunks(k, RET_CHUNK), _to_chunks(v, RET_CHUNK)))
    return _from_chunks(o), s_fin


def _retnet_branch(p_ctx, p_lat, log_decay, gn_gain, rows, need_ctx):
    f32 = jnp.float32
    log_decay, gn_gain = log_decay.astype(f32), gn_gain.astype(f32)

    def prep(p, rotate):
        q, k, v, g = (t.astype(f32) for t in p)
        q, k, v = _heads(q, RET_HEADS), _heads(k, RET_HEADS), _heads(v, RET_HEADS)
        if rotate:
            q, k = _rope_2d(q, rows), _rope_2d(k, rows)
        return q, k * RET_DK ** -0.5, v, g

    qc, kc, vc, gc = prep(p_ctx, False)
    ql, kl, vl, gl = prep(p_lat, True)
    zeros = jnp.zeros((ql.shape[0], RET_HEADS, RET_DK, RET_DV), f32)
    o_c = o_l = 0.0
    for d in range(2):
        rev = d == 1
        oc, s_ctx = _retention_chunked(_flip(qc, rev), _flip(kc, rev), _flip(vc, rev), log_decay[d], zeros)
        ol, _ = _retention_chunked(_flip(ql, rev), _flip(kl, rev), _flip(vl, rev), log_decay[d], s_ctx)
        o_c = o_c + _flip(oc, rev)
        o_l = o_l + _flip(ol, rev)

    def readout(o, g, dtype):
        return (_merge_heads(_group_norm(o)) * gn_gain * jax.nn.silu(g)).astype(dtype)

    y_l = readout(o_l, gl, p_lat[0].dtype)
    y_c = readout(o_c, gc, p_ctx[0].dtype) if need_ctx else None
    return y_c, y_l


def _gla_chunked(q, k, v, log_alpha, s0):
    mask = jnp.tril(jnp.ones((GLA_CHUNK, GLA_CHUNK), dtype=bool))

    def step(s, blk):
        qc, kc, vc, gc = blk
        b = jnp.cumsum(gc, axis=-2)
        b_last = b[..., -1:, :]
        q_t = qc * jnp.exp(b)
        k_t = kc * jnp.exp(-b)
        att = jnp.where(mask, jnp.einsum('bhid,bhjd->bhij', q_t, k_t), 0.0)
        o = jnp.einsum('bhij,bhjv->bhiv', att, vc) + jnp.einsum('bhid,bhdv->bhiv', q_t, s)
        s = jnp.exp(b_last)[..., 0, :, None] * s + jnp.einsum('bhjd,bhjv->bhdv', kc * jnp.exp(b_last - b), vc)
        return s, o

    xs = tuple(_to_chunks(t, GLA_CHUNK) for t in (q, k, v, log_alpha))
    s_fin, o = lax.scan(step, s0, xs)
    return _from_chunks(o), s_fin


def _gla_branch(p_ctx, p_lat, gate_w, gate_b, norm_gain, need_ctx):
    f32 = jnp.float32
    gate_w, gate_b, norm_gain = gate_w.astype(f32), gate_b.astype(f32), norm_gain.astype(f32)

    def prep(p):
        q, k, v, g, z_f, z_b = (t.astype(f32) for t in p)
        la = tuple(_heads(jax.nn.log_sigmoid(z @ gate_w[d] + gate_b[d]) / GLA_TAU, GLA_HEADS)
                   for d, z in enumerate((z_f, z_b)))
        return (_heads(q, GLA_HEADS) * GLA_DK ** -0.5, _heads(k, GLA_HEADS), _heads(v, GLA_HEADS), la, g)

    qc, kc, vc, lac, gc = prep(p_ctx)
    ql, kl, vl, lal, gl = prep(p_lat)
    zeros = jnp.zeros((ql.shape[0], GLA_HEADS, GLA_DK, GLA_DV), f32)
    o_c = o_l = 0.0
    for d in range(2):
        rev = d == 1
        oc, s_ctx = _gla_chunked(_flip(qc, rev), _flip(kc, rev), _flip(vc, rev), _flip(lac[d], rev), zeros)
        ol, _ = _gla_chunked(_flip(ql, rev), _flip(kl, rev), _flip(vl, rev), _flip(lal[d], rev), s_ctx)
        o_c = o_c + _flip(oc, rev)
        o_l = o_l + _flip(ol, rev)

    def readout(o, g, dtype):
        return (_merge_heads(_head_rms(o)) * norm_gain * jax.nn.silu(g)).astype(dtype)

    y_l = readout(o_l, gl, p_lat[0].dtype)
    y_c = readout(o_c, gc, p_ctx[0].dtype) if need_ctx else None
    return y_c, y_l


def _merge_branches(ys, gates, w_brs, w_out):
    m = jax.nn.sigmoid(gates[0]) * (ys[0] @ w_brs[0])
    for y, g, w in zip(ys[1:], gates[1:], w_brs[1:]):
        m = m + jax.nn.sigmoid(g) * (y @ w)
    return m @ w_out


def _swiglu(h, w_in, w_out):
    a, b = jnp.split(h @ w_in, 2, axis=-1)
    return (jax.nn.silu(a) * b) @ w_out


def setup_inputs(seed: int = 0) -> dict:
    key = jax.random.key(seed)
    ks = iter(jax.random.split(key, 40))
    f32 = jnp.float32

    def nrm(shape, scale):
        return scale * jax.random.normal(next(ks), shape, f32)

    G, P = S5_GROUPS, S5_STATE
    hippo_im = jnp.pi * jnp.arange(P, dtype=f32)
    ret_base = jnp.log(1.0 - 2.0 ** (-5.0 - jnp.arange(RET_HEADS, dtype=f32)))
    return dict(
        x=nrm((BATCH, SEQ, D_MODEL), 1.0),
        c=nrm((BATCH, D_MODEL), 1.0),
        ctx=nrm((BATCH, CTX_LEN, D_MODEL), 1.0),
        c_ctx=nrm((D_MODEL,), 1.0),
        w_mod=nrm((DEPTH, D_MODEL, N_MOD * D_MODEL), D_MODEL ** -0.5),
        b_mod=nrm((DEPTH, N_MOD * D_MODEL), 0.01),
        norm_mix=1.0 + nrm((DEPTH, D_MODEL), 0.01),
        norm_ffn=1.0 + nrm((DEPTH, D_MODEL), 0.01),
        w_in=nrm((DEPTH, D_MODEL, IN_DIM), D_MODEL ** -0.5),
        s5_lam_re=-0.5 + nrm((DEPTH, 2, G, P), 0.01),
        s5_lam_im=hippo_im + nrm((DEPTH, 2, G, P), 0.01),
        s5_log_dt=jax.random.uniform(next(ks), (DEPTH, 2, G), f32, math.log(S5_DT_MIN), math.log(S5_DT_MAX)),
        s5_b_re=nrm((DEPTH, G, P, S5_GROUP), (2 * S5_GROUP) ** -0.5),
        s5_b_im=nrm((DEPTH, G, P, S5_GROUP), (2 * S5_GROUP) ** -0.5),
        s5_c_re=nrm((DEPTH, G, S5_GROUP, P), P ** -0.5),
        s5_c_im=nrm((DEPTH, G, S5_GROUP, P), P ** -0.5),
        s5_d=nrm((DEPTH, S5_WIDTH), 1.0),
        s5_glu_w=nrm((DEPTH, S5_WIDTH, S5_WIDTH), S5_WIDTH ** -0.5),
        s5_glu_b=nrm((DEPTH, S5_WIDTH), 0.01),
        ret_log_decay=ret_base * (1.0 + nrm((DEPTH, 2, RET_HEADS), 0.05)),
        ret_gn=1.0 + nrm((DEPTH, RET_WIDTH), 0.01),
        gla_gate_w=nrm((DEPTH, 2, GLA_RANK, GLA_HEADS * GLA_DK), GLA_RANK ** -0.5),
        gla_gate_b=nrm((DEPTH, 2, GLA_HEADS * GLA_DK), 0.1),
        gla_norm=1.0 + nrm((DEPTH, GLA_WIDTH), 0.01),
        w_br_s5=nrm((DEPTH, S5_WIDTH, D_MODEL), S5_WIDTH ** -0.5),
        w_br_ret=nrm((DEPTH, RET_WIDTH, D_MODEL), RET_WIDTH ** -0.5),
        w_br_gla=nrm((DEPTH, GLA_WIDTH, D_MODEL), GLA_WIDTH ** -0.5),
        w_out=nrm((DEPTH, D_MODEL, D_MODEL), D_MODEL ** -0.5),
        w_ffn_in=nrm((DEPTH, D_MODEL, 2 * FFN_HIDDEN), D_MODEL ** -0.5),
        w_ffn_out=nrm((DEPTH, FFN_HIDDEN, D_MODEL), FFN_HIDDEN ** -0.5),
        norm_final=1.0 + nrm((D_MODEL,), 0.01),
    )


def reference(x, c, ctx, c_ctx, w_mod, b_mod, norm_mix, norm_ffn, w_in,
              s5_lam_re, s5_lam_im, s5_log_dt, s5_b_re, s5_b_im, s5_c_re, s5_c_im, s5_d, s5_glu_w, s5_glu_b,
              ret_log_decay, ret_gn, gla_gate_w, gla_gate_b, gla_norm,
              w_br_s5, w_br_ret, w_br_gla, w_out, w_ffn_in, w_ffn_out, norm_final):
    rows = x.shape[1] // GRID_W
    s_c = jax.nn.silu(c)
    s_cc = jax.nn.silu(c_ctx)
    for l in range(DEPTH):
        need_ctx = l < DEPTH - 1
        m_l = jnp.split((s_c @ w_mod[l] + b_mod[l])[:, None, :], N_MOD, axis=-1)
        m_c = jnp.split(s_cc @ w_mod[l] + b_mod[l], N_MOD, axis=-1)
        p_l = _split_in(_modulate(_rms_norm(x, norm_mix[l]), m_l[0], m_l[1]) @ w_in[l])
        p_c = _split_in(_modulate(_rms_norm(ctx, norm_mix[l]), m_c[0], m_c[1]) @ w_in[l])
        s5_c, s5_l = _s5_branch(p_c[0], p_l[0], s5_lam_re[l], s5_lam_im[l], s5_log_dt[l], s5_b_re[l], s5_b_im[l],
                                s5_c_re[l], s5_c_im[l], s5_d[l], s5_glu_w[l], s5_glu_b[l], need_ctx)
        ret_c, ret_l = _retnet_branch(p_c[1:5], p_l[1:5], ret_log_decay[l], ret_gn[l], rows, need_ctx)
        gla_c, gla_l = _gla_branch(p_c[5:11], p_l[5:11], gla_gate_w[l], gla_gate_b[l], gla_norm[l], need_ctx)
        w_brs = (w_br_s5[l], w_br_ret[l], w_br_gla[l])
        x = x + m_l[2] * _merge_branches((s5_l, ret_l, gla_l), p_l[11:14], w_brs, w_out[l])
        x = x + m_l[5] * _swiglu(_modulate(_rms_norm(x, norm_ffn[l]), m_l[3], m_l[4]), w_ffn_in[l], w_ffn_out[l])
        if need_ctx:
            ctx = ctx + m_c[2] * _merge_branches((s5_c, ret_c, gla_c), p_c[11:14], w_brs, w_out[l])
            ctx = ctx + m_c[5] * _swiglu(_modulate(_rms_norm(ctx, norm_ffn[l]), m_c[3], m_c[4]),
                                         w_ffn_in[l], w_ffn_out[l])
    return _rms_norm(x, norm_final)
```

```python
import functools
import math

import numpy as np
import jax
import jax.numpy as jnp
from jax import lax
from jax.experimental import pallas as pl
from jax.experimental.pallas import tpu as pltpu

F32 = jnp.float32
BF16 = jnp.bfloat16

D = 1024
EPS = 1e-6
N_MOD = 6
GRID_W = 64
S5_W = 256
S5_GROUP = 16
S5_GROUPS = 16
S5_STATE = 64
S5_NS = S5_GROUPS * S5_STATE
HEADS = 4
DK = 48
DV = 96
QK_W = HEADS * DK
V_W = HEADS * DV
GLA_RANK = 16
GLA_TAU = 16.0
ROPE_BASE = 10000.0
FFN_H = 2816

LANE = 128
SUB = 8
QK_P = 2 * LANE
V_P = HEADS * LANE
MIX_P = 2 * QK_P + 2 * V_P
GATE_W = 3 * D

OFF_GATE = 0
OFF_RET = GATE_W
OFF_GLA = OFF_RET + MIX_P
OFF_Z = OFF_GLA + MIX_P
NP = OFF_Z + LANE
OFF_S5 = NP
NW = NP + S5_W

ROW_T = 256
RET_C = 256
GLA_C = 64
S5_C = 128
VMEM_LIMIT = 56 * 1024 * 1024

IN_SIZES = (S5_W, QK_W, QK_W, V_W, V_W, QK_W, QK_W, V_W, V_W, GLA_RANK, GLA_RANK, D, D, D)
IN_OFFS = np.concatenate([[0], np.cumsum(IN_SIZES)]).astype(np.int64)


def _head_lane():
    m = np.zeros((HEADS, DK), np.int64)
    for h in range(HEADS):
        for i in range(DK):
            m[h, i] = (i % 2) * LANE + h * (DK // 2) + i // 2
    return m


HEAD_LANE = _head_lane()


def _static_tables():
    src = np.full((NW,), -1, np.int64)
    src[OFF_GATE:OFF_GATE + GATE_W] = IN_OFFS[11] + np.arange(GATE_W)
    for base, iq, ik, iv, ig in ((OFF_RET, 1, 2, 3, 4), (OFF_GLA, 5, 6, 7, 8)):
        for h in range(HEADS):
            for i in range(DK):
                src[base + HEAD_LANE[h, i]] = IN_OFFS[iq] + h * DK + i
                src[base + QK_P + HEAD_LANE[h, i]] = IN_OFFS[ik] + h * DK + i
            for j in range(DV):
                src[base + 2 * QK_P + h * LANE + j] = IN_OFFS[iv] + h * DV + j
                src[base + 2 * QK_P + V_P + h * LANE + j] = IN_OFFS[ig] + h * DV + j
    src[OFF_Z:OFF_Z + GLA_RANK] = IN_OFFS[9] + np.arange(GLA_RANK)
    src[OFF_Z + GLA_RANK:OFF_Z + 2 * GLA_RANK] = IN_OFFS[10] + np.arange(GLA_RANK)
    src[OFF_S5:OFF_S5 + S5_W] = np.arange(S5_W)
    vsrc = np.full((V_P,), -1, np.int64)
    for h in range(HEADS):
        vsrc[h * LANE:h * LANE + DV] = h * DV + np.arange(DV)
    qk_head = np.full((QK_P,), -1, np.int64)
    qk_src = np.full((QK_P,), -1, np.int64)
    for h in range(HEADS):
        for i in range(DK):
            qk_head[HEAD_LANE[h, i]] = h
            qk_src[HEAD_LANE[h, i]] = h * DK + i
    v_head = np.repeat(np.arange(HEADS), LANE)
    v_real = (np.arange(V_P) % LANE) < DV
    head_mask = np.zeros((SUB, QK_P), np.float32)
    for h in range(HEADS):
        head_mask[h] = (qk_head == h)
    bm_t = (v_head[:, None] == qk_head[None, :]).astype(np.float32)
    avg = ((v_head[:, None] == v_head[None, :]) & v_real[:, None]).astype(np.float32)
    return dict(src=src, vsrc=vsrc, qk_head=qk_head, qk_src=qk_src, head_mask=head_mask,
                bm_t=bm_t, avg=avg, v_real=v_real.astype(np.float32))


TAB = _static_tables()


def _gather_cols(w, src):
    valid = jnp.asarray(src >= 0)
    out = jnp.take(w, jnp.asarray(np.maximum(src, 0)), axis=-1)
    return jnp.where(valid, out, jnp.zeros((), w.dtype))


def _cparams(n_axes):
    return pltpu.CompilerParams(dimension_semantics=("arbitrary",) * n_axes,
                                vmem_limit_bytes=VMEM_LIMIT)


def _const_spec(shape):
    nd = len(shape)
    return pl.BlockSpec(shape, lambda *_: (0,) * nd, pipeline_mode=pl.Buffered(1))


def _mod_kernel(c_ref, w_ref, b_ref, o_ref):
    s = c_ref[...]
    s = s * jax.nn.sigmoid(s)
    o_ref[...] = jnp.dot(s, w_ref[...], preferred_element_type=F32,
                         precision=lax.Precision.HIGHEST) + b_ref[...]


def _mod_call(c_all, w_mod_l, b_mod_l):
    rows = c_all.shape[0]
    tn = D
    return pl.pallas_call(
        _mod_kernel,
        grid=(N_MOD * D // tn,),
        in_specs=[pl.BlockSpec((rows, D), lambda j: (0, 0)),
                  pl.BlockSpec((D, tn), lambda j: (0, j)),
                  pl.BlockSpec((1, tn), lambda j: (0, j))],
        out_specs=pl.BlockSpec((rows, tn), lambda j: (0, j)),
        out_shape=jax.ShapeDtypeStruct((rows, N_MOD * D), F32),
        compiler_params=_cparams(1),
    )(c_all, w_mod_l, b_mod_l.reshape(1, -1))


def _rms(x, gain):
    return x * lax.rsqrt(jnp.mean(x * x, axis=-1, keepdims=True) + EPS) * gain


def _inproj_kernel(x_ref, mod_ref, gain_ref, w_ref, p_ref, u_ref):
    h = _rms(x_ref[...], gain_ref[...])
    h = h * (1.0 + mod_ref[1:2, :]) + mod_ref[0:1, :]
    hb = h.astype(BF16)
    ch = 896
    for c0 in range(0, NP, ch):
        p_ref[:, c0:c0 + ch] = jnp.dot(hb, w_ref[:, c0:c0 + ch], preferred_element_type=F32)
    u_ref[...] = jnp.dot(hb, w_ref[:, OFF_S5:OFF_S5 + S5_W], preferred_element_type=F32)


def _inproj_call(xs, modsel, gain, w_all):
    b, lt, _ = xs.shape
    nt = lt // ROW_T
    return pl.pallas_call(
        _inproj_kernel,
        grid=(b, nt),
        in_specs=[pl.BlockSpec((None, ROW_T, D), lambda i, j: (i, j, 0)),
                  pl.BlockSpec((None, None, SUB, D), lambda i, j: (i, jnp.minimum(j, 1), 0, 0)),
                  _const_spec((1, D)),
                  _const_spec((D, NW))],
        out_specs=[pl.BlockSpec((None, ROW_T, NP), lambda i, j: (i, j, 0)),
                   pl.BlockSpec((None, ROW_T, S5_W), lambda i, j: (i, j, 0))],
        out_shape=[jax.ShapeDtypeStruct((b, lt, NP), F32),
                   jax.ShapeDtypeStruct((b, lt, S5_W), F32)],
        compiler_params=_cparams(2),
    )(xs, modsel, gain.reshape(1, D), w_all)


def _s5_kernel(u_ref, bb_ref, are_ref, aim_ref, cb_ref, y_ref, xh_ref, h_ref, *, tc):
    d = pl.program_id(1)
    c = pl.program_id(2)

    @pl.when(c == 0)
    def _():
        h_ref[...] = jnp.zeros_like(h_ref)

    u = u_ref[...].reshape(tc * SUB, S5_W).astype(BF16)
    xh_ref[...] = jnp.dot(u, bb_ref[...], preferred_element_type=F32)
    half = S5_NS // 2
    for lo in (0, half):
        re = slice(lo, lo + half)
        im = slice(S5_NS + lo, S5_NS + lo + half)
        ar = are_ref[:, re]
        ai = aim_ref[:, re]

        def body(t, carry, re=re, im=im, ar=ar, ai=ai):
            hr, hi = carry
            tt = t + d * (tc - 1 - 2 * t)
            r0 = pl.multiple_of(tt * SUB, SUB)
            nr = ar * hr - ai * hi + xh_ref[pl.ds(r0, SUB), re]
            ni = ar * hi + ai * hr + xh_ref[pl.ds(r0, SUB), im]
            xh_ref[pl.ds(r0, SUB), re] = nr
            xh_ref[pl.ds(r0, SUB), im] = ni
            return nr, ni

        hr, hi = lax.fori_loop(0, tc, body, (h_ref[:, re], h_ref[:, im]), unroll=4)
        h_ref[:, re] = hr
        h_ref[:, im] = hi
    y = jnp.dot(xh_ref[...].astype(BF16), cb_ref[...], preferred_element_type=F32)
    y_ref[...] = y.reshape(tc, SUB, S5_W)


def _dir_chunk(d, c, n_ctx, n_all):
    bwd = jnp.where(c < n_ctx, n_ctx - 1 - c, n_all + n_ctx - 1 - c)
    return jnp.where(d == 0, c, bwd)


def _s5_call(u, bblk, a_re, a_im, cblk, ctx_len):
    lt, b, _ = u.shape
    tc = S5_C
    n_all, n_ctx = lt // tc, ctx_len // tc
    cidx = functools.partial(_dir_chunk, n_ctx=n_ctx, n_all=n_all)
    return pl.pallas_call(
        functools.partial(_s5_kernel, tc=tc),
        grid=(b // SUB, 2, n_all),
        in_specs=[pl.BlockSpec((tc, SUB, S5_W), lambda g, d, c: (cidx(d, c), g, 0)),
                  pl.BlockSpec((None, S5_W, 2 * S5_NS), lambda g, d, c: (d, 0, 0)),
                  pl.BlockSpec((None, SUB, S5_NS), lambda g, d, c: (d, 0, 0)),
                  pl.BlockSpec((None, SUB, S5_NS), lambda g, d, c: (d, 0, 0)),
                  _const_spec((2 * S5_NS, S5_W))],
        out_specs=pl.BlockSpec((None, tc, SUB, S5_W), lambda g, d, c: (d, cidx(d, c), g, 0)),
        out_shape=jax.ShapeDtypeStruct((2, lt, b, S5_W), F32),
        scratch_shapes=[pltpu.VMEM((tc * SUB, 2 * S5_NS), F32),
                        pltpu.VMEM((SUB, 2 * S5_NS), F32)],
        compiler_params=_cparams(3),
    )(u, bblk, a_re, a_im, cblk)


_NT = (((1,), (1,)), ((), ()))
_TN = (((0,), (0,)), ((), ()))


def _ret_kernel(p_ref, cos_ref, sin_ref, dm_ref, qd_ref, kd_ref, cd_ref, hm_ref, bm_ref,
                o_ref, s_ref):
    c = pl.program_id(2)

    @pl.when(c == 0)
    def _():
        s_ref[...] = jnp.zeros_like(s_ref)

    cs = cos_ref[...]
    sn = sin_ref[...]

    def rope(off):
        a = p_ref[:, off:off + LANE]
        b = p_ref[:, off + LANE:off + 2 * LANE]
        return jnp.concatenate([a * cs - b * sn, a * sn + b * cs], axis=1)

    q = rope(0)
    k = rope(QK_P) * (DK ** -0.5)
    kb = k.astype(BF16)
    vb = p_ref[:, 2 * QK_P:2 * QK_P + V_P].astype(BF16)
    st = s_ref[...]
    inter = lax.dot_general((q * qd_ref[...]).astype(BF16), st.astype(BF16), _NT,
                            preferred_element_type=F32)
    for h in range(HEADS):
        qh = (q * hm_ref[h:h + 1, :]).astype(BF16)
        att = lax.dot_general(qh, kb, _NT, preferred_element_type=F32) * dm_ref[h]
        sl = slice(h * LANE, (h + 1) * LANE)
        o_ref[:, sl] = jnp.dot(att.astype(BF16), vb[:, sl], preferred_element_type=F32) + inter[:, sl]
    kdv = lax.dot_general(vb, (k * kd_ref[...]).astype(BF16), _TN, preferred_element_type=F32)
    s_ref[...] = st * cd_ref[...] + bm_ref[...] * kdv


def _ret_call(p, cos_t, sin_t, dmask, qdec, kdec, cdec, ctx_len):
    b, lt, _ = p.shape
    tc = RET_C
    n_all, n_ctx = lt // tc, ctx_len // tc
    cidx = functools.partial(_dir_chunk, n_ctx=n_ctx, n_all=n_all)
    return pl.pallas_call(
        _ret_kernel,
        grid=(b, 2, n_all),
        in_specs=[pl.BlockSpec((None, tc, MIX_P), lambda i, d, c: (i, cidx(d, c), OFF_RET // MIX_P)),
                  pl.BlockSpec((tc, LANE), lambda i, d, c: (cidx(d, c), 0)),
                  pl.BlockSpec((tc, LANE), lambda i, d, c: (cidx(d, c), 0)),
                  pl.BlockSpec((None, HEADS, tc, tc), lambda i, d, c: (d, 0, 0, 0)),
                  pl.BlockSpec((None, tc, QK_P), lambda i, d, c: (d, 0, 0)),
                  pl.BlockSpec((None, tc, QK_P), lambda i, d, c: (d, 0, 0)),
                  pl.BlockSpec((None, 1, QK_P), lambda i, d, c: (d, 0, 0)),
                  _const_spec((SUB, QK_P)),
                  _const_spec((V_P, QK_P))],
        out_specs=pl.BlockSpec((None, None, tc, V_P), lambda i, d, c: (d, i, cidx(d, c), 0)),
        out_shape=jax.ShapeDtypeStruct((2, b, lt, V_P), F32),
        scratch_shapes=[pltpu.VMEM((V_P, QK_P), F32)],
        compiler_params=_cparams(3),
    )(p, cos_t, sin_t, dmask, qdec, kdec, cdec,
      jnp.asarray(TAB["head_mask"]), jnp.asarray(TAB["bm_t"]))


def _gla_kernel(p_ref, z_ref, gw_ref, gb_ref, tri_ref, hm_ref, bm_ref, o_ref, s_ref):
    c = pl.program_id(2)

    @pl.when(c == 0)
    def _():
        s_ref[...] = jnp.zeros_like(s_ref)

    tri = tri_ref[...]
    la = jax.nn.log_sigmoid(
        jnp.dot(z_ref[...], gw_ref[...], preferred_element_type=F32,
                precision=lax.Precision.HIGHEST) + gb_ref[...]) / GLA_TAU
    bcum = jnp.dot(tri, la, preferred_element_type=F32, precision=lax.Precision.HIGHEST)
    blast = jnp.sum(la, axis=0, keepdims=True)
    q = p_ref[:, 0:QK_P] * (DK ** -0.5)
    k = p_ref[:, QK_P:2 * QK_P]
    vb = p_ref[:, 2 * QK_P:2 * QK_P + V_P].astype(BF16)
    qt = q * jnp.exp(bcum)
    ktb = (k * jnp.exp(-bcum)).astype(BF16)
    st = s_ref[...]
    inter = lax.dot_general(qt.astype(BF16), st.astype(BF16), _NT, preferred_element_type=F32)
    for h in range(HEADS):
        qh = (qt * hm_ref[h:h + 1, :]).astype(BF16)
        att = lax.dot_general(qh, ktb, _NT, preferred_element_type=F32) * tri
        sl = slice(h * LANE, (h + 1) * LANE)
        o_ref[:, sl] = jnp.dot(att.astype(BF16), vb[:, sl], preferred_element_type=F32) + inter[:, sl]
    kd = (k * jnp.exp(blast - bcum)).astype(BF16)
    kdv = lax.dot_general(vb, kd, _TN, preferred_element_type=F32)
    s_ref[...] = st * jnp.exp(blast) + bm_ref[...] * kdv


def _gla_call(p, gw, gb, tri, ctx_len):
    b, lt, _ = p.shape
    tc = GLA_C
    n_all, n_ctx = lt // tc, ctx_len // tc
    cidx = functools.partial(_dir_chunk, n_ctx=n_ctx, n_all=n_all)
    return pl.pallas_call(
        _gla_kernel,
        grid=(b, 2, n_all),
        in_specs=[pl.BlockSpec((None, tc, MIX_P), lambda i, d, c: (i, cidx(d, c), OFF_GLA // MIX_P)),
                  pl.BlockSpec((None, tc, LANE), lambda i, d, c: (i, cidx(d, c), OFF_Z // LANE)),
                  pl.BlockSpec((None, LANE, QK_P), lambda i, d, c: (d, 0, 0)),
                  pl.BlockSpec((None, 1, QK_P), lambda i, d, c: (d, 0, 0)),
                  pl.BlockSpec((None, tc, tc), lambda i, d, c: (d, 0, 0)),
                  _const_spec((SUB, QK_P)),
                  _const_spec((V_P, QK_P))],
        out_specs=pl.BlockSpec((None, None, tc, V_P), lambda i, d, c: (d, i, cidx(d, c), 0)),
        out_shape=jax.ShapeDtypeStruct((2, b, lt, V_P), F32),
        scratch_shapes=[pltpu.VMEM((V_P, QK_P), F32)],
        compiler_params=_cparams(3),
    )(p, p, gw, gb, tri, jnp.asarray(TAB["head_mask"]), jnp.asarray(TAB["bm_t"]))


def _merge_kernel(x_ref, mod_ref, u_ref, yf_ref, yb_ref, rf_ref, rb_ref, rg_ref,
                  gf_ref, gb_ref, gg_ref, gate_ref,
                  dsk_ref, gluw_ref, glub_ref, gn_ref, gln_ref, avg_ref, real_ref,
                  wbs_ref, wbr_ref, wbg_ref, wo_ref, o_ref):
    y = jax.nn.gelu(yf_ref[...] + yb_ref[...] + dsk_ref[...] * u_ref[...])
    glu = jnp.dot(y.astype(BF16), gluw_ref[...], preferred_element_type=F32) + glub_ref[...]
    ys = y * jax.nn.sigmoid(glu)
    m = jax.nn.sigmoid(gate_ref[:, 0:D]) * jnp.dot(ys.astype(BF16), wbs_ref[...],
                                                    preferred_element_type=F32)
    avg = avg_ref[...]
    o = rf_ref[...] + rb_ref[...]
    mu = jnp.dot(o.astype(BF16), avg, preferred_element_type=F32) * (1.0 / DV)
    dlt = (o - mu) * real_ref[...]
    var = jnp.dot((dlt * dlt).astype(BF16), avg, preferred_element_type=F32) * (1.0 / DV)
    g = rg_ref[...]
    yr = dlt * lax.rsqrt(var + EPS) * gn_ref[...] * (g * jax.nn.sigmoid(g))
    m = m + jax.nn.sigmoid(gate_ref[:, D:2 * D]) * jnp.dot(yr.astype(BF16), wbr_ref[...],
                                                          preferred_element_type=F32)
    o = gf_ref[...] + gb_ref[...]
    ms = jnp.dot((o * o).astype(BF16), avg, preferred_element_type=F32) * (1.0 / DV)
    g = gg_ref[...]
    yg = o * lax.rsqrt(ms + EPS) * gln_ref[...] * (g * jax.nn.sigmoid(g))
    m = m + jax.nn.sigmoid(gate_ref[:, 2 * D:3 * D]) * jnp.dot(yg.astype(BF16), wbg_ref[...],
                                                              preferred_element_type=F32)
    out = jnp.dot(m.astype(BF16), wo_ref[...], preferred_element_type=F32)
    o_ref[...] = x_ref[...] + mod_ref[2:3, :] * out


def _merge_call(xs, modsel, u, ys5, p, oret, ogla, consts, lat_only, ctx_len):
    b, lt, _ = xs.shape
    j0 = ctx_len // ROW_T if lat_only else 0
    nt = lt // ROW_T - j0
    (dsk, gluw, glub, gn, gln, wbs, wbr, wbg, wo) = consts
    tok = lambda w, blk: pl.BlockSpec((None, ROW_T, w), lambda i, j: (i, j + j0, blk))
    dirtok = lambda dd: pl.BlockSpec((None, None, ROW_T, V_P), lambda i, j: (dd, i, j + j0, 0))
    s5tok = lambda dd: pl.BlockSpec((None, None, ROW_T, S5_W), lambda i, j: (dd, i, j + j0, 0))
    in_specs = [
        tok(D, 0),
        pl.BlockSpec((None, None, SUB, D), lambda i, j: (i, jnp.minimum(j + j0, 1), 0, 0)),
        tok(S5_W, 0),
        s5tok(0), s5tok(1),
        dirtok(0), dirtok(1), tok(V_P, (OFF_RET + 2 * QK_P + V_P) // V_P),
        dirtok(0), dirtok(1), tok(V_P, (OFF_GLA + 2 * QK_P + V_P) // V_P),
        tok(GATE_W, 0),
        _const_spec((1, S5_W)), _const_spec((S5_W, S5_W)), _const_spec((1, S5_W)),
        _const_spec((1, V_P)), _const_spec((1, V_P)), _const_spec((V_P, V_P)), _const_spec((1, V_P)),
        _const_spec((S5_W, D)), _const_spec((V_P, D)), _const_spec((V_P, D)), _const_spec((D, D)),
    ]
    out_rows = nt * ROW_T
    return pl.pallas_call(
        _merge_kernel,
        grid=(b, nt),
        in_specs=in_specs,
        out_specs=pl.BlockSpec((None, ROW_T, D), lambda i, j: (i, j, 0)),
        out_shape=jax.ShapeDtypeStruct((b, out_rows, D), F32),
        compiler_params=_cparams(2),
    )(xs, modsel, u, ys5, ys5, oret, oret, p, ogla, ogla, p, p,
      dsk, gluw, glub, gn, gln, jnp.asarray(TAB["avg"], dtype=BF16),
      jnp.asarray(TAB["v_real"]).reshape(1, V_P),
      wbs, wbr, wbg, wo)


FFN_T = 256


def _ffn_kernel(x_ref, mod_ref, gain_ref, wa_ref, wb_ref, wo_ref, fin_ref, o_ref, *, final):
    x = x_ref[...]
    h = _rms(x, gain_ref[...])
    hb = (h * (1.0 + mod_ref[4:5, :]) + mod_ref[3:4, :]).astype(BF16)
    acc = jnp.zeros_like(x)
    for t in range(FFN_H // FFN_T):
        a = jnp.dot(hb, wa_ref[t], preferred_element_type=F32)
        bq = jnp.dot(hb, wb_ref[t], preferred_element_type=F32)
        act = (a * jax.nn.sigmoid(a) * bq).astype(BF16)
        acc = acc + jnp.dot(act, wo_ref[t], preferred_element_type=F32)
    y = x + mod_ref[5:6, :] * acc
    if final:
        y = _rms(y, fin_ref[...])
    o_ref[...] = y


def _ffn_call(xs, modsel, gain, wa, wb, wo, fin, final, lat_only):
    b, rows, _ = xs.shape
    nt = rows // ROW_T
    msel = (lambda j: 1) if lat_only else (lambda j: jnp.minimum(j, 1))
    nh = FFN_H // FFN_T
    return pl.pallas_call(
        functools.partial(_ffn_kernel, final=final),
        grid=(b, nt),
        in_specs=[pl.BlockSpec((None, ROW_T, D), lambda i, j: (i, j, 0)),
                  pl.BlockSpec((None, None, SUB, D), lambda i, j: (i, msel(j), 0, 0)),
                  _const_spec((1, D)),
                  _const_spec((nh, D, FFN_T)), _const_spec((nh, D, FFN_T)),
                  _const_spec((nh, FFN_T, D)), _const_spec((1, D))],
        out_specs=pl.BlockSpec((None, ROW_T, D), lambda i, j: (i, j, 0)),
        out_shape=jax.ShapeDtypeStruct((b, rows, D), F32),
        compiler_params=_cparams(2),
    )(xs, modsel, gain.reshape(1, D), wa, wb, wo, fin.reshape(1, D))


def _s5_params(lam_re, lam_im, log_dt, b_re, b_im, c_re, c_im):
    dt = jnp.exp(log_dt)[..., None]
    mag = jnp.exp(lam_re * dt)
    a_re, a_im = mag * jnp.cos(lam_im * dt), mag * jnp.sin(lam_im * dt)
    den = lam_re * lam_re + lam_im * lam_im
    f_re = ((a_re - 1.0) * lam_re + a_im * lam_im) / den
    f_im = (a_im * lam_re - (a_re - 1.0) * lam_im) / den
    bb_re = f_re[..., None] * b_re - f_im[..., None] * b_im
    bb_im = f_re[..., None] * b_im + f_im[..., None] * b_re
    eye = jnp.eye(S5_GROUPS, dtype=F32)

    def blk_in(bb):
        t = jnp.einsum("dgpc,gh->dgchp", bb, eye)
        return t.reshape(2, S5_W, S5_NS)

    bblk = jnp.concatenate([blk_in(bb_re), blk_in(bb_im)], axis=-1).astype(BF16)

    def blk_out(cc):
        t = jnp.einsum("gcp,gh->gphc", cc, eye)
        return t.reshape(S5_NS, S5_W)

    cblk = jnp.concatenate([blk_out(c_re), -blk_out(c_im)], axis=0).astype(BF16)
    bc = lambda a: jnp.broadcast_to(a.reshape(2, 1, S5_NS), (2, SUB, S5_NS))
    return bblk, bc(a_re), bc(a_im), cblk


def _ret_tables(log_decay):
    tc = RET_C
    pos = jnp.arange(tc, dtype=F32)
    w = jnp.stack([pos, tc - 1.0 - pos])
    rel = w[:, :, None] - w[:, None, :]
    lg = log_decay[:, :, None, None]
    dmask = jnp.where(rel[:, None] >= 0, jnp.exp(jnp.maximum(rel[:, None], 0.0) * lg), 0.0)
    qk_head = TAB["qk_head"]
    lane_lg = jnp.where(jnp.asarray(qk_head >= 0),
                        jnp.take(log_decay, jnp.asarray(np.maximum(qk_head, 0)), axis=1), 0.0)
    qdec = jnp.exp((w[:, :, None] + 1.0) * lane_lg[:, None, :])
    kdec = jnp.exp((tc - 1.0 - w[:, :, None]) * lane_lg[:, None, :])
    cdec = jnp.exp(tc * lane_lg)[:, None, :]
    return dmask, qdec, kdec, cdec


def _rope_tables(seq, ctx_len):
    rows = seq // GRID_W
    nf = DK // 4
    inv = 1.0 / (ROPE_BASE ** (np.arange(nf, dtype=np.float32) / nf))
    r = np.repeat(np.arange(rows, dtype=np.float32), GRID_W)
    col = np.tile(np.arange(GRID_W, dtype=np.float32), rows)
    ang = np.concatenate([r[:, None] * inv, col[:, None] * inv], axis=-1)
    cos_t = np.ones((ctx_len + seq, LANE), np.float32)
    sin_t = np.zeros((ctx_len + seq, LANE), np.float32)
    for h in range(HEADS):
        cos_t[ctx_len:, h * 24:(h + 1) * 24] = np.cos(ang)
        sin_t[ctx_len:, h * 24:(h + 1) * 24] = np.sin(ang)
    return jnp.asarray(cos_t), jnp.asarray(sin_t)


def _gla_tables(gate_w, gate_b):
    gw = _gather_cols(gate_w, TAB["qk_src"])
    gw_full = jnp.zeros((2, LANE, QK_P), F32)
    gw_full = gw_full.at[0, 0:GLA_RANK].set(gw[0]).at[1, GLA_RANK:2 * GLA_RANK].set(gw[1])
    gb = _gather_cols(gate_b, TAB["qk_src"])[:, None, :]
    i = np.arange(GLA_C)
    tri = np.stack([(i[:, None] >= i[None, :]), (i[:, None] <= i[None, :])]).astype(np.float32)
    return gw_full, gb, jnp.asarray(tri)


def _pad_rows(w, src):
    return _gather_cols(w.T, src).T


def kernel(x, c, ctx, c_ctx, w_mod, b_mod, norm_mix, norm_ffn, w_in, s5_lam_re, s5_lam_im, s5_log_dt, s5_b_re, s5_b_im, s5_c_re, s5_c_im, s5_d, s5_glu_w, s5_glu_b, ret_log_decay, ret_gn, gla_gate_w, gla_gate_b, gla_norm, w_br_s5, w_br_ret, w_br_gla, w_out, w_ffn_in, w_ffn_out, norm_final):
    b, seq, _ = x.shape
    ctx_len = ctx.shape[1]
    depth = w_mod.shape[0]
    assert b % SUB == 0 and ctx_len % RET_C == 0 and seq % RET_C == 0 and seq % GRID_W == 0
    lt = ctx_len + seq

    xs = jnp.concatenate([ctx, x], axis=1)
    rows = ((b + 1 + SUB - 1) // SUB) * SUB
    c_all = jnp.zeros((rows, D), F32).at[:b].set(c).at[b].set(c_ctx)
    cos_t, sin_t = _rope_tables(seq, ctx_len)
    nh = FFN_H // FFN_T

    for l in range(depth):
        last = l == depth - 1
        mods = _mod_call(c_all, w_mod[l], b_mod[l])
        m_lat = mods[:b].reshape(b, N_MOD, D)
        m_ctx = jnp.broadcast_to(mods[b].reshape(1, N_MOD, D), (b, N_MOD, D))
        modsel = jnp.stack([m_ctx, m_lat], axis=1)
        modsel = jnp.pad(modsel, ((0, 0), (0, 0), (0, SUB - N_MOD), (0, 0)))

        w_all = _gather_cols(w_in[l], TAB["src"]).astype(BF16)
        p, u = _inproj_call(xs, modsel, norm_mix[l], w_all)

        bblk, a_re, a_im, cblk = _s5_params(s5_lam_re[l], s5_lam_im[l], s5_log_dt[l],
                                            s5_b_re[l], s5_b_im[l], s5_c_re[l], s5_c_im[l])
        ys5 = _s5_call(jnp.transpose(u, (1, 0, 2)), bblk, a_re, a_im, cblk, ctx_len)
        ys5 = jnp.transpose(ys5, (0, 2, 1, 3))

        dmask, qdec, kdec, cdec = _ret_tables(ret_log_decay[l])
        oret = _ret_call(p, cos_t, sin_t, dmask, qdec, kdec, cdec, ctx_len)

        gw, gb, tri = _gla_tables(gla_gate_w[l], gla_gate_b[l])
        ogla = _gla_call(p, gw, gb, tri, ctx_len)

        consts = (s5_d[l].reshape(1, S5_W), s5_glu_w[l].astype(BF16), s5_glu_b[l].reshape(1, S5_W),
                  _gather_cols(ret_gn[l], TAB["vsrc"]).reshape(1, V_P),
                  _gather_cols(gla_norm[l], TAB["vsrc"]).reshape(1, V_P),
                  w_br_s5[l].astype(BF16),
                  _pad_rows(w_br_ret[l], TAB["vsrc"]).astype(BF16),
                  _pad_rows(w_br_gla[l], TAB["vsrc"]).astype(BF16),
                  w_out[l].astype(BF16))
        xs = _merge_call(xs, modsel, u, ys5, p, oret, ogla, consts, last, ctx_len)

        wfi = w_ffn_in[l].astype(BF16)
        wa = wfi[:, :FFN_H].reshape(D, nh, FFN_T).transpose(1, 0, 2)
        wb = wfi[:, FFN_H:].reshape(D, nh, FFN_T).transpose(1, 0, 2)
        wo = w_ffn_out[l].astype(BF16).reshape(nh, FFN_T, D)
        xs = _ffn_call(xs, modsel, norm_ffn[l], wa, wb, wo, norm_final, last, last)
    return xs
```

```python
import functools

import numpy as np
import jax
import jax.numpy as jnp
from jax import lax
from jax.experimental import pallas as pl
from jax.experimental.pallas import tpu as pltpu

F32 = jnp.float32
BF16 = jnp.bfloat16

D = 1024
EPS = 1e-6
N_MOD = 6
GRID_W = 64
S5_W = 256
S5_GROUPS = 16
S5_STATE = 64
S5_NS = S5_GROUPS * S5_STATE
HEADS = 4
DK = 48
DV = 96
QK_W = HEADS * DK
V_W = HEADS * DV
GLA_RANK = 16
GLA_TAU = 16.0
ROPE_BASE = 10000.0
FFN_H = 2816

LANE = 128
SUB = 8
QK_P = 2 * LANE
V_P = HEADS * LANE
QKV_P = 2 * QK_P + V_P
GATE_W = 3 * D

OFF_GATE = 0
OFF_RQKV = GATE_W
OFF_GQKV = OFF_RQKV + QKV_P
OFF_RG = OFF_GQKV + QKV_P
OFF_GG = OFF_RG + V_P
OFF_Z = OFF_GG + V_P
NP = OFF_Z + LANE
OFF_S5 = NP
NW = NP + S5_W

ROW_T = 256
RET_C = 256
GLA_C = 64
GLA_R = 256
S5_C = 128
VMEM_LIMIT = 56 * 1024 * 1024

IN_SIZES = (S5_W, QK_W, QK_W, V_W, V_W, QK_W, QK_W, V_W, V_W, GLA_RANK, GLA_RANK, D, D, D)
IN_OFFS = np.concatenate([[0], np.cumsum(IN_SIZES)]).astype(np.int64)


def _head_lane():
    m = np.zeros((HEADS, DK), np.int64)
    for h in range(HEADS):
        for i in range(DK):
            m[h, i] = (i % 2) * LANE + h * (DK // 2) + i // 2
    return m


HEAD_LANE = _head_lane()


def _static_tables():
    src = np.full((NW,), -1, np.int64)
    src[OFF_GATE:OFF_GATE + GATE_W] = IN_OFFS[11] + np.arange(GATE_W)
    for qkv, gg, iq, ik, iv, ig in ((OFF_RQKV, OFF_RG, 1, 2, 3, 4), (OFF_GQKV, OFF_GG, 5, 6, 7, 8)):
        for h in range(HEADS):
            for i in range(DK):
                src[qkv + HEAD_LANE[h, i]] = IN_OFFS[iq] + h * DK + i
                src[qkv + QK_P + HEAD_LANE[h, i]] = IN_OFFS[ik] + h * DK + i
            for j in range(DV):
                src[qkv + 2 * QK_P + h * LANE + j] = IN_OFFS[iv] + h * DV + j
                src[gg + h * LANE + j] = IN_OFFS[ig] + h * DV + j
    src[OFF_Z:OFF_Z + GLA_RANK] = IN_OFFS[9] + np.arange(GLA_RANK)
    src[OFF_Z + GLA_RANK:OFF_Z + 2 * GLA_RANK] = IN_OFFS[10] + np.arange(GLA_RANK)
    src[OFF_S5:OFF_S5 + S5_W] = np.arange(S5_W)
    vsrc = np.full((V_P,), -1, np.int64)
    for h in range(HEADS):
        vsrc[h * LANE:h * LANE + DV] = h * DV + np.arange(DV)
    qk_head = np.full((QK_P,), -1, np.int64)
    qk_src = np.full((QK_P,), -1, np.int64)
    for h in range(HEADS):
        for i in range(DK):
            qk_head[HEAD_LANE[h, i]] = h
            qk_src[HEAD_LANE[h, i]] = h * DK + i
    v_head = np.repeat(np.arange(HEADS), LANE)
    v_real = (np.arange(V_P) % LANE) < DV
    head_mask = np.zeros((SUB, QK_P), np.float32)
    for h in range(HEADS):
        head_mask[h] = (qk_head == h)
    bm_t = (v_head[:, None] == qk_head[None, :]).astype(np.float32)
    hsum = ((v_head[:, None] == v_head[None, :]) & v_real[:, None]).astype(np.float32)
    return dict(src=src, vsrc=vsrc, qk_head=qk_head, qk_src=qk_src, head_mask=head_mask,
                bm_t=bm_t, hsum=hsum, v_real=v_real.astype(np.float32))


TAB = _static_tables()


def _gather_cols(w, src):
    valid = jnp.asarray(src >= 0)
    out = jnp.take(w, jnp.asarray(np.maximum(src, 0)), axis=-1)
    return jnp.where(valid, out, jnp.zeros((), w.dtype))


def _cparams(n_axes):
    return pltpu.CompilerParams(dimension_semantics=("arbitrary",) * n_axes,
                                vmem_limit_bytes=VMEM_LIMIT)


def _const_spec(shape):
    nd = len(shape)
    return pl.BlockSpec(shape, lambda *_: (0,) * nd, pipeline_mode=pl.Buffered(1))


def _dir_chunk(d, c, n_ctx, n_all):
    bwd = jnp.where(c < n_ctx, n_ctx - 1 - c, n_all + n_ctx - 1 - c)
    return jnp.where(d == 0, c, bwd)


def _silu(x):
    return x * jax.nn.sigmoid(x)


def _mod_kernel(c_ref, w_ref, b_ref, o_ref):
    o_ref[...] = jnp.dot(_silu(c_ref[...]), w_ref[...], preferred_element_type=F32,
                         precision=lax.Precision.HIGHEST) + b_ref[...]


def _mod_call(c_all, w_mod_l, b_mod_l):
    rows = c_all.shape[0]
    tn = D
    return pl.pallas_call(
        _mod_kernel,
        grid=(N_MOD * D // tn,),
        in_specs=[pl.BlockSpec((rows, D), lambda j: (0, 0)),
                  pl.BlockSpec((D, tn), lambda j: (0, j)),
                  pl.BlockSpec((1, tn), lambda j: (0, j))],
        out_specs=pl.BlockSpec((rows, tn), lambda j: (0, j)),
        out_shape=jax.ShapeDtypeStruct((rows, N_MOD * D), F32),
        compiler_params=_cparams(1),
    )(c_all, w_mod_l, b_mod_l.reshape(1, -1))


def _rms(x, gain):
    return x * lax.rsqrt(jnp.mean(x * x, axis=-1, keepdims=True) + EPS) * gain


def _inproj_kernel(x_ref, mod_ref, gain_ref, w_ref, p_ref, u_ref):
    h = _rms(x_ref[...], gain_ref[...])
    h = h * (1.0 + mod_ref[1:2, :]) + mod_ref[0:1, :]
    hb = h.astype(BF16)
    ch = 896
    for c0 in range(0, NP, ch):
        p_ref[:, c0:c0 + ch] = jnp.dot(hb, w_ref[:, c0:c0 + ch],
                                       preferred_element_type=F32).astype(BF16)
    u_ref[...] = jnp.dot(hb, w_ref[:, OFF_S5:OFF_S5 + S5_W], preferred_element_type=F32)


def _inproj_call(xs, modsel, gain, w_all):
    b, lt, _ = xs.shape
    nt = lt // ROW_T
    return pl.pallas_call(
        _inproj_kernel,
        grid=(b, nt),
        in_specs=[pl.BlockSpec((None, ROW_T, D), lambda i, j: (i, j, 0)),
                  pl.BlockSpec((None, None, SUB, D), lambda i, j: (i, jnp.minimum(j, 1), 0, 0)),
                  _const_spec((1, D)),
                  _const_spec((D, NW))],
        out_specs=[pl.BlockSpec((None, ROW_T, NP), lambda i, j: (i, j, 0)),
                   pl.BlockSpec((None, ROW_T, S5_W), lambda i, j: (i, j, 0))],
        out_shape=[jax.ShapeDtypeStruct((b, lt, NP), BF16),
                   jax.ShapeDtypeStruct((b, lt, S5_W), F32)],
        compiler_params=_cparams(2),
    )(xs, modsel, gain.reshape(1, D), w_all)


def _s5_kernel(u_ref, bb_ref, are_ref, aim_ref, cb_ref, dsk_ref, gluw_ref, glub_ref,
               y_ref, xh_ref, h_ref, yf_ref, *, tc, n_ctx, n_all):
    d = pl.program_id(1)
    c = pl.program_id(2)
    rows = tc * SUB
    row0 = pl.multiple_of(_dir_chunk(d, c, n_ctx, n_all) * rows, rows)

    @pl.when(c == 0)
    def _():
        h_ref[...] = jnp.zeros_like(h_ref)

    u = u_ref[...].reshape(rows, S5_W)
    xh_ref[...] = jnp.dot(u.astype(BF16), bb_ref[...], preferred_element_type=F32)
    half = S5_NS // 2
    for lo in (0, half):
        re = slice(lo, lo + half)
        im = slice(S5_NS + lo, S5_NS + lo + half)
        ar = are_ref[:, re]
        ai = aim_ref[:, re]

        def body(t, carry, re=re, im=im, ar=ar, ai=ai):
            hr, hi = carry
            tt = t + d * (tc - 1 - 2 * t)
            r0 = pl.multiple_of(tt * SUB, SUB)
            nr = ar * hr - ai * hi + xh_ref[pl.ds(r0, SUB), re]
            ni = ar * hi + ai * hr + xh_ref[pl.ds(r0, SUB), im]
            xh_ref[pl.ds(r0, SUB), re] = nr
            xh_ref[pl.ds(r0, SUB), im] = ni
            return nr, ni

        hr, hi = lax.fori_loop(0, tc, body, (h_ref[:, re], h_ref[:, im]), unroll=4)
        h_ref[:, re] = hr
        h_ref[:, im] = hi
    y = jnp.dot(xh_ref[...].astype(BF16), cb_ref[...], preferred_element_type=F32)

    @pl.when(d == 0)
    def _():
        yf_ref[pl.ds(row0, rows), :] = y

    @pl.when(d == 1)
    def _():
        yy = jax.nn.gelu(y + yf_ref[pl.ds(row0, rows), :] + dsk_ref[...] * u)
        glu = jnp.dot(yy.astype(BF16), gluw_ref[...], preferred_element_type=F32) + glub_ref[...]
        y_ref[...] = (yy * jax.nn.sigmoid(glu)).reshape(tc, SUB, S5_W)


def _s5_call(u, bblk, a_re, a_im, cblk, dsk, gluw, glub, ctx_len):
    lt, b, _ = u.shape
    tc = S5_C
    n_all, n_ctx = lt // tc, ctx_len // tc
    cidx = functools.partial(_dir_chunk, n_ctx=n_ctx, n_all=n_all)
    return pl.pallas_call(
        functools.partial(_s5_kernel, tc=tc, n_ctx=n_ctx, n_all=n_all),
        grid=(b // SUB, 2, n_all),
        in_specs=[pl.BlockSpec((tc, SUB, S5_W), lambda g, d, c: (cidx(d, c), g, 0)),
                  pl.BlockSpec((None, S5_W, 2 * S5_NS), lambda g, d, c: (d, 0, 0)),
                  pl.BlockSpec((None, SUB, S5_NS), lambda g, d, c: (d, 0, 0)),
                  pl.BlockSpec((None, SUB, S5_NS), lambda g, d, c: (d, 0, 0)),
                  _const_spec((2 * S5_NS, S5_W)),
                  _const_spec((1, S5_W)), _const_spec((S5_W, S5_W)), _const_spec((1, S5_W))],
        out_specs=pl.BlockSpec((tc, SUB, S5_W), lambda g, d, c: (cidx(1, c * d), g, 0)),
        out_shape=jax.ShapeDtypeStruct((lt, b, S5_W), F32),
        scratch_shapes=[pltpu.VMEM((tc * SUB, 2 * S5_NS), F32),
                        pltpu.VMEM((SUB, 2 * S5_NS), F32),
                        pltpu.VMEM((lt * SUB, S5_W), F32)],
        compiler_params=_cparams(3),
    )(u, bblk, a_re, a_im, cblk, dsk, gluw, glub)


_NT = (((1,), (1,)), ((), ()))
_TN = (((0,), (0,)), ((), ()))


def _head_sums(x, hsum_ref):
    return jnp.dot(x.astype(BF16), hsum_ref[...], preferred_element_type=F32)


def _ret_kernel(p_ref, g_ref, cos_ref, sin_ref, dm_ref, qd_ref, kd_ref, cd_ref, hm_ref, bm_ref,
                hsum_ref, real_ref, gn_ref, y_ref, s_ref, oc_ref, of_ref, *, n_ctx, n_all):
    d = pl.program_id(1)
    c = pl.program_id(2)
    row0 = pl.multiple_of(_dir_chunk(d, c, n_ctx, n_all) * RET_C, RET_C)

    @pl.when(c == 0)
    def _():
        s_ref[...] = jnp.zeros_like(s_ref)

    cs = cos_ref[...]
    sn = sin_ref[...]

    def rope(off):
        a = p_ref[:, off:off + LANE].astype(F32)
        b = p_ref[:, off + LANE:off + 2 * LANE].astype(F32)
        return jnp.concatenate([a * cs - b * sn, a * sn + b * cs], axis=1)

    q = rope(0)
    k = rope(QK_P) * (DK ** -0.5)
    kb = k.astype(BF16)
    vb = p_ref[:, 2 * QK_P:2 * QK_P + V_P]
    st = s_ref[...]
    inter = lax.dot_general((q * qd_ref[...]).astype(BF16), st.astype(BF16), _NT,
                            preferred_element_type=F32)
    for h in range(HEADS):
        qh = (q * hm_ref[h:h + 1, :]).astype(BF16)
        att = lax.dot_general(qh, kb, _NT, preferred_element_type=F32) * dm_ref[h]
        sl = slice(h * LANE, (h + 1) * LANE)
        oc_ref[:, sl] = jnp.dot(att.astype(BF16), vb[:, sl], preferred_element_type=F32) + inter[:, sl]
    kdv = lax.dot_general(vb, (k * kd_ref[...]).astype(BF16), _TN, preferred_element_type=F32)
    s_ref[...] = st * cd_ref[...] + bm_ref[...] * kdv

    @pl.when(d == 0)
    def _():
        of_ref[pl.ds(row0, RET_C), :] = oc_ref[...]

    @pl.when(d == 1)
    def _():
        o = oc_ref[...] + of_ref[pl.ds(row0, RET_C), :]
        mu = _head_sums(o, hsum_ref) * (1.0 / DV)
        dlt = (o - mu) * real_ref[...]
        var = _head_sums(dlt * dlt, hsum_ref) * (1.0 / DV)
        y = dlt * lax.rsqrt(var + EPS) * gn_ref[...] * _silu(g_ref[...].astype(F32))
        y_ref[...] = y.astype(BF16)


def _mixer_specs(b, lt, ctx_len, tc, off_qkv, off_g):
    n_all, n_ctx = lt // tc, ctx_len // tc
    cidx = functools.partial(_dir_chunk, n_ctx=n_ctx, n_all=n_all)
    qkv = pl.BlockSpec((None, tc, QKV_P), lambda i, d, c: (i, cidx(d, c), off_qkv // QKV_P))
    gsp = pl.BlockSpec((None, tc, V_P), lambda i, d, c: (i, cidx(1, c * d), off_g // V_P))
    out = pl.BlockSpec((None, tc, V_P), lambda i, d, c: (i, cidx(1, c * d), 0))
    return n_ctx, n_all, cidx, qkv, gsp, out


def _ret_call(p, cos_t, sin_t, dmask, qdec, kdec, cdec, gn, ctx_len):
    b, lt, _ = p.shape
    tc = RET_C
    n_ctx, n_all, cidx, qkv, gsp, out = _mixer_specs(b, lt, ctx_len, tc, OFF_RQKV, OFF_RG)
    return pl.pallas_call(
        functools.partial(_ret_kernel, n_ctx=n_ctx, n_all=n_all),
        grid=(b, 2, n_all),
        in_specs=[qkv, gsp,
                  pl.BlockSpec((tc, LANE), lambda i, d, c: (cidx(d, c), 0)),
                  pl.BlockSpec((tc, LANE), lambda i, d, c: (cidx(d, c), 0)),
                  pl.BlockSpec((None, HEADS, tc, tc), lambda i, d, c: (d, 0, 0, 0)),
                  pl.BlockSpec((None, tc, QK_P), lambda i, d, c: (d, 0, 0)),
                  pl.BlockSpec((None, tc, QK_P), lambda i, d, c: (d, 0, 0)),
                  pl.BlockSpec((None, 1, QK_P), lambda i, d, c: (d, 0, 0)),
                  _const_spec((SUB, QK_P)), _const_spec((V_P, QK_P)),
                  _const_spec((V_P, V_P)), _const_spec((1, V_P)), _const_spec((1, V_P))],
        out_specs=out,
        out_shape=jax.ShapeDtypeStruct((b, lt, V_P), BF16),
        scratch_shapes=[pltpu.VMEM((V_P, QK_P), F32),
                        pltpu.VMEM((tc, V_P), F32),
                        pltpu.VMEM((lt, V_P), F32)],
        compiler_params=_cparams(3),
    )(p, p, cos_t, sin_t, dmask, qdec, kdec, cdec,
      jnp.asarray(TAB["head_mask"]), jnp.asarray(TAB["bm_t"]),
      jnp.asarray(TAB["hsum"], dtype=BF16), jnp.asarray(TAB["v_real"]).reshape(1, V_P), gn)


def _gla_kernel(p_ref, g_ref, z_ref, gw_ref, gb_ref, tri_ref, hm_ref, bm_ref, hsum_ref, gn_ref,
                y_ref, s_ref, oc_ref, of_ref, *, n_ctx, n_all):
    d = pl.program_id(1)
    c = pl.program_id(2)
    row0 = pl.multiple_of(_dir_chunk(d, c, n_ctx, n_all) * GLA_R, GLA_R)

    @pl.when(c == 0)
    def _():
        s_ref[...] = jnp.zeros_like(s_ref)

    trif = tri_ref[...]
    tri = trif.astype(BF16)
    mask = trif > 0.5
    la = jax.nn.log_sigmoid(
        jnp.dot(z_ref[...], gw_ref[...], preferred_element_type=F32) + gb_ref[...]) / GLA_TAU
    hi = la.astype(BF16)
    lo = (la - hi.astype(F32)).astype(BF16)
    bcum = jnp.dot(tri, hi, preferred_element_type=F32) + jnp.dot(tri, lo, preferred_element_type=F32)
    q = p_ref[:, 0:QK_P].astype(F32) * (DK ** -0.5)
    k = p_ref[:, QK_P:2 * QK_P].astype(F32)
    vb = p_ref[:, 2 * QK_P:2 * QK_P + V_P]
    qt = q * jnp.exp(bcum)
    qtb = qt.astype(BF16)
    ktb = (k * jnp.exp(-bcum)).astype(BF16)
    for h in range(HEADS):
        qh = (qt * hm_ref[h:h + 1, :]).astype(BF16)
        att = jnp.where(mask, lax.dot_general(qh, ktb, _NT, preferred_element_type=F32), 0.0)
        sl = slice(h * LANE, (h + 1) * LANE)
        oc_ref[:, sl] = jnp.dot(att.astype(BF16), vb[:, sl], preferred_element_type=F32)

    def chain(order, last_row):
        s = s_ref[...]
        for cc in order:
            r0 = cc * GLA_C
            rows = slice(r0, r0 + GLA_C)
            bl = bcum[r0 + last_row:r0 + last_row + 1, :]
            oc_ref[rows, :] += lax.dot_general(qtb[rows], s.astype(BF16), _NT, preferred_element_type=F32)
            kd = (k[rows] * jnp.exp(bl - bcum[rows])).astype(BF16)
            kdv = lax.dot_general(vb[rows], kd, _TN, preferred_element_type=F32)
            s = s * jnp.exp(bl) + bm_ref[...] * kdv
        s_ref[...] = s

    n_sub = GLA_R // GLA_C

    @pl.when(d == 0)
    def _():
        chain(range(n_sub), GLA_C - 1)
        of_ref[pl.ds(row0, GLA_R), :] = oc_ref[...]

    @pl.when(d == 1)
    def _():
        chain(range(n_sub - 1, -1, -1), 0)
        o = oc_ref[...] + of_ref[pl.ds(row0, GLA_R), :]
        ms = _head_sums(o * o, hsum_ref) * (1.0 / DV)
        y = o * lax.rsqrt(ms + EPS) * gn_ref[...] * _silu(g_ref[...].astype(F32))
        y_ref[...] = y.astype(BF16)


def _gla_call(p, gw, gb, tri, gn, ctx_len):
    b, lt, _ = p.shape
    tc = GLA_R
    n_ctx, n_all, cidx, qkv, gsp, out = _mixer_specs(b, lt, ctx_len, tc, OFF_GQKV, OFF_GG)
    return pl.pallas_call(
        functools.partial(_gla_kernel, n_ctx=n_ctx, n_all=n_all),
        grid=(b, 2, n_all),
        in_specs=[qkv, gsp,
                  pl.BlockSpec((None, tc, LANE), lambda i, d, c: (i, cidx(d, c), OFF_Z // LANE)),
                  pl.BlockSpec((None, LANE, QK_P), lambda i, d, c: (d, 0, 0)),
                  pl.BlockSpec((None, 1, QK_P), lambda i, d, c: (d, 0, 0)),
                  pl.BlockSpec((None, tc, tc), lambda i, d, c: (d, 0, 0)),
                  _const_spec((SUB, QK_P)), _const_spec((V_P, QK_P)),
                  _const_spec((V_P, V_P)), _const_spec((1, V_P))],
        out_specs=out,
        out_shape=jax.ShapeDtypeStruct((b, lt, V_P), BF16),
        scratch_shapes=[pltpu.VMEM((V_P, QK_P), F32),
                        pltpu.VMEM((tc, V_P), F32),
                        pltpu.VMEM((lt, V_P), F32)],
        compiler_params=_cparams(3),
    )(p, p, p, gw, gb, tri, jnp.asarray(TAB["head_mask"]), jnp.asarray(TAB["bm_t"]),
      jnp.asarray(TAB["hsum"], dtype=BF16), gn)


def _merge_kernel(x_ref, mod_ref, ys_ref, yr_ref, yg_ref, gate_ref,
                  wbs_ref, wbr_ref, wbg_ref, wo_ref, o_ref):
    def branch(y, w_ref, g0):
        gate = jax.nn.sigmoid(gate_ref[:, g0:g0 + D].astype(F32))
        return gate * jnp.dot(y, w_ref[...], preferred_element_type=F32)

    m = (branch(ys_ref[...].astype(BF16), wbs_ref, 0) + branch(yr_ref[...], wbr_ref, D)
         + branch(yg_ref[...], wbg_ref, 2 * D))
    out = jnp.dot(m.astype(BF16), wo_ref[...], preferred_element_type=F32)
    o_ref[...] = x_ref[...] + mod_ref[2:3, :] * out


def _merge_call(xs, modsel, ys5, yret, ygla, p, wbs, wbr, wbg, wo, lat_only, ctx_len):
    b, lt, _ = xs.shape
    j0 = ctx_len // ROW_T if lat_only else 0
    nt = lt // ROW_T - j0
    tok = lambda w: pl.BlockSpec((None, ROW_T, w), lambda i, j: (i, j + j0, 0))
    return pl.pallas_call(
        _merge_kernel,
        grid=(b, nt),
        in_specs=[tok(D),
                  pl.BlockSpec((None, None, SUB, D), lambda i, j: (i, jnp.minimum(j + j0, 1), 0, 0)),
                  tok(S5_W), tok(V_P), tok(V_P), tok(GATE_W),
                  _const_spec((S5_W, D)), _const_spec((V_P, D)), _const_spec((V_P, D)),
                  _const_spec((D, D))],
        out_specs=pl.BlockSpec((None, ROW_T, D), lambda i, j: (i, j, 0)),
        out_shape=jax.ShapeDtypeStruct((b, nt * ROW_T, D), F32),
        compiler_params=_cparams(2),
    )(xs, modsel, ys5, yret, ygla, p, wbs, wbr, wbg, wo)


FFN_T = 256


def _ffn_kernel(x_ref, mod_ref, gain_ref, wa_ref, wb_ref, wo_ref, fin_ref, o_ref, *, final):
    x = x_ref[...]
    h = _rms(x, gain_ref[...])
    hb = (h * (1.0 + mod_ref[4:5, :]) + mod_ref[3:4, :]).astype(BF16)
    acc = jnp.zeros_like(x)
    for t in range(FFN_H // FFN_T):
        a = jnp.dot(hb, wa_ref[t], preferred_element_type=F32)
        bq = jnp.dot(hb, wb_ref[t], preferred_element_type=F32)
        act = (_silu(a) * bq).astype(BF16)
        acc = acc + jnp.dot(act, wo_ref[t], preferred_element_type=F32)
    y = x + mod_ref[5:6, :] * acc
    if final:
        y = _rms(y, fin_ref[...])
    o_ref[...] = y


def _ffn_call(xs, modsel, gain, wa, wb, wo, fin, final, lat_only):
    b, rows, _ = xs.shape
    nt = rows // ROW_T
    msel = (lambda j: 1) if lat_only else (lambda j: jnp.minimum(j, 1))
    nh = FFN_H // FFN_T
    return pl.pallas_call(
        functools.partial(_ffn_kernel, final=final),
        grid=(b, nt),
        in_specs=[pl.BlockSpec((None, ROW_T, D), lambda i, j: (i, j, 0)),
                  pl.BlockSpec((None, None, SUB, D), lambda i, j: (i, msel(j), 0, 0)),
                  _const_spec((1, D)),
                  _const_spec((nh, D, FFN_T)), _const_spec((nh, D, FFN_T)),
                  _const_spec((nh, FFN_T, D)), _const_spec((1, D))],
        out_specs=pl.BlockSpec((None, ROW_T, D), lambda i, j: (i, j, 0)),
        out_shape=jax.ShapeDtypeStruct((b, rows, D), F32),
        compiler_params=_cparams(2),
    )(xs, modsel, gain.reshape(1, D), wa, wb, wo, fin.reshape(1, D))


def _s5_params(lam_re, lam_im, log_dt, b_re, b_im, c_re, c_im):
    dt = jnp.exp(log_dt)[..., None]
    mag = jnp.exp(lam_re * dt)
    a_re, a_im = mag * jnp.cos(lam_im * dt), mag * jnp.sin(lam_im * dt)
    den = lam_re * lam_re + lam_im * lam_im
    f_re = ((a_re - 1.0) * lam_re + a_im * lam_im) / den
    f_im = (a_im * lam_re - (a_re - 1.0) * lam_im) / den
    bb_re = f_re[..., None] * b_re - f_im[..., None] * b_im
    bb_im = f_re[..., None] * b_im + f_im[..., None] * b_re
    eye = jnp.eye(S5_GROUPS, dtype=F32)

    def blk_in(bb):
        t = jnp.einsum("dgpc,gh->dgchp", bb, eye)
        return t.reshape(2, S5_W, S5_NS)

    bblk = jnp.concatenate([blk_in(bb_re), blk_in(bb_im)], axis=-1).astype(BF16)

    def blk_out(cc):
        t = jnp.einsum("gcp,gh->gphc", cc, eye)
        return t.reshape(S5_NS, S5_W)

    cblk = jnp.concatenate([blk_out(c_re), -blk_out(c_im)], axis=0).astype(BF16)
    bc = lambda a: jnp.broadcast_to(a.reshape(2, 1, S5_NS), (2, SUB, S5_NS))
    return bblk, bc(a_re), bc(a_im), cblk


def _ret_tables(log_decay):
    tc = RET_C
    pos = jnp.arange(tc, dtype=F32)
    w = jnp.stack([pos, tc - 1.0 - pos])
    rel = w[:, :, None] - w[:, None, :]
    lg = log_decay[:, :, None, None]
    dmask = jnp.where(rel[:, None] >= 0, jnp.exp(jnp.maximum(rel[:, None], 0.0) * lg), 0.0)
    qk_head = TAB["qk_head"]
    lane_lg = jnp.where(jnp.asarray(qk_head >= 0),
                        jnp.take(log_decay, jnp.asarray(np.maximum(qk_head, 0)), axis=1), 0.0)
    qdec = jnp.exp((w[:, :, None] + 1.0) * lane_lg[:, None, :])
    kdec = jnp.exp((tc - 1.0 - w[:, :, None]) * lane_lg[:, None, :])
    cdec = jnp.exp(tc * lane_lg)[:, None, :]
    return dmask, qdec, kdec, cdec


def _rope_tables(seq, ctx_len):
    rows = seq // GRID_W
    nf = DK // 4
    inv = 1.0 / (ROPE_BASE ** (np.arange(nf, dtype=np.float32) / nf))
    r = np.repeat(np.arange(rows, dtype=np.float32), GRID_W)
    col = np.tile(np.arange(GRID_W, dtype=np.float32), rows)
    ang = np.concatenate([r[:, None] * inv, col[:, None] * inv], axis=-1)
    cos_t = np.ones((ctx_len + seq, LANE), np.float32)
    sin_t = np.zeros((ctx_len + seq, LANE), np.float32)
    for h in range(HEADS):
        cos_t[ctx_len:, h * 24:(h + 1) * 24] = np.cos(ang)
        sin_t[ctx_len:, h * 24:(h + 1) * 24] = np.sin(ang)
    return jnp.asarray(cos_t), jnp.asarray(sin_t)


def _gla_tables(gate_w, gate_b):
    gw = _gather_cols(gate_w, TAB["qk_src"])
    gw_full = jnp.zeros((2, LANE, QK_P), F32)
    gw_full = gw_full.at[0, 0:GLA_RANK].set(gw[0]).at[1, GLA_RANK:2 * GLA_RANK].set(gw[1])
    gb = _gather_cols(gate_b, TAB["qk_src"])[:, None, :]
    i = np.arange(GLA_R)
    same = (i[:, None] // GLA_C) == (i[None, :] // GLA_C)
    tri = np.stack([same & (i[:, None] >= i[None, :]), same & (i[:, None] <= i[None, :])]).astype(np.float32)
    return gw_full.astype(BF16), gb, jnp.asarray(tri)


def _pad_rows(w, src):
    return _gather_cols(w.T, src).T


def kernel(x, c, ctx, c_ctx, w_mod, b_mod, norm_mix, norm_ffn, w_in, s5_lam_re, s5_lam_im, s5_log_dt, s5_b_re, s5_b_im, s5_c_re, s5_c_im, s5_d, s5_glu_w, s5_glu_b, ret_log_decay, ret_gn, gla_gate_w, gla_gate_b, gla_norm, w_br_s5, w_br_ret, w_br_gla, w_out, w_ffn_in, w_ffn_out, norm_final):
    b, seq, _ = x.shape
    ctx_len = ctx.shape[1]
    depth = w_mod.shape[0]
    assert b % SUB == 0 and ctx_len % RET_C == 0 and seq % RET_C == 0 and seq % GRID_W == 0

    xs = jnp.concatenate([ctx, x], axis=1)
    rows = ((b + 1 + SUB - 1) // SUB) * SUB
    c_all = jnp.zeros((rows, D), F32).at[:b].set(c).at[b].set(c_ctx)
    cos_t, sin_t = _rope_tables(seq, ctx_len)
    nh = FFN_H // FFN_T

    for l in range(depth):
        last = l == depth - 1
        mods = _mod_call(c_all, w_mod[l], b_mod[l])
        m_lat = mods[:b].reshape(b, N_MOD, D)
        m_ctx = jnp.broadcast_to(mods[b].reshape(1, N_MOD, D), (b, N_MOD, D))
        modsel = jnp.stack([m_ctx, m_lat], axis=1)
        modsel = jnp.pad(modsel, ((0, 0), (0, 0), (0, SUB - N_MOD), (0, 0)))

        w_all = _gather_cols(w_in[l], TAB["src"]).astype(BF16)
        p, u = _inproj_call(xs, modsel, norm_mix[l], w_all)

        bblk, a_re, a_im, cblk = _s5_params(s5_lam_re[l], s5_lam_im[l], s5_log_dt[l],
                                            s5_b_re[l], s5_b_im[l], s5_c_re[l], s5_c_im[l])
        ys5 = _s5_call(jnp.transpose(u, (1, 0, 2)), bblk, a_re, a_im, cblk,
                       s5_d[l].reshape(1, S5_W), s5_glu_w[l].astype(BF16),
                       s5_glu_b[l].reshape(1, S5_W), ctx_len)
        ys5 = jnp.transpose(ys5, (1, 0, 2))

        dmask, qdec, kdec, cdec = _ret_tables(ret_log_decay[l])
        yret = _ret_call(p, cos_t, sin_t, dmask, qdec, kdec, cdec,
                         _gather_cols(ret_gn[l], TAB["vsrc"]).reshape(1, V_P), ctx_len)

        gw, gb, tri = _gla_tables(gla_gate_w[l], gla_gate_b[l])
        ygla = _gla_call(p, gw, gb, tri, _gather_cols(gla_norm[l], TAB["vsrc"]).reshape(1, V_P),
                         ctx_len)

        xs = _merge_call(xs, modsel, ys5, yret, ygla, p, w_br_s5[l].astype(BF16),
                         _pad_rows(w_br_ret[l], TAB["vsrc"]).astype(BF16),
                         _pad_rows(w_br_gla[l], TAB["vsrc"]).astype(BF16),
                         w_out[l].astype(BF16), last, ctx_len)

        wfi = w_ffn_in[l].astype(BF16)
        wa = wfi[:, :FFN_H].reshape(D, nh, FFN_T).transpose(1, 0, 2)
        wb = wfi[:, FFN_H:].reshape(D, nh, FFN_T).transpose(1, 0, 2)
        wo = w_ffn_out[l].astype(BF16).reshape(nh, FFN_T, D)
        xs = _ffn_call(xs, modsel, norm_ffn[l], wa, wb, wo, norm_final, last, last)
    return xs
```

```python
import functools

import numpy as np
import jax
import jax.numpy as jnp
from jax import lax
from jax.experimental import pallas as pl
from jax.experimental.pallas import tpu as pltpu

F32 = jnp.float32
BF16 = jnp.bfloat16

D = 1024
EPS = 1e-6
N_MOD = 6
GRID_W = 64
S5_W = 256
S5_GROUPS = 16
S5_STATE = 64
S5_NS = S5_GROUPS * S5_STATE
HEADS = 4
DK = 48
DV = 96
QK_W = HEADS * DK
V_W = HEADS * DV
GLA_RANK = 16
GLA_TAU = 16.0
ROPE_BASE = 10000.0
FFN_H = 2816

LANE = 128
SUB = 8
QK_P = 2 * LANE
V_P = HEADS * LANE
QKV_P = 2 * QK_P + V_P
GATE_W = 3 * D

OFF_GATE = 0
OFF_RQKV = GATE_W
OFF_GQKV = OFF_RQKV + QKV_P
OFF_RG = OFF_GQKV + QKV_P
OFF_GG = OFF_RG + V_P
OFF_Z = OFF_GG + V_P
NP = OFF_Z + LANE
OFF_S5 = NP
NW = NP + S5_W

ROW_T = 256
RET_C = 256
GLA_C = 64
GLA_R = 256
S5_C = 128
VMEM_LIMIT = 56 * 1024 * 1024

IN_SIZES = (S5_W, QK_W, QK_W, V_W, V_W, QK_W, QK_W, V_W, V_W, GLA_RANK, GLA_RANK, D, D, D)
IN_OFFS = np.concatenate([[0], np.cumsum(IN_SIZES)]).astype(np.int64)


def _head_lane():
    m = np.zeros((HEADS, DK), np.int64)
    for h in range(HEADS):
        for i in range(DK):
            m[h, i] = (i % 2) * LANE + h * (DK // 2) + i // 2
    return m


HEAD_LANE = _head_lane()


def _static_tables():
    src = np.full((NW,), -1, np.int64)
    src[OFF_GATE:OFF_GATE + GATE_W] = IN_OFFS[11] + np.arange(GATE_W)
    for qkv, gg, iq, ik, iv, ig in ((OFF_RQKV, OFF_RG, 1, 2, 3, 4), (OFF_GQKV, OFF_GG, 5, 6, 7, 8)):
        for h in range(HEADS):
            for i in range(DK):
                src[qkv + HEAD_LANE[h, i]] = IN_OFFS[iq] + h * DK + i
                src[qkv + QK_P + HEAD_LANE[h, i]] = IN_OFFS[ik] + h * DK + i
            for j in range(DV):
                src[qkv + 2 * QK_P + h * LANE + j] = IN_OFFS[iv] + h * DV + j
                src[gg + h * LANE + j] = IN_OFFS[ig] + h * DV + j
    src[OFF_Z:OFF_Z + GLA_RANK] = IN_OFFS[9] + np.arange(GLA_RANK)
    src[OFF_Z + GLA_RANK:OFF_Z + 2 * GLA_RANK] = IN_OFFS[10] + np.arange(GLA_RANK)
    src[OFF_S5:OFF_S5 + S5_W] = np.arange(S5_W)
    vsrc = np.full((V_P,), -1, np.int64)
    for h in range(HEADS):
        vsrc[h * LANE:h * LANE + DV] = h * DV + np.arange(DV)
    qk_head = np.full((QK_P,), -1, np.int64)
    qk_src = np.full((QK_P,), -1, np.int64)
    for h in range(HEADS):
        for i in range(DK):
            qk_head[HEAD_LANE[h, i]] = h
            qk_src[HEAD_LANE[h, i]] = h * DK + i
    v_head = np.repeat(np.arange(HEADS), LANE)
    v_real = (np.arange(V_P) % LANE) < DV
    head_mask = np.zeros((SUB, QK_P), np.float32)
    for h in range(HEADS):
        head_mask[h] = (qk_head == h)
    bm_t = (v_head[:, None] == qk_head[None, :]).astype(np.float32)
    return dict(src=src, vsrc=vsrc, qk_head=qk_head, qk_src=qk_src, head_mask=head_mask,
                bm_t=bm_t, v_real=v_real.astype(np.float32))


TAB = _static_tables()


def _gather_cols(w, src):
    valid = jnp.asarray(src >= 0)
    out = jnp.take(w, jnp.asarray(np.maximum(src, 0)), axis=-1)
    return jnp.where(valid, out, jnp.zeros((), w.dtype))


def _cparams(n_axes):
    return pltpu.CompilerParams(dimension_semantics=("arbitrary",) * n_axes,
                                vmem_limit_bytes=VMEM_LIMIT)


def _const_spec(shape):
    nd = len(shape)
    return pl.BlockSpec(shape, lambda *_: (0,) * nd, pipeline_mode=pl.Buffered(1))


def _dir_chunk(d, c, n_ctx, n_all):
    bwd = jnp.where(c < n_ctx, n_ctx - 1 - c, n_all + n_ctx - 1 - c)
    return jnp.where(d == 0, c, bwd)


def _silu(x):
    return x * jax.nn.sigmoid(x)


def _mod_kernel(c_ref, w_ref, b_ref, o_ref):
    o_ref[...] = jnp.dot(_silu(c_ref[...]), w_ref[...], preferred_element_type=F32,
                         precision=lax.Precision.HIGHEST) + b_ref[...]


def _mod_call(c_all, w_mod_l, b_mod_l):
    rows = c_all.shape[0]
    tn = D
    return pl.pallas_call(
        _mod_kernel,
        grid=(N_MOD * D // tn,),
        in_specs=[pl.BlockSpec((rows, D), lambda j: (0, 0)),
                  pl.BlockSpec((D, tn), lambda j: (0, j)),
                  pl.BlockSpec((1, tn), lambda j: (0, j))],
        out_specs=pl.BlockSpec((rows, tn), lambda j: (0, j)),
        out_shape=jax.ShapeDtypeStruct((rows, N_MOD * D), F32),
        compiler_params=_cparams(1),
    )(c_all, w_mod_l, b_mod_l.reshape(1, -1))


def _rms(x, gain):
    return x * lax.rsqrt(jnp.mean(x * x, axis=-1, keepdims=True) + EPS) * gain


def _inproj_kernel(x_ref, mod_ref, gain_ref, w_ref, p_ref, u_ref):
    h = _rms(x_ref[...], gain_ref[...])
    h = h * (1.0 + mod_ref[1:2, :]) + mod_ref[0:1, :]
    hb = h.astype(BF16)
    ch = 896
    for c0 in range(0, NP, ch):
        p_ref[:, c0:c0 + ch] = jnp.dot(hb, w_ref[:, c0:c0 + ch],
                                       preferred_element_type=F32).astype(BF16)
    u_ref[...] = jnp.dot(hb, w_ref[:, OFF_S5:OFF_S5 + S5_W], preferred_element_type=F32)


def _inproj_call(xs, modsel, gain, w_all):
    b, lt, _ = xs.shape
    nt = lt // ROW_T
    return pl.pallas_call(
        _inproj_kernel,
        grid=(b, nt),
        in_specs=[pl.BlockSpec((None, ROW_T, D), lambda i, j: (i, j, 0)),
                  pl.BlockSpec((None, None, SUB, D), lambda i, j: (i, jnp.minimum(j, 1), 0, 0)),
                  _const_spec((1, D)),
                  _const_spec((D, NW))],
        out_specs=[pl.BlockSpec((None, ROW_T, NP), lambda i, j: (i, j, 0)),
                   pl.BlockSpec((None, ROW_T, S5_W), lambda i, j: (i, j, 0))],
        out_shape=[jax.ShapeDtypeStruct((b, lt, NP), BF16),
                   jax.ShapeDtypeStruct((b, lt, S5_W), F32)],
        compiler_params=_cparams(2),
    )(xs, modsel, gain.reshape(1, D), w_all)


def _s5_kernel(u_ref, bb_ref, are_ref, aim_ref, cb_ref, dsk_ref, gluw_ref, glub_ref,
               y_ref, xh_ref, h_ref, yf_ref, *, tc, n_ctx, n_all):
    d = pl.program_id(1)
    c = pl.program_id(2)
    rows = tc * SUB
    row0 = pl.multiple_of(_dir_chunk(d, c, n_ctx, n_all) * rows, rows)

    @pl.when(c == 0)
    def _():
        h_ref[...] = jnp.zeros_like(h_ref)

    u = u_ref[...].reshape(rows, S5_W)
    xh_ref[...] = jnp.dot(u.astype(BF16), bb_ref[...], preferred_element_type=F32)
    half = S5_NS // 2
    for lo in (0, half):
        re = slice(lo, lo + half)
        im = slice(S5_NS + lo, S5_NS + lo + half)
        ar = are_ref[:, re]
        ai = aim_ref[:, re]

        def body(t, carry, re=re, im=im, ar=ar, ai=ai):
            hr, hi = carry
            tt = t + d * (tc - 1 - 2 * t)
            r0 = pl.multiple_of(tt * SUB, SUB)
            nr = ar * hr - ai * hi + xh_ref[pl.ds(r0, SUB), re]
            ni = ar * hi + ai * hr + xh_ref[pl.ds(r0, SUB), im]
            xh_ref[pl.ds(r0, SUB), re] = nr
            xh_ref[pl.ds(r0, SUB), im] = ni
            return nr, ni

        hr, hi = lax.fori_loop(0, tc, body, (h_ref[:, re], h_ref[:, im]), unroll=4)
        h_ref[:, re] = hr
        h_ref[:, im] = hi
    y = jnp.dot(xh_ref[...].astype(BF16), cb_ref[...], preferred_element_type=F32)

    @pl.when(d == 0)
    def _():
        yf_ref[pl.ds(row0, rows), :] = y

    @pl.when(d == 1)
    def _():
        yy = jax.nn.gelu(y + yf_ref[pl.ds(row0, rows), :] + dsk_ref[...] * u)
        glu = jnp.dot(yy.astype(BF16), gluw_ref[...], preferred_element_type=F32) + glub_ref[...]
        y_ref[...] = (yy * jax.nn.sigmoid(glu)).reshape(tc, SUB, S5_W)


def _s5_call(u, bblk, a_re, a_im, cblk, dsk, gluw, glub, ctx_len):
    lt, b, _ = u.shape
    tc = S5_C
    n_all, n_ctx = lt // tc, ctx_len // tc
    cidx = functools.partial(_dir_chunk, n_ctx=n_ctx, n_all=n_all)
    return pl.pallas_call(
        functools.partial(_s5_kernel, tc=tc, n_ctx=n_ctx, n_all=n_all),
        grid=(b // SUB, 2, n_all),
        in_specs=[pl.BlockSpec((tc, SUB, S5_W), lambda g, d, c: (cidx(d, c), g, 0)),
                  pl.BlockSpec((None, S5_W, 2 * S5_NS), lambda g, d, c: (d, 0, 0)),
                  pl.BlockSpec((None, SUB, S5_NS), lambda g, d, c: (d, 0, 0)),
                  pl.BlockSpec((None, SUB, S5_NS), lambda g, d, c: (d, 0, 0)),
                  _const_spec((2 * S5_NS, S5_W)),
                  _const_spec((1, S5_W)), _const_spec((S5_W, S5_W)), _const_spec((1, S5_W))],
        out_specs=pl.BlockSpec((tc, SUB, S5_W), lambda g, d, c: (cidx(1, c * d), g, 0)),
        out_shape=jax.ShapeDtypeStruct((lt, b, S5_W), F32),
        scratch_shapes=[pltpu.VMEM((tc * SUB, 2 * S5_NS), F32),
                        pltpu.VMEM((SUB, 2 * S5_NS), F32),
                        pltpu.VMEM((lt * SUB, S5_W), F32)],
        compiler_params=_cparams(3),
    )(u, bblk, a_re, a_im, cblk, dsk, gluw, glub)


_NT = (((1,), (1,)), ((), ()))
_TN = (((0,), (0,)), ((), ()))


def _ret_kernel(p_ref, g_ref, cos_ref, sin_ref, dm_ref, qd_ref, kd_ref, cd_ref, hm_ref, bm_ref,
                real_ref, gn_ref, y_ref, s_ref, oc_ref, of_ref, *, n_ctx, n_all):
    d = pl.program_id(1)
    c = pl.program_id(2)
    row0 = pl.multiple_of(_dir_chunk(d, c, n_ctx, n_all) * RET_C, RET_C)

    @pl.when(c == 0)
    def _():
        s_ref[...] = jnp.zeros_like(s_ref)

    cs = cos_ref[...]
    sn = sin_ref[...]

    def rope(off):
        a = p_ref[:, off:off + LANE].astype(F32)
        b = p_ref[:, off + LANE:off + 2 * LANE].astype(F32)
        return jnp.concatenate([a * cs - b * sn, a * sn + b * cs], axis=1)

    q = rope(0)
    k = rope(QK_P) * (DK ** -0.5)
    kb = k.astype(BF16)
    vb = p_ref[:, 2 * QK_P:2 * QK_P + V_P]
    st = s_ref[...]
    inter = lax.dot_general((q * qd_ref[...]).astype(BF16), st.astype(BF16), _NT,
                            preferred_element_type=F32)
    for h in range(HEADS):
        qh = (q * hm_ref[h:h + 1, :]).astype(BF16)
        att = lax.dot_general(qh, kb, _NT, preferred_element_type=F32) * dm_ref[h]
        sl = slice(h * LANE, (h + 1) * LANE)
        oc_ref[:, sl] = jnp.dot(att.astype(BF16), vb[:, sl], preferred_element_type=F32) + inter[:, sl]
    kdv = lax.dot_general(vb, (k * kd_ref[...]).astype(BF16), _TN, preferred_element_type=F32)
    s_ref[...] = st * cd_ref[...] + bm_ref[...] * kdv

    @pl.when(d == 0)
    def _():
        of_ref[pl.ds(row0, RET_C), :] = oc_ref[...]

    @pl.when(d == 1)
    def _():
        for h in range(HEADS):
            sl = slice(h * LANE, (h + 1) * LANE)
            o = oc_ref[:, sl] + of_ref[pl.ds(row0, RET_C), sl]
            mu = jnp.sum(o, axis=-1, keepdims=True) * (1.0 / DV)
            dlt = (o - mu) * real_ref[:, sl]
            var = jnp.sum(dlt * dlt, axis=-1, keepdims=True) * (1.0 / DV)
            y = dlt * lax.rsqrt(var + EPS) * gn_ref[:, sl] * _silu(g_ref[:, sl].astype(F32))
            y_ref[:, sl] = y.astype(BF16)


def _mixer_specs(b, lt, ctx_len, tc, off_qkv, off_g):
    n_all, n_ctx = lt // tc, ctx_len // tc
    cidx = functools.partial(_dir_chunk, n_ctx=n_ctx, n_all=n_all)
    qkv = pl.BlockSpec((None, tc, QKV_P), lambda i, d, c: (i, cidx(d, c), off_qkv // QKV_P))
    gsp = pl.BlockSpec((None, tc, V_P), lambda i, d, c: (i, cidx(1, c * d), off_g // V_P))
    out = pl.BlockSpec((None, tc, V_P), lambda i, d, c: (i, cidx(1, c * d), 0))
    return n_ctx, n_all, cidx, qkv, gsp, out


def _ret_call(p, cos_t, sin_t, dmask, qdec, kdec, cdec, gn, ctx_len):
    b, lt, _ = p.shape
    tc = RET_C
    n_ctx, n_all, cidx, qkv, gsp, out = _mixer_specs(b, lt, ctx_len, tc, OFF_RQKV, OFF_RG)
    return pl.pallas_call(
        functools.partial(_ret_kernel, n_ctx=n_ctx, n_all=n_all),
        grid=(b, 2, n_all),
        in_specs=[qkv, gsp,
                  pl.BlockSpec((tc, LANE), lambda i, d, c: (cidx(d, c), 0)),
                  pl.BlockSpec((tc, LANE), lambda i, d, c: (cidx(d, c), 0)),
                  pl.BlockSpec((None, HEADS, tc, tc), lambda i, d, c: (d, 0, 0, 0)),
                  pl.BlockSpec((None, tc, QK_P), lambda i, d, c: (d, 0, 0)),
                  pl.BlockSpec((None, tc, QK_P), lambda i, d, c: (d, 0, 0)),
                  pl.BlockSpec((None, 1, QK_P), lambda i, d, c: (d, 0, 0)),
                  _const_spec((SUB, QK_P)), _const_spec((V_P, QK_P)),
                  _const_spec((1, V_P)), _const_spec((1, V_P))],
        out_specs=out,
        out_shape=jax.ShapeDtypeStruct((b, lt, V_P), BF16),
        scratch_shapes=[pltpu.VMEM((V_P, QK_P), F32),
                        pltpu.VMEM((tc, V_P), F32),
                        pltpu.VMEM((lt, V_P), F32)],
        compiler_params=_cparams(3),
    )(p, p, cos_t, sin_t, dmask, qdec, kdec, cdec,
      jnp.asarray(TAB["head_mask"]), jnp.asarray(TAB["bm_t"]),
      jnp.asarray(TAB["v_real"]).reshape(1, V_P), gn)


def _gla_kernel(p_ref, g_ref, z_ref, gw_ref, gb_ref, tri_ref, hm_ref, bm_ref, gn_ref,
                y_ref, s_ref, oc_ref, of_ref, *, n_ctx, n_all):
    d = pl.program_id(1)
    c = pl.program_id(2)
    row0 = pl.multiple_of(_dir_chunk(d, c, n_ctx, n_all) * GLA_R, GLA_R)

    @pl.when(c == 0)
    def _():
        s_ref[...] = jnp.zeros_like(s_ref)

    trif = tri_ref[...]
    tri = trif.astype(BF16)
    mask = trif > 0.5
    la = jax.nn.log_sigmoid(
        jnp.dot(z_ref[...], gw_ref[...], preferred_element_type=F32) + gb_ref[...]) / GLA_TAU
    hi = la.astype(BF16)
    lo = (la - hi.astype(F32)).astype(BF16)
    bcum = jnp.dot(tri, hi, preferred_element_type=F32) + jnp.dot(tri, lo, preferred_element_type=F32)
    q = p_ref[:, 0:QK_P].astype(F32) * (DK ** -0.5)
    k = p_ref[:, QK_P:2 * QK_P].astype(F32)
    vb = p_ref[:, 2 * QK_P:2 * QK_P + V_P]
    qt = q * jnp.exp(bcum)
    qtb = qt.astype(BF16)
    ktb = (k * jnp.exp(-bcum)).astype(BF16)
    for h in range(HEADS):
        qh = (qt * hm_ref[h:h + 1, :]).astype(BF16)
        att = jnp.where(mask, lax.dot_general(qh, ktb, _NT, preferred_element_type=F32), 0.0)
        sl = slice(h * LANE, (h + 1) * LANE)
        oc_ref[:, sl] = jnp.dot(att.astype(BF16), vb[:, sl], preferred_element_type=F32)

    def chain(order, last_row):
        s = s_ref[...]
        for cc in order:
            r0 = cc * GLA_C
            rows = slice(r0, r0 + GLA_C)
            bl = bcum[r0 + last_row:r0 + last_row + 1, :]
            oc_ref[rows, :] += lax.dot_general(qtb[rows], s.astype(BF16), _NT, preferred_element_type=F32)
            kd = (k[rows] * jnp.exp(bl - bcum[rows])).astype(BF16)
            kdv = lax.dot_general(vb[rows], kd, _TN, preferred_element_type=F32)
            s = s * jnp.exp(bl) + bm_ref[...] * kdv
        s_ref[...] = s

    n_sub = GLA_R // GLA_C

    @pl.when(d == 0)
    def _():
        chain(range(n_sub), GLA_C - 1)
        of_ref[pl.ds(row0, GLA_R), :] = oc_ref[...]

    @pl.when(d == 1)
    def _():
        chain(range(n_sub - 1, -1, -1), 0)
        for h in range(HEADS):
            sl = slice(h * LANE, (h + 1) * LANE)
            o = oc_ref[:, sl] + of_ref[pl.ds(row0, GLA_R), sl]
            ms = jnp.sum(o * o, axis=-1, keepdims=True) * (1.0 / DV)
            y = o * lax.rsqrt(ms + EPS) * gn_ref[:, sl] * _silu(g_ref[:, sl].astype(F32))
            y_ref[:, sl] = y.astype(BF16)


def _gla_call(p, gw, gb, tri, gn, ctx_len):
    b, lt, _ = p.shape
    tc = GLA_R
    n_ctx, n_all, cidx, qkv, gsp, out = _mixer_specs(b, lt, ctx_len, tc, OFF_GQKV, OFF_GG)
    return pl.pallas_call(
        functools.partial(_gla_kernel, n_ctx=n_ctx, n_all=n_all),
        grid=(b, 2, n_all),
        in_specs=[qkv, gsp,
                  pl.BlockSpec((None, tc, LANE), lambda i, d, c: (i, cidx(d, c), OFF_Z // LANE)),
                  pl.BlockSpec((None, LANE, QK_P), lambda i, d, c: (d, 0, 0)),
                  pl.BlockSpec((None, 1, QK_P), lambda i, d, c: (d, 0, 0)),
                  pl.BlockSpec((None, tc, tc), lambda i, d, c: (d, 0, 0)),
                  _const_spec((SUB, QK_P)), _const_spec((V_P, QK_P)),
                  _const_spec((1, V_P))],
        out_specs=out,
        out_shape=jax.ShapeDtypeStruct((b, lt, V_P), BF16),
        scratch_shapes=[pltpu.VMEM((V_P, QK_P), F32),
                        pltpu.VMEM((tc, V_P), F32),
                        pltpu.VMEM((lt, V_P), F32)],
        compiler_params=_cparams(3),
    )(p, p, p, gw, gb, tri, jnp.asarray(TAB["head_mask"]), jnp.asarray(TAB["bm_t"]), gn)


FFN_T = 256


def _merge_ffn_kernel(x_ref, mod_ref, ys_ref, yr_ref, yg_ref, gate_ref,
                      wbs_ref, wbr_ref, wbg_ref, wo_ref,
                      gain_ref, wa_ref, wb_ref, wfo_ref, fin_ref, o_ref, act_ref, *, final):
    def branch(y, w_ref, g0):
        gate = jax.nn.sigmoid(gate_ref[:, g0:g0 + D].astype(F32))
        return gate * jnp.dot(y, w_ref[...], preferred_element_type=F32)

    m = (branch(ys_ref[...].astype(BF16), wbs_ref, 0) + branch(yr_ref[...], wbr_ref, D)
         + branch(yg_ref[...], wbg_ref, 2 * D))
    x = x_ref[...] + mod_ref[2:3, :] * jnp.dot(m.astype(BF16), wo_ref[...],
                                               preferred_element_type=F32)
    h = _rms(x, gain_ref[...])
    hb = (h * (1.0 + mod_ref[4:5, :]) + mod_ref[3:4, :]).astype(BF16)
    for t in range(FFN_H // FFN_T):
        a = jnp.dot(hb, wa_ref[t], preferred_element_type=F32)
        bq = jnp.dot(hb, wb_ref[t], preferred_element_type=F32)
        act_ref[:, t * FFN_T:(t + 1) * FFN_T] = (_silu(a) * bq).astype(BF16)
    y = x + mod_ref[5:6, :] * jnp.dot(act_ref[...], wfo_ref[...], preferred_element_type=F32)
    if final:
        y = _rms(y, fin_ref[...])
    o_ref[...] = y


def _merge_ffn_call(xs, modsel, ys5, yret, ygla, p, wbs, wbr, wbg, wo, gain, wa, wb, wfo, fin,
                    final, lat_only, ctx_len):
    b, lt, _ = xs.shape
    j0 = ctx_len // ROW_T if lat_only else 0
    nt = lt // ROW_T - j0
    nh = FFN_H // FFN_T
    tok = lambda w: pl.BlockSpec((None, ROW_T, w), lambda i, j: (i, j + j0, 0))
    return pl.pallas_call(
        functools.partial(_merge_ffn_kernel, final=final),
        grid=(b, nt),
        in_specs=[tok(D),
                  pl.BlockSpec((None, None, SUB, D), lambda i, j: (i, jnp.minimum(j + j0, 1), 0, 0)),
                  tok(S5_W), tok(V_P), tok(V_P), tok(GATE_W),
                  _const_spec((S5_W, D)), _const_spec((V_P, D)), _const_spec((V_P, D)),
                  _const_spec((D, D)),
                  _const_spec((1, D)),
                  _const_spec((nh, D, FFN_T)), _const_spec((nh, D, FFN_T)),
                  _const_spec((FFN_H, D)), _const_spec((1, D))],
        out_specs=pl.BlockSpec((None, ROW_T, D), lambda i, j: (i, j, 0)),
        out_shape=jax.ShapeDtypeStruct((b, nt * ROW_T, D), F32),
        scratch_shapes=[pltpu.VMEM((ROW_T, FFN_H), BF16)],
        compiler_params=_cparams(2),
    )(xs, modsel, ys5, yret, ygla, p, wbs, wbr, wbg, wo,
      gain.reshape(1, D), wa, wb, wfo, fin.reshape(1, D))


def _s5_params(lam_re, lam_im, log_dt, b_re, b_im, c_re, c_im):
    dt = jnp.exp(log_dt)[..., None]
    mag = jnp.exp(lam_re * dt)
    a_re, a_im = mag * jnp.cos(lam_im * dt), mag * jnp.sin(lam_im * dt)
    den = lam_re * lam_re + lam_im * lam_im
    f_re = ((a_re - 1.0) * lam_re + a_im * lam_im) / den
    f_im = (a_im * lam_re - (a_re - 1.0) * lam_im) / den
    bb_re = f_re[..., None] * b_re - f_im[..., None] * b_im
    bb_im = f_re[..., None] * b_im + f_im[..., None] * b_re
    eye = jnp.eye(S5_GROUPS, dtype=F32)

    def blk_in(bb):
        t = jnp.einsum("dgpc,gh->dgchp", bb, eye)
        return t.reshape(2, S5_W, S5_NS)

    bblk = jnp.concatenate([blk_in(bb_re), blk_in(bb_im)], axis=-1).astype(BF16)

    def blk_out(cc):
        t = jnp.einsum("gcp,gh->gphc", cc, eye)
        return t.reshape(S5_NS, S5_W)

    cblk = jnp.concatenate([blk_out(c_re), -blk_out(c_im)], axis=0).astype(BF16)
    bc = lambda a: jnp.broadcast_to(a.reshape(2, 1, S5_NS), (2, SUB, S5_NS))
    return bblk, bc(a_re), bc(a_im), cblk


def _ret_tables(log_decay):
    tc = RET_C
    pos = jnp.arange(tc, dtype=F32)
    w = jnp.stack([pos, tc - 1.0 - pos])
    rel = w[:, :, None] - w[:, None, :]
    lg = log_decay[:, :, None, None]
    dmask = jnp.where(rel[:, None] >= 0, jnp.exp(jnp.maximum(rel[:, None], 0.0) * lg), 0.0)
    qk_head = TAB["qk_head"]
    lane_lg = jnp.where(jnp.asarray(qk_head >= 0),
                        jnp.take(log_decay, jnp.asarray(np.maximum(qk_head, 0)), axis=1), 0.0)
    qdec = jnp.exp((w[:, :, None] + 1.0) * lane_lg[:, None, :])
    kdec = jnp.exp((tc - 1.0 - w[:, :, None]) * lane_lg[:, None, :])
    cdec = jnp.exp(tc * lane_lg)[:, None, :]
    return dmask, qdec, kdec, cdec


def _rope_tables(seq, ctx_len):
    rows = seq // GRID_W
    nf = DK // 4
    inv = 1.0 / (ROPE_BASE ** (np.arange(nf, dtype=np.float32) / nf))
    r = np.repeat(np.arange(rows, dtype=np.float32), GRID_W)
    col = np.tile(np.arange(GRID_W, dtype=np.float32), rows)
    ang = np.concatenate([r[:, None] * inv, col[:, None] * inv], axis=-1)
    cos_t = np.ones((ctx_len + seq, LANE), np.float32)
    sin_t = np.zeros((ctx_len + seq, LANE), np.float32)
    for h in range(HEADS):
        cos_t[ctx_len:, h * 24:(h + 1) * 24] = np.cos(ang)
        sin_t[ctx_len:, h * 24:(h + 1) * 24] = np.sin(ang)
    return jnp.asarray(cos_t), jnp.asarray(sin_t)


def _gla_tables(gate_w, gate_b):
    gw = _gather_cols(gate_w, TAB["qk_src"])
    gw_full = jnp.zeros((2, LANE, QK_P), F32)
    gw_full = gw_full.at[0, 0:GLA_RANK].set(gw[0]).at[1, GLA_RANK:2 * GLA_RANK].set(gw[1])
    gb = _gather_cols(gate_b, TAB["qk_src"])[:, None, :]
    i = np.arange(GLA_R)
    same = (i[:, None] // GLA_C) == (i[None, :] // GLA_C)
    tri = np.stack([same & (i[:, None] >= i[None, :]), same & (i[:, None] <= i[None, :])]).astype(np.float32)
    return gw_full.astype(BF16), gb, jnp.asarray(tri)


def _pad_rows(w, src):
    return _gather_cols(w.T, src).T


def kernel(x, c, ctx, c_ctx, w_mod, b_mod, norm_mix, norm_ffn, w_in, s5_lam_re, s5_lam_im, s5_log_dt, s5_b_re, s5_b_im, s5_c_re, s5_c_im, s5_d, s5_glu_w, s5_glu_b, ret_log_decay, ret_gn, gla_gate_w, gla_gate_b, gla_norm, w_br_s5, w_br_ret, w_br_gla, w_out, w_ffn_in, w_ffn_out, norm_final):
    b, seq, _ = x.shape
    ctx_len = ctx.shape[1]
    depth = w_mod.shape[0]
    assert b % SUB == 0 and ctx_len % RET_C == 0 and seq % RET_C == 0 and seq % GRID_W == 0

    xs = jnp.concatenate([ctx, x], axis=1)
    rows = ((b + 1 + SUB - 1) // SUB) * SUB
    c_all = jnp.zeros((rows, D), F32).at[:b].set(c).at[b].set(c_ctx)
    cos_t, sin_t = _rope_tables(seq, ctx_len)
    nh = FFN_H // FFN_T

    for l in range(depth):
        last = l == depth - 1
        mods = _mod_call(c_all, w_mod[l], b_mod[l])
        m_lat = mods[:b].reshape(b, N_MOD, D)
        m_ctx = jnp.broadcast_to(mods[b].reshape(1, N_MOD, D), (b, N_MOD, D))
        modsel = jnp.stack([m_ctx, m_lat], axis=1)
        modsel = jnp.pad(modsel, ((0, 0), (0, 0), (0, SUB - N_MOD), (0, 0)))

        w_all = _gather_cols(w_in[l], TAB["src"]).astype(BF16)
        p, u = _inproj_call(xs, modsel, norm_mix[l], w_all)

        bblk, a_re, a_im, cblk = _s5_params(s5_lam_re[l], s5_lam_im[l], s5_log_dt[l],
                                            s5_b_re[l], s5_b_im[l], s5_c_re[l], s5_c_im[l])
        ys5 = _s5_call(jnp.transpose(u, (1, 0, 2)), bblk, a_re, a_im, cblk,
                       s5_d[l].reshape(1, S5_W), s5_glu_w[l].astype(BF16),
                       s5_glu_b[l].reshape(1, S5_W), ctx_len)
        ys5 = jnp.transpose(ys5, (1, 0, 2))

        dmask, qdec, kdec, cdec = _ret_tables(ret_log_decay[l])
        yret = _ret_call(p, cos_t, sin_t, dmask, qdec, kdec, cdec,
                         _gather_cols(ret_gn[l], TAB["vsrc"]).reshape(1, V_P), ctx_len)

        gw, gb, tri = _gla_tables(gla_gate_w[l], gla_gate_b[l])
        ygla = _gla_call(p, gw, gb, tri, _gather_cols(gla_norm[l], TAB["vsrc"]).reshape(1, V_P),
                         ctx_len)

        wfi = w_ffn_in[l].astype(BF16)
        wa = wfi[:, :FFN_H].reshape(D, nh, FFN_T).transpose(1, 0, 2)
        wb = wfi[:, FFN_H:].reshape(D, nh, FFN_T).transpose(1, 0, 2)
        xs = _merge_ffn_call(xs, modsel, ys5, yret, ygla, p, w_br_s5[l].astype(BF16),
                             _pad_rows(w_br_ret[l], TAB["vsrc"]).astype(BF16),
                             _pad_rows(w_br_gla[l], TAB["vsrc"]).astype(BF16),
                             w_out[l].astype(BF16), norm_ffn[l], wa, wb,
                             w_ffn_out[l].astype(BF16), norm_final, last, last, ctx_len)
    return xs
```

```python
import functools

import numpy as np
import jax
import jax.numpy as jnp
from jax import lax
from jax.experimental import pallas as pl
from jax.experimental.pallas import tpu as pltpu

F32 = jnp.float32
BF16 = jnp.bfloat16

D = 1024
EPS = 1e-6
N_MOD = 6
GRID_W = 64
S5_W = 256
S5_GROUPS = 16
S5_STATE = 64
S5_NS = S5_GROUPS * S5_STATE
HEADS = 4
DK = 48
DV = 96
QK_W = HEADS * DK
V_W = HEADS * DV
GLA_RANK = 16
GLA_TAU = 16.0
ROPE_BASE = 10000.0
FFN_H = 2816

LANE = 128
SUB = 8
QK_P = 2 * LANE
V_P = HEADS * LANE
QKV_P = 2 * QK_P + V_P
GATE_W = 3 * D

OFF_GATE = 0
OFF_RQKV = GATE_W
OFF_GQKV = OFF_RQKV + QKV_P
OFF_RG = OFF_GQKV + QKV_P
OFF_GG = OFF_RG + V_P
OFF_Z = OFF_GG + V_P
NP = OFF_Z + LANE
OFF_S5 = NP
NW = NP + S5_W

ROW_T = 256
RET_C = 256
GLA_C = 64
GLA_R = 256
MIX_NB = 2
S5_C = 128
VMEM_LIMIT = 56 * 1024 * 1024

IN_SIZES = (S5_W, QK_W, QK_W, V_W, V_W, QK_W, QK_W, V_W, V_W, GLA_RANK, GLA_RANK, D, D, D)
IN_OFFS = np.concatenate([[0], np.cumsum(IN_SIZES)]).astype(np.int64)


def _head_lane():
    m = np.zeros((HEADS, DK), np.int64)
    for h in range(HEADS):
        for i in range(DK):
            m[h, i] = (i % 2) * LANE + h * (DK // 2) + i // 2
    return m


HEAD_LANE = _head_lane()


def _static_tables():
    src = np.full((NW,), -1, np.int64)
    src[OFF_GATE:OFF_GATE + GATE_W] = IN_OFFS[11] + np.arange(GATE_W)
    for qkv, gg, iq, ik, iv, ig in ((OFF_RQKV, OFF_RG, 1, 2, 3, 4), (OFF_GQKV, OFF_GG, 5, 6, 7, 8)):
        for h in range(HEADS):
            for i in range(DK):
                src[qkv + HEAD_LANE[h, i]] = IN_OFFS[iq] + h * DK + i
                src[qkv + QK_P + HEAD_LANE[h, i]] = IN_OFFS[ik] + h * DK + i
            for j in range(DV):
                src[qkv + 2 * QK_P + h * LANE + j] = IN_OFFS[iv] + h * DV + j
                src[gg + h * LANE + j] = IN_OFFS[ig] + h * DV + j
    src[OFF_Z:OFF_Z + GLA_RANK] = IN_OFFS[9] + np.arange(GLA_RANK)
    src[OFF_Z + GLA_RANK:OFF_Z + 2 * GLA_RANK] = IN_OFFS[10] + np.arange(GLA_RANK)
    src[OFF_S5:OFF_S5 + S5_W] = np.arange(S5_W)
    vsrc = np.full((V_P,), -1, np.int64)
    for h in range(HEADS):
        vsrc[h * LANE:h * LANE + DV] = h * DV + np.arange(DV)
    qk_head = np.full((QK_P,), -1, np.int64)
    qk_src = np.full((QK_P,), -1, np.int64)
    for h in range(HEADS):
        for i in range(DK):
            qk_head[HEAD_LANE[h, i]] = h
            qk_src[HEAD_LANE[h, i]] = h * DK + i
    v_head = np.repeat(np.arange(HEADS), LANE)
    v_real = (np.arange(V_P) % LANE) < DV
    head_mask = np.zeros((SUB, QK_P), np.float32)
    for h in range(HEADS):
        head_mask[h] = (qk_head == h)
    bm_t = (v_head[:, None] == qk_head[None, :]).astype(np.float32)
    return dict(src=src, vsrc=vsrc, qk_head=qk_head, qk_src=qk_src, head_mask=head_mask,
                bm_t=bm_t, v_real=v_real.astype(np.float32))


TAB = _static_tables()


def _gather_cols(w, src):
    valid = jnp.asarray(src >= 0)
    out = jnp.take(w, jnp.asarray(np.maximum(src, 0)), axis=-1)
    return jnp.where(valid, out, jnp.zeros((), w.dtype))


def _cparams(n_axes):
    return pltpu.CompilerParams(dimension_semantics=("arbitrary",) * n_axes,
                                vmem_limit_bytes=VMEM_LIMIT)


def _const_spec(shape):
    nd = len(shape)
    return pl.BlockSpec(shape, lambda *_: (0,) * nd, pipeline_mode=pl.Buffered(1))


def _dir_chunk(d, c, n_ctx, n_all):
    bwd = jnp.where(c < n_ctx, n_ctx - 1 - c, n_all + n_ctx - 1 - c)
    return jnp.where(d == 0, c, bwd)


def _silu(x):
    return x * jax.nn.sigmoid(x)


def _mod_kernel(c_ref, w_ref, b_ref, o_ref):
    o_ref[...] = jnp.dot(_silu(c_ref[...]), w_ref[...], preferred_element_type=F32,
                         precision=lax.Precision.HIGHEST) + b_ref[...]


def _mod_call(c_all, w_mod_l, b_mod_l):
    rows = c_all.shape[0]
    tn = D
    return pl.pallas_call(
        _mod_kernel,
        grid=(N_MOD * D // tn,),
        in_specs=[pl.BlockSpec((rows, D), lambda j: (0, 0)),
                  pl.BlockSpec((D, tn), lambda j: (0, j)),
                  pl.BlockSpec((1, tn), lambda j: (0, j))],
        out_specs=pl.BlockSpec((rows, tn), lambda j: (0, j)),
        out_shape=jax.ShapeDtypeStruct((rows, N_MOD * D), F32),
        compiler_params=_cparams(1),
    )(c_all, w_mod_l, b_mod_l.reshape(1, -1))


def _rms(x, gain):
    return x * lax.rsqrt(jnp.mean(x * x, axis=-1, keepdims=True) + EPS) * gain


def _inproj_kernel(x_ref, mod_ref, gain_ref, cos_ref, sin_ref, w_ref, p_ref, u_ref):
    h = _rms(x_ref[...], gain_ref[...])
    h = h * (1.0 + mod_ref[1:2, :]) + mod_ref[0:1, :]
    hb = h.astype(BF16)

    def proj(c0, c1):
        return jnp.dot(hb, w_ref[:, c0:c1], preferred_element_type=F32)

    for c0 in range(0, GATE_W, D):
        p_ref[:, c0:c0 + D] = proj(c0, c0 + D).astype(BF16)
    cs = cos_ref[...]
    sn = sin_ref[...]
    qk = proj(OFF_RQKV, OFF_RQKV + 2 * QK_P)
    for o, scale in ((0, 1.0), (QK_P, DK ** -0.5)):
        a = qk[:, o:o + LANE] * scale
        b = qk[:, o + LANE:o + 2 * LANE] * scale
        p_ref[:, OFF_RQKV + o:OFF_RQKV + o + LANE] = (a * cs - b * sn).astype(BF16)
        p_ref[:, OFF_RQKV + o + LANE:OFF_RQKV + o + 2 * LANE] = (a * sn + b * cs).astype(BF16)
    c0 = OFF_RQKV + 2 * QK_P
    p_ref[:, c0:OFF_GQKV] = proj(c0, OFF_GQKV).astype(BF16)
    p_ref[:, OFF_GQKV:OFF_GQKV + QK_P] = (proj(OFF_GQKV, OFF_GQKV + QK_P) * (DK ** -0.5)).astype(BF16)
    c0 = OFF_GQKV + QK_P
    p_ref[:, c0:OFF_RG] = proj(c0, OFF_RG).astype(BF16)
    p_ref[:, OFF_RG:NP] = proj(OFF_RG, NP).astype(BF16)
    u_ref[...] = proj(OFF_S5, OFF_S5 + S5_W)


def _inproj_call(xs, modsel, gain, cos_t, sin_t, w_all):
    b, lt, _ = xs.shape
    nt = lt // ROW_T
    return pl.pallas_call(
        _inproj_kernel,
        grid=(b, nt),
        in_specs=[pl.BlockSpec((None, ROW_T, D), lambda i, j: (i, j, 0)),
                  pl.BlockSpec((None, None, SUB, D), lambda i, j: (i, jnp.minimum(j, 1), 0, 0)),
                  _const_spec((1, D)),
                  pl.BlockSpec((ROW_T, LANE), lambda i, j: (j, 0)),
                  pl.BlockSpec((ROW_T, LANE), lambda i, j: (j, 0)),
                  _const_spec((D, NW))],
        out_specs=[pl.BlockSpec((None, ROW_T, NP), lambda i, j: (i, j, 0)),
                   pl.BlockSpec((None, ROW_T, S5_W), lambda i, j: (i, j, 0))],
        out_shape=[jax.ShapeDtypeStruct((b, lt, NP), BF16),
                   jax.ShapeDtypeStruct((b, lt, S5_W), F32)],
        compiler_params=_cparams(2),
    )(xs, modsel, gain.reshape(1, D), cos_t, sin_t, w_all)


def _s5_kernel(u_ref, bb_ref, are_ref, aim_ref, cb_ref, dsk_ref, gluw_ref, glub_ref,
               y_ref, xh_ref, h_ref, yf_ref, *, tc, n_ctx, n_all):
    d = pl.program_id(1)
    c = pl.program_id(2)
    rows = tc * SUB
    row0 = pl.multiple_of(_dir_chunk(d, c, n_ctx, n_all) * rows, rows)

    @pl.when(c == 0)
    def _():
        h_ref[...] = jnp.zeros_like(h_ref)

    u = u_ref[...].reshape(rows, S5_W)
    xh_ref[...] = jnp.dot(u.astype(BF16), bb_ref[...], preferred_element_type=F32)
    half = S5_NS // 2
    for lo in (0, half):
        re = slice(lo, lo + half)
        im = slice(S5_NS + lo, S5_NS + lo + half)
        ar = are_ref[:, re]
        ai = aim_ref[:, re]

        def body(t, carry, re=re, im=im, ar=ar, ai=ai):
            hr, hi = carry
            tt = t + d * (tc - 1 - 2 * t)
            r0 = pl.multiple_of(tt * SUB, SUB)
            nr = ar * hr - ai * hi + xh_ref[pl.ds(r0, SUB), re]
            ni = ar * hi + ai * hr + xh_ref[pl.ds(r0, SUB), im]
            xh_ref[pl.ds(r0, SUB), re] = nr
            xh_ref[pl.ds(r0, SUB), im] = ni
            return nr, ni

        hr, hi = lax.fori_loop(0, tc, body, (h_ref[:, re], h_ref[:, im]), unroll=4)
        h_ref[:, re] = hr
        h_ref[:, im] = hi
    y = jnp.dot(xh_ref[...].astype(BF16), cb_ref[...], preferred_element_type=F32)

    @pl.when(d == 0)
    def _():
        yf_ref[pl.ds(row0, rows), :] = y

    @pl.when(d == 1)
    def _():
        yy = jax.nn.gelu(y + yf_ref[pl.ds(row0, rows), :] + dsk_ref[...] * u)
        glu = jnp.dot(yy.astype(BF16), gluw_ref[...], preferred_element_type=F32) + glub_ref[...]
        y_ref[...] = (yy * jax.nn.sigmoid(glu)).reshape(tc, SUB, S5_W)


def _s5_call(u, bblk, a_re, a_im, cblk, dsk, gluw, glub, ctx_len):
    lt, b, _ = u.shape
    tc = S5_C
    n_all, n_ctx = lt // tc, ctx_len // tc
    cidx = functools.partial(_dir_chunk, n_ctx=n_ctx, n_all=n_all)
    return pl.pallas_call(
        functools.partial(_s5_kernel, tc=tc, n_ctx=n_ctx, n_all=n_all),
        grid=(b // SUB, 2, n_all),
        in_specs=[pl.BlockSpec((tc, SUB, S5_W), lambda g, d, c: (cidx(d, c), g, 0)),
                  pl.BlockSpec((None, S5_W, 2 * S5_NS), lambda g, d, c: (d, 0, 0)),
                  pl.BlockSpec((None, SUB, S5_NS), lambda g, d, c: (d, 0, 0)),
                  pl.BlockSpec((None, SUB, S5_NS), lambda g, d, c: (d, 0, 0)),
                  _const_spec((2 * S5_NS, S5_W)),
                  _const_spec((1, S5_W)), _const_spec((S5_W, S5_W)), _const_spec((1, S5_W))],
        out_specs=pl.BlockSpec((tc, SUB, S5_W), lambda g, d, c: (cidx(1, c * d), g, 0)),
        out_shape=jax.ShapeDtypeStruct((lt, b, S5_W), F32),
        scratch_shapes=[pltpu.VMEM((tc * SUB, 2 * S5_NS), F32),
                        pltpu.VMEM((SUB, 2 * S5_NS), F32),
                        pltpu.VMEM((lt * SUB, S5_W), F32)],
        compiler_params=_cparams(3),
    )(u, bblk, a_re, a_im, cblk, dsk, gluw, glub)


_NT = (((1,), (1,)), ((), ()))
_TN = (((0,), (0,)), ((), ()))


def _ret_kernel(p_ref, g_ref, dm_ref, qd_ref, kd_ref, cd_ref, hm_ref, bm_ref,
                real_ref, gn_ref, y_ref, s_ref, oc_ref, of_ref, *, n_ctx, n_all):
    d = pl.program_id(1)
    c = pl.program_id(2)
    row0 = pl.multiple_of(_dir_chunk(d, c, n_ctx, n_all) * RET_C, RET_C)

    @pl.when(c == 0)
    def _():
        s_ref[...] = jnp.zeros_like(s_ref)

    def chunk(bi):
        qb = p_ref[bi, :, 0:QK_P]
        kb = p_ref[bi, :, QK_P:2 * QK_P]
        vb = p_ref[bi, :, 2 * QK_P:2 * QK_P + V_P]
        st = s_ref[bi]
        inter = lax.dot_general(qb * qd_ref[...], st.astype(BF16), _NT, preferred_element_type=F32)
        for h in range(HEADS):
            att = lax.dot_general(qb * hm_ref[h], kb, _NT, preferred_element_type=F32)
            att = att.astype(BF16) * dm_ref[h]
            sl = slice(h * LANE, (h + 1) * LANE)
            oc_ref[bi, :, sl] = (jnp.dot(att, vb[:, sl], preferred_element_type=F32) + inter[:, sl])
        kdv = lax.dot_general(vb, kb * kd_ref[...], _TN, preferred_element_type=F32)
        s_ref[bi] = st * cd_ref[...] + bm_ref[...] * kdv

    for bi in range(MIX_NB):
        chunk(bi)

    @pl.when(d == 0)
    def _():
        for bi in range(MIX_NB):
            of_ref[bi, pl.ds(row0, RET_C), :] = oc_ref[bi]

    @pl.when(d == 1)
    def _():
        for bi in range(MIX_NB):
            for h in range(HEADS):
                sl = slice(h * LANE, (h + 1) * LANE)
                o = oc_ref[bi, :, sl] + of_ref[bi, pl.ds(row0, RET_C), sl]
                mu = jnp.sum(o, axis=-1, keepdims=True) * (1.0 / DV)
                dlt = (o - mu) * real_ref[:, sl]
                var = jnp.sum(dlt * dlt, axis=-1, keepdims=True) * (1.0 / DV)
                y = dlt * lax.rsqrt(var + EPS) * gn_ref[:, sl] * _silu(g_ref[bi, :, sl].astype(F32))
                y_ref[bi, :, sl] = y.astype(BF16)


def _mixer_specs(b, lt, ctx_len, tc, off_qkv, off_g):
    n_all, n_ctx = lt // tc, ctx_len // tc
    cidx = functools.partial(_dir_chunk, n_ctx=n_ctx, n_all=n_all)
    qkv = pl.BlockSpec((MIX_NB, tc, QKV_P), lambda i, d, c: (i, cidx(d, c), off_qkv // QKV_P))
    gsp = pl.BlockSpec((MIX_NB, tc, V_P), lambda i, d, c: (i, cidx(1, c * d), off_g // V_P))
    out = pl.BlockSpec((MIX_NB, tc, V_P), lambda i, d, c: (i, cidx(1, c * d), 0))
    scratch = [pltpu.VMEM((MIX_NB, V_P, QK_P), F32),
               pltpu.VMEM((MIX_NB, tc, V_P), F32),
               pltpu.VMEM((MIX_NB, lt, V_P), F32)]
    return n_ctx, n_all, cidx, qkv, gsp, out, scratch


def _head_mask_rows(rows):
    return jnp.asarray(np.broadcast_to(TAB["head_mask"][:HEADS, None, :], (HEADS, rows, QK_P)), dtype=BF16)


def _ret_call(p, dmask, qdec, kdec, cdec, gn, ctx_len):
    b, lt, _ = p.shape
    tc = RET_C
    n_ctx, n_all, cidx, qkv, gsp, out, scratch = _mixer_specs(b, lt, ctx_len, tc, OFF_RQKV, OFF_RG)
    return pl.pallas_call(
        functools.partial(_ret_kernel, n_ctx=n_ctx, n_all=n_all),
        grid=(b // MIX_NB, 2, n_all),
        in_specs=[qkv, gsp,
                  pl.BlockSpec((None, HEADS, tc, tc), lambda i, d, c: (d, 0, 0, 0)),
                  pl.BlockSpec((None, tc, QK_P), lambda i, d, c: (d, 0, 0)),
                  pl.BlockSpec((None, tc, QK_P), lambda i, d, c: (d, 0, 0)),
                  pl.BlockSpec((None, 1, QK_P), lambda i, d, c: (d, 0, 0)),
                  _const_spec((HEADS, tc, QK_P)), _const_spec((V_P, QK_P)),
                  _const_spec((1, V_P)), _const_spec((1, V_P))],
        out_specs=out,
        out_shape=jax.ShapeDtypeStruct((b, lt, V_P), BF16),
        scratch_shapes=scratch,
        compiler_params=_cparams(3),
    )(p, p, dmask.astype(BF16), qdec.astype(BF16), kdec.astype(BF16), cdec,
      _head_mask_rows(tc), jnp.asarray(TAB["bm_t"]),
      jnp.asarray(TAB["v_real"]).reshape(1, V_P), gn)


def _gla_kernel(p_ref, g_ref, z_ref, gw_ref, gb_ref, tri_ref, hm_ref, bm_ref, gn_ref,
                y_ref, s_ref, oc_ref, of_ref, *, n_ctx, n_all):
    d = pl.program_id(1)
    c = pl.program_id(2)
    row0 = pl.multiple_of(_dir_chunk(d, c, n_ctx, n_all) * GLA_R, GLA_R)

    @pl.when(c == 0)
    def _():
        s_ref[...] = jnp.zeros_like(s_ref)

    trif = tri_ref[...]
    tri = trif.astype(BF16)
    mask = trif > 0.5

    def intra(bi):
        la = jax.nn.log_sigmoid(
            jnp.dot(z_ref[bi], gw_ref[...], preferred_element_type=F32) + gb_ref[...]) / GLA_TAU
        hi = la.astype(BF16)
        lo = (la - hi.astype(F32)).astype(BF16)
        bcum = (jnp.dot(tri, hi, preferred_element_type=F32)
                + jnp.dot(tri, lo, preferred_element_type=F32))
        q = p_ref[bi, :, 0:QK_P].astype(F32)
        k = p_ref[bi, :, QK_P:2 * QK_P].astype(F32)
        vb = p_ref[bi, :, 2 * QK_P:2 * QK_P + V_P]
        qtb = (q * jnp.exp(bcum)).astype(BF16)
        ktb = (k * jnp.exp(-bcum)).astype(BF16)
        for h in range(HEADS):
            att = lax.dot_general(qtb * hm_ref[h], ktb, _NT, preferred_element_type=F32)
            att = jnp.where(mask, att.astype(BF16), jnp.zeros((), BF16))
            sl = slice(h * LANE, (h + 1) * LANE)
            oc_ref[bi, :, sl] = jnp.dot(att, vb[:, sl], preferred_element_type=F32)
        return bcum, k, qtb, vb

    parts = [intra(bi) for bi in range(MIX_NB)]

    def chain(order, last_row):
        ss = [s_ref[bi] for bi in range(MIX_NB)]
        for cc in order:
            r0 = cc * GLA_C
            rows = slice(r0, r0 + GLA_C)
            for bi, (bcum, k, qtb, vb) in enumerate(parts):
                s = ss[bi]
                bl = bcum[r0 + last_row:r0 + last_row + 1, :]
                oc_ref[bi, rows, :] += lax.dot_general(qtb[rows], s.astype(BF16), _NT,
                                                       preferred_element_type=F32)
                kd = (k[rows] * jnp.exp(bl - bcum[rows])).astype(BF16)
                kdv = lax.dot_general(vb[rows], kd, _TN, preferred_element_type=F32)
                ss[bi] = s * jnp.exp(bl) + bm_ref[...] * kdv
        for bi in range(MIX_NB):
            s_ref[bi] = ss[bi]

    n_sub = GLA_R // GLA_C

    @pl.when(d == 0)
    def _():
        chain(range(n_sub), GLA_C - 1)
        for bi in range(MIX_NB):
            of_ref[bi, pl.ds(row0, GLA_R), :] = oc_ref[bi]

    @pl.when(d == 1)
    def _():
        chain(range(n_sub - 1, -1, -1), 0)
        for bi in range(MIX_NB):
            for h in range(HEADS):
                sl = slice(h * LANE, (h + 1) * LANE)
                o = oc_ref[bi, :, sl] + of_ref[bi, pl.ds(row0, GLA_R), sl]
                ms = jnp.sum(o * o, axis=-1, keepdims=True) * (1.0 / DV)
                y = o * lax.rsqrt(ms + EPS) * gn_ref[:, sl] * _silu(g_ref[bi, :, sl].astype(F32))
                y_ref[bi, :, sl] = y.astype(BF16)


def _gla_call(p, gw, gb, tri, gn, ctx_len):
    b, lt, _ = p.shape
    tc = GLA_R
    n_ctx, n_all, cidx, qkv, gsp, out, scratch = _mixer_specs(b, lt, ctx_len, tc, OFF_GQKV, OFF_GG)
    return pl.pallas_call(
        functools.partial(_gla_kernel, n_ctx=n_ctx, n_all=n_all),
        grid=(b // MIX_NB, 2, n_all),
        in_specs=[qkv, gsp,
                  pl.BlockSpec((MIX_NB, tc, LANE), lambda i, d, c: (i, cidx(d, c), OFF_Z // LANE)),
                  pl.BlockSpec((None, LANE, QK_P), lambda i, d, c: (d, 0, 0)),
                  pl.BlockSpec((None, 1, QK_P), lambda i, d, c: (d, 0, 0)),
                  pl.BlockSpec((None, tc, tc), lambda i, d, c: (d, 0, 0)),
                  _const_spec((HEADS, tc, QK_P)), _const_spec((V_P, QK_P)),
                  _const_spec((1, V_P))],
        out_specs=out,
        out_shape=jax.ShapeDtypeStruct((b, lt, V_P), BF16),
        scratch_shapes=scratch,
        compiler_params=_cparams(3),
    )(p, p, p, gw, gb, tri, _head_mask_rows(tc), jnp.asarray(TAB["bm_t"]), gn)


FFN_T = 256


def _merge_ffn_kernel(x_ref, mod_ref, ys_ref, yr_ref, yg_ref, gate_ref,
                      wbs_ref, wbr_ref, wbg_ref, wo_ref,
                      gain_ref, wa_ref, wb_ref, wfo_ref, fin_ref, o_ref, act_ref, *, final):
    def branch(y, w_ref, g0):
        gate = jax.nn.sigmoid(gate_ref[:, g0:g0 + D].astype(F32))
        return gate * jnp.dot(y, w_ref[...], preferred_element_type=F32)

    m = (branch(ys_ref[...].astype(BF16), wbs_ref, 0) + branch(yr_ref[...], wbr_ref, D)
         + branch(yg_ref[...], wbg_ref, 2 * D))
    x = x_ref[...] + mod_ref[2:3, :] * jnp.dot(m.astype(BF16), wo_ref[...],
                                               preferred_element_type=F32)
    h = _rms(x, gain_ref[...])
    hb = (h * (1.0 + mod_ref[4:5, :]) + mod_ref[3:4, :]).astype(BF16)
    for t in range(FFN_H // FFN_T):
        a = jnp.dot(hb, wa_ref[t], preferred_element_type=F32)
        bq = jnp.dot(hb, wb_ref[t], preferred_element_type=F32)
        act_ref[:, t * FFN_T:(t + 1) * FFN_T] = (_silu(a) * bq).astype(BF16)
    y = x + mod_ref[5:6, :] * jnp.dot(act_ref[...], wfo_ref[...], preferred_element_type=F32)
    if final:
        y = _rms(y, fin_ref[...])
    o_ref[...] = y


def _merge_ffn_call(xs, modsel, ys5, yret, ygla, p, wbs, wbr, wbg, wo, gain, wa, wb, wfo, fin,
                    final, lat_only, ctx_len):
    b, lt, _ = xs.shape
    j0 = ctx_len // ROW_T if lat_only else 0
    nt = lt // ROW_T - j0
    nh = FFN_H // FFN_T
    tok = lambda w: pl.BlockSpec((None, ROW_T, w), lambda i, j: (i, j + j0, 0))
    return pl.pallas_call(
        functools.partial(_merge_ffn_kernel, final=final),
        grid=(b, nt),
        in_specs=[tok(D),
                  pl.BlockSpec((None, None, SUB, D), lambda i, j: (i, jnp.minimum(j + j0, 1), 0, 0)),
                  tok(S5_W), tok(V_P), tok(V_P), tok(GATE_W),
                  _const_spec((S5_W, D)), _const_spec((V_P, D)), _const_spec((V_P, D)),
                  _const_spec((D, D)),
                  _const_spec((1, D)),
                  _const_spec((nh, D, FFN_T)), _const_spec((nh, D, FFN_T)),
                  _const_spec((FFN_H, D)), _const_spec((1, D))],
        out_specs=pl.BlockSpec((None, ROW_T, D), lambda i, j: (i, j, 0)),
        out_shape=jax.ShapeDtypeStruct((b, nt * ROW_T, D), F32),
        scratch_shapes=[pltpu.VMEM((ROW_T, FFN_H), BF16)],
        compiler_params=_cparams(2),
    )(xs, modsel, ys5, yret, ygla, p, wbs, wbr, wbg, wo,
      gain.reshape(1, D), wa, wb, wfo, fin.reshape(1, D))


def _s5_params(lam_re, lam_im, log_dt, b_re, b_im, c_re, c_im):
    dt = jnp.exp(log_dt)[..., None]
    mag = jnp.exp(lam_re * dt)
    a_re, a_im = mag * jnp.cos(lam_im * dt), mag * jnp.sin(lam_im * dt)
    den = lam_re * lam_re + lam_im * lam_im
    f_re = ((a_re - 1.0) * lam_re + a_im * lam_im) / den
    f_im = (a_im * lam_re - (a_re - 1.0) * lam_im) / den
    bb_re = f_re[..., None] * b_re - f_im[..., None] * b_im
    bb_im = f_re[..., None] * b_im + f_im[..., None] * b_re
    eye = jnp.eye(S5_GROUPS, dtype=F32)

    def blk_in(bb):
        t = jnp.einsum("dgpc,gh->dgchp", bb, eye)
        return t.reshape(2, S5_W, S5_NS)

    bblk = jnp.concatenate([blk_in(bb_re), blk_in(bb_im)], axis=-1).astype(BF16)

    def blk_out(cc):
        t = jnp.einsum("gcp,gh->gphc", cc, eye)
        return t.reshape(S5_NS, S5_W)

    cblk = jnp.concatenate([blk_out(c_re), -blk_out(c_im)], axis=0).astype(BF16)
    bc = lambda a: jnp.broadcast_to(a.reshape(2, 1, S5_NS), (2, SUB, S5_NS))
    return bblk, bc(a_re), bc(a_im), cblk


def _ret_tables(log_decay):
    tc = RET_C
    pos = jnp.arange(tc, dtype=F32)
    w = jnp.stack([pos, tc - 1.0 - pos])
    rel = w[:, :, None] - w[:, None, :]
    lg = log_decay[:, :, None, None]
    dmask = jnp.where(rel[:, None] >= 0, jnp.exp(jnp.maximum(rel[:, None], 0.0) * lg), 0.0)
    qk_head = TAB["qk_head"]
    lane_lg = jnp.where(jnp.asarray(qk_head >= 0),
                        jnp.take(log_decay, jnp.asarray(np.maximum(qk_head, 0)), axis=1), 0.0)
    qdec = jnp.exp((w[:, :, None] + 1.0) * lane_lg[:, None, :])
    kdec = jnp.exp((tc - 1.0 - w[:, :, None]) * lane_lg[:, None, :])
    cdec = jnp.exp(tc * lane_lg)[:, None, :]
    return dmask, qdec, kdec, cdec


def _rope_tables(seq, ctx_len):
    rows = seq // GRID_W
    nf = DK // 4
    inv = 1.0 / (ROPE_BASE ** (np.arange(nf, dtype=np.float32) / nf))
    r = np.repeat(np.arange(rows, dtype=np.float32), GRID_W)
    col = np.tile(np.arange(GRID_W, dtype=np.float32), rows)
    ang = np.concatenate([r[:, None] * inv, col[:, None] * inv], axis=-1)
    cos_t = np.ones((ctx_len + seq, LANE), np.float32)
    sin_t = np.zeros((ctx_len + seq, LANE), np.float32)
    for h in range(HEADS):
        cos_t[ctx_len:, h * 24:(h + 1) * 24] = np.cos(ang)
        sin_t[ctx_len:, h * 24:(h + 1) * 24] = np.sin(ang)
    return jnp.asarray(cos_t), jnp.asarray(sin_t)


def _gla_tables(gate_w, gate_b):
    gw = _gather_cols(gate_w, TAB["qk_src"])
    gw_full = jnp.zeros((2, LANE, QK_P), F32)
    gw_full = gw_full.at[0, 0:GLA_RANK].set(gw[0]).at[1, GLA_RANK:2 * GLA_RANK].set(gw[1])
    gb = _gather_cols(gate_b, TAB["qk_src"])[:, None, :]
    i = np.arange(GLA_R)
    same = (i[:, None] // GLA_C) == (i[None, :] // GLA_C)
    tri = np.stack([same & (i[:, None] >= i[None, :]), same & (i[:, None] <= i[None, :])]).astype(np.float32)
    return gw_full.astype(BF16), gb, jnp.asarray(tri)


def _pad_rows(w, src):
    return _gather_cols(w.T, src).T


def kernel(x, c, ctx, c_ctx, w_mod, b_mod, norm_mix, norm_ffn, w_in, s5_lam_re, s5_lam_im, s5_log_dt, s5_b_re, s5_b_im, s5_c_re, s5_c_im, s5_d, s5_glu_w, s5_glu_b, ret_log_decay, ret_gn, gla_gate_w, gla_gate_b, gla_norm, w_br_s5, w_br_ret, w_br_gla, w_out, w_ffn_in, w_ffn_out, norm_final):
    b, seq, _ = x.shape
    ctx_len = ctx.shape[1]
    depth = w_mod.shape[0]
    assert b % SUB == 0 and ctx_len % RET_C == 0 and seq % RET_C == 0 and seq % GRID_W == 0

    xs = jnp.concatenate([ctx, x], axis=1)
    rows = ((b + 1 + SUB - 1) // SUB) * SUB
    c_all = jnp.zeros((rows, D), F32).at[:b].set(c).at[b].set(c_ctx)
    cos_t, sin_t = _rope_tables(seq, ctx_len)
    nh = FFN_H // FFN_T

    for l in range(depth):
        last = l == depth - 1
        mods = _mod_call(c_all, w_mod[l], b_mod[l])
        m_lat = mods[:b].reshape(b, N_MOD, D)
        m_ctx = jnp.broadcast_to(mods[b].reshape(1, N_MOD, D), (b, N_MOD, D))
        modsel = jnp.stack([m_ctx, m_lat], axis=1)
        modsel = jnp.pad(modsel, ((0, 0), (0, 0), (0, SUB - N_MOD), (0, 0)))

        w_all = _gather_cols(w_in[l], TAB["src"]).astype(BF16)
        p, u = _inproj_call(xs, modsel, norm_mix[l], cos_t, sin_t, w_all)

        bblk, a_re, a_im, cblk = _s5_params(s5_lam_re[l], s5_lam_im[l], s5_log_dt[l],
                                            s5_b_re[l], s5_b_im[l], s5_c_re[l], s5_c_im[l])
        ys5 = _s5_call(jnp.transpose(u, (1, 0, 2)), bblk, a_re, a_im, cblk,
                       s5_d[l].reshape(1, S5_W), s5_glu_w[l].astype(BF16),
                       s5_glu_b[l].reshape(1, S5_W), ctx_len)
        ys5 = jnp.transpose(ys5, (1, 0, 2))

        dmask, qdec, kdec, cdec = _ret_tables(ret_log_decay[l])
        yret = _ret_call(p, dmask, qdec, kdec, cdec,
                         _gather_cols(ret_gn[l], TAB["vsrc"]).reshape(1, V_P), ctx_len)

        gw, gb, tri = _gla_tables(gla_gate_w[l], gla_gate_b[l])
        ygla = _gla_call(p, gw, gb, tri, _gather_cols(gla_norm[l], TAB["vsrc"]).reshape(1, V_P),
                         ctx_len)

        wfi = w_ffn_in[l].astype(BF16)
        wa = wfi[:, :FFN_H].reshape(D, nh, FFN_T).transpose(1, 0, 2)
        wb = wfi[:, FFN_H:].reshape(D, nh, FFN_T).transpose(1, 0, 2)
        xs = _merge_ffn_call(xs, modsel, ys5, yret, ygla, p, w_br_s5[l].astype(BF16),
                             _pad_rows(w_br_ret[l], TAB["vsrc"]).astype(BF16),
                             _pad_rows(w_br_gla[l], TAB["vsrc"]).astype(BF16),
                             w_out[l].astype(BF16), norm_ffn[l], wa, wb,
                             w_ffn_out[l].astype(BF16), norm_final, last, last, ctx_len)
    return xs
```

```python
import functools

import numpy as np
import jax
import jax.numpy as jnp
from jax import lax
from jax.experimental import pallas as pl
from jax.experimental.pallas import tpu as pltpu

F32 = jnp.float32
BF16 = jnp.bfloat16

D = 1024
EPS = 1e-6
N_MOD = 6
GRID_W = 64
S5_W = 256
S5_GROUPS = 16
S5_STATE = 64
S5_NS = S5_GROUPS * S5_STATE
HEADS = 4
DK = 48
DV = 96
QK_W = HEADS * DK
V_W = HEADS * DV
GLA_RANK = 16
GLA_TAU = 16.0
ROPE_BASE = 10000.0
FFN_H = 2816

LANE = 128
SUB = 8
QK_P = 2 * LANE
V_P = HEADS * LANE
QKV_P = 2 * QK_P + V_P
GATE_W = 3 * D

OFF_GATE = 0
OFF_RQKV = GATE_W
OFF_GQKV = OFF_RQKV + QKV_P
OFF_RG = OFF_GQKV + QKV_P
OFF_GG = OFF_RG + V_P
OFF_Z = OFF_GG + V_P
NP = OFF_Z + LANE
OFF_S5 = NP
NW = NP + S5_W

ROW_T = 256
RET_C = 256
GLA_C = 64
GLA_R = 256
MIX_NB = 2
S5_C = 64
VMEM_LIMIT = 56 * 1024 * 1024

IN_SIZES = (S5_W, QK_W, QK_W, V_W, V_W, QK_W, QK_W, V_W, V_W, GLA_RANK, GLA_RANK, D, D, D)
IN_OFFS = np.concatenate([[0], np.cumsum(IN_SIZES)]).astype(np.int64)


def _head_lane():
    m = np.zeros((HEADS, DK), np.int64)
    for h in range(HEADS):
        for i in range(DK):
            m[h, i] = (i % 2) * LANE + h * (DK // 2) + i // 2
    return m


HEAD_LANE = _head_lane()


def _static_tables():
    src = np.full((NW,), -1, np.int64)
    src[OFF_GATE:OFF_GATE + GATE_W] = IN_OFFS[11] + np.arange(GATE_W)
    for qkv, gg, iq, ik, iv, ig in ((OFF_RQKV, OFF_RG, 1, 2, 3, 4), (OFF_GQKV, OFF_GG, 5, 6, 7, 8)):
        for h in range(HEADS):
            for i in range(DK):
                src[qkv + HEAD_LANE[h, i]] = IN_OFFS[iq] + h * DK + i
                src[qkv + QK_P + HEAD_LANE[h, i]] = IN_OFFS[ik] + h * DK + i
            for j in range(DV):
                src[qkv + 2 * QK_P + h * LANE + j] = IN_OFFS[iv] + h * DV + j
                src[gg + h * LANE + j] = IN_OFFS[ig] + h * DV + j
    src[OFF_Z:OFF_Z + GLA_RANK] = IN_OFFS[9] + np.arange(GLA_RANK)
    src[OFF_Z + GLA_RANK:OFF_Z + 2 * GLA_RANK] = IN_OFFS[10] + np.arange(GLA_RANK)
    src[OFF_S5:OFF_S5 + S5_W] = np.arange(S5_W)
    vsrc = np.full((V_P,), -1, np.int64)
    for h in range(HEADS):
        vsrc[h * LANE:h * LANE + DV] = h * DV + np.arange(DV)
    qk_head = np.full((QK_P,), -1, np.int64)
    qk_src = np.full((QK_P,), -1, np.int64)
    for h in range(HEADS):
        for i in range(DK):
            qk_head[HEAD_LANE[h, i]] = h
            qk_src[HEAD_LANE[h, i]] = h * DK + i
    v_head = np.repeat(np.arange(HEADS), LANE)
    v_real = (np.arange(V_P) % LANE) < DV
    head_mask = np.zeros((SUB, QK_P), np.float32)
    for h in range(HEADS):
        head_mask[h] = (qk_head == h)
    bm_t = (v_head[:, None] == qk_head[None, :]).astype(np.float32)
    return dict(src=src, vsrc=vsrc, qk_head=qk_head, qk_src=qk_src, head_mask=head_mask,
                bm_t=bm_t, v_real=v_real.astype(np.float32))


TAB = _static_tables()


def _gather_cols(w, src):
    valid = jnp.asarray(src >= 0)
    out = jnp.take(w, jnp.asarray(np.maximum(src, 0)), axis=-1)
    return jnp.where(valid, out, jnp.zeros((), w.dtype))


def _cparams(n_axes):
    return pltpu.CompilerParams(dimension_semantics=("arbitrary",) * n_axes,
                                vmem_limit_bytes=VMEM_LIMIT)


def _const_spec(shape):
    nd = len(shape)
    return pl.BlockSpec(shape, lambda *_: (0,) * nd, pipeline_mode=pl.Buffered(1))


def _dir_chunk(d, c, n_ctx, n_all):
    bwd = jnp.where(c < n_ctx, n_ctx - 1 - c, n_all + n_ctx - 1 - c)
    return jnp.where(d == 0, c, bwd)


def _sigmoid(x):
    return 0.5 * jnp.tanh(0.5 * x) + 0.5


def _silu(x):
    return x * _sigmoid(x)


def _mod_kernel(c_ref, w_ref, b_ref, o_ref):
    o_ref[...] = jnp.dot(_silu(c_ref[...]), w_ref[...], preferred_element_type=F32,
                         precision=lax.Precision.HIGHEST) + b_ref[...]


def _mod_call(c_all, w_mod_l, b_mod_l):
    rows = c_all.shape[0]
    tn = D
    return pl.pallas_call(
        _mod_kernel,
        grid=(N_MOD * D // tn,),
        in_specs=[pl.BlockSpec((rows, D), lambda j: (0, 0)),
                  pl.BlockSpec((D, tn), lambda j: (0, j)),
                  pl.BlockSpec((1, tn), lambda j: (0, j))],
        out_specs=pl.BlockSpec((rows, tn), lambda j: (0, j)),
        out_shape=jax.ShapeDtypeStruct((rows, N_MOD * D), F32),
        compiler_params=_cparams(1),
    )(c_all, w_mod_l, b_mod_l.reshape(1, -1))


def _rms(x, gain):
    return x * lax.rsqrt(jnp.mean(x * x, axis=-1, keepdims=True) + EPS) * gain


def _inproj_kernel(x_ref, mod_ref, gain_ref, cos_ref, sin_ref, w_ref, p_ref, u_ref):
    h = _rms(x_ref[...], gain_ref[...])
    h = h * (1.0 + mod_ref[1:2, :]) + mod_ref[0:1, :]
    hb = h.astype(BF16)

    def proj(c0, c1):
        return jnp.dot(hb, w_ref[:, c0:c1], preferred_element_type=F32)

    for c0 in range(0, GATE_W, D):
        p_ref[:, c0:c0 + D] = proj(c0, c0 + D).astype(BF16)
    cs = cos_ref[...]
    sn = sin_ref[...]
    qk = proj(OFF_RQKV, OFF_RQKV + 2 * QK_P)
    for o, scale in ((0, 1.0), (QK_P, DK ** -0.5)):
        a = qk[:, o:o + LANE] * scale
        b = qk[:, o + LANE:o + 2 * LANE] * scale
        p_ref[:, OFF_RQKV + o:OFF_RQKV + o + LANE] = (a * cs - b * sn).astype(BF16)
        p_ref[:, OFF_RQKV + o + LANE:OFF_RQKV + o + 2 * LANE] = (a * sn + b * cs).astype(BF16)
    c0 = OFF_RQKV + 2 * QK_P
    p_ref[:, c0:OFF_GQKV] = proj(c0, OFF_GQKV).astype(BF16)
    p_ref[:, OFF_GQKV:OFF_GQKV + QK_P] = (proj(OFF_GQKV, OFF_GQKV + QK_P) * (DK ** -0.5)).astype(BF16)
    c0 = OFF_GQKV + QK_P
    p_ref[:, c0:OFF_RG] = proj(c0, OFF_RG).astype(BF16)
    p_ref[:, OFF_RG:NP] = proj(OFF_RG, NP).astype(BF16)
    u_ref[...] = proj(OFF_S5, OFF_S5 + S5_W)


def _inproj_call(xs, modsel, gain, cos_t, sin_t, w_all):
    b, lt, _ = xs.shape
    nt = lt // ROW_T
    return pl.pallas_call(
        _inproj_kernel,
        grid=(b, nt),
        in_specs=[pl.BlockSpec((None, ROW_T, D), lambda i, j: (i, j, 0)),
                  pl.BlockSpec((None, None, SUB, D), lambda i, j: (i, jnp.minimum(j, 1), 0, 0)),
                  _const_spec((1, D)),
                  pl.BlockSpec((ROW_T, LANE), lambda i, j: (j, 0)),
                  pl.BlockSpec((ROW_T, LANE), lambda i, j: (j, 0)),
                  _const_spec((D, NW))],
        out_specs=[pl.BlockSpec((None, ROW_T, NP), lambda i, j: (i, j, 0)),
                   pl.BlockSpec((None, ROW_T, S5_W), lambda i, j: (i, j, 0))],
        out_shape=[jax.ShapeDtypeStruct((b, lt, NP), BF16),
                   jax.ShapeDtypeStruct((b, lt, S5_W), F32)],
        compiler_params=_cparams(2),
    )(xs, modsel, gain.reshape(1, D), cos_t, sin_t, w_all)


S5_NB = 2 * SUB
S5_RB = 256


def _s5_kernel(*refs, tc, backward):
    if backward:
        (u_ref, bb_ref, are_ref, aim_ref, cb_ref, yf_ref, dsk_ref, gluw_ref, glub_ref,
         y_ref, x_ref, hb_ref, h_ref) = refs
    else:
        u_ref, bb_ref, are_ref, aim_ref, cb_ref, y_ref, x_ref, hb_ref, h_ref = refs
    rows = tc * S5_NB
    steps = S5_RB // S5_NB
    n_blk = rows // S5_RB
    order = range(n_blk - 1, -1, -1) if backward else range(n_blk)

    @pl.when(pl.program_id(1) == 0)
    def _():
        h_ref[...] = jnp.zeros_like(h_ref)

    u = u_ref[...].reshape(rows, S5_W)
    ub = u.astype(BF16)
    for k in order:
        rs = slice(k * S5_RB, (k + 1) * S5_RB)
        x_ref[rs, :] = jnp.dot(ub[rs], bb_ref[...], preferred_element_type=F32)

    half = S5_NS // 2
    lanes = [(slice(lo, lo + half), slice(S5_NS + lo, S5_NS + lo + half)) for lo in (0, half)]
    coef = [(are_ref[:, re], aim_ref[:, re]) for re, _ in lanes]
    carry = [[(h_ref[g * SUB:(g + 1) * SUB, re], h_ref[g * SUB:(g + 1) * SUB, im])
              for g in range(S5_NB // SUB)] for re, im in lanes]
    for k in order:
        ts = range(k * steps, (k + 1) * steps)
        for t in (reversed(ts) if backward else ts):
            r0 = t * S5_NB
            for li, (re, im) in enumerate(lanes):
                ar, ai = coef[li]
                new = []
                for g, (hr, hi) in enumerate(carry[li]):
                    rw = slice(r0 + g * SUB, r0 + (g + 1) * SUB)
                    new.append((ar * hr - ai * hi + x_ref[rw, re], ar * hi + ai * hr + x_ref[rw, im]))
                carry[li] = new
                hb_ref[r0:r0 + S5_NB, re] = jnp.concatenate([n[0] for n in new], axis=0).astype(BF16)
                hb_ref[r0:r0 + S5_NB, im] = jnp.concatenate([n[1] for n in new], axis=0).astype(BF16)
        rs = slice(k * S5_RB, (k + 1) * S5_RB)
        y = jnp.dot(hb_ref[rs, :], cb_ref[...], preferred_element_type=F32)
        if backward:
            yf = yf_ref[k * steps:(k + 1) * steps].reshape(S5_RB, S5_W)
            yy = jax.nn.gelu(y + yf + dsk_ref[...] * u[rs])
            glu = jnp.dot(yy.astype(BF16), gluw_ref[...], preferred_element_type=F32) + glub_ref[...]
            y = yy * _sigmoid(glu)
        y_ref[k * steps:(k + 1) * steps] = y.reshape(steps, S5_NB, S5_W)
    for li, (re, im) in enumerate(lanes):
        for g, (hr, hi) in enumerate(carry[li]):
            h_ref[g * SUB:(g + 1) * SUB, re] = hr
            h_ref[g * SUB:(g + 1) * SUB, im] = hi


def _s5_call(u, bblk, a_re, a_im, cblk, dsk, gluw, glub, ctx_len):
    lt, b, _ = u.shape
    tc = S5_C
    n_all, n_ctx = lt // tc, ctx_len // tc
    tok = lambda d: pl.BlockSpec((tc, S5_NB, S5_W), lambda g, c: (_dir_chunk(d, c, n_ctx, n_all), g, 0))
    par = lambda d: [pl.BlockSpec((None, S5_W, 2 * S5_NS), lambda g, c: (d, 0, 0)),
                     pl.BlockSpec((None, SUB, S5_NS), lambda g, c: (d, 0, 0)),
                     pl.BlockSpec((None, SUB, S5_NS), lambda g, c: (d, 0, 0)),
                     _const_spec((2 * S5_NS, S5_W))]
    common = dict(
        grid=(b // S5_NB, n_all),
        out_shape=jax.ShapeDtypeStruct((lt, b, S5_W), F32),
        scratch_shapes=[pltpu.VMEM((tc * S5_NB, 2 * S5_NS), F32),
                        pltpu.VMEM((tc * S5_NB, 2 * S5_NS), BF16),
                        pltpu.VMEM((S5_NB, 2 * S5_NS), F32)],
        compiler_params=_cparams(2))
    yf = pl.pallas_call(
        functools.partial(_s5_kernel, tc=tc, backward=False),
        in_specs=[tok(0)] + par(0), out_specs=tok(0), **common,
    )(u, bblk, a_re, a_im, cblk)
    return pl.pallas_call(
        functools.partial(_s5_kernel, tc=tc, backward=True),
        in_specs=[tok(1)] + par(1) + [tok(1), _const_spec((1, S5_W)), _const_spec((S5_W, S5_W)),
                                      _const_spec((1, S5_W))],
        out_specs=tok(1), **common,
    )(u, bblk, a_re, a_im, cblk, yf, dsk, gluw, glub)


_NT = (((1,), (1,)), ((), ()))
_TN = (((0,), (0,)), ((), ()))


def _ret_kernel(p_ref, g_ref, dm_ref, qd_ref, kd_ref, cd_ref, hm_ref, bm_ref,
                real_ref, gn_ref, y_ref, s_ref, oc_ref, of_ref, *, n_ctx, n_all):
    d = pl.program_id(1)
    c = pl.program_id(2)
    row0 = pl.multiple_of(_dir_chunk(d, c, n_ctx, n_all) * RET_C, RET_C)

    @pl.when(c == 0)
    def _():
        s_ref[...] = jnp.zeros_like(s_ref)

    def chunk(bi):
        qb = p_ref[bi, :, 0:QK_P]
        kb = p_ref[bi, :, QK_P:2 * QK_P]
        vb = p_ref[bi, :, 2 * QK_P:2 * QK_P + V_P]
        st = s_ref[bi]
        inter = lax.dot_general(qb * qd_ref[...], st.astype(BF16) * bm_ref[...], _NT,
                                preferred_element_type=F32)
        for h in range(HEADS):
            att = lax.dot_general(qb * hm_ref[h], kb, _NT, preferred_element_type=F32)
            att = att.astype(BF16) * dm_ref[h]
            sl = slice(h * LANE, (h + 1) * LANE)
            oc_ref[bi, :, sl] = (jnp.dot(att, vb[:, sl], preferred_element_type=F32) + inter[:, sl])
        kdv = lax.dot_general(vb, kb * kd_ref[...], _TN, preferred_element_type=F32)
        s_ref[bi] = st * cd_ref[...] + kdv

    for bi in range(MIX_NB):
        chunk(bi)

    @pl.when(d == 0)
    def _():
        for bi in range(MIX_NB):
            of_ref[bi, pl.ds(row0, RET_C), :] = oc_ref[bi]

    @pl.when(d == 1)
    def _():
        for bi in range(MIX_NB):
            for h in range(HEADS):
                sl = slice(h * LANE, (h + 1) * LANE)
                o = oc_ref[bi, :, sl] + of_ref[bi, pl.ds(row0, RET_C), sl]
                mu = jnp.sum(o, axis=-1, keepdims=True) * (1.0 / DV)
                dlt = (o - mu) * real_ref[:, sl]
                var = jnp.sum(dlt * dlt, axis=-1, keepdims=True) * (1.0 / DV)
                y = dlt * lax.rsqrt(var + EPS) * gn_ref[:, sl] * _silu(g_ref[bi, :, sl].astype(F32))
                y_ref[bi, :, sl] = y.astype(BF16)


def _mixer_specs(b, lt, ctx_len, tc, off_qkv, off_g):
    n_all, n_ctx = lt // tc, ctx_len // tc
    cidx = functools.partial(_dir_chunk, n_ctx=n_ctx, n_all=n_all)
    qkv = pl.BlockSpec((MIX_NB, tc, QKV_P), lambda i, d, c: (i, cidx(d, c), off_qkv // QKV_P))
    gsp = pl.BlockSpec((MIX_NB, tc, V_P), lambda i, d, c: (i, cidx(1, c * d), off_g // V_P))
    out = pl.BlockSpec((MIX_NB, tc, V_P), lambda i, d, c: (i, cidx(1, c * d), 0))
    scratch = [pltpu.VMEM((MIX_NB, V_P, QK_P), F32),
               pltpu.VMEM((MIX_NB, tc, V_P), F32),
               pltpu.VMEM((MIX_NB, lt, V_P), F32)]
    return n_ctx, n_all, cidx, qkv, gsp, out, scratch


def _head_mask_rows(rows):
    return jnp.asarray(np.broadcast_to(TAB["head_mask"][:HEADS, None, :], (HEADS, rows, QK_P)), dtype=BF16)


def _ret_call(p, dmask, qdec, kdec, cdec, gn, ctx_len):
    b, lt, _ = p.shape
    tc = RET_C
    n_ctx, n_all, cidx, qkv, gsp, out, scratch = _mixer_specs(b, lt, ctx_len, tc, OFF_RQKV, OFF_RG)
    return pl.pallas_call(
        functools.partial(_ret_kernel, n_ctx=n_ctx, n_all=n_all),
        grid=(b // MIX_NB, 2, n_all),
        in_specs=[qkv, gsp,
                  pl.BlockSpec((None, HEADS, tc, tc), lambda i, d, c: (d, 0, 0, 0)),
                  pl.BlockSpec((None, tc, QK_P), lambda i, d, c: (d, 0, 0)),
                  pl.BlockSpec((None, tc, QK_P), lambda i, d, c: (d, 0, 0)),
                  pl.BlockSpec((None, 1, QK_P), lambda i, d, c: (d, 0, 0)),
                  _const_spec((HEADS, tc, QK_P)), _const_spec((V_P, QK_P)),
                  _const_spec((1, V_P)), _const_spec((1, V_P))],
        out_specs=out,
        out_shape=jax.ShapeDtypeStruct((b, lt, V_P), BF16),
        scratch_shapes=scratch,
        compiler_params=_cparams(3),
    )(p, p, dmask.astype(BF16), qdec.astype(BF16), kdec.astype(BF16), cdec,
      _head_mask_rows(tc), jnp.asarray(TAB["bm_t"], dtype=BF16),
      jnp.asarray(TAB["v_real"]).reshape(1, V_P), gn)


def _gla_kernel(p_ref, g_ref, z_ref, gw_ref, gb_ref, tri_ref, hm_ref, bm_ref, gn_ref,
                y_ref, s_ref, oc_ref, of_ref, *, n_ctx, n_all):
    d = pl.program_id(1)
    c = pl.program_id(2)
    row0 = pl.multiple_of(_dir_chunk(d, c, n_ctx, n_all) * GLA_R, GLA_R)

    @pl.when(c == 0)
    def _():
        s_ref[...] = jnp.zeros_like(s_ref)

    trif = tri_ref[...]
    tri = trif.astype(BF16)
    mask = trif > 0.5

    def intra(bi):
        la = jax.nn.log_sigmoid(
            jnp.dot(z_ref[bi], gw_ref[...], preferred_element_type=F32) + gb_ref[...]) / GLA_TAU
        hi = la.astype(BF16)
        lo = (la - hi.astype(F32)).astype(BF16)
        bcum = (jnp.dot(tri, hi, preferred_element_type=F32)
                + jnp.dot(tri, lo, preferred_element_type=F32))
        q = p_ref[bi, :, 0:QK_P].astype(F32)
        k = p_ref[bi, :, QK_P:2 * QK_P].astype(F32)
        vb = p_ref[bi, :, 2 * QK_P:2 * QK_P + V_P]
        qtb = (q * jnp.exp(bcum)).astype(BF16)
        ktb = (k * jnp.exp(-bcum)).astype(BF16)
        for h in range(HEADS):
            att = lax.dot_general(qtb * hm_ref[h], ktb, _NT, preferred_element_type=F32)
            att = jnp.where(mask, att.astype(BF16), jnp.zeros((), BF16))
            sl = slice(h * LANE, (h + 1) * LANE)
            oc_ref[bi, :, sl] = jnp.dot(att, vb[:, sl], preferred_element_type=F32)
        return bcum, k, qtb, vb

    parts = [intra(bi) for bi in range(MIX_NB)]

    def chain(order, last_row):
        ss = [s_ref[bi] for bi in range(MIX_NB)]
        for cc in order:
            r0 = cc * GLA_C
            rows = slice(r0, r0 + GLA_C)
            for bi, (bcum, k, qtb, vb) in enumerate(parts):
                s = ss[bi]
                bl = bcum[r0 + last_row:r0 + last_row + 1, :]
                oc_ref[bi, rows, :] += lax.dot_general(qtb[rows], s.astype(BF16) * bm_ref[...], _NT,
                                                       preferred_element_type=F32)
                kd = (k[rows] * jnp.exp(bl - bcum[rows])).astype(BF16)
                kdv = lax.dot_general(vb[rows], kd, _TN, preferred_element_type=F32)
                ss[bi] = s * jnp.exp(bl) + kdv
        for bi in range(MIX_NB):
            s_ref[bi] = ss[bi]

    n_sub = GLA_R // GLA_C

    @pl.when(d == 0)
    def _():
        chain(range(n_sub), GLA_C - 1)
        for bi in range(MIX_NB):
            of_ref[bi, pl.ds(row0, GLA_R), :] = oc_ref[bi]

    @pl.when(d == 1)
    def _():
        chain(range(n_sub - 1, -1, -1), 0)
        for bi in range(MIX_NB):
            for h in range(HEADS):
                sl = slice(h * LANE, (h + 1) * LANE)
                o = oc_ref[bi, :, sl] + of_ref[bi, pl.ds(row0, GLA_R), sl]
                ms = jnp.sum(o * o, axis=-1, keepdims=True) * (1.0 / DV)
                y = o * lax.rsqrt(ms + EPS) * gn_ref[:, sl] * _silu(g_ref[bi, :, sl].astype(F32))
                y_ref[bi, :, sl] = y.astype(BF16)


def _gla_call(p, gw, gb, tri, gn, ctx_len):
    b, lt, _ = p.shape
    tc = GLA_R
    n_ctx, n_all, cidx, qkv, gsp, out, scratch = _mixer_specs(b, lt, ctx_len, tc, OFF_GQKV, OFF_GG)
    return pl.pallas_call(
        functools.partial(_gla_kernel, n_ctx=n_ctx, n_all=n_all),
        grid=(b // MIX_NB, 2, n_all),
        in_specs=[qkv, gsp,
                  pl.BlockSpec((MIX_NB, tc, LANE), lambda i, d, c: (i, cidx(d, c), OFF_Z // LANE)),
                  pl.BlockSpec((None, LANE, QK_P), lambda i, d, c: (d, 0, 0)),
                  pl.BlockSpec((None, 1, QK_P), lambda i, d, c: (d, 0, 0)),
                  pl.BlockSpec((None, tc, tc), lambda i, d, c: (d, 0, 0)),
                  _const_spec((HEADS, tc, QK_P)), _const_spec((V_P, QK_P)),
                  _const_spec((1, V_P))],
        out_specs=out,
        out_shape=jax.ShapeDtypeStruct((b, lt, V_P), BF16),
        scratch_shapes=scratch,
        compiler_params=_cparams(3),
    )(p, p, p, gw, gb, tri, _head_mask_rows(tc), jnp.asarray(TAB["bm_t"], dtype=BF16), gn)


FFN_T = 256


def _merge_ffn_kernel(x_ref, mod_ref, ys_ref, yr_ref, yg_ref, gate_ref,
                      wbs_ref, wbr_ref, wbg_ref, wo_ref,
                      gain_ref, wa_ref, wb_ref, wfo_ref, fin_ref, o_ref, act_ref, *, final):
    def branch(y, w_ref, g0):
        gate = _sigmoid(gate_ref[:, g0:g0 + D].astype(F32))
        return gate * jnp.dot(y, w_ref[...], preferred_element_type=F32)

    m = (branch(ys_ref[...].astype(BF16), wbs_ref, 0) + branch(yr_ref[...], wbr_ref, D)
         + branch(yg_ref[...], wbg_ref, 2 * D))
    x = x_ref[...] + mod_ref[2:3, :] * jnp.dot(m.astype(BF16), wo_ref[...],
                                               preferred_element_type=F32)
    h = _rms(x, gain_ref[...])
    hb = (h * (1.0 + mod_ref[4:5, :]) + mod_ref[3:4, :]).astype(BF16)
    for t in range(FFN_H // FFN_T):
        a = jnp.dot(hb, wa_ref[t], preferred_element_type=F32)
        bq = jnp.dot(hb, wb_ref[t], preferred_element_type=F32)
        act_ref[:, t * FFN_T:(t + 1) * FFN_T] = (_silu(a) * bq).astype(BF16)
    y = x + mod_ref[5:6, :] * jnp.dot(act_ref[...], wfo_ref[...], preferred_element_type=F32)
    if final:
        y = _rms(y, fin_ref[...])
    o_ref[...] = y


def _merge_ffn_call(xs, modsel, ys5, yret, ygla, p, wbs, wbr, wbg, wo, gain, wa, wb, wfo, fin,
                    final, lat_only, ctx_len):
    b, lt, _ = xs.shape
    j0 = ctx_len // ROW_T if lat_only else 0
    nt = lt // ROW_T - j0
    nh = FFN_H // FFN_T
    tok = lambda w: pl.BlockSpec((None, ROW_T, w), lambda i, j: (i, j + j0, 0))
    return pl.pallas_call(
        functools.partial(_merge_ffn_kernel, final=final),
        grid=(b, nt),
        in_specs=[tok(D),
                  pl.BlockSpec((None, None, SUB, D), lambda i, j: (i, jnp.minimum(j + j0, 1), 0, 0)),
                  tok(S5_W), tok(V_P), tok(V_P), tok(GATE_W),
                  _const_spec((S5_W, D)), _const_spec((V_P, D)), _const_spec((V_P, D)),
                  _const_spec((D, D)),
                  _const_spec((1, D)),
                  _const_spec((nh, D, FFN_T)), _const_spec((nh, D, FFN_T)),
                  _const_spec((FFN_H, D)), _const_spec((1, D))],
        out_specs=pl.BlockSpec((None, ROW_T, D), lambda i, j: (i, j, 0)),
        out_shape=jax.ShapeDtypeStruct((b, nt * ROW_T, D), F32),
        scratch_shapes=[pltpu.VMEM((ROW_T, FFN_H), BF16)],
        compiler_params=_cparams(2),
    )(xs, modsel, ys5, yret, ygla, p, wbs, wbr, wbg, wo,
      gain.reshape(1, D), wa, wb, wfo, fin.reshape(1, D))


def _s5_params(lam_re, lam_im, log_dt, b_re, b_im, c_re, c_im):
    dt = jnp.exp(log_dt)[..., None]
    mag = jnp.exp(lam_re * dt)
    a_re, a_im = mag * jnp.cos(lam_im * dt), mag * jnp.sin(lam_im * dt)
    den = lam_re * lam_re + lam_im * lam_im
    f_re = ((a_re - 1.0) * lam_re + a_im * lam_im) / den
    f_im = (a_im * lam_re - (a_re - 1.0) * lam_im) / den
    bb_re = f_re[..., None] * b_re - f_im[..., None] * b_im
    bb_im = f_re[..., None] * b_im + f_im[..., None] * b_re
    eye = jnp.eye(S5_GROUPS, dtype=F32)

    def blk_in(bb):
        t = jnp.einsum("dgpc,gh->dgchp", bb, eye)
        return t.reshape(2, S5_W, S5_NS)

    bblk = jnp.concatenate([blk_in(bb_re), blk_in(bb_im)], axis=-1).astype(BF16)

    def blk_out(cc):
        t = jnp.einsum("gcp,gh->gphc", cc, eye)
        return t.reshape(S5_NS, S5_W)

    cblk = jnp.concatenate([blk_out(c_re), -blk_out(c_im)], axis=0).astype(BF16)
    bc = lambda a: jnp.broadcast_to(a.reshape(2, 1, S5_NS), (2, SUB, S5_NS))
    return bblk, bc(a_re), bc(a_im), cblk


def _ret_tables(log_decay):
    tc = RET_C
    pos = jnp.arange(tc, dtype=F32)
    w = jnp.stack([pos, tc - 1.0 - pos])
    rel = w[:, :, None] - w[:, None, :]
    lg = log_decay[:, :, None, None]
    dmask = jnp.where(rel[:, None] >= 0, jnp.exp(jnp.maximum(rel[:, None], 0.0) * lg), 0.0)
    qk_head = TAB["qk_head"]
    lane_lg = jnp.where(jnp.asarray(qk_head >= 0),
                        jnp.take(log_decay, jnp.asarray(np.maximum(qk_head, 0)), axis=1), 0.0)
    qdec = jnp.exp((w[:, :, None] + 1.0) * lane_lg[:, None, :])
    kdec = jnp.exp((tc - 1.0 - w[:, :, None]) * lane_lg[:, None, :])
    cdec = jnp.exp(tc * lane_lg)[:, None, :]
    return dmask, qdec, kdec, cdec


def _rope_tables(seq, ctx_len):
    rows = seq // GRID_W
    nf = DK // 4
    inv = 1.0 / (ROPE_BASE ** (np.arange(nf, dtype=np.float32) / nf))
    r = np.repeat(np.arange(rows, dtype=np.float32), GRID_W)
    col = np.tile(np.arange(GRID_W, dtype=np.float32), rows)
    ang = np.concatenate([r[:, None] * inv, col[:, None] * inv], axis=-1)
    cos_t = np.ones((ctx_len + seq, LANE), np.float32)
    sin_t = np.zeros((ctx_len + seq, LANE), np.float32)
    for h in range(HEADS):
        cos_t[ctx_len:, h * 24:(h + 1) * 24] = np.cos(ang)
        sin_t[ctx_len:, h * 24:(h + 1) * 24] = np.sin(ang)
    return jnp.asarray(cos_t), jnp.asarray(sin_t)


def _gla_tables(gate_w, gate_b):
    gw = _gather_cols(gate_w, TAB["qk_src"])
    gw_full = jnp.zeros((2, LANE, QK_P), F32)
    gw_full = gw_full.at[0, 0:GLA_RANK].set(gw[0]).at[1, GLA_RANK:2 * GLA_RANK].set(gw[1])
    gb = _gather_cols(gate_b, TAB["qk_src"])[:, None, :]
    i = np.arange(GLA_R)
    same = (i[:, None] // GLA_C) == (i[None, :] // GLA_C)
    tri = np.stack([same & (i[:, None] >= i[None, :]), same & (i[:, None] <= i[None, :])]).astype(np.float32)
    return gw_full.astype(BF16), gb, jnp.asarray(tri)


def _pad_rows(w, src):
    return _gather_cols(w.T, src).T


def kernel(x, c, ctx, c_ctx, w_mod, b_mod, norm_mix, norm_ffn, w_in, s5_lam_re, s5_lam_im, s5_log_dt, s5_b_re, s5_b_im, s5_c_re, s5_c_im, s5_d, s5_glu_w, s5_glu_b, ret_log_decay, ret_gn, gla_gate_w, gla_gate_b, gla_norm, w_br_s5, w_br_ret, w_br_gla, w_out, w_ffn_in, w_ffn_out, norm_final):
    b, seq, _ = x.shape
    ctx_len = ctx.shape[1]
    depth = w_mod.shape[0]
    assert b % S5_NB == 0 and ctx_len % RET_C == 0 and seq % RET_C == 0 and seq % GRID_W == 0

    xs = jnp.concatenate([ctx, x], axis=1)
    rows = ((b + 1 + SUB - 1) // SUB) * SUB
    c_all = jnp.zeros((rows, D), F32).at[:b].set(c).at[b].set(c_ctx)
    cos_t, sin_t = _rope_tables(seq, ctx_len)
    nh = FFN_H // FFN_T

    for l in range(depth):
        last = l == depth - 1
        mods = _mod_call(c_all, w_mod[l], b_mod[l])
        m_lat = mods[:b].reshape(b, N_MOD, D)
        m_ctx = jnp.broadcast_to(mods[b].reshape(1, N_MOD, D), (b, N_MOD, D))
        modsel = jnp.stack([m_ctx, m_lat], axis=1)
        modsel = jnp.pad(modsel, ((0, 0), (0, 0), (0, SUB - N_MOD), (0, 0)))

        w_all = _gather_cols(w_in[l], TAB["src"]).astype(BF16)
        p, u = _inproj_call(xs, modsel, norm_mix[l], cos_t, sin_t, w_all)

        bblk, a_re, a_im, cblk = _s5_params(s5_lam_re[l], s5_lam_im[l], s5_log_dt[l],
                                            s5_b_re[l], s5_b_im[l], s5_c_re[l], s5_c_im[l])
        ys5 = _s5_call(jnp.transpose(u, (1, 0, 2)), bblk, a_re, a_im, cblk,
                       s5_d[l].reshape(1, S5_W), s5_glu_w[l].astype(BF16),
                       s5_glu_b[l].reshape(1, S5_W), ctx_len)
        ys5 = jnp.transpose(ys5, (1, 0, 2))

        dmask, qdec, kdec, cdec = _ret_tables(ret_log_decay[l])
        yret = _ret_call(p, dmask, qdec, kdec, cdec,
                         _gather_cols(ret_gn[l], TAB["vsrc"]).reshape(1, V_P), ctx_len)

        gw, gb, tri = _gla_tables(gla_gate_w[l], gla_gate_b[l])
        ygla = _gla_call(p, gw, gb, tri, _gather_cols(gla_norm[l], TAB["vsrc"]).reshape(1, V_P),
                         ctx_len)

        wfi = w_ffn_in[l].astype(BF16)
        wa = wfi[:, :FFN_H].reshape(D, nh, FFN_T).transpose(1, 0, 2)
        wb = wfi[:, FFN_H:].reshape(D, nh, FFN_T).transpose(1, 0, 2)
        xs = _merge_ffn_call(xs, modsel, ys5, yret, ygla, p, w_br_s5[l].astype(BF16),
                             _pad_rows(w_br_ret[l], TAB["vsrc"]).astype(BF16),
                             _pad_rows(w_br_gla[l], TAB["vsrc"]).astype(BF16),
                             w_out[l].astype(BF16), norm_ffn[l], wa, wb,
                             w_ffn_out[l].astype(BF16), norm_final, last, last, ctx_len)
    return xs
```

```python
import functools

import numpy as np
import jax
import jax.numpy as jnp
from jax import lax
from jax.experimental import pallas as pl
from jax.experimental.pallas import tpu as pltpu

F32 = jnp.float32
BF16 = jnp.bfloat16

D = 1024
EPS = 1e-6
N_MOD = 6
GRID_W = 64
S5_W = 256
S5_GROUPS = 16
S5_STATE = 64
S5_NS = S5_GROUPS * S5_STATE
HEADS = 4
DK = 48
DV = 96
QK_W = HEADS * DK
V_W = HEADS * DV
GLA_RANK = 16
GLA_TAU = 16.0
ROPE_BASE = 10000.0
FFN_H = 2816

LANE = 128
SUB = 8
QK_P = 2 * LANE
V_P = HEADS * LANE
QKV_P = 2 * QK_P + V_P
GATE_W = 3 * D

OFF_GATE = 0
OFF_RQKV = GATE_W
OFF_GQKV = OFF_RQKV + QKV_P
OFF_RG = OFF_GQKV + QKV_P
OFF_GG = OFF_RG + V_P
OFF_Z = OFF_GG + V_P
NP = OFF_Z + LANE
OFF_S5 = NP
NW = NP + S5_W

ROW_T = 256
RET_C = 256
GLA_C = 64
GLA_R = 256
MIX_NB = 4
S5_C = 64
VMEM_LIMIT = 56 * 1024 * 1024

IN_SIZES = (S5_W, QK_W, QK_W, V_W, V_W, QK_W, QK_W, V_W, V_W, GLA_RANK, GLA_RANK, D, D, D)
IN_OFFS = np.concatenate([[0], np.cumsum(IN_SIZES)]).astype(np.int64)


def _head_lane():
    m = np.zeros((HEADS, DK), np.int64)
    for h in range(HEADS):
        for i in range(DK):
            m[h, i] = (i % 2) * LANE + h * (DK // 2) + i // 2
    return m


HEAD_LANE = _head_lane()


def _static_tables():
    src = np.full((NW,), -1, np.int64)
    src[OFF_GATE:OFF_GATE + GATE_W] = IN_OFFS[11] + np.arange(GATE_W)
    for qkv, gg, iq, ik, iv, ig in ((OFF_RQKV, OFF_RG, 1, 2, 3, 4), (OFF_GQKV, OFF_GG, 5, 6, 7, 8)):
        for h in range(HEADS):
            for i in range(DK):
                src[qkv + HEAD_LANE[h, i]] = IN_OFFS[iq] + h * DK + i
                src[qkv + QK_P + HEAD_LANE[h, i]] = IN_OFFS[ik] + h * DK + i
            for j in range(DV):
                src[qkv + 2 * QK_P + h * LANE + j] = IN_OFFS[iv] + h * DV + j
                src[gg + h * LANE + j] = IN_OFFS[ig] + h * DV + j
    src[OFF_Z:OFF_Z + GLA_RANK] = IN_OFFS[9] + np.arange(GLA_RANK)
    src[OFF_Z + GLA_RANK:OFF_Z + 2 * GLA_RANK] = IN_OFFS[10] + np.arange(GLA_RANK)
    src[OFF_S5:OFF_S5 + S5_W] = np.arange(S5_W)
    vsrc = np.full((V_P,), -1, np.int64)
    for h in range(HEADS):
        vsrc[h * LANE:h * LANE + DV] = h * DV + np.arange(DV)
    qk_head = np.full((QK_P,), -1, np.int64)
    qk_src = np.full((QK_P,), -1, np.int64)
    for h in range(HEADS):
        for i in range(DK):
            qk_head[HEAD_LANE[h, i]] = h
            qk_src[HEAD_LANE[h, i]] = h * DK + i
    v_head = np.repeat(np.arange(HEADS), LANE)
    v_real = (np.arange(V_P) % LANE) < DV
    head_mask = np.zeros((SUB, QK_P), np.float32)
    for h in range(HEADS):
        head_mask[h] = (qk_head == h)
    bm_t = (v_head[:, None] == qk_head[None, :]).astype(np.float32)
    return dict(src=src, vsrc=vsrc, qk_head=qk_head, qk_src=qk_src, head_mask=head_mask,
                bm_t=bm_t, v_real=v_real.astype(np.float32))


TAB = _static_tables()


def _gather_cols(w, src):
    valid = jnp.asarray(src >= 0)
    out = jnp.take(w, jnp.asarray(np.maximum(src, 0)), axis=-1)
    return jnp.where(valid, out, jnp.zeros((), w.dtype))


def _cparams(n_axes):
    return pltpu.CompilerParams(dimension_semantics=("arbitrary",) * n_axes,
                                vmem_limit_bytes=VMEM_LIMIT)


def _const_spec(shape):
    nd = len(shape)
    return pl.BlockSpec(shape, lambda *_: (0,) * nd, pipeline_mode=pl.Buffered(1))


def _dir_chunk(d, c, n_ctx, n_all):
    bwd = jnp.where(c < n_ctx, n_ctx - 1 - c, n_all + n_ctx - 1 - c)
    return jnp.where(d == 0, c, bwd)


def _sigmoid(x):
    return 0.5 * jnp.tanh(0.5 * x) + 0.5


def _silu(x):
    return x * _sigmoid(x)


def _mod_kernel(c_ref, w_ref, b_ref, o_ref):
    o_ref[...] = jnp.dot(_silu(c_ref[...]), w_ref[...], preferred_element_type=F32,
                         precision=lax.Precision.HIGHEST) + b_ref[...]


def _mod_call(c_all, w_mod_l, b_mod_l):
    rows = c_all.shape[0]
    tn = D
    return pl.pallas_call(
        _mod_kernel,
        grid=(N_MOD * D // tn,),
        in_specs=[pl.BlockSpec((rows, D), lambda j: (0, 0)),
                  pl.BlockSpec((D, tn), lambda j: (0, j)),
                  pl.BlockSpec((1, tn), lambda j: (0, j))],
        out_specs=pl.BlockSpec((rows, tn), lambda j: (0, j)),
        out_shape=jax.ShapeDtypeStruct((rows, N_MOD * D), F32),
        compiler_params=_cparams(1),
    )(c_all, w_mod_l, b_mod_l.reshape(1, -1))


def _rms(x, gain):
    return x * lax.rsqrt(jnp.mean(x * x, axis=-1, keepdims=True) + EPS) * gain


def _residual_rows(refs, split):
    if not split:
        return refs[0][...]
    return jnp.where(pl.program_id(1) == 0, refs[0][...], refs[1][...])


def _residual_specs(split, j0):
    if not split:
        return [pl.BlockSpec((None, ROW_T, D), lambda i, j: (i, j + j0, 0))]
    return [pl.BlockSpec((None, ROW_T, D), lambda i, j: (i, 0, 0)),
            pl.BlockSpec((None, ROW_T, D), lambda i, j: (i, jnp.maximum(j - 1, 0), 0))]


def _inproj_kernel(*refs, split):
    mod_ref, gain_ref, cos_ref, sin_ref, w_ref, p_ref, u_ref = refs[1 + split:]
    h = _rms(_residual_rows(refs, split), gain_ref[...])
    h = h * (1.0 + mod_ref[1:2, :]) + mod_ref[0:1, :]
    hb = h.astype(BF16)

    def proj(c0, c1):
        return jnp.dot(hb, w_ref[:, c0:c1], preferred_element_type=F32)

    for c0 in range(0, GATE_W, D):
        p_ref[:, c0:c0 + D] = proj(c0, c0 + D).astype(BF16)
    cs = cos_ref[...]
    sn = sin_ref[...]
    qk = proj(OFF_RQKV, OFF_RQKV + 2 * QK_P)
    for o, scale in ((0, 1.0), (QK_P, DK ** -0.5)):
        a = qk[:, o:o + LANE] * scale
        b = qk[:, o + LANE:o + 2 * LANE] * scale
        p_ref[:, OFF_RQKV + o:OFF_RQKV + o + LANE] = (a * cs - b * sn).astype(BF16)
        p_ref[:, OFF_RQKV + o + LANE:OFF_RQKV + o + 2 * LANE] = (a * sn + b * cs).astype(BF16)
    c0 = OFF_RQKV + 2 * QK_P
    p_ref[:, c0:OFF_GQKV] = proj(c0, OFF_GQKV).astype(BF16)
    p_ref[:, OFF_GQKV:OFF_GQKV + QK_P] = (proj(OFF_GQKV, OFF_GQKV + QK_P) * (DK ** -0.5)).astype(BF16)
    c0 = OFF_GQKV + QK_P
    p_ref[:, c0:OFF_RG] = proj(c0, OFF_RG).astype(BF16)
    p_ref[:, OFF_RG:NP] = proj(OFF_RG, NP).astype(BF16)
    u_ref[...] = proj(OFF_S5, OFF_S5 + S5_W)


def _inproj_call(xs, modsel, gain, cos_t, sin_t, w_all):
    split = isinstance(xs, tuple)
    xs = xs if split else (xs,)
    b = xs[0].shape[0]
    lt = sum(t.shape[1] for t in xs)
    nt = lt // ROW_T
    return pl.pallas_call(
        functools.partial(_inproj_kernel, split=split),
        grid=(b, nt),
        in_specs=_residual_specs(split, 0) + [
                  pl.BlockSpec((None, None, SUB, D), lambda i, j: (i, jnp.minimum(j, 1), 0, 0)),
                  _const_spec((1, D)),
                  pl.BlockSpec((ROW_T, LANE), lambda i, j: (j, 0)),
                  pl.BlockSpec((ROW_T, LANE), lambda i, j: (j, 0)),
                  _const_spec((D, NW))],
        out_specs=[pl.BlockSpec((None, ROW_T, NP), lambda i, j: (i, j, 0)),
                   pl.BlockSpec((None, ROW_T, S5_W), lambda i, j: (i, j, 0))],
        out_shape=[jax.ShapeDtypeStruct((b, lt, NP), BF16),
                   jax.ShapeDtypeStruct((b, lt, S5_W), F32)],
        compiler_params=_cparams(2),
    )(*xs, modsel, gain.reshape(1, D), cos_t, sin_t, w_all)


S5_NB = 2 * SUB
S5_RB = 256


def _s5_kernel(*refs, tc, backward):
    if backward:
        (u_ref, bb_ref, are_ref, aim_ref, cb_ref, yf_ref, dsk_ref, gluw_ref, glub_ref,
         y_ref, x_ref, hb_ref, h_ref) = refs
    else:
        u_ref, bb_ref, are_ref, aim_ref, cb_ref, y_ref, x_ref, hb_ref, h_ref = refs
    rows = tc * S5_NB
    steps = S5_RB // S5_NB
    n_blk = rows // S5_RB
    order = range(n_blk - 1, -1, -1) if backward else range(n_blk)

    @pl.when(pl.program_id(1) == 0)
    def _():
        h_ref[...] = jnp.zeros_like(h_ref)

    u = u_ref[...].reshape(rows, S5_W)
    ub = u.astype(BF16)
    for k in order:
        rs = slice(k * S5_RB, (k + 1) * S5_RB)
        x_ref[rs, :] = jnp.dot(ub[rs], bb_ref[...], preferred_element_type=F32)

    half = S5_NS // 2
    lanes = [(slice(lo, lo + half), slice(S5_NS + lo, S5_NS + lo + half)) for lo in (0, half)]
    coef = [(are_ref[:, re], aim_ref[:, re]) for re, _ in lanes]
    carry = [[(h_ref[g * SUB:(g + 1) * SUB, re], h_ref[g * SUB:(g + 1) * SUB, im])
              for g in range(S5_NB // SUB)] for re, im in lanes]
    for k in order:
        ts = range(k * steps, (k + 1) * steps)
        for t in (reversed(ts) if backward else ts):
            r0 = t * S5_NB
            for li, (re, im) in enumerate(lanes):
                ar, ai = coef[li]
                new = []
                for g, (hr, hi) in enumerate(carry[li]):
                    rw = slice(r0 + g * SUB, r0 + (g + 1) * SUB)
                    new.append((ar * hr - ai * hi + x_ref[rw, re], ar * hi + ai * hr + x_ref[rw, im]))
                carry[li] = new
                hb_ref[r0:r0 + S5_NB, re] = jnp.concatenate([n[0] for n in new], axis=0).astype(BF16)
                hb_ref[r0:r0 + S5_NB, im] = jnp.concatenate([n[1] for n in new], axis=0).astype(BF16)
        rs = slice(k * S5_RB, (k + 1) * S5_RB)
        y = jnp.dot(hb_ref[rs, :], cb_ref[...], preferred_element_type=F32)
        if backward:
            yf = yf_ref[k * steps:(k + 1) * steps].reshape(S5_RB, S5_W)
            yy = jax.nn.gelu(y + yf + dsk_ref[...] * u[rs])
            glu = jnp.dot(yy.astype(BF16), gluw_ref[...], preferred_element_type=F32) + glub_ref[...]
            y = yy * _sigmoid(glu)
        y_ref[k * steps:(k + 1) * steps] = y.reshape(steps, S5_NB, S5_W)
    for li, (re, im) in enumerate(lanes):
        for g, (hr, hi) in enumerate(carry[li]):
            h_ref[g * SUB:(g + 1) * SUB, re] = hr
            h_ref[g * SUB:(g + 1) * SUB, im] = hi


def _s5_call(u, bblk, a_re, a_im, cblk, dsk, gluw, glub, ctx_len):
    lt, b, _ = u.shape
    tc = S5_C
    n_all, n_ctx = lt // tc, ctx_len // tc
    tok = lambda d: pl.BlockSpec((tc, S5_NB, S5_W), lambda g, c: (_dir_chunk(d, c, n_ctx, n_all), g, 0))
    par = lambda d: [pl.BlockSpec((None, S5_W, 2 * S5_NS), lambda g, c: (d, 0, 0)),
                     pl.BlockSpec((None, SUB, S5_NS), lambda g, c: (d, 0, 0)),
                     pl.BlockSpec((None, SUB, S5_NS), lambda g, c: (d, 0, 0)),
                     _const_spec((2 * S5_NS, S5_W))]
    common = dict(
        grid=(b // S5_NB, n_all),
        out_shape=jax.ShapeDtypeStruct((lt, b, S5_W), F32),
        scratch_shapes=[pltpu.VMEM((tc * S5_NB, 2 * S5_NS), F32),
                        pltpu.VMEM((tc * S5_NB, 2 * S5_NS), BF16),
                        pltpu.VMEM((S5_NB, 2 * S5_NS), F32)],
        compiler_params=_cparams(2))
    yf = pl.pallas_call(
        functools.partial(_s5_kernel, tc=tc, backward=False),
        in_specs=[tok(0)] + par(0), out_specs=tok(0), **common,
    )(u, bblk, a_re, a_im, cblk)
    return pl.pallas_call(
        functools.partial(_s5_kernel, tc=tc, backward=True),
        in_specs=[tok(1)] + par(1) + [tok(1), _const_spec((1, S5_W)), _const_spec((S5_W, S5_W)),
                                      _const_spec((1, S5_W))],
        out_specs=tok(1), **common,
    )(u, bblk, a_re, a_im, cblk, yf, dsk, gluw, glub)


_NT = (((1,), (1,)), ((), ()))
_TN = (((0,), (0,)), ((), ()))


def _ret_kernel(p_ref, g_ref, dm_ref, qd_ref, kd_ref, cd_ref, hm_ref, bm_ref,
                real_ref, gn_ref, y_ref, s_ref, oc_ref, of_ref, *, n_ctx, n_all):
    d = pl.program_id(1)
    c = pl.program_id(2)
    row0 = pl.multiple_of(_dir_chunk(d, c, n_ctx, n_all) * RET_C, RET_C)

    @pl.when(c == 0)
    def _():
        s_ref[...] = jnp.zeros_like(s_ref)

    def chunk(bi):
        qb = p_ref[bi, :, 0:QK_P]
        kb = p_ref[bi, :, QK_P:2 * QK_P]
        vb = p_ref[bi, :, 2 * QK_P:2 * QK_P + V_P]
        st = s_ref[bi]
        inter = lax.dot_general(qb * qd_ref[...], st.astype(BF16) * bm_ref[...], _NT,
                                preferred_element_type=F32)
        for h in range(HEADS):
            att = lax.dot_general(qb * hm_ref[h], kb, _NT, preferred_element_type=F32)
            att = att.astype(BF16) * dm_ref[h]
            sl = slice(h * LANE, (h + 1) * LANE)
            oc_ref[bi, :, sl] = (jnp.dot(att, vb[:, sl], preferred_element_type=F32) + inter[:, sl])
        kdv = lax.dot_general(vb, kb * kd_ref[...], _TN, preferred_element_type=F32)
        s_ref[bi] = st * cd_ref[...] + kdv

    for bi in range(MIX_NB):
        chunk(bi)

    @pl.when(d == 0)
    def _():
        for bi in range(MIX_NB):
            of_ref[bi, pl.ds(row0, RET_C), :] = oc_ref[bi]

    @pl.when(d == 1)
    def _():
        for bi in range(MIX_NB):
            for h in range(HEADS):
                sl = slice(h * LANE, (h + 1) * LANE)
                o = oc_ref[bi, :, sl] + of_ref[bi, pl.ds(row0, RET_C), sl]
                mu = jnp.sum(o, axis=-1, keepdims=True) * (1.0 / DV)
                dlt = (o - mu) * real_ref[:, sl]
                var = jnp.sum(dlt * dlt, axis=-1, keepdims=True) * (1.0 / DV)
                y = dlt * lax.rsqrt(var + EPS) * gn_ref[:, sl] * _silu(g_ref[bi, :, sl].astype(F32))
                y_ref[bi, :, sl] = y.astype(BF16)


def _mixer_specs(b, lt, ctx_len, tc, off_qkv, off_g):
    n_all, n_ctx = lt // tc, ctx_len // tc
    cidx = functools.partial(_dir_chunk, n_ctx=n_ctx, n_all=n_all)
    qkv = pl.BlockSpec((MIX_NB, tc, QKV_P), lambda i, d, c: (i, cidx(d, c), off_qkv // QKV_P))
    gsp = pl.BlockSpec((MIX_NB, tc, V_P), lambda i, d, c: (i, cidx(1, c * d), off_g // V_P))
    out = pl.BlockSpec((MIX_NB, tc, V_P), lambda i, d, c: (i, cidx(1, c * d), 0))
    scratch = [pltpu.VMEM((MIX_NB, V_P, QK_P), F32),
               pltpu.VMEM((MIX_NB, tc, V_P), F32),
               pltpu.VMEM((MIX_NB, lt, V_P), F32)]
    return n_ctx, n_all, cidx, qkv, gsp, out, scratch


def _head_mask_rows(rows):
    return jnp.asarray(np.broadcast_to(TAB["head_mask"][:HEADS, None, :], (HEADS, rows, QK_P)), dtype=BF16)


def _ret_call(p, dmask, qdec, kdec, cdec, gn, ctx_len):
    b, lt, _ = p.shape
    tc = RET_C
    n_ctx, n_all, cidx, qkv, gsp, out, scratch = _mixer_specs(b, lt, ctx_len, tc, OFF_RQKV, OFF_RG)
    return pl.pallas_call(
        functools.partial(_ret_kernel, n_ctx=n_ctx, n_all=n_all),
        grid=(b // MIX_NB, 2, n_all),
        in_specs=[qkv, gsp,
                  pl.BlockSpec((None, HEADS, tc, tc), lambda i, d, c: (d, 0, 0, 0)),
                  pl.BlockSpec((None, tc, QK_P), lambda i, d, c: (d, 0, 0)),
                  pl.BlockSpec((None, tc, QK_P), lambda i, d, c: (d, 0, 0)),
                  pl.BlockSpec((None, 1, QK_P), lambda i, d, c: (d, 0, 0)),
                  _const_spec((HEADS, tc, QK_P)), _const_spec((V_P, QK_P)),
                  _const_spec((1, V_P)), _const_spec((1, V_P))],
        out_specs=out,
        out_shape=jax.ShapeDtypeStruct((b, lt, V_P), BF16),
        scratch_shapes=scratch,
        compiler_params=_cparams(3),
    )(p, p, dmask.astype(BF16), qdec.astype(BF16), kdec.astype(BF16), cdec,
      _head_mask_rows(tc), jnp.asarray(TAB["bm_t"], dtype=BF16),
      jnp.asarray(TAB["v_real"]).reshape(1, V_P), gn)


def _gla_kernel(*refs, backward):
    if backward:
        (p_ref, z_ref, gw_ref, gb_ref, tri_ref, hm_ref, bm_ref, of_ref, g_ref, gn_ref,
         y_ref, s_ref, oc_ref) = refs
    else:
        p_ref, z_ref, gw_ref, gb_ref, tri_ref, hm_ref, bm_ref, oc_ref, s_ref = refs

    @pl.when(pl.program_id(1) == 0)
    def _():
        s_ref[...] = jnp.zeros_like(s_ref)

    trif = tri_ref[...]
    tri = trif.astype(BF16)
    mask = trif > 0.5

    def intra(bi):
        la = jax.nn.log_sigmoid(
            jnp.dot(z_ref[bi], gw_ref[...], preferred_element_type=F32) + gb_ref[...]) / GLA_TAU
        hi = la.astype(BF16)
        lo = (la - hi.astype(F32)).astype(BF16)
        bcum = (jnp.dot(tri, hi, preferred_element_type=F32)
                + jnp.dot(tri, lo, preferred_element_type=F32))
        q = p_ref[bi, :, 0:QK_P].astype(F32)
        k = p_ref[bi, :, QK_P:2 * QK_P].astype(F32)
        vb = p_ref[bi, :, 2 * QK_P:2 * QK_P + V_P]
        qtb = (q * jnp.exp(bcum)).astype(BF16)
        ktb = (k * jnp.exp(-bcum)).astype(BF16)
        for h in range(HEADS):
            att = lax.dot_general(qtb * hm_ref[h], ktb, _NT, preferred_element_type=F32)
            att = jnp.where(mask, att.astype(BF16), jnp.zeros((), BF16))
            sl = slice(h * LANE, (h + 1) * LANE)
            oc_ref[bi, :, sl] = jnp.dot(att, vb[:, sl], preferred_element_type=F32)
        return bcum, k, qtb, vb

    parts = [intra(bi) for bi in range(MIX_NB)]

    n_sub = GLA_R // GLA_C
    order = range(n_sub - 1, -1, -1) if backward else range(n_sub)
    last_row = 0 if backward else GLA_C - 1
    ss = [s_ref[bi] for bi in range(MIX_NB)]
    for cc in order:
        r0 = cc * GLA_C
        rows = slice(r0, r0 + GLA_C)
        for bi, (bcum, k, qtb, vb) in enumerate(parts):
            s = ss[bi]
            bl = bcum[r0 + last_row:r0 + last_row + 1, :]
            oc_ref[bi, rows, :] += lax.dot_general(qtb[rows], s.astype(BF16) * bm_ref[...], _NT,
                                                   preferred_element_type=F32)
            kd = (k[rows] * jnp.exp(bl - bcum[rows])).astype(BF16)
            kdv = lax.dot_general(vb[rows], kd, _TN, preferred_element_type=F32)
            ss[bi] = s * jnp.exp(bl) + kdv
    for bi in range(MIX_NB):
        s_ref[bi] = ss[bi]

    if backward:
        for bi in range(MIX_NB):
            for h in range(HEADS):
                sl = slice(h * LANE, (h + 1) * LANE)
                o = oc_ref[bi, :, sl] + of_ref[bi, :, sl]
                ms = jnp.sum(o * o, axis=-1, keepdims=True) * (1.0 / DV)
                y = o * lax.rsqrt(ms + EPS) * gn_ref[:, sl] * _silu(g_ref[bi, :, sl].astype(F32))
                y_ref[bi, :, sl] = y.astype(BF16)


def _gla_call(p, gw, gb, tri, gn, ctx_len):
    b, lt, _ = p.shape
    tc = GLA_R
    n_all, n_ctx = lt // tc, ctx_len // tc
    chunk = lambda d: (lambda c: _dir_chunk(d, c, n_ctx, n_all))
    tok = lambda d, w, blk: pl.BlockSpec((MIX_NB, tc, w), lambda i, c: (i, chunk(d)(c), blk))
    ins = lambda d: [tok(d, QKV_P, OFF_GQKV // QKV_P), tok(d, LANE, OFF_Z // LANE),
                     _const_spec((LANE, QK_P)), _const_spec((1, QK_P)), _const_spec((tc, tc)),
                     _const_spec((HEADS, tc, QK_P)), _const_spec((V_P, QK_P))]
    hm, bm = _head_mask_rows(tc), jnp.asarray(TAB["bm_t"], dtype=BF16)
    state = pltpu.VMEM((MIX_NB, V_P, QK_P), F32)
    of = pl.pallas_call(
        functools.partial(_gla_kernel, backward=False),
        grid=(b // MIX_NB, n_all),
        in_specs=ins(0), out_specs=tok(0, V_P, 0),
        out_shape=jax.ShapeDtypeStruct((b, lt, V_P), F32),
        scratch_shapes=[state],
        compiler_params=_cparams(2),
    )(p, p, gw[0], gb[0], tri[0], hm, bm)
    return pl.pallas_call(
        functools.partial(_gla_kernel, backward=True),
        grid=(b // MIX_NB, n_all),
        in_specs=ins(1) + [tok(1, V_P, 0), tok(1, V_P, OFF_GG // V_P), _const_spec((1, V_P))],
        out_specs=tok(1, V_P, 0),
        out_shape=jax.ShapeDtypeStruct((b, lt, V_P), BF16),
        scratch_shapes=[state, pltpu.VMEM((MIX_NB, tc, V_P), F32)],
        compiler_params=_cparams(2),
    )(p, p, gw[1], gb[1], tri[1], hm, bm, of, p, gn)


FFN_T = 256


def _merge_ffn_kernel(*refs, final, split):
    (mod_ref, ys_ref, yr_ref, yg_ref, gate_ref, wbs_ref, wbr_ref, wbg_ref, wo_ref,
     gain_ref, wa_ref, wb_ref, wfo_ref, fin_ref, o_ref, act_ref) = refs[1 + split:]

    def branch(y, w_ref, g0):
        gate = _sigmoid(gate_ref[:, g0:g0 + D].astype(F32))
        return gate * jnp.dot(y, w_ref[...], preferred_element_type=F32)

    m = (branch(ys_ref[...].astype(BF16), wbs_ref, 0) + branch(yr_ref[...], wbr_ref, D)
         + branch(yg_ref[...], wbg_ref, 2 * D))
    x = _residual_rows(refs, split) + mod_ref[2:3, :] * jnp.dot(m.astype(BF16), wo_ref[...],
                                                                  preferred_element_type=F32)
    h = _rms(x, gain_ref[...])
    hb = (h * (1.0 + mod_ref[4:5, :]) + mod_ref[3:4, :]).astype(BF16)
    for t in range(FFN_H // FFN_T):
        a = jnp.dot(hb, wa_ref[t], preferred_element_type=F32)
        bq = jnp.dot(hb, wb_ref[t], preferred_element_type=F32)
        act_ref[:, t * FFN_T:(t + 1) * FFN_T] = (_silu(a) * bq).astype(BF16)
    y = x + mod_ref[5:6, :] * jnp.dot(act_ref[...], wfo_ref[...], preferred_element_type=F32)
    if final:
        y = _rms(y, fin_ref[...])
    o_ref[...] = y


def _merge_ffn_call(xs, modsel, ys5, yret, ygla, p, wbs, wbr, wbg, wo, gain, wa, wb, wfo, fin,
                    final, lat_only, ctx_len):
    split = isinstance(xs, tuple)
    xs = xs if split else (xs,)
    b = xs[0].shape[0]
    lt = sum(t.shape[1] for t in xs)
    j0 = ctx_len // ROW_T if lat_only else 0
    nt = lt // ROW_T - j0
    nh = FFN_H // FFN_T
    tok = lambda w: pl.BlockSpec((None, ROW_T, w), lambda i, j: (i, j + j0, 0))
    return pl.pallas_call(
        functools.partial(_merge_ffn_kernel, final=final, split=split),
        grid=(b, nt),
        in_specs=_residual_specs(split, j0) + [
                  pl.BlockSpec((None, None, SUB, D), lambda i, j: (i, jnp.minimum(j + j0, 1), 0, 0)),
                  tok(S5_W), tok(V_P), tok(V_P), tok(GATE_W),
                  _const_spec((S5_W, D)), _const_spec((V_P, D)), _const_spec((V_P, D)),
                  _const_spec((D, D)),
                  _const_spec((1, D)),
                  _const_spec((nh, D, FFN_T)), _const_spec((nh, D, FFN_T)),
                  _const_spec((FFN_H, D)), _const_spec((1, D))],
        out_specs=pl.BlockSpec((None, ROW_T, D), lambda i, j: (i, j, 0)),
        out_shape=jax.ShapeDtypeStruct((b, nt * ROW_T, D), F32),
        scratch_shapes=[pltpu.VMEM((ROW_T, FFN_H), BF16)],
        compiler_params=_cparams(2),
    )(*xs, modsel, ys5, yret, ygla, p, wbs, wbr, wbg, wo,
      gain.reshape(1, D), wa, wb, wfo, fin.reshape(1, D))


def _s5_params(lam_re, lam_im, log_dt, b_re, b_im, c_re, c_im):
    dt = jnp.exp(log_dt)[..., None]
    mag = jnp.exp(lam_re * dt)
    a_re, a_im = mag * jnp.cos(lam_im * dt), mag * jnp.sin(lam_im * dt)
    den = lam_re * lam_re + lam_im * lam_im
    f_re = ((a_re - 1.0) * lam_re + a_im * lam_im) / den
    f_im = (a_im * lam_re - (a_re - 1.0) * lam_im) / den
    bb_re = f_re[..., None] * b_re - f_im[..., None] * b_im
    bb_im = f_re[..., None] * b_im + f_im[..., None] * b_re
    eye = jnp.eye(S5_GROUPS, dtype=F32)

    def blk_in(bb):
        t = jnp.einsum("dgpc,gh->dgchp", bb, eye)
        return t.reshape(2, S5_W, S5_NS)

    bblk = jnp.concatenate([blk_in(bb_re), blk_in(bb_im)], axis=-1).astype(BF16)

    def blk_out(cc):
        t = jnp.einsum("gcp,gh->gphc", cc, eye)
        return t.reshape(S5_NS, S5_W)

    cblk = jnp.concatenate([blk_out(c_re), -blk_out(c_im)], axis=0).astype(BF16)
    bc = lambda a: jnp.broadcast_to(a.reshape(2, 1, S5_NS), (2, SUB, S5_NS))
    return bblk, bc(a_re), bc(a_im), cblk


def _ret_tables(log_decay):
    tc = RET_C
    pos = jnp.arange(tc, dtype=F32)
    w = jnp.stack([pos, tc - 1.0 - pos])
    rel = w[:, :, None] - w[:, None, :]
    lg = log_decay[:, :, None, None]
    dmask = jnp.where(rel[:, None] >= 0, jnp.exp(jnp.maximum(rel[:, None], 0.0) * lg), 0.0)
    qk_head = TAB["qk_head"]
    lane_lg = jnp.where(jnp.asarray(qk_head >= 0),
                        jnp.take(log_decay, jnp.asarray(np.maximum(qk_head, 0)), axis=1), 0.0)
    qdec = jnp.exp((w[:, :, None] + 1.0) * lane_lg[:, None, :])
    kdec = jnp.exp((tc - 1.0 - w[:, :, None]) * lane_lg[:, None, :])
    cdec = jnp.exp(tc * lane_lg)[:, None, :]
    return dmask, qdec, kdec, cdec


def _rope_tables(seq, ctx_len):
    rows = seq // GRID_W
    nf = DK // 4
    inv = 1.0 / (ROPE_BASE ** (np.arange(nf, dtype=np.float32) / nf))
    r = np.repeat(np.arange(rows, dtype=np.float32), GRID_W)
    col = np.tile(np.arange(GRID_W, dtype=np.float32), rows)
    ang = np.concatenate([r[:, None] * inv, col[:, None] * inv], axis=-1)
    cos_t = np.ones((ctx_len + seq, LANE), np.float32)
    sin_t = np.zeros((ctx_len + seq, LANE), np.float32)
    for h in range(HEADS):
        cos_t[ctx_len:, h * 24:(h + 1) * 24] = np.cos(ang)
        sin_t[ctx_len:, h * 24:(h + 1) * 24] = np.sin(ang)
    return jnp.asarray(cos_t), jnp.asarray(sin_t)


def _gla_tables(gate_w, gate_b):
    gw = _gather_cols(gate_w, TAB["qk_src"])
    gw_full = jnp.zeros((2, LANE, QK_P), F32)
    gw_full = gw_full.at[0, 0:GLA_RANK].set(gw[0]).at[1, GLA_RANK:2 * GLA_RANK].set(gw[1])
    gb = _gather_cols(gate_b, TAB["qk_src"])[:, None, :]
    i = np.arange(GLA_R)
    same = (i[:, None] // GLA_C) == (i[None, :] // GLA_C)
    tri = np.stack([same & (i[:, None] >= i[None, :]), same & (i[:, None] <= i[None, :])]).astype(np.float32)
    return gw_full.astype(BF16), gb, jnp.asarray(tri)


def _pad_rows(w, src):
    return _gather_cols(w.T, src).T


def kernel(x, c, ctx, c_ctx, w_mod, b_mod, norm_mix, norm_ffn, w_in, s5_lam_re, s5_lam_im, s5_log_dt, s5_b_re, s5_b_im, s5_c_re, s5_c_im, s5_d, s5_glu_w, s5_glu_b, ret_log_decay, ret_gn, gla_gate_w, gla_gate_b, gla_norm, w_br_s5, w_br_ret, w_br_gla, w_out, w_ffn_in, w_ffn_out, norm_final):
    b, seq, _ = x.shape
    ctx_len = ctx.shape[1]
    depth = w_mod.shape[0]
    assert b % S5_NB == 0 and b % MIX_NB == 0 and seq % RET_C == 0 and seq % GRID_W == 0
    assert ctx_len == ROW_T == RET_C

    xs = (ctx, x)
    rows = ((b + 1 + SUB - 1) // SUB) * SUB
    c_all = jnp.zeros((rows, D), F32).at[:b].set(c).at[b].set(c_ctx)
    cos_t, sin_t = _rope_tables(seq, ctx_len)
    nh = FFN_H // FFN_T

    for l in range(depth):
        last = l == depth - 1
        mods = _mod_call(c_all, w_mod[l], b_mod[l])
        m_lat = mods[:b].reshape(b, N_MOD, D)
        m_ctx = jnp.broadcast_to(mods[b].reshape(1, N_MOD, D), (b, N_MOD, D))
        modsel = jnp.stack([m_ctx, m_lat], axis=1)
        modsel = jnp.pad(modsel, ((0, 0), (0, 0), (0, SUB - N_MOD), (0, 0)))

        w_all = _gather_cols(w_in[l], TAB["src"]).astype(BF16)
        p, u = _inproj_call(xs, modsel, norm_mix[l], cos_t, sin_t, w_all)

        bblk, a_re, a_im, cblk = _s5_params(s5_lam_re[l], s5_lam_im[l], s5_log_dt[l],
                                            s5_b_re[l], s5_b_im[l], s5_c_re[l], s5_c_im[l])
        ys5 = _s5_call(jnp.transpose(u, (1, 0, 2)), bblk, a_re, a_im, cblk,
                       s5_d[l].reshape(1, S5_W), s5_glu_w[l].astype(BF16),
                       s5_glu_b[l].reshape(1, S5_W), ctx_len)
        ys5 = jnp.transpose(ys5, (1, 0, 2))

        dmask, qdec, kdec, cdec = _ret_tables(ret_log_decay[l])
        yret = _ret_call(p, dmask, qdec, kdec, cdec,
                         _gather_cols(ret_gn[l], TAB["vsrc"]).reshape(1, V_P), ctx_len)

        gw, gb, tri = _gla_tables(gla_gate_w[l], gla_gate_b[l])
        ygla = _gla_call(p, gw, gb, tri, _gather_cols(gla_norm[l], TAB["vsrc"]).reshape(1, V_P),
                         ctx_len)

        wfi = w_ffn_in[l].astype(BF16)
        wa = wfi[:, :FFN_H].reshape(D, nh, FFN_T).transpose(1, 0, 2)
        wb = wfi[:, FFN_H:].reshape(D, nh, FFN_T).transpose(1, 0, 2)
        xs = _merge_ffn_call(xs, modsel, ys5, yret, ygla, p, w_br_s5[l].astype(BF16),
                             _pad_rows(w_br_ret[l], TAB["vsrc"]).astype(BF16),
                             _pad_rows(w_br_gla[l], TAB["vsrc"]).astype(BF16),
                             w_out[l].astype(BF16), norm_ffn[l], wa, wb,
                             w_ffn_out[l].astype(BF16), norm_final, last, last, ctx_len)
    return xs
```

```python
import functools

import numpy as np
import jax
import jax.numpy as jnp
from jax import lax
from jax.experimental import pallas as pl
from jax.experimental.pallas import tpu as pltpu

F32 = jnp.float32
BF16 = jnp.bfloat16

D = 1024
EPS = 1e-6
N_MOD = 6
GRID_W = 64
S5_W = 256
S5_GROUPS = 16
S5_STATE = 64
S5_NS = S5_GROUPS * S5_STATE
HEADS = 4
DK = 48
DV = 96
QK_W = HEADS * DK
V_W = HEADS * DV
GLA_RANK = 16
GLA_TAU = 16.0
ROPE_BASE = 10000.0
FFN_H = 2816

LANE = 128
SUB = 8
QK_P = 2 * LANE
V_P = HEADS * LANE
QKV_P = 2 * QK_P + V_P
GATE_W = 3 * D

OFF_GATE = 0
OFF_RQKV = GATE_W
OFF_GQKV = OFF_RQKV + QKV_P
OFF_RG = OFF_GQKV + QKV_P
OFF_GG = OFF_RG + V_P
OFF_Z = OFF_GG + V_P
NP = OFF_Z + LANE
OFF_S5 = NP
NW = NP + S5_W

ROW_T = 256
RET_C = 256
GLA_C = 64
GLA_R = 256
MIX_NB = 4
S5_C = 64
VMEM_LIMIT = 56 * 1024 * 1024

IN_SIZES = (S5_W, QK_W, QK_W, V_W, V_W, QK_W, QK_W, V_W, V_W, GLA_RANK, GLA_RANK, D, D, D)
IN_OFFS = np.concatenate([[0], np.cumsum(IN_SIZES)]).astype(np.int64)


def _head_lane():
    m = np.zeros((HEADS, DK), np.int64)
    for h in range(HEADS):
        for i in range(DK):
            m[h, i] = (i % 2) * LANE + h * (DK // 2) + i // 2
    return m


HEAD_LANE = _head_lane()


def _static_tables():
    src = np.full((NW,), -1, np.int64)
    src[OFF_GATE:OFF_GATE + GATE_W] = IN_OFFS[11] + np.arange(GATE_W)
    for qkv, gg, iq, ik, iv, ig in ((OFF_RQKV, OFF_RG, 1, 2, 3, 4), (OFF_GQKV, OFF_GG, 5, 6, 7, 8)):
        for h in range(HEADS):
            for i in range(DK):
                src[qkv + HEAD_LANE[h, i]] = IN_OFFS[iq] + h * DK + i
                src[qkv + QK_P + HEAD_LANE[h, i]] = IN_OFFS[ik] + h * DK + i
            for j in range(DV):
                src[qkv + 2 * QK_P + h * LANE + j] = IN_OFFS[iv] + h * DV + j
                src[gg + h * LANE + j] = IN_OFFS[ig] + h * DV + j
    src[OFF_Z:OFF_Z + GLA_RANK] = IN_OFFS[9] + np.arange(GLA_RANK)
    src[OFF_Z + GLA_RANK:OFF_Z + 2 * GLA_RANK] = IN_OFFS[10] + np.arange(GLA_RANK)
    src[OFF_S5:OFF_S5 + S5_W] = np.arange(S5_W)
    vsrc = np.full((V_P,), -1, np.int64)
    for h in range(HEADS):
        vsrc[h * LANE:h * LANE + DV] = h * DV + np.arange(DV)
    qk_head = np.full((QK_P,), -1, np.int64)
    qk_src = np.full((QK_P,), -1, np.int64)
    for h in range(HEADS):
        for i in range(DK):
            qk_head[HEAD_LANE[h, i]] = h
            qk_src[HEAD_LANE[h, i]] = h * DK + i
    v_head = np.repeat(np.arange(HEADS), LANE)
    v_real = (np.arange(V_P) % LANE) < DV
    head_mask = np.zeros((SUB, QK_P), np.float32)
    for h in range(HEADS):
        head_mask[h] = (qk_head == h)
    bm_t = (v_head[:, None] == qk_head[None, :]).astype(np.float32)
    return dict(src=src, vsrc=vsrc, qk_head=qk_head, qk_src=qk_src, head_mask=head_mask,
                bm_t=bm_t, v_real=v_real.astype(np.float32))


TAB = _static_tables()


def _gather_cols(w, src):
    valid = jnp.asarray(src >= 0)
    out = jnp.take(w, jnp.asarray(np.maximum(src, 0)), axis=-1)
    return jnp.where(valid, out, jnp.zeros((), w.dtype))


def _cparams(n_axes):
    return pltpu.CompilerParams(dimension_semantics=("arbitrary",) * n_axes,
                                vmem_limit_bytes=VMEM_LIMIT)


def _const_spec(shape):
    nd = len(shape)
    return pl.BlockSpec(shape, lambda *_: (0,) * nd, pipeline_mode=pl.Buffered(1))


def _dir_chunk(d, c, n_ctx, n_all):
    bwd = jnp.where(c < n_ctx, n_ctx - 1 - c, n_all + n_ctx - 1 - c)
    return jnp.where(d == 0, c, bwd)


def _sigmoid(x):
    return 0.5 * jnp.tanh(0.5 * x) + 0.5


def _silu(x):
    return x * _sigmoid(x)


def _mod_kernel(c_ref, w_ref, b_ref, o_ref):
    o_ref[...] = jnp.dot(_silu(c_ref[...]), w_ref[...], preferred_element_type=F32,
                         precision=lax.Precision.HIGHEST) + b_ref[...]


def _mod_call(c_all, w_mod_l, b_mod_l):
    rows = c_all.shape[0]
    tn = D
    return pl.pallas_call(
        _mod_kernel,
        grid=(N_MOD * D // tn,),
        in_specs=[pl.BlockSpec((rows, D), lambda j: (0, 0)),
                  pl.BlockSpec((D, tn), lambda j: (0, j)),
                  pl.BlockSpec((1, tn), lambda j: (0, j))],
        out_specs=pl.BlockSpec((rows, tn), lambda j: (0, j)),
        out_shape=jax.ShapeDtypeStruct((rows, N_MOD * D), F32),
        compiler_params=_cparams(1),
    )(c_all, w_mod_l, b_mod_l.reshape(1, -1))


def _rms(x, gain):
    return x * lax.rsqrt(jnp.mean(x * x, axis=-1, keepdims=True) + EPS) * gain


def _residual_rows(refs, split):
    if not split:
        return refs[0][...]
    return jnp.where(pl.program_id(1) == 0, refs[0][...], refs[1][...])


def _residual_specs(split, j0):
    if not split:
        return [pl.BlockSpec((None, ROW_T, D), lambda i, j: (i, j + j0, 0))]
    return [pl.BlockSpec((None, ROW_T, D), lambda i, j: (i, 0, 0)),
            pl.BlockSpec((None, ROW_T, D), lambda i, j: (i, jnp.maximum(j - 1, 0), 0))]


def _inproj_kernel(*refs, split):
    mod_ref, gain_ref, cos_ref, sin_ref, w_ref, p_ref, u_ref = refs[1 + split:]
    h = _rms(_residual_rows(refs, split), gain_ref[...])
    h = h * (1.0 + mod_ref[1:2, :]) + mod_ref[0:1, :]
    hb = h.astype(BF16)

    def proj(c0, c1):
        return jnp.dot(hb, w_ref[:, c0:c1], preferred_element_type=F32)

    for c0 in range(0, GATE_W, D):
        p_ref[:, c0:c0 + D] = proj(c0, c0 + D).astype(BF16)
    cs = cos_ref[...]
    sn = sin_ref[...]
    qk = proj(OFF_RQKV, OFF_RQKV + 2 * QK_P)
    for o, scale in ((0, 1.0), (QK_P, DK ** -0.5)):
        a = qk[:, o:o + LANE] * scale
        b = qk[:, o + LANE:o + 2 * LANE] * scale
        p_ref[:, OFF_RQKV + o:OFF_RQKV + o + LANE] = (a * cs - b * sn).astype(BF16)
        p_ref[:, OFF_RQKV + o + LANE:OFF_RQKV + o + 2 * LANE] = (a * sn + b * cs).astype(BF16)
    c0 = OFF_RQKV + 2 * QK_P
    p_ref[:, c0:OFF_GQKV] = proj(c0, OFF_GQKV).astype(BF16)
    p_ref[:, OFF_GQKV:OFF_GQKV + QK_P] = (proj(OFF_GQKV, OFF_GQKV + QK_P) * (DK ** -0.5)).astype(BF16)
    c0 = OFF_GQKV + QK_P
    p_ref[:, c0:OFF_RG] = proj(c0, OFF_RG).astype(BF16)
    p_ref[:, OFF_RG:NP] = proj(OFF_RG, NP).astype(BF16)
    u_ref[...] = proj(OFF_S5, OFF_S5 + S5_W)


def _inproj_call(xs, modsel, gain, cos_t, sin_t, w_all):
    split = isinstance(xs, tuple)
    xs = xs if split else (xs,)
    b = xs[0].shape[0]
    lt = sum(t.shape[1] for t in xs)
    nt = lt // ROW_T
    return pl.pallas_call(
        functools.partial(_inproj_kernel, split=split),
        grid=(b, nt),
        in_specs=_residual_specs(split, 0) + [
                  pl.BlockSpec((None, None, SUB, D), lambda i, j: (i, jnp.minimum(j, 1), 0, 0)),
                  _const_spec((1, D)),
                  pl.BlockSpec((ROW_T, LANE), lambda i, j: (j, 0)),
                  pl.BlockSpec((ROW_T, LANE), lambda i, j: (j, 0)),
                  _const_spec((D, NW))],
        out_specs=[pl.BlockSpec((None, ROW_T, NP), lambda i, j: (i, j, 0)),
                   pl.BlockSpec((None, ROW_T, S5_W), lambda i, j: (i, j, 0))],
        out_shape=[jax.ShapeDtypeStruct((b, lt, NP), BF16),
                   jax.ShapeDtypeStruct((b, lt, S5_W), F32)],
        compiler_params=_cparams(2),
    )(*xs, modsel, gain.reshape(1, D), cos_t, sin_t, w_all)


S5_NB = 2 * SUB
S5_RB = 256


def _s5_kernel(*refs, tc, backward):
    if backward:
        (u_ref, bb_ref, are_ref, aim_ref, cb_ref, yf_ref, dsk_ref, gluw_ref, glub_ref,
         y_ref, x_ref, hb_ref, h_ref) = refs
    else:
        u_ref, bb_ref, are_ref, aim_ref, cb_ref, y_ref, x_ref, hb_ref, h_ref = refs
    rows = tc * S5_NB
    steps = S5_RB // S5_NB
    n_blk = rows // S5_RB
    order = range(n_blk - 1, -1, -1) if backward else range(n_blk)

    @pl.when(pl.program_id(1) == 0)
    def _():
        h_ref[...] = jnp.zeros_like(h_ref)

    u = pltpu.einshape("bts->(tb)s", u_ref[...])
    ub = u.astype(BF16)
    for k in order:
        rs = slice(k * S5_RB, (k + 1) * S5_RB)
        x_ref[rs, :] = jnp.dot(ub[rs], bb_ref[...], preferred_element_type=F32)

    half = S5_NS // 2
    lanes = [(slice(lo, lo + half), slice(S5_NS + lo, S5_NS + lo + half)) for lo in (0, half)]
    coef = [(are_ref[:, re], aim_ref[:, re]) for re, _ in lanes]
    carry = [[(h_ref[g * SUB:(g + 1) * SUB, re], h_ref[g * SUB:(g + 1) * SUB, im])
              for g in range(S5_NB // SUB)] for re, im in lanes]
    for k in order:
        ts = range(k * steps, (k + 1) * steps)
        for t in (reversed(ts) if backward else ts):
            r0 = t * S5_NB
            for li, (re, im) in enumerate(lanes):
                ar, ai = coef[li]
                new = []
                for g, (hr, hi) in enumerate(carry[li]):
                    rw = slice(r0 + g * SUB, r0 + (g + 1) * SUB)
                    new.append((ar * hr - ai * hi + x_ref[rw, re], ar * hi + ai * hr + x_ref[rw, im]))
                carry[li] = new
                hb_ref[r0:r0 + S5_NB, re] = jnp.concatenate([n[0] for n in new], axis=0).astype(BF16)
                hb_ref[r0:r0 + S5_NB, im] = jnp.concatenate([n[1] for n in new], axis=0).astype(BF16)
        rs = slice(k * S5_RB, (k + 1) * S5_RB)
        y = jnp.dot(hb_ref[rs, :], cb_ref[...], preferred_element_type=F32)
        if backward:
            yf = pltpu.einshape("bts->(tb)s", yf_ref[:, k * steps:(k + 1) * steps, :])
            yy = jax.nn.gelu(y + yf + dsk_ref[...] * u[rs])
            glu = jnp.dot(yy.astype(BF16), gluw_ref[...], preferred_element_type=F32) + glub_ref[...]
            y = yy * _sigmoid(glu)
        y_ref[:, k * steps:(k + 1) * steps, :] = pltpu.einshape("(tb)s->bts", y, b=S5_NB)
    for li, (re, im) in enumerate(lanes):
        for g, (hr, hi) in enumerate(carry[li]):
            h_ref[g * SUB:(g + 1) * SUB, re] = hr
            h_ref[g * SUB:(g + 1) * SUB, im] = hi


def _s5_call(u, bblk, a_re, a_im, cblk, dsk, gluw, glub, ctx_len):
    b, lt, _ = u.shape
    tc = S5_C
    n_all, n_ctx = lt // tc, ctx_len // tc
    tok = lambda d: pl.BlockSpec((S5_NB, tc, S5_W), lambda g, c: (g, _dir_chunk(d, c, n_ctx, n_all), 0))
    par = lambda d: [pl.BlockSpec((None, S5_W, 2 * S5_NS), lambda g, c: (d, 0, 0)),
                     pl.BlockSpec((None, SUB, S5_NS), lambda g, c: (d, 0, 0)),
                     pl.BlockSpec((None, SUB, S5_NS), lambda g, c: (d, 0, 0)),
                     _const_spec((2 * S5_NS, S5_W))]
    common = dict(
        grid=(b // S5_NB, n_all),
        out_shape=jax.ShapeDtypeStruct((b, lt, S5_W), F32),
        scratch_shapes=[pltpu.VMEM((tc * S5_NB, 2 * S5_NS), F32),
                        pltpu.VMEM((tc * S5_NB, 2 * S5_NS), BF16),
                        pltpu.VMEM((S5_NB, 2 * S5_NS), F32)],
        compiler_params=_cparams(2))
    yf = pl.pallas_call(
        functools.partial(_s5_kernel, tc=tc, backward=False),
        in_specs=[tok(0)] + par(0), out_specs=tok(0), **common,
    )(u, bblk, a_re, a_im, cblk)
    return pl.pallas_call(
        functools.partial(_s5_kernel, tc=tc, backward=True),
        in_specs=[tok(1)] + par(1) + [tok(1), _const_spec((1, S5_W)), _const_spec((S5_W, S5_W)),
                                      _const_spec((1, S5_W))],
        out_specs=tok(1), **common,
    )(u, bblk, a_re, a_im, cblk, yf, dsk, gluw, glub)


_NT = (((1,), (1,)), ((), ()))
_TN = (((0,), (0,)), ((), ()))


def _ret_kernel(p_ref, g_ref, dm_ref, qd_ref, kd_ref, cd_ref, hm_ref, bm_ref,
                real_ref, gn_ref, y_ref, s_ref, oc_ref, of_ref, *, n_ctx, n_all):
    d = pl.program_id(1)
    c = pl.program_id(2)
    row0 = pl.multiple_of(_dir_chunk(d, c, n_ctx, n_all) * RET_C, RET_C)

    @pl.when(c == 0)
    def _():
        s_ref[...] = jnp.zeros_like(s_ref)

    def chunk(bi):
        qb = p_ref[bi, :, 0:QK_P]
        kb = p_ref[bi, :, QK_P:2 * QK_P]
        vb = p_ref[bi, :, 2 * QK_P:2 * QK_P + V_P]
        st = s_ref[bi]
        inter = lax.dot_general(qb * qd_ref[...], st.astype(BF16) * bm_ref[...], _NT,
                                preferred_element_type=F32)
        for h in range(HEADS):
            att = lax.dot_general(qb * hm_ref[h], kb, _NT, preferred_element_type=F32)
            att = att.astype(BF16) * dm_ref[h]
            sl = slice(h * LANE, (h + 1) * LANE)
            oc_ref[bi, :, sl] = (jnp.dot(att, vb[:, sl], preferred_element_type=F32) + inter[:, sl])
        kdv = lax.dot_general(vb, kb * kd_ref[...], _TN, preferred_element_type=F32)
        s_ref[bi] = st * cd_ref[...] + kdv

    for bi in range(MIX_NB):
        chunk(bi)

    @pl.when(d == 0)
    def _():
        for bi in range(MIX_NB):
            of_ref[bi, pl.ds(row0, RET_C), :] = oc_ref[bi]

    @pl.when(d == 1)
    def _():
        for bi in range(MIX_NB):
            for h in range(HEADS):
                sl = slice(h * LANE, (h + 1) * LANE)
                o = oc_ref[bi, :, sl] + of_ref[bi, pl.ds(row0, RET_C), sl]
                mu = jnp.sum(o, axis=-1, keepdims=True) * (1.0 / DV)
                dlt = (o - mu) * real_ref[:, sl]
                var = jnp.sum(dlt * dlt, axis=-1, keepdims=True) * (1.0 / DV)
                y = dlt * lax.rsqrt(var + EPS) * gn_ref[:, sl] * _silu(g_ref[bi, :, sl].astype(F32))
                y_ref[bi, :, sl] = y.astype(BF16)


def _mixer_specs(b, lt, ctx_len, tc, off_qkv, off_g):
    n_all, n_ctx = lt // tc, ctx_len // tc
    cidx = functools.partial(_dir_chunk, n_ctx=n_ctx, n_all=n_all)
    qkv = pl.BlockSpec((MIX_NB, tc, QKV_P), lambda i, d, c: (i, cidx(d, c), off_qkv // QKV_P))
    gsp = pl.BlockSpec((MIX_NB, tc, V_P), lambda i, d, c: (i, cidx(1, c * d), off_g // V_P))
    out = pl.BlockSpec((MIX_NB, tc, V_P), lambda i, d, c: (i, cidx(1, c * d), 0))
    scratch = [pltpu.VMEM((MIX_NB, V_P, QK_P), F32),
               pltpu.VMEM((MIX_NB, tc, V_P), F32),
               pltpu.VMEM((MIX_NB, lt, V_P), F32)]
    return n_ctx, n_all, cidx, qkv, gsp, out, scratch


def _head_mask_rows(rows):
    return jnp.asarray(np.broadcast_to(TAB["head_mask"][:HEADS, None, :], (HEADS, rows, QK_P)), dtype=BF16)


def _ret_call(p, dmask, qdec, kdec, cdec, gn, ctx_len):
    b, lt, _ = p.shape
    tc = RET_C
    n_ctx, n_all, cidx, qkv, gsp, out, scratch = _mixer_specs(b, lt, ctx_len, tc, OFF_RQKV, OFF_RG)
    return pl.pallas_call(
        functools.partial(_ret_kernel, n_ctx=n_ctx, n_all=n_all),
        grid=(b // MIX_NB, 2, n_all),
        in_specs=[qkv, gsp,
                  pl.BlockSpec((None, HEADS, tc, tc), lambda i, d, c: (d, 0, 0, 0)),
                  pl.BlockSpec((None, tc, QK_P), lambda i, d, c: (d, 0, 0)),
                  pl.BlockSpec((None, tc, QK_P), lambda i, d, c: (d, 0, 0)),
                  pl.BlockSpec((None, 1, QK_P), lambda i, d, c: (d, 0, 0)),
                  _const_spec((HEADS, tc, QK_P)), _const_spec((V_P, QK_P)),
                  _const_spec((1, V_P)), _const_spec((1, V_P))],
        out_specs=out,
        out_shape=jax.ShapeDtypeStruct((b, lt, V_P), BF16),
        scratch_shapes=scratch,
        compiler_params=_cparams(3),
    )(p, p, dmask.astype(BF16), qdec.astype(BF16), kdec.astype(BF16), cdec,
      _head_mask_rows(tc), jnp.asarray(TAB["bm_t"], dtype=BF16),
      jnp.asarray(TAB["v_real"]).reshape(1, V_P), gn)


def _gla_kernel(*refs, backward):
    if backward:
        (p_ref, z_ref, gw_ref, gb_ref, tri_ref, hm_ref, bm_ref, of_ref, g_ref, gn_ref,
         y_ref, s_ref, oc_ref) = refs
    else:
        p_ref, z_ref, gw_ref, gb_ref, tri_ref, hm_ref, bm_ref, oc_ref, s_ref = refs

    @pl.when(pl.program_id(1) == 0)
    def _():
        s_ref[...] = jnp.zeros_like(s_ref)

    trif = tri_ref[...]
    tri = trif.astype(BF16)
    mask = trif > 0.5

    def intra(bi):
        la = jax.nn.log_sigmoid(
            jnp.dot(z_ref[bi], gw_ref[...], preferred_element_type=F32) + gb_ref[...]) / GLA_TAU
        hi = la.astype(BF16)
        lo = (la - hi.astype(F32)).astype(BF16)
        bcum = (jnp.dot(tri, hi, preferred_element_type=F32)
                + jnp.dot(tri, lo, preferred_element_type=F32))
        q = p_ref[bi, :, 0:QK_P].astype(F32)
        k = p_ref[bi, :, QK_P:2 * QK_P].astype(F32)
        vb = p_ref[bi, :, 2 * QK_P:2 * QK_P + V_P]
        qtb = (q * jnp.exp(bcum)).astype(BF16)
        ktb = (k * jnp.exp(-bcum)).astype(BF16)
        for h in range(HEADS):
            att = lax.dot_general(qtb * hm_ref[h], ktb, _NT, preferred_element_type=F32)
            att = jnp.where(mask, att.astype(BF16), jnp.zeros((), BF16))
            sl = slice(h * LANE, (h + 1) * LANE)
            oc_ref[bi, :, sl] = jnp.dot(att, vb[:, sl], preferred_element_type=F32)
        return bcum, k, qtb, vb

    parts = [intra(bi) for bi in range(MIX_NB)]

    n_sub = GLA_R // GLA_C
    order = range(n_sub - 1, -1, -1) if backward else range(n_sub)
    last_row = 0 if backward else GLA_C - 1
    ss = [s_ref[bi] for bi in range(MIX_NB)]
    for cc in order:
        r0 = cc * GLA_C
        rows = slice(r0, r0 + GLA_C)
        for bi, (bcum, k, qtb, vb) in enumerate(parts):
            s = ss[bi]
            bl = bcum[r0 + last_row:r0 + last_row + 1, :]
            oc_ref[bi, rows, :] += lax.dot_general(qtb[rows], s.astype(BF16) * bm_ref[...], _NT,
                                                   preferred_element_type=F32)
            kd = (k[rows] * jnp.exp(bl - bcum[rows])).astype(BF16)
            kdv = lax.dot_general(vb[rows], kd, _TN, preferred_element_type=F32)
            ss[bi] = s * jnp.exp(bl) + kdv
    for bi in range(MIX_NB):
        s_ref[bi] = ss[bi]

    if backward:
        for bi in range(MIX_NB):
            for h in range(HEADS):
                sl = slice(h * LANE, (h + 1) * LANE)
                o = oc_ref[bi, :, sl] + of_ref[bi, :, sl]
                ms = jnp.sum(o * o, axis=-1, keepdims=True) * (1.0 / DV)
                y = o * lax.rsqrt(ms + EPS) * gn_ref[:, sl] * _silu(g_ref[bi, :, sl].astype(F32))
                y_ref[bi, :, sl] = y.astype(BF16)


def _gla_call(p, gw, gb, tri, gn, ctx_len):
    b, lt, _ = p.shape
    tc = GLA_R
    n_all, n_ctx = lt // tc, ctx_len // tc
    chunk = lambda d: (lambda c: _dir_chunk(d, c, n_ctx, n_all))
    tok = lambda d, w, blk: pl.BlockSpec((MIX_NB, tc, w), lambda i, c: (i, chunk(d)(c), blk))
    ins = lambda d: [tok(d, QKV_P, OFF_GQKV // QKV_P), tok(d, LANE, OFF_Z // LANE),
                     _const_spec((LANE, QK_P)), _const_spec((1, QK_P)), _const_spec((tc, tc)),
                     _const_spec((HEADS, tc, QK_P)), _const_spec((V_P, QK_P))]
    hm, bm = _head_mask_rows(tc), jnp.asarray(TAB["bm_t"], dtype=BF16)
    state = pltpu.VMEM((MIX_NB, V_P, QK_P), F32)
    of = pl.pallas_call(
        functools.partial(_gla_kernel, backward=False),
        grid=(b // MIX_NB, n_all),
        in_specs=ins(0), out_specs=tok(0, V_P, 0),
        out_shape=jax.ShapeDtypeStruct((b, lt, V_P), F32),
        scratch_shapes=[state],
        compiler_params=_cparams(2),
    )(p, p, gw[0], gb[0], tri[0], hm, bm)
    return pl.pallas_call(
        functools.partial(_gla_kernel, backward=True),
        grid=(b // MIX_NB, n_all),
        in_specs=ins(1) + [tok(1, V_P, 0), tok(1, V_P, OFF_GG // V_P), _const_spec((1, V_P))],
        out_specs=tok(1, V_P, 0),
        out_shape=jax.ShapeDtypeStruct((b, lt, V_P), BF16),
        scratch_shapes=[state, pltpu.VMEM((MIX_NB, tc, V_P), F32)],
        compiler_params=_cparams(2),
    )(p, p, gw[1], gb[1], tri[1], hm, bm, of, p, gn)


FFN_T = 256


def _merge_ffn_kernel(*refs, final, split):
    (mod_ref, ys_ref, yr_ref, yg_ref, gate_ref, wbs_ref, wbr_ref, wbg_ref, wo_ref,
     gain_ref, wfi_ref, wfo_ref, fin_ref, o_ref, act_ref) = refs[1 + split:]

    def branch(y, w_ref, g0):
        gate = _sigmoid(gate_ref[:, g0:g0 + D].astype(F32))
        return gate * jnp.dot(y, w_ref[...], preferred_element_type=F32)

    m = (branch(ys_ref[...].astype(BF16), wbs_ref, 0) + branch(yr_ref[...], wbr_ref, D)
         + branch(yg_ref[...], wbg_ref, 2 * D))
    x = _residual_rows(refs, split) + mod_ref[2:3, :] * jnp.dot(m.astype(BF16), wo_ref[...],
                                                                  preferred_element_type=F32)
    h = _rms(x, gain_ref[...])
    hb = (h * (1.0 + mod_ref[4:5, :]) + mod_ref[3:4, :]).astype(BF16)
    for t in range(FFN_H // FFN_T):
        c0 = t * FFN_T
        a = jnp.dot(hb, wfi_ref[:, c0:c0 + FFN_T], preferred_element_type=F32)
        bq = jnp.dot(hb, wfi_ref[:, FFN_H + c0:FFN_H + c0 + FFN_T], preferred_element_type=F32)
        act_ref[:, t * FFN_T:(t + 1) * FFN_T] = (_silu(a) * bq).astype(BF16)
    y = x + mod_ref[5:6, :] * jnp.dot(act_ref[...], wfo_ref[...], preferred_element_type=F32)
    if final:
        y = _rms(y, fin_ref[...])
    o_ref[...] = y


def _merge_ffn_call(xs, modsel, ys5, yret, ygla, p, wbs, wbr, wbg, wo, gain, wfi, wfo, fin,
                    final, lat_only, ctx_len):
    split = isinstance(xs, tuple)
    xs = xs if split else (xs,)
    b = xs[0].shape[0]
    lt = sum(t.shape[1] for t in xs)
    j0 = ctx_len // ROW_T if lat_only else 0
    nt = lt // ROW_T - j0
    nh = FFN_H // FFN_T
    tok = lambda w: pl.BlockSpec((None, ROW_T, w), lambda i, j: (i, j + j0, 0))
    return pl.pallas_call(
        functools.partial(_merge_ffn_kernel, final=final, split=split),
        grid=(b, nt),
        in_specs=_residual_specs(split, j0) + [
                  pl.BlockSpec((None, None, SUB, D), lambda i, j: (i, jnp.minimum(j + j0, 1), 0, 0)),
                  tok(S5_W), tok(V_P), tok(V_P), tok(GATE_W),
                  _const_spec((S5_W, D)), _const_spec((V_P, D)), _const_spec((V_P, D)),
                  _const_spec((D, D)),
                  _const_spec((1, D)),
                  _const_spec((D, 2 * FFN_H)),
                  _const_spec((FFN_H, D)), _const_spec((1, D))],
        out_specs=pl.BlockSpec((None, ROW_T, D), lambda i, j: (i, j, 0)),
        out_shape=jax.ShapeDtypeStruct((b, nt * ROW_T, D), F32),
        scratch_shapes=[pltpu.VMEM((ROW_T, FFN_H), BF16)],
        compiler_params=_cparams(2),
    )(*xs, modsel, ys5, yret, ygla, p, wbs, wbr, wbg, wo,
      gain.reshape(1, D), wfi, wfo, fin.reshape(1, D))


def _s5_params(lam_re, lam_im, log_dt, b_re, b_im, c_re, c_im):
    dt = jnp.exp(log_dt)[..., None]
    mag = jnp.exp(lam_re * dt)
    a_re, a_im = mag * jnp.cos(lam_im * dt), mag * jnp.sin(lam_im * dt)
    den = lam_re * lam_re + lam_im * lam_im
    f_re = ((a_re - 1.0) * lam_re + a_im * lam_im) / den
    f_im = (a_im * lam_re - (a_re - 1.0) * lam_im) / den
    bb_re = f_re[..., None] * b_re - f_im[..., None] * b_im
    bb_im = f_re[..., None] * b_im + f_im[..., None] * b_re
    eye = jnp.eye(S5_GROUPS, dtype=F32)

    def blk_in(bb):
        t = jnp.einsum("dgpc,gh->dgchp", bb, eye)
        return t.reshape(2, S5_W, S5_NS)

    bblk = jnp.concatenate([blk_in(bb_re), blk_in(bb_im)], axis=-1).astype(BF16)

    def blk_out(cc):
        t = jnp.einsum("gcp,gh->gphc", cc, eye)
        return t.reshape(S5_NS, S5_W)

    cblk = jnp.concatenate([blk_out(c_re), -blk_out(c_im)], axis=0).astype(BF16)
    bc = lambda a: jnp.broadcast_to(a.reshape(2, 1, S5_NS), (2, SUB, S5_NS))
    return bblk, bc(a_re), bc(a_im), cblk


def _ret_tables(log_decay):
    tc = RET_C
    pos = jnp.arange(tc, dtype=F32)
    w = jnp.stack([pos, tc - 1.0 - pos])
    rel = w[:, :, None] - w[:, None, :]
    lg = log_decay[:, :, None, None]
    dmask = jnp.where(rel[:, None] >= 0, jnp.exp(jnp.maximum(rel[:, None], 0.0) * lg), 0.0)
    qk_head = TAB["qk_head"]
    lane_lg = jnp.where(jnp.asarray(qk_head >= 0),
                        jnp.take(log_decay, jnp.asarray(np.maximum(qk_head, 0)), axis=1), 0.0)
    qdec = jnp.exp((w[:, :, None] + 1.0) * lane_lg[:, None, :])
    kdec = jnp.exp((tc - 1.0 - w[:, :, None]) * lane_lg[:, None, :])
    cdec = jnp.exp(tc * lane_lg)[:, None, :]
    return dmask, qdec, kdec, cdec


def _rope_tables(seq, ctx_len):
    rows = seq // GRID_W
    nf = DK // 4
    inv = 1.0 / (ROPE_BASE ** (np.arange(nf, dtype=np.float32) / nf))
    r = np.repeat(np.arange(rows, dtype=np.float32), GRID_W)
    col = np.tile(np.arange(GRID_W, dtype=np.float32), rows)
    ang = np.concatenate([r[:, None] * inv, col[:, None] * inv], axis=-1)
    cos_t = np.ones((ctx_len + seq, LANE), np.float32)
    sin_t = np.zeros((ctx_len + seq, LANE), np.float32)
    for h in range(HEADS):
        cos_t[ctx_len:, h * 24:(h + 1) * 24] = np.cos(ang)
        sin_t[ctx_len:, h * 24:(h + 1) * 24] = np.sin(ang)
    return jnp.asarray(cos_t), jnp.asarray(sin_t)


def _gla_tables(gate_w, gate_b):
    gw = _gather_cols(gate_w, TAB["qk_src"])
    gw_full = jnp.zeros((2, LANE, QK_P), F32)
    gw_full = gw_full.at[0, 0:GLA_RANK].set(gw[0]).at[1, GLA_RANK:2 * GLA_RANK].set(gw[1])
    gb = _gather_cols(gate_b, TAB["qk_src"])[:, None, :]
    i = np.arange(GLA_R)
    same = (i[:, None] // GLA_C) == (i[None, :] // GLA_C)
    tri = np.stack([same & (i[:, None] >= i[None, :]), same & (i[:, None] <= i[None, :])]).astype(np.float32)
    return gw_full.astype(BF16), gb, jnp.asarray(tri)


def _pad_rows(w, src):
    return _gather_cols(w.T, src).T


def kernel(x, c, ctx, c_ctx, w_mod, b_mod, norm_mix, norm_ffn, w_in, s5_lam_re, s5_lam_im, s5_log_dt, s5_b_re, s5_b_im, s5_c_re, s5_c_im, s5_d, s5_glu_w, s5_glu_b, ret_log_decay, ret_gn, gla_gate_w, gla_gate_b, gla_norm, w_br_s5, w_br_ret, w_br_gla, w_out, w_ffn_in, w_ffn_out, norm_final):
    b, seq, _ = x.shape
    ctx_len = ctx.shape[1]
    depth = w_mod.shape[0]
    assert b % S5_NB == 0 and b % MIX_NB == 0 and seq % RET_C == 0 and seq % GRID_W == 0
    assert ctx_len == ROW_T == RET_C

    xs = (ctx, x)
    rows = ((b + 1 + SUB - 1) // SUB) * SUB
    c_all = jnp.zeros((rows, D), F32).at[:b].set(c).at[b].set(c_ctx)
    cos_t, sin_t = _rope_tables(seq, ctx_len)
    nh = FFN_H // FFN_T

    for l in range(depth):
        last = l == depth - 1
        mods = _mod_call(c_all, w_mod[l], b_mod[l])
        m_lat = mods[:b].reshape(b, N_MOD, D)
        m_ctx = jnp.broadcast_to(mods[b].reshape(1, N_MOD, D), (b, N_MOD, D))
        modsel = jnp.stack([m_ctx, m_lat], axis=1)
        modsel = jnp.pad(modsel, ((0, 0), (0, 0), (0, SUB - N_MOD), (0, 0)))

        w_all = _gather_cols(w_in[l], TAB["src"]).astype(BF16)
        p, u = _inproj_call(xs, modsel, norm_mix[l], cos_t, sin_t, w_all)

        bblk, a_re, a_im, cblk = _s5_params(s5_lam_re[l], s5_lam_im[l], s5_log_dt[l],
                                            s5_b_re[l], s5_b_im[l], s5_c_re[l], s5_c_im[l])
        ys5 = _s5_call(u, bblk, a_re, a_im, cblk,
                       s5_d[l].reshape(1, S5_W), s5_glu_w[l].astype(BF16),
                       s5_glu_b[l].reshape(1, S5_W), ctx_len)

        dmask, qdec, kdec, cdec = _ret_tables(ret_log_decay[l])
        yret = _ret_call(p, dmask, qdec, kdec, cdec,
                         _gather_cols(ret_gn[l], TAB["vsrc"]).reshape(1, V_P), ctx_len)

        gw, gb, tri = _gla_tables(gla_gate_w[l], gla_gate_b[l])
        ygla = _gla_call(p, gw, gb, tri, _gather_cols(gla_norm[l], TAB["vsrc"]).reshape(1, V_P),
                         ctx_len)

        xs = _merge_ffn_call(xs, modsel, ys5, yret, ygla, p, w_br_s5[l].astype(BF16),
                             _pad_rows(w_br_ret[l], TAB["vsrc"]).astype(BF16),
                             _pad_rows(w_br_gla[l], TAB["vsrc"]).astype(BF16),
                             w_out[l].astype(BF16), norm_ffn[l], w_ffn_in[l].astype(BF16),
                             w_ffn_out[l].astype(BF16), norm_final, last, last, ctx_len)
    return xs
```

```python
import functools

import numpy as np
import jax
import jax.numpy as jnp
from jax import lax
from jax.experimental import pallas as pl
from jax.experimental.pallas import tpu as pltpu

F32 = jnp.float32
BF16 = jnp.bfloat16

D = 1024
EPS = 1e-6
N_MOD = 6
GRID_W = 64
S5_W = 256
S5_GROUPS = 16
S5_STATE = 64
S5_NS = S5_GROUPS * S5_STATE
HEADS = 4
DK = 48
DV = 96
QK_W = HEADS * DK
V_W = HEADS * DV
GLA_RANK = 16
GLA_TAU = 16.0
ROPE_BASE = 10000.0
FFN_H = 2816

LANE = 128
SUB = 8
QK_P = 2 * LANE
V_P = HEADS * LANE
QKV_P = 2 * QK_P + V_P
GATE_W = 3 * D

OFF_GATE = 0
OFF_RQKV = GATE_W
OFF_GQKV = OFF_RQKV + QKV_P
OFF_RG = OFF_GQKV + QKV_P
OFF_GG = OFF_RG + V_P
OFF_Z = OFF_GG + V_P
NP = OFF_Z + LANE
OFF_S5 = NP
NW = NP + S5_W

ROW_T = 256
RET_C = 256
GLA_C = 64
GLA_R = 256
MIX_NB = 4
S5_C = 64
V7X_VMEM_BYTES = 64 * 1024 * 1024
VMEM_LIMIT = V7X_VMEM_BYTES * 7 // 8

IN_SIZES = (S5_W, QK_W, QK_W, V_W, V_W, QK_W, QK_W, V_W, V_W, GLA_RANK, GLA_RANK, D, D, D)
IN_OFFS = np.concatenate([[0], np.cumsum(IN_SIZES)]).astype(np.int64)


def _head_lane():
    m = np.zeros((HEADS, DK), np.int64)
    for h in range(HEADS):
        for i in range(DK):
            m[h, i] = (i % 2) * LANE + h * (DK // 2) + i // 2
    return m


HEAD_LANE = _head_lane()


def _static_tables():
    src = np.full((NW,), -1, np.int64)
    src[OFF_GATE:OFF_GATE + GATE_W] = IN_OFFS[11] + np.arange(GATE_W)
    for qkv, gg, iq, ik, iv, ig in ((OFF_RQKV, OFF_RG, 1, 2, 3, 4), (OFF_GQKV, OFF_GG, 5, 6, 7, 8)):
        for h in range(HEADS):
            for i in range(DK):
                src[qkv + HEAD_LANE[h, i]] = IN_OFFS[iq] + h * DK + i
                src[qkv + QK_P + HEAD_LANE[h, i]] = IN_OFFS[ik] + h * DK + i
            for j in range(DV):
                src[qkv + 2 * QK_P + h * LANE + j] = IN_OFFS[iv] + h * DV + j
                src[gg + h * LANE + j] = IN_OFFS[ig] + h * DV + j
    src[OFF_Z:OFF_Z + GLA_RANK] = IN_OFFS[9] + np.arange(GLA_RANK)
    src[OFF_Z + GLA_RANK:OFF_Z + 2 * GLA_RANK] = IN_OFFS[10] + np.arange(GLA_RANK)
    src[OFF_S5:OFF_S5 + S5_W] = np.arange(S5_W)
    vsrc = np.full((V_P,), -1, np.int64)
    for h in range(HEADS):
        vsrc[h * LANE:h * LANE + DV] = h * DV + np.arange(DV)
    qk_head = np.full((QK_P,), -1, np.int64)
    qk_src = np.full((QK_P,), -1, np.int64)
    for h in range(HEADS):
        for i in range(DK):
            qk_head[HEAD_LANE[h, i]] = h
            qk_src[HEAD_LANE[h, i]] = h * DK + i
    v_head = np.repeat(np.arange(HEADS), LANE)
    v_real = (np.arange(V_P) % LANE) < DV
    head_mask = np.zeros((SUB, QK_P), np.float32)
    for h in range(HEADS):
        head_mask[h] = (qk_head == h)
    bm_t = (v_head[:, None] == qk_head[None, :]).astype(np.float32)
    return dict(src=src, vsrc=vsrc, qk_head=qk_head, qk_src=qk_src, head_mask=head_mask,
                bm_t=bm_t, v_real=v_real.astype(np.float32))


TAB = _static_tables()


def _gather_cols(w, src):
    valid = jnp.asarray(src >= 0)
    out = jnp.take(w, jnp.asarray(np.maximum(src, 0)), axis=-1)
    return jnp.where(valid, out, jnp.zeros((), w.dtype))


def _cparams(n_axes):
    return pltpu.CompilerParams(dimension_semantics=("arbitrary",) * n_axes,
                                vmem_limit_bytes=VMEM_LIMIT)


def _const_spec(shape):
    nd = len(shape)
    return pl.BlockSpec(shape, lambda *_: (0,) * nd, pipeline_mode=pl.Buffered(1))


def _dir_chunk(d, c, n_ctx, n_all):
    bwd = jnp.where(c < n_ctx, n_ctx - 1 - c, n_all + n_ctx - 1 - c)
    return jnp.where(d == 0, c, bwd)


def _sigmoid(x):
    return 0.5 * jnp.tanh(0.5 * x) + 0.5


def _silu(x):
    return x * _sigmoid(x)


def _mod_kernel(c_ref, w_ref, b_ref, o_ref):
    o_ref[...] = jnp.dot(_silu(c_ref[...]), w_ref[...], preferred_element_type=F32,
                         precision=lax.Precision.HIGHEST) + b_ref[...]


def _mod_call(c_all, w_mod_l, b_mod_l):
    rows = c_all.shape[0]
    tn = D
    return pl.pallas_call(
        _mod_kernel,
        grid=(N_MOD * D // tn,),
        in_specs=[pl.BlockSpec((rows, D), lambda j: (0, 0)),
                  pl.BlockSpec((D, tn), lambda j: (0, j)),
                  pl.BlockSpec((1, tn), lambda j: (0, j))],
        out_specs=pl.BlockSpec((rows, tn), lambda j: (0, j)),
        out_shape=jax.ShapeDtypeStruct((rows, N_MOD * D), F32),
        compiler_params=_cparams(1),
    )(c_all, w_mod_l, b_mod_l.reshape(1, -1))


def _rms(x, gain):
    return x * lax.rsqrt(jnp.mean(x * x, axis=-1, keepdims=True) + EPS) * gain


def _residual_rows(refs, split):
    if not split:
        return refs[0][...]
    return jnp.where(pl.program_id(1) == 0, refs[0][...], refs[1][...])


def _residual_specs(split, j0):
    if not split:
        return [pl.BlockSpec((None, ROW_T, D), lambda i, j: (i, j + j0, 0))]
    return [pl.BlockSpec((None, ROW_T, D), lambda i, j: (i, 0, 0)),
            pl.BlockSpec((None, ROW_T, D), lambda i, j: (i, jnp.maximum(j - 1, 0), 0))]


def _inproj_kernel(*refs, split):
    mod_ref, gain_ref, cos_ref, sin_ref, w_ref, p_ref, u_ref = refs[1 + split:]
    h = _rms(_residual_rows(refs, split), gain_ref[...])
    h = h * (1.0 + mod_ref[1:2, :]) + mod_ref[0:1, :]
    hb = h.astype(BF16)

    def proj(c0, c1):
        return jnp.dot(hb, w_ref[:, c0:c1], preferred_element_type=F32)

    for c0 in range(0, GATE_W, D):
        p_ref[:, c0:c0 + D] = proj(c0, c0 + D).astype(BF16)
    cs = cos_ref[...]
    sn = sin_ref[...]
    qk = proj(OFF_RQKV, OFF_RQKV + 2 * QK_P)
    for o, scale in ((0, 1.0), (QK_P, DK ** -0.5)):
        a = qk[:, o:o + LANE] * scale
        b = qk[:, o + LANE:o + 2 * LANE] * scale
        p_ref[:, OFF_RQKV + o:OFF_RQKV + o + LANE] = (a * cs - b * sn).astype(BF16)
        p_ref[:, OFF_RQKV + o + LANE:OFF_RQKV + o + 2 * LANE] = (a * sn + b * cs).astype(BF16)
    c0 = OFF_RQKV + 2 * QK_P
    p_ref[:, c0:OFF_GQKV] = proj(c0, OFF_GQKV).astype(BF16)
    p_ref[:, OFF_GQKV:OFF_GQKV + QK_P] = (proj(OFF_GQKV, OFF_GQKV + QK_P) * (DK ** -0.5)).astype(BF16)
    c0 = OFF_GQKV + QK_P
    p_ref[:, c0:OFF_RG] = proj(c0, OFF_RG).astype(BF16)
    p_ref[:, OFF_RG:NP] = proj(OFF_RG, NP).astype(BF16)
    u_ref[...] = proj(OFF_S5, OFF_S5 + S5_W)


def _inproj_call(xs, modsel, gain, cos_t, sin_t, w_all):
    split = isinstance(xs, tuple)
    xs = xs if split else (xs,)
    b = xs[0].shape[0]
    lt = sum(t.shape[1] for t in xs)
    nt = lt // ROW_T
    return pl.pallas_call(
        functools.partial(_inproj_kernel, split=split),
        grid=(b, nt),
        in_specs=_residual_specs(split, 0) + [
                  pl.BlockSpec((None, None, SUB, D), lambda i, j: (i, jnp.minimum(j, 1), 0, 0)),
                  _const_spec((1, D)),
                  pl.BlockSpec((ROW_T, LANE), lambda i, j: (j, 0)),
                  pl.BlockSpec((ROW_T, LANE), lambda i, j: (j, 0)),
                  _const_spec((D, NW))],
        out_specs=[pl.BlockSpec((None, ROW_T, NP), lambda i, j: (i, j, 0)),
                   pl.BlockSpec((None, ROW_T, S5_W), lambda i, j: (i, j, 0))],
        out_shape=[jax.ShapeDtypeStruct((b, lt, NP), BF16),
                   jax.ShapeDtypeStruct((b, lt, S5_W), F32)],
        compiler_params=_cparams(2),
    )(*xs, modsel, gain.reshape(1, D), cos_t, sin_t, w_all)


S5_NB = 2 * SUB
S5_RB = 256


def _s5_kernel(*refs, tc, backward):
    if backward:
        (u_ref, bb_ref, are_ref, aim_ref, cb_ref, yf_ref, dsk_ref, gluw_ref, glub_ref,
         y_ref, x_ref, hb_ref, h_ref) = refs
    else:
        u_ref, bb_ref, are_ref, aim_ref, cb_ref, y_ref, x_ref, hb_ref, h_ref = refs
    rows = tc * S5_NB
    steps = S5_RB // S5_NB
    n_blk = rows // S5_RB
    order = range(n_blk - 1, -1, -1) if backward else range(n_blk)

    @pl.when(pl.program_id(1) == 0)
    def _():
        h_ref[...] = jnp.zeros_like(h_ref)

    u = pltpu.einshape("bts->(tb)s", u_ref[...])
    ub = u.astype(BF16)
    for k in order:
        rs = slice(k * S5_RB, (k + 1) * S5_RB)
        x_ref[rs, :] = jnp.dot(ub[rs], bb_ref[...], preferred_element_type=F32)

    half = S5_NS // 2
    lanes = [(slice(lo, lo + half), slice(S5_NS + lo, S5_NS + lo + half)) for lo in (0, half)]
    coef = [(are_ref[:, re], aim_ref[:, re]) for re, _ in lanes]
    carry = [[(h_ref[g * SUB:(g + 1) * SUB, re], h_ref[g * SUB:(g + 1) * SUB, im])
              for g in range(S5_NB // SUB)] for re, im in lanes]
    for k in order:
        ts = range(k * steps, (k + 1) * steps)
        for t in (reversed(ts) if backward else ts):
            r0 = t * S5_NB
            for li, (re, im) in enumerate(lanes):
                ar, ai = coef[li]
                new = []
                for g, (hr, hi) in enumerate(carry[li]):
                    rw = slice(r0 + g * SUB, r0 + (g + 1) * SUB)
                    new.append((ar * hr - ai * hi + x_ref[rw, re], ar * hi + ai * hr + x_ref[rw, im]))
                carry[li] = new
                hb_ref[r0:r0 + S5_NB, re] = jnp.concatenate([n[0] for n in new], axis=0).astype(BF16)
                hb_ref[r0:r0 + S5_NB, im] = jnp.concatenate([n[1] for n in new], axis=0).astype(BF16)
        rs = slice(k * S5_RB, (k + 1) * S5_RB)
        y = jnp.dot(hb_ref[rs, :], cb_ref[...], preferred_element_type=F32)
        if backward:
            yf = pltpu.einshape("bts->(tb)s", yf_ref[:, k * steps:(k + 1) * steps, :])
            yy = jax.nn.gelu(y + yf + dsk_ref[...] * u[rs])
            glu = jnp.dot(yy.astype(BF16), gluw_ref[...], preferred_element_type=F32) + glub_ref[...]
            y = yy * _sigmoid(glu)
        y_ref[:, k * steps:(k + 1) * steps, :] = pltpu.einshape("(tb)s->bts", y, b=S5_NB)
    for li, (re, im) in enumerate(lanes):
        for g, (hr, hi) in enumerate(carry[li]):
            h_ref[g * SUB:(g + 1) * SUB, re] = hr
            h_ref[g * SUB:(g + 1) * SUB, im] = hi


def _s5_call(u, bblk, a_re, a_im, cblk, dsk, gluw, glub, ctx_len):
    b, lt, _ = u.shape
    tc = S5_C
    n_all, n_ctx = lt // tc, ctx_len // tc
    tok = lambda d: pl.BlockSpec((S5_NB, tc, S5_W), lambda g, c: (g, _dir_chunk(d, c, n_ctx, n_all), 0))
    par = lambda d: [pl.BlockSpec((None, S5_W, 2 * S5_NS), lambda g, c: (d, 0, 0)),
                     pl.BlockSpec((None, SUB, S5_NS), lambda g, c: (d, 0, 0)),
                     pl.BlockSpec((None, SUB, S5_NS), lambda g, c: (d, 0, 0)),
                     _const_spec((2 * S5_NS, S5_W))]
    common = dict(
        grid=(b // S5_NB, n_all),
        out_shape=jax.ShapeDtypeStruct((b, lt, S5_W), F32),
        scratch_shapes=[pltpu.VMEM((tc * S5_NB, 2 * S5_NS), F32),
                        pltpu.VMEM((tc * S5_NB, 2 * S5_NS), BF16),
                        pltpu.VMEM((S5_NB, 2 * S5_NS), F32)],
        compiler_params=_cparams(2))
    yf = pl.pallas_call(
        functools.partial(_s5_kernel, tc=tc, backward=False),
        in_specs=[tok(0)] + par(0), out_specs=tok(0), **common,
    )(u, bblk, a_re, a_im, cblk)
    return pl.pallas_call(
        functools.partial(_s5_kernel, tc=tc, backward=True),
        in_specs=[tok(1)] + par(1) + [tok(1), _const_spec((1, S5_W)), _const_spec((S5_W, S5_W)),
                                      _const_spec((1, S5_W))],
        out_specs=tok(1), **common,
    )(u, bblk, a_re, a_im, cblk, yf, dsk, gluw, glub)


_NT = (((1,), (1,)), ((), ()))
_TN = (((0,), (0,)), ((), ()))


def _ret_kernel(*refs, backward):
    if backward:
        (p_ref, dm_ref, qd_ref, kd_ref, cd_ref, hm_ref, bm_ref, of_ref, g_ref, real_ref, gn_ref,
         y_ref, s_ref) = refs
    else:
        p_ref, dm_ref, qd_ref, kd_ref, cd_ref, hm_ref, bm_ref, o_ref, s_ref = refs

    @pl.when(pl.program_id(1) == 0)
    def _():
        s_ref[...] = jnp.zeros_like(s_ref)

    def chunk(bi):
        qb = p_ref[bi, :, 0:QK_P]
        kb = p_ref[bi, :, QK_P:2 * QK_P]
        vb = p_ref[bi, :, 2 * QK_P:2 * QK_P + V_P]
        st = s_ref[bi]
        inter = lax.dot_general(qb * qd_ref[...], st.astype(BF16) * bm_ref[...], _NT,
                                preferred_element_type=F32)
        for h in range(HEADS):
            att = lax.dot_general(qb * hm_ref[h], kb, _NT, preferred_element_type=F32)
            att = att.astype(BF16) * dm_ref[h]
            sl = slice(h * LANE, (h + 1) * LANE)
            o = jnp.dot(att, vb[:, sl], preferred_element_type=F32) + inter[:, sl]
            if backward:
                o = o + of_ref[bi, :, sl]
                mu = jnp.sum(o, axis=-1, keepdims=True) * (1.0 / DV)
                dlt = (o - mu) * real_ref[:, sl]
                var = jnp.sum(dlt * dlt, axis=-1, keepdims=True) * (1.0 / DV)
                y = dlt * lax.rsqrt(var + EPS) * gn_ref[:, sl] * _silu(g_ref[bi, :, sl].astype(F32))
                y_ref[bi, :, sl] = y.astype(BF16)
            else:
                o_ref[bi, :, sl] = o
        kdv = lax.dot_general(vb, kb * kd_ref[...], _TN, preferred_element_type=F32)
        s_ref[bi] = st * cd_ref[...] + kdv

    for bi in range(MIX_NB):
        chunk(bi)


def _mixer_specs(b, lt, ctx_len, tc, off_qkv, off_g):
    n_all, n_ctx = lt // tc, ctx_len // tc
    tok = lambda d, w, blk: pl.BlockSpec(
        (MIX_NB, tc, w), lambda i, c: (i, _dir_chunk(d, c, n_ctx, n_all), blk))
    qkv = lambda d: tok(d, QKV_P, off_qkv // QKV_P)
    gsp = lambda d: tok(d, V_P, off_g // V_P)
    out = lambda d: tok(d, V_P, 0)
    state = pltpu.VMEM((MIX_NB, V_P, QK_P), F32)
    return (b // MIX_NB, n_all), tok, qkv, gsp, out, state


def _head_mask_rows(rows):
    return jnp.asarray(np.broadcast_to(TAB["head_mask"][:HEADS, None, :], (HEADS, rows, QK_P)), dtype=BF16)


def _ret_call(p, dmask, qdec, kdec, cdec, gn, ctx_len):
    b, lt, _ = p.shape
    tc = RET_C
    grid, _, qkv, gsp, out, state = _mixer_specs(b, lt, ctx_len, tc, OFF_RQKV, OFF_RG)
    ins = lambda d: [qkv(d), _const_spec((HEADS, tc, tc)), _const_spec((tc, QK_P)),
                     _const_spec((tc, QK_P)), _const_spec((1, QK_P)),
                     _const_spec((HEADS, tc, QK_P)), _const_spec((V_P, QK_P))]
    tabs = lambda d: (dmask[d].astype(BF16), qdec[d].astype(BF16), kdec[d].astype(BF16), cdec[d],
                      _head_mask_rows(tc), jnp.asarray(TAB["bm_t"], dtype=BF16))
    of = pl.pallas_call(
        functools.partial(_ret_kernel, backward=False),
        grid=grid, in_specs=ins(0), out_specs=out(0),
        out_shape=jax.ShapeDtypeStruct((b, lt, V_P), F32),
        scratch_shapes=[state], compiler_params=_cparams(2),
    )(p, *tabs(0))
    return pl.pallas_call(
        functools.partial(_ret_kernel, backward=True),
        grid=grid,
        in_specs=ins(1) + [out(1), gsp(1), _const_spec((1, V_P)), _const_spec((1, V_P))],
        out_specs=out(1),
        out_shape=jax.ShapeDtypeStruct((b, lt, V_P), BF16),
        scratch_shapes=[state], compiler_params=_cparams(2),
    )(p, *tabs(1), of, p, jnp.asarray(TAB["v_real"]).reshape(1, V_P), gn)


def _gla_kernel(*refs, backward):
    if backward:
        (p_ref, z_ref, gw_ref, gb_ref, tri_ref, hm_ref, bm_ref, of_ref, g_ref, gn_ref,
         y_ref, s_ref, oc_ref) = refs
    else:
        p_ref, z_ref, gw_ref, gb_ref, tri_ref, hm_ref, bm_ref, oc_ref, s_ref = refs

    @pl.when(pl.program_id(1) == 0)
    def _():
        s_ref[...] = jnp.zeros_like(s_ref)

    tri = tri_ref[...]
    mask = tri > 0

    def intra(bi):
        la = jax.nn.log_sigmoid(
            jnp.dot(z_ref[bi], gw_ref[...], preferred_element_type=F32) + gb_ref[...]) / GLA_TAU
        hi = la.astype(BF16)
        lo = (la - hi.astype(F32)).astype(BF16)
        bcum = (jnp.dot(tri, hi, preferred_element_type=F32)
                + jnp.dot(tri, lo, preferred_element_type=F32))
        q = p_ref[bi, :, 0:QK_P].astype(F32)
        k = p_ref[bi, :, QK_P:2 * QK_P].astype(F32)
        vb = p_ref[bi, :, 2 * QK_P:2 * QK_P + V_P]
        qtb = (q * jnp.exp(bcum)).astype(BF16)
        ktb = (k * jnp.exp(-bcum)).astype(BF16)
        for h in range(HEADS):
            att = lax.dot_general(qtb * hm_ref[h], ktb, _NT, preferred_element_type=F32)
            att = jnp.where(mask, att.astype(BF16), jnp.zeros((), BF16))
            sl = slice(h * LANE, (h + 1) * LANE)
            oc_ref[bi, :, sl] = jnp.dot(att, vb[:, sl], preferred_element_type=F32)
        return bcum, k, qtb, vb

    parts = [intra(bi) for bi in range(MIX_NB)]

    n_sub = GLA_R // GLA_C
    order = range(n_sub - 1, -1, -1) if backward else range(n_sub)
    last_row = 0 if backward else GLA_C - 1
    ss = [s_ref[bi] for bi in range(MIX_NB)]
    for cc in order:
        r0 = cc * GLA_C
        rows = slice(r0, r0 + GLA_C)
        for bi, (bcum, k, qtb, vb) in enumerate(parts):
            s = ss[bi]
            bl = bcum[r0 + last_row:r0 + last_row + 1, :]
            oc_ref[bi, rows, :] += lax.dot_general(qtb[rows], s.astype(BF16) * bm_ref[...], _NT,
                                                   preferred_element_type=F32)
            kd = (k[rows] * jnp.exp(bl - bcum[rows])).astype(BF16)
            kdv = lax.dot_general(vb[rows], kd, _TN, preferred_element_type=F32)
            ss[bi] = s * jnp.exp(bl) + kdv
    for bi in range(MIX_NB):
        s_ref[bi] = ss[bi]

    if backward:
        for bi in range(MIX_NB):
            for h in range(HEADS):
                sl = slice(h * LANE, (h + 1) * LANE)
                o = oc_ref[bi, :, sl] + of_ref[bi, :, sl]
                ms = jnp.sum(o * o, axis=-1, keepdims=True) * (1.0 / DV)
                y = o * lax.rsqrt(ms + EPS) * gn_ref[:, sl] * _silu(g_ref[bi, :, sl].astype(F32))
                y_ref[bi, :, sl] = y.astype(BF16)


def _gla_call(p, gw, gb, tri, gn, ctx_len):
    b, lt, _ = p.shape
    tc = GLA_R
    grid, tok, qkv, gsp, out, state = _mixer_specs(b, lt, ctx_len, tc, OFF_GQKV, OFF_GG)
    ins = lambda d: [qkv(d), tok(d, LANE, OFF_Z // LANE),
                     _const_spec((LANE, QK_P)), _const_spec((1, QK_P)), _const_spec((tc, tc)),
                     _const_spec((HEADS, tc, QK_P)), _const_spec((V_P, QK_P))]
    hm, bm = _head_mask_rows(tc), jnp.asarray(TAB["bm_t"], dtype=BF16)
    of = pl.pallas_call(
        functools.partial(_gla_kernel, backward=False),
        grid=grid, in_specs=ins(0), out_specs=out(0),
        out_shape=jax.ShapeDtypeStruct((b, lt, V_P), F32),
        scratch_shapes=[state], compiler_params=_cparams(2),
    )(p, p, gw[0], gb[0], tri[0], hm, bm)
    return pl.pallas_call(
        functools.partial(_gla_kernel, backward=True),
        grid=grid,
        in_specs=ins(1) + [out(1), gsp(1), _const_spec((1, V_P))],
        out_specs=out(1),
        out_shape=jax.ShapeDtypeStruct((b, lt, V_P), BF16),
        scratch_shapes=[state, pltpu.VMEM((MIX_NB, tc, V_P), F32)],
        compiler_params=_cparams(2),
    )(p, p, gw[1], gb[1], tri[1], hm, bm, of, p, gn)


FFN_T = 256


def _merge_ffn_kernel(*refs, final, split):
    (mod_ref, ys_ref, yr_ref, yg_ref, gate_ref, wbs_ref, wbr_ref, wbg_ref, wo_ref,
     gain_ref, wfi_ref, wfo_ref, fin_ref, o_ref, act_ref) = refs[1 + split:]

    def branch(y, w_ref, g0):
        gate = _sigmoid(gate_ref[:, g0:g0 + D].astype(F32))
        return gate * jnp.dot(y, w_ref[...], preferred_element_type=F32)

    m = (branch(ys_ref[...].astype(BF16), wbs_ref, 0) + branch(yr_ref[...], wbr_ref, D)
         + branch(yg_ref[...], wbg_ref, 2 * D))
    x = _residual_rows(refs, split) + mod_ref[2:3, :] * jnp.dot(m.astype(BF16), wo_ref[...],
                                                                  preferred_element_type=F32)
    h = _rms(x, gain_ref[...])
    hb = (h * (1.0 + mod_ref[4:5, :]) + mod_ref[3:4, :]).astype(BF16)
    for t in range(FFN_H // FFN_T):
        c0 = t * FFN_T
        a = jnp.dot(hb, wfi_ref[:, c0:c0 + FFN_T], preferred_element_type=F32)
        bq = jnp.dot(hb, wfi_ref[:, FFN_H + c0:FFN_H + c0 + FFN_T], preferred_element_type=F32)
        act_ref[:, t * FFN_T:(t + 1) * FFN_T] = (_silu(a) * bq).astype(BF16)
    y = x + mod_ref[5:6, :] * jnp.dot(act_ref[...], wfo_ref[...], preferred_element_type=F32)
    if final:
        y = _rms(y, fin_ref[...])
    o_ref[...] = y


def _merge_ffn_call(xs, modsel, ys5, yret, ygla, p, wbs, wbr, wbg, wo, gain, wfi, wfo, fin,
                    final, lat_only, ctx_len):
    split = isinstance(xs, tuple)
    xs = xs if split else (xs,)
    b = xs[0].shape[0]
    lt = sum(t.shape[1] for t in xs)
    j0 = ctx_len // ROW_T if lat_only else 0
    nt = lt // ROW_T - j0
    nh = FFN_H // FFN_T
    tok = lambda w: pl.BlockSpec((None, ROW_T, w), lambda i, j: (i, j + j0, 0))
    return pl.pallas_call(
        functools.partial(_merge_ffn_kernel, final=final, split=split),
        grid=(b, nt),
        in_specs=_residual_specs(split, j0) + [
                  pl.BlockSpec((None, None, SUB, D), lambda i, j: (i, jnp.minimum(j + j0, 1), 0, 0)),
                  tok(S5_W), tok(V_P), tok(V_P), tok(GATE_W),
                  _const_spec((S5_W, D)), _const_spec((V_P, D)), _const_spec((V_P, D)),
                  _const_spec((D, D)),
                  _const_spec((1, D)),
                  _const_spec((D, 2 * FFN_H)),
                  _const_spec((FFN_H, D)), _const_spec((1, D))],
        out_specs=pl.BlockSpec((None, ROW_T, D), lambda i, j: (i, j, 0)),
        out_shape=jax.ShapeDtypeStruct((b, nt * ROW_T, D), F32),
        scratch_shapes=[pltpu.VMEM((ROW_T, FFN_H), BF16)],
        compiler_params=_cparams(2),
    )(*xs, modsel, ys5, yret, ygla, p, wbs, wbr, wbg, wo,
      gain.reshape(1, D), wfi, wfo, fin.reshape(1, D))


def _s5_params(lam_re, lam_im, log_dt, b_re, b_im, c_re, c_im):
    dt = jnp.exp(log_dt)[..., None]
    mag = jnp.exp(lam_re * dt)
    a_re, a_im = mag * jnp.cos(lam_im * dt), mag * jnp.sin(lam_im * dt)
    den = lam_re * lam_re + lam_im * lam_im
    f_re = ((a_re - 1.0) * lam_re + a_im * lam_im) / den
    f_im = (a_im * lam_re - (a_re - 1.0) * lam_im) / den
    bb_re = f_re[..., None] * b_re - f_im[..., None] * b_im
    bb_im = f_re[..., None] * b_im + f_im[..., None] * b_re
    eye = jnp.eye(S5_GROUPS, dtype=F32)

    def blk_in(bb):
        t = jnp.einsum("dgpc,gh->dgchp", bb, eye)
        return t.reshape(2, S5_W, S5_NS)

    bblk = jnp.concatenate([blk_in(bb_re), blk_in(bb_im)], axis=-1).astype(BF16)

    def blk_out(cc):
        t = jnp.einsum("gcp,gh->gphc", cc, eye)
        return t.reshape(S5_NS, S5_W)

    cblk = jnp.concatenate([blk_out(c_re), -blk_out(c_im)], axis=0).astype(BF16)
    bc = lambda a: jnp.broadcast_to(a.reshape(2, 1, S5_NS), (2, SUB, S5_NS))
    return bblk, bc(a_re), bc(a_im), cblk


def _ret_tables(log_decay):
    tc = RET_C
    pos = jnp.arange(tc, dtype=F32)
    w = jnp.stack([pos, tc - 1.0 - pos])
    rel = w[:, :, None] - w[:, None, :]
    lg = log_decay[:, :, None, None]
    dmask = jnp.where(rel[:, None] >= 0, jnp.exp(jnp.maximum(rel[:, None], 0.0) * lg), 0.0)
    qk_head = TAB["qk_head"]
    lane_lg = jnp.where(jnp.asarray(qk_head >= 0),
                        jnp.take(log_decay, jnp.asarray(np.maximum(qk_head, 0)), axis=1), 0.0)
    qdec = jnp.exp((w[:, :, None] + 1.0) * lane_lg[:, None, :])
    kdec = jnp.exp((tc - 1.0 - w[:, :, None]) * lane_lg[:, None, :])
    cdec = jnp.exp(tc * lane_lg)[:, None, :]
    return dmask, qdec, kdec, cdec


def _rope_tables(seq, ctx_len):
    rows = seq // GRID_W
    nf = DK // 4
    inv = 1.0 / (ROPE_BASE ** (np.arange(nf, dtype=np.float32) / nf))
    r = np.repeat(np.arange(rows, dtype=np.float32), GRID_W)
    col = np.tile(np.arange(GRID_W, dtype=np.float32), rows)
    ang = np.concatenate([r[:, None] * inv, col[:, None] * inv], axis=-1)
    cos_t = np.ones((ctx_len + seq, LANE), np.float32)
    sin_t = np.zeros((ctx_len + seq, LANE), np.float32)
    for h in range(HEADS):
        cos_t[ctx_len:, h * 24:(h + 1) * 24] = np.cos(ang)
        sin_t[ctx_len:, h * 24:(h + 1) * 24] = np.sin(ang)
    return jnp.asarray(cos_t), jnp.asarray(sin_t)


def _gla_tables(gate_w, gate_b):
    gw = _gather_cols(gate_w, TAB["qk_src"])
    gw_full = jnp.zeros((2, LANE, QK_P), F32)
    gw_full = gw_full.at[0, 0:GLA_RANK].set(gw[0]).at[1, GLA_RANK:2 * GLA_RANK].set(gw[1])
    gb = _gather_cols(gate_b, TAB["qk_src"])[:, None, :]
    i = np.arange(GLA_R)
    same = (i[:, None] // GLA_C) == (i[None, :] // GLA_C)
    tri = np.stack([same & (i[:, None] >= i[None, :]), same & (i[:, None] <= i[None, :])]).astype(np.float32)
    return gw_full.astype(BF16), gb, jnp.asarray(tri, dtype=BF16)


def _pad_rows(w, src):
    return _gather_cols(w.T, src).T


def kernel(x, c, ctx, c_ctx, w_mod, b_mod, norm_mix, norm_ffn, w_in, s5_lam_re, s5_lam_im, s5_log_dt, s5_b_re, s5_b_im, s5_c_re, s5_c_im, s5_d, s5_glu_w, s5_glu_b, ret_log_decay, ret_gn, gla_gate_w, gla_gate_b, gla_norm, w_br_s5, w_br_ret, w_br_gla, w_out, w_ffn_in, w_ffn_out, norm_final):
    b, seq, _ = x.shape
    ctx_len = ctx.shape[1]
    depth = w_mod.shape[0]
    assert b % S5_NB == 0 and b % MIX_NB == 0 and seq % RET_C == 0 and seq % GRID_W == 0
    assert ctx_len == ROW_T == RET_C

    xs = (ctx, x)
    rows = ((b + 1 + SUB - 1) // SUB) * SUB
    c_all = jnp.zeros((rows, D), F32).at[:b].set(c).at[b].set(c_ctx)
    cos_t, sin_t = _rope_tables(seq, ctx_len)
    nh = FFN_H // FFN_T

    for l in range(depth):
        last = l == depth - 1
        mods = _mod_call(c_all, w_mod[l], b_mod[l])
        m_lat = mods[:b].reshape(b, N_MOD, D)
        m_ctx = jnp.broadcast_to(mods[b].reshape(1, N_MOD, D), (b, N_MOD, D))
        modsel = jnp.stack([m_ctx, m_lat], axis=1)
        modsel = jnp.pad(modsel, ((0, 0), (0, 0), (0, SUB - N_MOD), (0, 0)))

        w_all = _gather_cols(w_in[l], TAB["src"]).astype(BF16)
        p, u = _inproj_call(xs, modsel, norm_mix[l], cos_t, sin_t, w_all)

        bblk, a_re, a_im, cblk = _s5_params(s5_lam_re[l], s5_lam_im[l], s5_log_dt[l],
                                            s5_b_re[l], s5_b_im[l], s5_c_re[l], s5_c_im[l])
        ys5 = _s5_call(u, bblk, a_re, a_im, cblk,
                       s5_d[l].reshape(1, S5_W), s5_glu_w[l].astype(BF16),
                       s5_glu_b[l].reshape(1, S5_W), ctx_len)

        dmask, qdec, kdec, cdec = _ret_tables(ret_log_decay[l])
        yret = _ret_call(p, dmask, qdec, kdec, cdec,
                         _gather_cols(ret_gn[l], TAB["vsrc"]).reshape(1, V_P), ctx_len)

        gw, gb, tri = _gla_tables(gla_gate_w[l], gla_gate_b[l])
        ygla = _gla_call(p, gw, gb, tri, _gather_cols(gla_norm[l], TAB["vsrc"]).reshape(1, V_P),
                         ctx_len)

        xs = _merge_ffn_call(xs, modsel, ys5, yret, ygla, p, w_br_s5[l].astype(BF16),
                             _pad_rows(w_br_ret[l], TAB["vsrc"]).astype(BF16),
                             _pad_rows(w_br_gla[l], TAB["vsrc"]).astype(BF16),
                             w_out[l].astype(BF16), norm_ffn[l], w_ffn_in[l].astype(BF16),
                             w_ffn_out[l].astype(BF16), norm_final, last, last, ctx_len)
    return xs
```

```python
import functools

import numpy as np
import jax
import jax.numpy as jnp
from jax import lax
from jax.experimental import pallas as pl
from jax.experimental.pallas import tpu as pltpu

F32 = jnp.float32
BF16 = jnp.bfloat16

D = 1024
EPS = 1e-6
N_MOD = 6
GRID_W = 64
S5_W = 256
S5_GROUPS = 16
S5_STATE = 64
S5_NS = S5_GROUPS * S5_STATE
HEADS = 4
DK = 48
DV = 96
QK_W = HEADS * DK
V_W = HEADS * DV
GLA_RANK = 16
GLA_TAU = 16.0
ROPE_BASE = 10000.0
FFN_H = 2816

LANE = 128
SUB = 8
QK_P = 2 * LANE
V_P = HEADS * LANE
QKV_P = 2 * QK_P + V_P
GATE_W = 3 * D

OFF_GATE = 0
OFF_RQKV = GATE_W
OFF_GQKV = OFF_RQKV + QKV_P
OFF_RG = OFF_GQKV + QKV_P
OFF_GG = OFF_RG + V_P
OFF_Z = OFF_GG + V_P
NP = OFF_Z + LANE
OFF_S5 = NP
NW = NP + S5_W

ROW_T = 256
RET_C = 256
GLA_C = 64
GLA_R = 256
MIX_NB = 4
S5_C = 64
V7X_VMEM_BYTES = 64 * 1024 * 1024
VMEM_LIMIT = V7X_VMEM_BYTES * 7 // 8

IN_SIZES = (S5_W, QK_W, QK_W, V_W, V_W, QK_W, QK_W, V_W, V_W, GLA_RANK, GLA_RANK, D, D, D)
IN_OFFS = np.concatenate([[0], np.cumsum(IN_SIZES)]).astype(np.int64)


def _head_lane():
    m = np.zeros((HEADS, DK), np.int64)
    for h in range(HEADS):
        for i in range(DK):
            m[h, i] = (i % 2) * LANE + h * (DK // 2) + i // 2
    return m


HEAD_LANE = _head_lane()
HALF = LANE // 2
GLA_LANE = np.array([[(h // 2) * LANE + (h % 2) * HALF + i for i in range(DK)] for h in range(HEADS)])


def _static_tables():
    src = np.full((NW,), -1, np.int64)
    src[OFF_GATE:OFF_GATE + GATE_W] = IN_OFFS[11] + np.arange(GATE_W)
    for qkv, gg, lane, iq, ik, iv, ig in ((OFF_RQKV, OFF_RG, HEAD_LANE, 1, 2, 3, 4),
                                          (OFF_GQKV, OFF_GG, GLA_LANE, 5, 6, 7, 8)):
        for h in range(HEADS):
            for i in range(DK):
                src[qkv + lane[h, i]] = IN_OFFS[iq] + h * DK + i
                src[qkv + QK_P + lane[h, i]] = IN_OFFS[ik] + h * DK + i
            for j in range(DV):
                src[qkv + 2 * QK_P + h * LANE + j] = IN_OFFS[iv] + h * DV + j
                src[gg + h * LANE + j] = IN_OFFS[ig] + h * DV + j
    src[OFF_Z:OFF_Z + GLA_RANK] = IN_OFFS[9] + np.arange(GLA_RANK)
    src[OFF_Z + GLA_RANK:OFF_Z + 2 * GLA_RANK] = IN_OFFS[10] + np.arange(GLA_RANK)
    src[OFF_S5:OFF_S5 + S5_W] = np.arange(S5_W)
    vsrc = np.full((V_P,), -1, np.int64)
    for h in range(HEADS):
        vsrc[h * LANE:h * LANE + DV] = h * DV + np.arange(DV)
    qk_head = np.full((QK_P,), -1, np.int64)
    qk_src = np.full((QK_P,), -1, np.int64)
    for h in range(HEADS):
        for i in range(DK):
            qk_head[HEAD_LANE[h, i]] = h
            qk_src[HEAD_LANE[h, i]] = h * DK + i
    v_head = np.repeat(np.arange(HEADS), LANE)
    v_real = (np.arange(V_P) % LANE) < DV
    head_mask = np.zeros((SUB, QK_P), np.float32)
    for h in range(HEADS):
        head_mask[h] = (qk_head == h)
    bm_t = (v_head[:, None] == qk_head[None, :]).astype(np.float32)
    gla_src = np.full((QK_P,), -1, np.int64)
    for h in range(HEADS):
        gla_src[GLA_LANE[h]] = h * DK + np.arange(DK)
    lane_in_pair = np.arange(LANE)
    gla_hm = np.stack([(lane_in_pair // HALF == r) & (lane_in_pair % HALF < DK) for r in range(2)])
    gla_bm = (np.arange(2 * LANE)[:, None] // LANE) == (lane_in_pair[None, :] // HALF)
    return dict(src=src, vsrc=vsrc, qk_head=qk_head, qk_src=qk_src, head_mask=head_mask,
                bm_t=bm_t, v_real=v_real.astype(np.float32), gla_src=gla_src,
                gla_hm=gla_hm.astype(np.float32), gla_bm=gla_bm.astype(np.float32))


TAB = _static_tables()


def _gather_cols(w, src):
    valid = jnp.asarray(src >= 0)
    out = jnp.take(w, jnp.asarray(np.maximum(src, 0)), axis=-1)
    return jnp.where(valid, out, jnp.zeros((), w.dtype))


def _cparams(n_axes):
    return pltpu.CompilerParams(dimension_semantics=("arbitrary",) * n_axes,
                                vmem_limit_bytes=VMEM_LIMIT)


def _const_spec(shape):
    nd = len(shape)
    return pl.BlockSpec(shape, lambda *_: (0,) * nd, pipeline_mode=pl.Buffered(1))


def _dir_chunk(d, c, n_ctx, n_all):
    bwd = jnp.where(c < n_ctx, n_ctx - 1 - c, n_all + n_ctx - 1 - c)
    return jnp.where(d == 0, c, bwd)


def _sigmoid(x):
    return 0.5 * jnp.tanh(0.5 * x) + 0.5


def _silu(x):
    return x * _sigmoid(x)


def _mod_kernel(c_ref, w_ref, b_ref, o_ref):
    o_ref[...] = jnp.dot(_silu(c_ref[...]), w_ref[...], preferred_element_type=F32,
                         precision=lax.Precision.HIGHEST) + b_ref[...]


def _mod_call(c_all, w_mod_l, b_mod_l):
    rows = c_all.shape[0]
    tn = D
    return pl.pallas_call(
        _mod_kernel,
        grid=(N_MOD * D // tn,),
        in_specs=[pl.BlockSpec((rows, D), lambda j: (0, 0)),
                  pl.BlockSpec((D, tn), lambda j: (0, j)),
                  pl.BlockSpec((1, tn), lambda j: (0, j))],
        out_specs=pl.BlockSpec((rows, tn), lambda j: (0, j)),
        out_shape=jax.ShapeDtypeStruct((rows, N_MOD * D), F32),
        compiler_params=_cparams(1),
    )(c_all, w_mod_l, b_mod_l.reshape(1, -1))


def _rms(x, gain):
    return x * lax.rsqrt(jnp.mean(x * x, axis=-1, keepdims=True) + EPS) * gain


def _residual_rows(refs, split):
    if not split:
        return refs[0][...]
    return jnp.where(pl.program_id(1) == 0, refs[0][...], refs[1][...])


def _residual_specs(split, j0):
    if not split:
        return [pl.BlockSpec((None, ROW_T, D), lambda i, j: (i, j + j0, 0))]
    return [pl.BlockSpec((None, ROW_T, D), lambda i, j: (i, 0, 0)),
            pl.BlockSpec((None, ROW_T, D), lambda i, j: (i, jnp.maximum(j - 1, 0), 0))]


def _inproj_kernel(*refs, split):
    mod_ref, gain_ref, cos_ref, sin_ref, w_ref, p_ref, u_ref = refs[1 + split:]
    h = _rms(_residual_rows(refs, split), gain_ref[...])
    h = h * (1.0 + mod_ref[1:2, :]) + mod_ref[0:1, :]
    hb = h.astype(BF16)

    def proj(c0, c1):
        return jnp.dot(hb, w_ref[:, c0:c1], preferred_element_type=F32)

    for c0 in range(0, GATE_W, D):
        p_ref[:, c0:c0 + D] = proj(c0, c0 + D).astype(BF16)
    cs = cos_ref[...]
    sn = sin_ref[...]
    qk = proj(OFF_RQKV, OFF_RQKV + 2 * QK_P)
    for o, scale in ((0, 1.0), (QK_P, DK ** -0.5)):
        a = qk[:, o:o + LANE] * scale
        b = qk[:, o + LANE:o + 2 * LANE] * scale
        p_ref[:, OFF_RQKV + o:OFF_RQKV + o + LANE] = (a * cs - b * sn).astype(BF16)
        p_ref[:, OFF_RQKV + o + LANE:OFF_RQKV + o + 2 * LANE] = (a * sn + b * cs).astype(BF16)
    c0 = OFF_RQKV + 2 * QK_P
    p_ref[:, c0:OFF_GQKV] = proj(c0, OFF_GQKV).astype(BF16)
    p_ref[:, OFF_GQKV:OFF_GQKV + QK_P] = (proj(OFF_GQKV, OFF_GQKV + QK_P) * (DK ** -0.5)).astype(BF16)
    c0 = OFF_GQKV + QK_P
    p_ref[:, c0:OFF_RG] = proj(c0, OFF_RG).astype(BF16)
    p_ref[:, OFF_RG:NP] = proj(OFF_RG, NP).astype(BF16)
    u_ref[...] = proj(OFF_S5, OFF_S5 + S5_W)


def _inproj_call(xs, modsel, gain, cos_t, sin_t, w_all):
    split = isinstance(xs, tuple)
    xs = xs if split else (xs,)
    b = xs[0].shape[0]
    lt = sum(t.shape[1] for t in xs)
    nt = lt // ROW_T
    return pl.pallas_call(
        functools.partial(_inproj_kernel, split=split),
        grid=(b, nt),
        in_specs=_residual_specs(split, 0) + [
                  pl.BlockSpec((None, None, SUB, D), lambda i, j: (i, jnp.minimum(j, 1), 0, 0)),
                  _const_spec((1, D)),
                  pl.BlockSpec((ROW_T, LANE), lambda i, j: (j, 0)),
                  pl.BlockSpec((ROW_T, LANE), lambda i, j: (j, 0)),
                  _const_spec((D, NW))],
        out_specs=[pl.BlockSpec((None, ROW_T, NP), lambda i, j: (i, j, 0)),
                   pl.BlockSpec((None, ROW_T, S5_W), lambda i, j: (i, j, 0))],
        out_shape=[jax.ShapeDtypeStruct((b, lt, NP), BF16),
                   jax.ShapeDtypeStruct((b, lt, S5_W), F32)],
        compiler_params=_cparams(2),
    )(*xs, modsel, gain.reshape(1, D), cos_t, sin_t, w_all)


S5_NB = 2 * SUB
S5_RB = 256


def _s5_kernel(*refs, tc, backward):
    if backward:
        (u_ref, bb_ref, are_ref, aim_ref, cb_ref, yf_ref, dsk_ref, gluw_ref, glub_ref,
         y_ref, x_ref, hb_ref, h_ref) = refs
    else:
        u_ref, bb_ref, are_ref, aim_ref, cb_ref, y_ref, x_ref, hb_ref, h_ref = refs
    rows = tc * S5_NB
    steps = S5_RB // S5_NB
    n_blk = rows // S5_RB
    order = range(n_blk - 1, -1, -1) if backward else range(n_blk)

    @pl.when(pl.program_id(1) == 0)
    def _():
        h_ref[...] = jnp.zeros_like(h_ref)

    u = pltpu.einshape("bts->(tb)s", u_ref[...])
    ub = u.astype(BF16)
    for k in order:
        rs = slice(k * S5_RB, (k + 1) * S5_RB)
        x_ref[rs, :] = jnp.dot(ub[rs], bb_ref[...], preferred_element_type=F32)

    half = S5_NS // 2
    lanes = [(slice(lo, lo + half), slice(S5_NS + lo, S5_NS + lo + half)) for lo in (0, half)]
    coef = [(are_ref[:, re], aim_ref[:, re]) for re, _ in lanes]
    carry = [[(h_ref[g * SUB:(g + 1) * SUB, re], h_ref[g * SUB:(g + 1) * SUB, im])
              for g in range(S5_NB // SUB)] for re, im in lanes]
    for k in order:
        ts = range(k * steps, (k + 1) * steps)
        for t in (reversed(ts) if backward else ts):
            r0 = t * S5_NB
            for li, (re, im) in enumerate(lanes):
                ar, ai = coef[li]
                new = []
                for g, (hr, hi) in enumerate(carry[li]):
                    rw = slice(r0 + g * SUB, r0 + (g + 1) * SUB)
                    new.append((ar * hr - ai * hi + x_ref[rw, re], ar * hi + ai * hr + x_ref[rw, im]))
                carry[li] = new
                hb_ref[r0:r0 + S5_NB, re] = jnp.concatenate([n[0] for n in new], axis=0).astype(BF16)
                hb_ref[r0:r0 + S5_NB, im] = jnp.concatenate([n[1] for n in new], axis=0).astype(BF16)
        rs = slice(k * S5_RB, (k + 1) * S5_RB)
        y = jnp.dot(hb_ref[rs, :], cb_ref[...], preferred_element_type=F32)
        if backward:
            yf = pltpu.einshape("bts->(tb)s", yf_ref[:, k * steps:(k + 1) * steps, :])
            yy = jax.nn.gelu(y + yf + dsk_ref[...] * u[rs])
            glu = jnp.dot(yy.astype(BF16), gluw_ref[...], preferred_element_type=F32) + glub_ref[...]
            y = yy * _sigmoid(glu)
        y_ref[:, k * steps:(k + 1) * steps, :] = pltpu.einshape("(tb)s->bts", y, b=S5_NB)
    for li, (re, im) in enumerate(lanes):
        for g, (hr, hi) in enumerate(carry[li]):
            h_ref[g * SUB:(g + 1) * SUB, re] = hr
            h_ref[g * SUB:(g + 1) * SUB, im] = hi


def _s5_call(u, bblk, a_re, a_im, cblk, dsk, gluw, glub, ctx_len):
    b, lt, _ = u.shape
    tc = S5_C
    n_all, n_ctx = lt // tc, ctx_len // tc
    tok = lambda d: pl.BlockSpec((S5_NB, tc, S5_W), lambda g, c: (g, _dir_chunk(d, c, n_ctx, n_all), 0))
    par = lambda d: [pl.BlockSpec((None, S5_W, 2 * S5_NS), lambda g, c: (d, 0, 0)),
                     pl.BlockSpec((None, SUB, S5_NS), lambda g, c: (d, 0, 0)),
                     pl.BlockSpec((None, SUB, S5_NS), lambda g, c: (d, 0, 0)),
                     _const_spec((2 * S5_NS, S5_W))]
    common = dict(
        grid=(b // S5_NB, n_all),
        out_shape=jax.ShapeDtypeStruct((b, lt, S5_W), F32),
        scratch_shapes=[pltpu.VMEM((tc * S5_NB, 2 * S5_NS), F32),
                        pltpu.VMEM((tc * S5_NB, 2 * S5_NS), BF16),
                        pltpu.VMEM((S5_NB, 2 * S5_NS), F32)],
        compiler_params=_cparams(2))
    yf = pl.pallas_call(
        functools.partial(_s5_kernel, tc=tc, backward=False),
        in_specs=[tok(0)] + par(0), out_specs=tok(0), **common,
    )(u, bblk, a_re, a_im, cblk)
    return pl.pallas_call(
        functools.partial(_s5_kernel, tc=tc, backward=True),
        in_specs=[tok(1)] + par(1) + [tok(1), _const_spec((1, S5_W)), _const_spec((S5_W, S5_W)),
                                      _const_spec((1, S5_W))],
        out_specs=tok(1), **common,
    )(u, bblk, a_re, a_im, cblk, yf, dsk, gluw, glub)


_NT = (((1,), (1,)), ((), ()))
_TN = (((0,), (0,)), ((), ()))


def _ret_kernel(*refs, backward):
    if backward:
        (p_ref, dm_ref, qd_ref, kd_ref, cd_ref, hm_ref, bm_ref, of_ref, g_ref, real_ref, gn_ref,
         y_ref, s_ref) = refs
    else:
        p_ref, dm_ref, qd_ref, kd_ref, cd_ref, hm_ref, bm_ref, o_ref, s_ref = refs

    @pl.when(pl.program_id(1) == 0)
    def _():
        s_ref[...] = jnp.zeros_like(s_ref)

    def chunk(bi):
        qb = p_ref[bi, :, 0:QK_P]
        kb = p_ref[bi, :, QK_P:2 * QK_P]
        vb = p_ref[bi, :, 2 * QK_P:2 * QK_P + V_P]
        st = s_ref[bi]
        inter = lax.dot_general(qb * qd_ref[...], st.astype(BF16) * bm_ref[...], _NT,
                                preferred_element_type=F32)
        for h in range(HEADS):
            att = lax.dot_general(qb * hm_ref[h], kb, _NT, preferred_element_type=F32)
            att = att.astype(BF16) * dm_ref[h]
            sl = slice(h * LANE, (h + 1) * LANE)
            o = jnp.dot(att, vb[:, sl], preferred_element_type=F32) + inter[:, sl]
            if backward:
                o = o + of_ref[bi, :, sl]
                mu = jnp.sum(o, axis=-1, keepdims=True) * (1.0 / DV)
                dlt = (o - mu) * real_ref[:, sl]
                var = jnp.sum(dlt * dlt, axis=-1, keepdims=True) * (1.0 / DV)
                y = dlt * lax.rsqrt(var + EPS) * gn_ref[:, sl] * _silu(g_ref[bi, :, sl].astype(F32))
                y_ref[bi, :, sl] = y.astype(BF16)
            else:
                o_ref[bi, :, sl] = o
        kdv = lax.dot_general(vb, kb * kd_ref[...], _TN, preferred_element_type=F32)
        s_ref[bi] = st * cd_ref[...] + kdv

    for bi in range(MIX_NB):
        chunk(bi)


def _mixer_specs(b, lt, ctx_len, tc, off_qkv, off_g):
    n_all, n_ctx = lt // tc, ctx_len // tc
    tok = lambda d, w, blk: pl.BlockSpec(
        (MIX_NB, tc, w), lambda i, c: (i, _dir_chunk(d, c, n_ctx, n_all), blk))
    qkv = lambda d: tok(d, QKV_P, off_qkv // QKV_P)
    gsp = lambda d: tok(d, V_P, off_g // V_P)
    out = lambda d: tok(d, V_P, 0)
    state = pltpu.VMEM((MIX_NB, V_P, QK_P), F32)
    return (b // MIX_NB, n_all), tok, qkv, gsp, out, state


def _head_mask_rows(rows):
    return jnp.asarray(np.broadcast_to(TAB["head_mask"][:HEADS, None, :], (HEADS, rows, QK_P)), dtype=BF16)


def _ret_call(p, dmask, qdec, kdec, cdec, gn, ctx_len):
    b, lt, _ = p.shape
    tc = RET_C
    grid, _, qkv, gsp, out, state = _mixer_specs(b, lt, ctx_len, tc, OFF_RQKV, OFF_RG)
    ins = lambda d: [qkv(d), _const_spec((HEADS, tc, tc)), _const_spec((tc, QK_P)),
                     _const_spec((tc, QK_P)), _const_spec((1, QK_P)),
                     _const_spec((HEADS, tc, QK_P)), _const_spec((V_P, QK_P))]
    tabs = lambda d: (dmask[d].astype(BF16), qdec[d].astype(BF16), kdec[d].astype(BF16), cdec[d],
                      _head_mask_rows(tc), jnp.asarray(TAB["bm_t"], dtype=BF16))
    of = pl.pallas_call(
        functools.partial(_ret_kernel, backward=False),
        grid=grid, in_specs=ins(0), out_specs=out(0),
        out_shape=jax.ShapeDtypeStruct((b, lt, V_P), F32),
        scratch_shapes=[state], compiler_params=_cparams(2),
    )(p, *tabs(0))
    return pl.pallas_call(
        functools.partial(_ret_kernel, backward=True),
        grid=grid,
        in_specs=ins(1) + [out(1), gsp(1), _const_spec((1, V_P)), _const_spec((1, V_P))],
        out_specs=out(1),
        out_shape=jax.ShapeDtypeStruct((b, lt, V_P), BF16),
        scratch_shapes=[state], compiler_params=_cparams(2),
    )(p, *tabs(1), of, p, jnp.asarray(TAB["v_real"]).reshape(1, V_P), gn)


def _gla_kernel(*refs, backward):
    if backward:
        (p_ref, z_ref, gw_ref, gb_ref, tri_ref, hm_ref, bm_ref, of_ref, g_ref, gn_ref,
         y_ref, s_ref, oc_ref) = refs
    else:
        p_ref, z_ref, gw_ref, gb_ref, tri_ref, hm_ref, bm_ref, oc_ref, s_ref = refs

    @pl.when(pl.program_id(1) == 0)
    def _():
        s_ref[...] = jnp.zeros_like(s_ref)

    tri = tri_ref[...]
    mask = tri > 0

    def intra(bi):
        la = jax.nn.log_sigmoid(
            jnp.dot(z_ref[bi], gw_ref[...], preferred_element_type=F32) + gb_ref[...]) / GLA_TAU
        hi = la.astype(BF16)
        lo = (la - hi.astype(F32)).astype(BF16)
        bcum = (jnp.dot(tri, hi, preferred_element_type=F32)
                + jnp.dot(tri, lo, preferred_element_type=F32))
        q = p_ref[bi, :, 0:QK_P].astype(F32)
        k = p_ref[bi, :, QK_P:2 * QK_P].astype(F32)
        vb = p_ref[bi, :, 2 * QK_P:2 * QK_P + V_P]
        qtb = (q * jnp.exp(bcum)).astype(BF16)
        ktb = (k * jnp.exp(-bcum)).astype(BF16)
        for h in range(HEADS):
            pr = slice(h // 2 * LANE, (h // 2 + 1) * LANE)
            att = lax.dot_general(qtb[:, pr] * hm_ref[h % 2], ktb[:, pr], _NT,
                                  preferred_element_type=F32)
            att = jnp.where(mask, att.astype(BF16), jnp.zeros((), BF16))
            sl = slice(h * LANE, (h + 1) * LANE)
            oc_ref[bi, :, sl] = jnp.dot(att, vb[:, sl], preferred_element_type=F32)
        return bcum, k, qtb, vb

    parts = [intra(bi) for bi in range(MIX_NB)]

    n_sub = GLA_R // GLA_C
    n_pair = HEADS // 2
    order = range(n_sub - 1, -1, -1) if backward else range(n_sub)
    last_row = 0 if backward else GLA_C - 1
    ss = [[s_ref[bi, pi] for pi in range(n_pair)] for bi in range(MIX_NB)]
    for cc in order:
        r0 = cc * GLA_C
        rows = slice(r0, r0 + GLA_C)
        for bi, (bcum, k, qtb, vb) in enumerate(parts):
            bl = bcum[r0 + last_row:r0 + last_row + 1, :]
            kd = (k[rows] * jnp.exp(bl - bcum[rows])).astype(BF16)
            ebl = jnp.exp(bl)
            for pi in range(n_pair):
                pr = slice(pi * LANE, (pi + 1) * LANE)
                vs = slice(2 * pi * LANE, 2 * (pi + 1) * LANE)
                s = ss[bi][pi]
                oc_ref[bi, rows, vs] += lax.dot_general(qtb[rows, pr], s.astype(BF16) * bm_ref[...],
                                                        _NT, preferred_element_type=F32)
                kdv = lax.dot_general(vb[rows, vs], kd[:, pr], _TN, preferred_element_type=F32)
                ss[bi][pi] = s * ebl[:, pr] + kdv
    for bi in range(MIX_NB):
        for pi in range(n_pair):
            s_ref[bi, pi] = ss[bi][pi]

    if backward:
        for bi in range(MIX_NB):
            for h in range(HEADS):
                sl = slice(h * LANE, (h + 1) * LANE)
                o = oc_ref[bi, :, sl] + of_ref[bi, :, sl]
                ms = jnp.sum(o * o, axis=-1, keepdims=True) * (1.0 / DV)
                y = o * lax.rsqrt(ms + EPS) * gn_ref[:, sl] * _silu(g_ref[bi, :, sl].astype(F32))
                y_ref[bi, :, sl] = y.astype(BF16)


def _gla_call(p, gw, gb, tri, gn, ctx_len):
    b, lt, _ = p.shape
    tc = GLA_R
    grid, tok, qkv, gsp, out, _ = _mixer_specs(b, lt, ctx_len, tc, OFF_GQKV, OFF_GG)
    ins = lambda d: [qkv(d), tok(d, LANE, OFF_Z // LANE),
                     _const_spec((LANE, QK_P)), _const_spec((1, QK_P)), _const_spec((tc, tc)),
                     _const_spec((2, tc, LANE)), _const_spec((2 * LANE, LANE))]
    hm = jnp.asarray(np.broadcast_to(TAB["gla_hm"][:, None, :], (2, tc, LANE)), dtype=BF16)
    bm = jnp.asarray(TAB["gla_bm"], dtype=BF16)
    state = pltpu.VMEM((MIX_NB, HEADS // 2, 2 * LANE, LANE), F32)
    of = pl.pallas_call(
        functools.partial(_gla_kernel, backward=False),
        grid=grid, in_specs=ins(0), out_specs=out(0),
        out_shape=jax.ShapeDtypeStruct((b, lt, V_P), F32),
        scratch_shapes=[state], compiler_params=_cparams(2),
    )(p, p, gw[0], gb[0], tri[0], hm, bm)
    return pl.pallas_call(
        functools.partial(_gla_kernel, backward=True),
        grid=grid,
        in_specs=ins(1) + [out(1), gsp(1), _const_spec((1, V_P))],
        out_specs=out(1),
        out_shape=jax.ShapeDtypeStruct((b, lt, V_P), BF16),
        scratch_shapes=[state, pltpu.VMEM((MIX_NB, tc, V_P), F32)],
        compiler_params=_cparams(2),
    )(p, p, gw[1], gb[1], tri[1], hm, bm, of, p, gn)


FFN_T = 256


def _merge_ffn_kernel(*refs, final, split):
    (mod_ref, ys_ref, yr_ref, yg_ref, gate_ref, wbs_ref, wbr_ref, wbg_ref, wo_ref,
     gain_ref, wfi_ref, wfo_ref, fin_ref, o_ref, act_ref) = refs[1 + split:]

    def branch(y, w_ref, g0):
        gate = _sigmoid(gate_ref[:, g0:g0 + D].astype(F32))
        return gate * jnp.dot(y, w_ref[...], preferred_element_type=F32)

    m = (branch(ys_ref[...].astype(BF16), wbs_ref, 0) + branch(yr_ref[...], wbr_ref, D)
         + branch(yg_ref[...], wbg_ref, 2 * D))
    x = _residual_rows(refs, split) + mod_ref[2:3, :] * jnp.dot(m.astype(BF16), wo_ref[...],
                                                                  preferred_element_type=F32)
    h = _rms(x, gain_ref[...])
    hb = (h * (1.0 + mod_ref[4:5, :]) + mod_ref[3:4, :]).astype(BF16)
    for t in range(FFN_H // FFN_T):
        c0 = t * FFN_T
        a = jnp.dot(hb, wfi_ref[:, c0:c0 + FFN_T], preferred_element_type=F32)
        bq = jnp.dot(hb, wfi_ref[:, FFN_H + c0:FFN_H + c0 + FFN_T], preferred_element_type=F32)
        act_ref[:, t * FFN_T:(t + 1) * FFN_T] = (_silu(a) * bq).astype(BF16)
    y = x + mod_ref[5:6, :] * jnp.dot(act_ref[...], wfo_ref[...], preferred_element_type=F32)
    if final:
        y = _rms(y, fin_ref[...])
    o_ref[...] = y


def _merge_ffn_call(xs, modsel, ys5, yret, ygla, p, wbs, wbr, wbg, wo, gain, wfi, wfo, fin,
                    final, lat_only, ctx_len):
    split = isinstance(xs, tuple)
    xs = xs if split else (xs,)
    b = xs[0].shape[0]
    lt = sum(t.shape[1] for t in xs)
    j0 = ctx_len // ROW_T if lat_only else 0
    nt = lt // ROW_T - j0
    nh = FFN_H // FFN_T
    tok = lambda w: pl.BlockSpec((None, ROW_T, w), lambda i, j: (i, j + j0, 0))
    return pl.pallas_call(
        functools.partial(_merge_ffn_kernel, final=final, split=split),
        grid=(b, nt),
        in_specs=_residual_specs(split, j0) + [
                  pl.BlockSpec((None, None, SUB, D), lambda i, j: (i, jnp.minimum(j + j0, 1), 0, 0)),
                  tok(S5_W), tok(V_P), tok(V_P), tok(GATE_W),
                  _const_spec((S5_W, D)), _const_spec((V_P, D)), _const_spec((V_P, D)),
                  _const_spec((D, D)),
                  _const_spec((1, D)),
                  _const_spec((D, 2 * FFN_H)),
                  _const_spec((FFN_H, D)), _const_spec((1, D))],
        out_specs=pl.BlockSpec((None, ROW_T, D), lambda i, j: (i, j, 0)),
        out_shape=jax.ShapeDtypeStruct((b, nt * ROW_T, D), F32),
        scratch_shapes=[pltpu.VMEM((ROW_T, FFN_H), BF16)],
        compiler_params=_cparams(2),
    )(*xs, modsel, ys5, yret, ygla, p, wbs, wbr, wbg, wo,
      gain.reshape(1, D), wfi, wfo, fin.reshape(1, D))


def _s5_params(lam_re, lam_im, log_dt, b_re, b_im, c_re, c_im):
    dt = jnp.exp(log_dt)[..., None]
    mag = jnp.exp(lam_re * dt)
    a_re, a_im = mag * jnp.cos(lam_im * dt), mag * jnp.sin(lam_im * dt)
    den = lam_re * lam_re + lam_im * lam_im
    f_re = ((a_re - 1.0) * lam_re + a_im * lam_im) / den
    f_im = (a_im * lam_re - (a_re - 1.0) * lam_im) / den
    bb_re = f_re[..., None] * b_re - f_im[..., None] * b_im
    bb_im = f_re[..., None] * b_im + f_im[..., None] * b_re
    eye = jnp.eye(S5_GROUPS, dtype=F32)

    def blk_in(bb):
        t = jnp.einsum("dgpc,gh->dgchp", bb, eye)
        return t.reshape(2, S5_W, S5_NS)

    bblk = jnp.concatenate([blk_in(bb_re), blk_in(bb_im)], axis=-1).astype(BF16)

    def blk_out(cc):
        t = jnp.einsum("gcp,gh->gphc", cc, eye)
        return t.reshape(S5_NS, S5_W)

    cblk = jnp.concatenate([blk_out(c_re), -blk_out(c_im)], axis=0).astype(BF16)
    bc = lambda a: jnp.broadcast_to(a.reshape(2, 1, S5_NS), (2, SUB, S5_NS))
    return bblk, bc(a_re), bc(a_im), cblk


def _ret_tables(log_decay):
    tc = RET_C
    pos = jnp.arange(tc, dtype=F32)
    w = jnp.stack([pos, tc - 1.0 - pos])
    rel = w[:, :, None] - w[:, None, :]
    lg = log_decay[:, :, None, None]
    dmask = jnp.where(rel[:, None] >= 0, jnp.exp(jnp.maximum(rel[:, None], 0.0) * lg), 0.0)
    qk_head = TAB["qk_head"]
    lane_lg = jnp.where(jnp.asarray(qk_head >= 0),
                        jnp.take(log_decay, jnp.asarray(np.maximum(qk_head, 0)), axis=1), 0.0)
    qdec = jnp.exp((w[:, :, None] + 1.0) * lane_lg[:, None, :])
    kdec = jnp.exp((tc - 1.0 - w[:, :, None]) * lane_lg[:, None, :])
    cdec = jnp.exp(tc * lane_lg)[:, None, :]
    return dmask, qdec, kdec, cdec


def _rope_tables(seq, ctx_len):
    rows = seq // GRID_W
    nf = DK // 4
    inv = 1.0 / (ROPE_BASE ** (np.arange(nf, dtype=np.float32) / nf))
    r = np.repeat(np.arange(rows, dtype=np.float32), GRID_W)
    col = np.tile(np.arange(GRID_W, dtype=np.float32), rows)
    ang = np.concatenate([r[:, None] * inv, col[:, None] * inv], axis=-1)
    cos_t = np.ones((ctx_len + seq, LANE), np.float32)
    sin_t = np.zeros((ctx_len + seq, LANE), np.float32)
    for h in range(HEADS):
        cos_t[ctx_len:, h * 24:(h + 1) * 24] = np.cos(ang)
        sin_t[ctx_len:, h * 24:(h + 1) * 24] = np.sin(ang)
    return jnp.asarray(cos_t), jnp.asarray(sin_t)


def _gla_tables(gate_w, gate_b):
    gw = _gather_cols(gate_w, TAB["gla_src"])
    gw_full = jnp.zeros((2, LANE, QK_P), F32)
    gw_full = gw_full.at[0, 0:GLA_RANK].set(gw[0]).at[1, GLA_RANK:2 * GLA_RANK].set(gw[1])
    gb = _gather_cols(gate_b, TAB["gla_src"])[:, None, :]
    i = np.arange(GLA_R)
    same = (i[:, None] // GLA_C) == (i[None, :] // GLA_C)
    tri = np.stack([same & (i[:, None] >= i[None, :]), same & (i[:, None] <= i[None, :])]).astype(np.float32)
    return gw_full.astype(BF16), gb, jnp.asarray(tri, dtype=BF16)


def _pad_rows(w, src):
    return _gather_cols(w.T, src).T


def kernel(x, c, ctx, c_ctx, w_mod, b_mod, norm_mix, norm_ffn, w_in, s5_lam_re, s5_lam_im, s5_log_dt, s5_b_re, s5_b_im, s5_c_re, s5_c_im, s5_d, s5_glu_w, s5_glu_b, ret_log_decay, ret_gn, gla_gate_w, gla_gate_b, gla_norm, w_br_s5, w_br_ret, w_br_gla, w_out, w_ffn_in, w_ffn_out, norm_final):
    b, seq, _ = x.shape
    ctx_len = ctx.shape[1]
    depth = w_mod.shape[0]
    assert b % S5_NB == 0 and b % MIX_NB == 0 and seq % RET_C == 0 and seq % GRID_W == 0
    assert ctx_len == ROW_T == RET_C

    xs = (ctx, x)
    rows = ((b + 1 + SUB - 1) // SUB) * SUB
    c_all = jnp.zeros((rows, D), F32).at[:b].set(c).at[b].set(c_ctx)
    cos_t, sin_t = _rope_tables(seq, ctx_len)
    nh = FFN_H // FFN_T

    for l in range(depth):
        last = l == depth - 1
        mods = _mod_call(c_all, w_mod[l], b_mod[l])
        m_lat = mods[:b].reshape(b, N_MOD, D)
        m_ctx = jnp.broadcast_to(mods[b].reshape(1, N_MOD, D), (b, N_MOD, D))
        modsel = jnp.stack([m_ctx, m_lat], axis=1)
        modsel = jnp.pad(modsel, ((0, 0), (0, 0), (0, SUB - N_MOD), (0, 0)))

        w_all = _gather_cols(w_in[l], TAB["src"]).astype(BF16)
        p, u = _inproj_call(xs, modsel, norm_mix[l], cos_t, sin_t, w_all)

        bblk, a_re, a_im, cblk = _s5_params(s5_lam_re[l], s5_lam_im[l], s5_log_dt[l],
                                            s5_b_re[l], s5_b_im[l], s5_c_re[l], s5_c_im[l])
        ys5 = _s5_call(u, bblk, a_re, a_im, cblk,
                       s5_d[l].reshape(1, S5_W), s5_glu_w[l].astype(BF16),
                       s5_glu_b[l].reshape(1, S5_W), ctx_len)

        dmask, qdec, kdec, cdec = _ret_tables(ret_log_decay[l])
        yret = _ret_call(p, dmask, qdec, kdec, cdec,
                         _gather_cols(ret_gn[l], TAB["vsrc"]).reshape(1, V_P), ctx_len)

        gw, gb, tri = _gla_tables(gla_gate_w[l], gla_gate_b[l])
        ygla = _gla_call(p, gw, gb, tri, _gather_cols(gla_norm[l], TAB["vsrc"]).reshape(1, V_P),
                         ctx_len)

        xs = _merge_ffn_call(xs, modsel, ys5, yret, ygla, p, w_br_s5[l].astype(BF16),
                             _pad_rows(w_br_ret[l], TAB["vsrc"]).astype(BF16),
                             _pad_rows(w_br_gla[l], TAB["vsrc"]).astype(BF16),
                             w_out[l].astype(BF16), norm_ffn[l], w_ffn_in[l].astype(BF16),
                             w_ffn_out[l].astype(BF16), norm_final, last, last, ctx_len)
    return xs
```

```python
import functools

import numpy as np
import jax
import jax.numpy as jnp
from jax import lax
from jax.experimental import pallas as pl
from jax.experimental.pallas import tpu as pltpu

F32 = jnp.float32
BF16 = jnp.bfloat16

D = 1024
EPS = 1e-6
N_MOD = 6
GRID_W = 64
S5_W = 256
S5_GROUPS = 16
S5_STATE = 64
S5_NS = S5_GROUPS * S5_STATE
HEADS = 4
DK = 48
DV = 96
QK_W = HEADS * DK
V_W = HEADS * DV
GLA_RANK = 16
GLA_TAU = 16.0
ROPE_BASE = 10000.0
FFN_H = 2816

LANE = 128
SUB = 8
QK_P = 2 * LANE
V_P = HEADS * LANE
QKV_P = 2 * QK_P + V_P
GATE_W = 3 * D

OFF_GATE = 0
OFF_RQKV = GATE_W
OFF_GQKV = OFF_RQKV + QKV_P
OFF_RG = OFF_GQKV + QKV_P
OFF_GG = OFF_RG + V_P
OFF_Z = OFF_GG + V_P
NP = OFF_Z + LANE
OFF_S5 = NP
NW = NP + S5_W

ROW_T = 256
RET_C = 256
GLA_C = 64
GLA_R = 256
MIX_NB = 4
S5_C = 64
V7X_VMEM_BYTES = 64 * 1024 * 1024
VMEM_LIMIT = V7X_VMEM_BYTES * 7 // 8

IN_SIZES = (S5_W, QK_W, QK_W, V_W, V_W, QK_W, QK_W, V_W, V_W, GLA_RANK, GLA_RANK, D, D, D)
IN_OFFS = np.concatenate([[0], np.cumsum(IN_SIZES)]).astype(np.int64)


def _head_lane():
    m = np.zeros((HEADS, DK), np.int64)
    for h in range(HEADS):
        for i in range(DK):
            m[h, i] = (i % 2) * LANE + h * (DK // 2) + i // 2
    return m


HEAD_LANE = _head_lane()
HALF = LANE // 2
GLA_LANE = np.array([[(h // 2) * LANE + (h % 2) * HALF + i for i in range(DK)] for h in range(HEADS)])


def _static_tables():
    src = np.full((NW,), -1, np.int64)
    src[OFF_GATE:OFF_GATE + GATE_W] = IN_OFFS[11] + np.arange(GATE_W)
    for qkv, gg, lane, iq, ik, iv, ig in ((OFF_RQKV, OFF_RG, HEAD_LANE, 1, 2, 3, 4),
                                          (OFF_GQKV, OFF_GG, GLA_LANE, 5, 6, 7, 8)):
        for h in range(HEADS):
            for i in range(DK):
                src[qkv + lane[h, i]] = IN_OFFS[iq] + h * DK + i
                src[qkv + QK_P + lane[h, i]] = IN_OFFS[ik] + h * DK + i
            for j in range(DV):
                src[qkv + 2 * QK_P + h * LANE + j] = IN_OFFS[iv] + h * DV + j
                src[gg + h * LANE + j] = IN_OFFS[ig] + h * DV + j
    src[OFF_Z:OFF_Z + GLA_RANK] = IN_OFFS[9] + np.arange(GLA_RANK)
    src[OFF_Z + GLA_RANK:OFF_Z + 2 * GLA_RANK] = IN_OFFS[10] + np.arange(GLA_RANK)
    src[OFF_S5:OFF_S5 + S5_W] = np.arange(S5_W)
    vsrc = np.full((V_P,), -1, np.int64)
    for h in range(HEADS):
        vsrc[h * LANE:h * LANE + DV] = h * DV + np.arange(DV)
    qk_head = np.full((QK_P,), -1, np.int64)
    qk_src = np.full((QK_P,), -1, np.int64)
    for h in range(HEADS):
        for i in range(DK):
            qk_head[HEAD_LANE[h, i]] = h
            qk_src[HEAD_LANE[h, i]] = h * DK + i
    v_head = np.repeat(np.arange(HEADS), LANE)
    v_real = (np.arange(V_P) % LANE) < DV
    head_mask = np.zeros((SUB, QK_P), np.float32)
    for h in range(HEADS):
        head_mask[h] = (qk_head == h)
    bm_t = (v_head[:, None] == qk_head[None, :]).astype(np.float32)
    gla_src = np.full((QK_P,), -1, np.int64)
    for h in range(HEADS):
        gla_src[GLA_LANE[h]] = h * DK + np.arange(DK)
    lane_in_pair = np.arange(LANE)
    gla_hm = np.stack([(lane_in_pair // HALF == r) & (lane_in_pair % HALF < DK) for r in range(2)])
    gla_bm = (np.arange(2 * LANE)[:, None] // LANE) == (lane_in_pair[None, :] // HALF)
    return dict(src=src, vsrc=vsrc, qk_head=qk_head, qk_src=qk_src, head_mask=head_mask,
                bm_t=bm_t, v_real=v_real.astype(np.float32), gla_src=gla_src,
                gla_hm=gla_hm.astype(np.float32), gla_bm=gla_bm.astype(np.float32))


TAB = _static_tables()


def _gather_cols(w, src):
    valid = jnp.asarray(src >= 0)
    out = jnp.take(w, jnp.asarray(np.maximum(src, 0)), axis=-1)
    return jnp.where(valid, out, jnp.zeros((), w.dtype))


def _cparams(n_axes):
    return pltpu.CompilerParams(dimension_semantics=("arbitrary",) * n_axes,
                                vmem_limit_bytes=VMEM_LIMIT)


def _const_spec(shape):
    nd = len(shape)
    return pl.BlockSpec(shape, lambda *_: (0,) * nd, pipeline_mode=pl.Buffered(1))


def _dir_chunk(d, c, n_ctx, n_all):
    bwd = jnp.where(c < n_ctx, n_ctx - 1 - c, n_all + n_ctx - 1 - c)
    return jnp.where(d == 0, c, bwd)


def _sigmoid(x):
    return 0.5 * jnp.tanh(0.5 * x) + 0.5


def _silu(x):
    return x * _sigmoid(x)


def _mod_kernel(c_ref, w_ref, b_ref, o_ref):
    o_ref[...] = jnp.dot(_silu(c_ref[...]), w_ref[...], preferred_element_type=F32,
                         precision=lax.Precision.HIGHEST) + b_ref[...]


def _mod_call(c_all, w_mod_l, b_mod_l):
    rows = c_all.shape[0]
    tn = D
    return pl.pallas_call(
        _mod_kernel,
        grid=(N_MOD * D // tn,),
        in_specs=[pl.BlockSpec((rows, D), lambda j: (0, 0)),
                  pl.BlockSpec((D, tn), lambda j: (0, j)),
                  pl.BlockSpec((1, tn), lambda j: (0, j))],
        out_specs=pl.BlockSpec((rows, tn), lambda j: (0, j)),
        out_shape=jax.ShapeDtypeStruct((rows, N_MOD * D), F32),
        compiler_params=_cparams(1),
    )(c_all, w_mod_l, b_mod_l.reshape(1, -1))


def _rms(x, gain):
    return x * lax.rsqrt(jnp.mean(x * x, axis=-1, keepdims=True) + EPS) * gain


def _residual_rows(refs, split):
    if not split:
        return refs[0][...]
    return jnp.where(pl.program_id(1) == 0, refs[0][...], refs[1][...])


def _residual_specs(split, j0):
    if not split:
        return [pl.BlockSpec((None, ROW_T, D), lambda i, j: (i, j + j0, 0))]
    return [pl.BlockSpec((None, ROW_T, D), lambda i, j: (i, 0, 0)),
            pl.BlockSpec((None, ROW_T, D), lambda i, j: (i, jnp.maximum(j - 1, 0), 0))]


def _inproj_kernel(*refs, split):
    mod_ref, gain_ref, cos_ref, sin_ref, w_ref, p_ref, u_ref = refs[1 + split:]
    h = _rms(_residual_rows(refs, split), gain_ref[...])
    h = h * (1.0 + mod_ref[1:2, :]) + mod_ref[0:1, :]
    hb = h.astype(BF16)

    def proj(c0, c1):
        return jnp.dot(hb, w_ref[:, c0:c1], preferred_element_type=F32)

    for c0 in range(0, GATE_W, D):
        p_ref[:, c0:c0 + D] = proj(c0, c0 + D).astype(BF16)
    cs = cos_ref[...]
    sn = sin_ref[...]
    qk = proj(OFF_RQKV, OFF_RQKV + 2 * QK_P)
    for o, scale in ((0, 1.0), (QK_P, DK ** -0.5)):
        a = qk[:, o:o + LANE] * scale
        b = qk[:, o + LANE:o + 2 * LANE] * scale
        p_ref[:, OFF_RQKV + o:OFF_RQKV + o + LANE] = (a * cs - b * sn).astype(BF16)
        p_ref[:, OFF_RQKV + o + LANE:OFF_RQKV + o + 2 * LANE] = (a * sn + b * cs).astype(BF16)
    c0 = OFF_RQKV + 2 * QK_P
    p_ref[:, c0:OFF_GQKV] = proj(c0, OFF_GQKV).astype(BF16)
    p_ref[:, OFF_GQKV:OFF_GQKV + QK_P] = (proj(OFF_GQKV, OFF_GQKV + QK_P) * (DK ** -0.5)).astype(BF16)
    c0 = OFF_GQKV + QK_P
    p_ref[:, c0:OFF_RG] = proj(c0, OFF_RG).astype(BF16)
    p_ref[:, OFF_RG:NP] = proj(OFF_RG, NP).astype(BF16)
    u_ref[...] = proj(OFF_S5, OFF_S5 + S5_W)


def _inproj_call(xs, modsel, gain, cos_t, sin_t, w_all):
    split = isinstance(xs, tuple)
    xs = xs if split else (xs,)
    b = xs[0].shape[0]
    lt = sum(t.shape[1] for t in xs)
    nt = lt // ROW_T
    return pl.pallas_call(
        functools.partial(_inproj_kernel, split=split),
        grid=(b, nt),
        in_specs=_residual_specs(split, 0) + [
                  pl.BlockSpec((None, None, SUB, D), lambda i, j: (i, jnp.minimum(j, 1), 0, 0)),
                  _const_spec((1, D)),
                  pl.BlockSpec((ROW_T, LANE), lambda i, j: (j, 0)),
                  pl.BlockSpec((ROW_T, LANE), lambda i, j: (j, 0)),
                  _const_spec((D, NW))],
        out_specs=[pl.BlockSpec((None, ROW_T, NP), lambda i, j: (i, j, 0)),
                   pl.BlockSpec((None, ROW_T, S5_W), lambda i, j: (i, j, 0))],
        out_shape=[jax.ShapeDtypeStruct((b, lt, NP), BF16),
                   jax.ShapeDtypeStruct((b, lt, S5_W), F32)],
        compiler_params=_cparams(2),
    )(*xs, modsel, gain.reshape(1, D), cos_t, sin_t, w_all)


S5_NB = 2 * SUB
S5_RB = 256


def _s5_kernel(u_ref, bb_ref, are_ref, aim_ref, cb_ref, y_ref, x_ref, hb_ref, h_ref, *, tc, backward):
    rows = tc * S5_NB
    steps = S5_RB // S5_NB
    n_blk = rows // S5_RB
    order = range(n_blk - 1, -1, -1) if backward else range(n_blk)

    @pl.when(pl.program_id(1) == 0)
    def _():
        h_ref[...] = jnp.zeros_like(h_ref)

    u = pltpu.einshape("bts->(tb)s", u_ref[...])
    ub = u.astype(BF16)
    for k in order:
        rs = slice(k * S5_RB, (k + 1) * S5_RB)
        x_ref[rs, :] = jnp.dot(ub[rs], bb_ref[...], preferred_element_type=F32)

    half = S5_NS // 2
    lanes = [(slice(lo, lo + half), slice(S5_NS + lo, S5_NS + lo + half)) for lo in (0, half)]
    coef = [(are_ref[:, re], aim_ref[:, re]) for re, _ in lanes]
    carry = [[(h_ref[g * SUB:(g + 1) * SUB, re], h_ref[g * SUB:(g + 1) * SUB, im])
              for g in range(S5_NB // SUB)] for re, im in lanes]
    for k in order:
        ts = range(k * steps, (k + 1) * steps)
        for t in (reversed(ts) if backward else ts):
            r0 = t * S5_NB
            for li, (re, im) in enumerate(lanes):
                ar, ai = coef[li]
                new = []
                for g, (hr, hi) in enumerate(carry[li]):
                    rw = slice(r0 + g * SUB, r0 + (g + 1) * SUB)
                    new.append((ar * hr - ai * hi + x_ref[rw, re], ar * hi + ai * hr + x_ref[rw, im]))
                carry[li] = new
                hb_ref[r0:r0 + S5_NB, re] = jnp.concatenate([n[0] for n in new], axis=0).astype(BF16)
                hb_ref[r0:r0 + S5_NB, im] = jnp.concatenate([n[1] for n in new], axis=0).astype(BF16)
        rs = slice(k * S5_RB, (k + 1) * S5_RB)
        y = jnp.dot(hb_ref[rs, :], cb_ref[...], preferred_element_type=F32)
        y_ref[:, k * steps:(k + 1) * steps, :] = pltpu.einshape("(tb)s->bts", y, b=S5_NB)
    for li, (re, im) in enumerate(lanes):
        for g, (hr, hi) in enumerate(carry[li]):
            h_ref[g * SUB:(g + 1) * SUB, re] = hr
            h_ref[g * SUB:(g + 1) * SUB, im] = hi


def _s5_call(u, bblk, a_re, a_im, cblk, ctx_len):
    b, lt, _ = u.shape
    tc = S5_C
    n_all, n_ctx = lt // tc, ctx_len // tc
    tok = lambda d: pl.BlockSpec((S5_NB, tc, S5_W), lambda g, c: (g, _dir_chunk(d, c, n_ctx, n_all), 0))
    par = lambda d: [pl.BlockSpec((None, S5_W, 2 * S5_NS), lambda g, c: (d, 0, 0)),
                     pl.BlockSpec((None, SUB, S5_NS), lambda g, c: (d, 0, 0)),
                     pl.BlockSpec((None, SUB, S5_NS), lambda g, c: (d, 0, 0)),
                     _const_spec((2 * S5_NS, S5_W))]
    common = dict(
        grid=(b // S5_NB, n_all),
        out_shape=jax.ShapeDtypeStruct((b, lt, S5_W), F32),
        scratch_shapes=[pltpu.VMEM((tc * S5_NB, 2 * S5_NS), F32),
                        pltpu.VMEM((tc * S5_NB, 2 * S5_NS), BF16),
                        pltpu.VMEM((S5_NB, 2 * S5_NS), F32)],
        compiler_params=_cparams(2))
    return tuple(
        pl.pallas_call(
            functools.partial(_s5_kernel, tc=tc, backward=bool(d)),
            in_specs=[tok(d)] + par(d), out_specs=tok(d), **common,
        )(u, bblk, a_re, a_im, cblk)
        for d in (0, 1))


_NT = (((1,), (1,)), ((), ()))
_TN = (((0,), (0,)), ((), ()))


def _ret_kernel(*refs, backward):
    if backward:
        (p_ref, dm_ref, qd_ref, kd_ref, cd_ref, hm_ref, bm_ref, of_ref, g_ref, real_ref, gn_ref,
         y_ref, s_ref) = refs
    else:
        p_ref, dm_ref, qd_ref, kd_ref, cd_ref, hm_ref, bm_ref, o_ref, s_ref = refs

    @pl.when(pl.program_id(1) == 0)
    def _():
        s_ref[...] = jnp.zeros_like(s_ref)

    def chunk(bi):
        qb = p_ref[bi, :, 0:QK_P]
        kb = p_ref[bi, :, QK_P:2 * QK_P]
        vb = p_ref[bi, :, 2 * QK_P:2 * QK_P + V_P]
        st = s_ref[bi]
        inter = lax.dot_general(qb * qd_ref[...], st.astype(BF16) * bm_ref[...], _NT,
                                preferred_element_type=F32)
        for h in range(HEADS):
            att = lax.dot_general(qb * hm_ref[h], kb, _NT, preferred_element_type=F32)
            att = att.astype(BF16) * dm_ref[h]
            sl = slice(h * LANE, (h + 1) * LANE)
            o = jnp.dot(att, vb[:, sl], preferred_element_type=F32) + inter[:, sl]
            if backward:
                o = o + of_ref[bi, :, sl]
                mu = jnp.sum(o, axis=-1, keepdims=True) * (1.0 / DV)
                dlt = (o - mu) * real_ref[:, sl]
                var = jnp.sum(dlt * dlt, axis=-1, keepdims=True) * (1.0 / DV)
                y = dlt * lax.rsqrt(var + EPS) * gn_ref[:, sl] * _silu(g_ref[bi, :, sl].astype(F32))
                y_ref[bi, :, sl] = y.astype(BF16)
            else:
                o_ref[bi, :, sl] = o
        kdv = lax.dot_general(vb, kb * kd_ref[...], _TN, preferred_element_type=F32)
        s_ref[bi] = st * cd_ref[...] + kdv

    for bi in range(MIX_NB):
        chunk(bi)


def _mixer_specs(b, lt, ctx_len, tc, off_qkv, off_g):
    n_all, n_ctx = lt // tc, ctx_len // tc
    tok = lambda d, w, blk: pl.BlockSpec(
        (MIX_NB, tc, w), lambda i, c: (i, _dir_chunk(d, c, n_ctx, n_all), blk))
    qkv = lambda d: tok(d, QKV_P, off_qkv // QKV_P)
    gsp = lambda d: tok(d, V_P, off_g // V_P)
    out = lambda d: tok(d, V_P, 0)
    state = pltpu.VMEM((MIX_NB, V_P, QK_P), F32)
    return (b // MIX_NB, n_all), tok, qkv, gsp, out, state


def _head_mask_rows(rows):
    return jnp.asarray(np.broadcast_to(TAB["head_mask"][:HEADS, None, :], (HEADS, rows, QK_P)), dtype=BF16)


def _ret_call(p, dmask, qdec, kdec, cdec, gn, ctx_len):
    b, lt, _ = p.shape
    tc = RET_C
    grid, _, qkv, gsp, out, state = _mixer_specs(b, lt, ctx_len, tc, OFF_RQKV, OFF_RG)
    ins = lambda d: [qkv(d), _const_spec((HEADS, tc, tc)), _const_spec((tc, QK_P)),
                     _const_spec((tc, QK_P)), _const_spec((1, QK_P)),
                     _const_spec((HEADS, tc, QK_P)), _const_spec((V_P, QK_P))]
    tabs = lambda d: (dmask[d].astype(BF16), qdec[d].astype(BF16), kdec[d].astype(BF16), cdec[d],
                      _head_mask_rows(tc), jnp.asarray(TAB["bm_t"], dtype=BF16))
    of = pl.pallas_call(
        functools.partial(_ret_kernel, backward=False),
        grid=grid, in_specs=ins(0), out_specs=out(0),
        out_shape=jax.ShapeDtypeStruct((b, lt, V_P), F32),
        scratch_shapes=[state], compiler_params=_cparams(2),
    )(p, *tabs(0))
    return pl.pallas_call(
        functools.partial(_ret_kernel, backward=True),
        grid=grid,
        in_specs=ins(1) + [out(1), gsp(1), _const_spec((1, V_P)), _const_spec((1, V_P))],
        out_specs=out(1),
        out_shape=jax.ShapeDtypeStruct((b, lt, V_P), BF16),
        scratch_shapes=[state], compiler_params=_cparams(2),
    )(p, *tabs(1), of, p, jnp.asarray(TAB["v_real"]).reshape(1, V_P), gn)


def _gla_kernel(*refs, backward):
    if backward:
        (p_ref, z_ref, gw_ref, gb_ref, tri_ref, hm_ref, bm_ref, of_ref, g_ref, gn_ref,
         y_ref, s_ref, oc_ref) = refs
    else:
        p_ref, z_ref, gw_ref, gb_ref, tri_ref, hm_ref, bm_ref, oc_ref, s_ref = refs

    @pl.when(pl.program_id(1) == 0)
    def _():
        s_ref[...] = jnp.zeros_like(s_ref)

    tri = tri_ref[...]
    mask = tri > 0

    def intra(bi):
        la = jax.nn.log_sigmoid(
            jnp.dot(z_ref[bi], gw_ref[...], preferred_element_type=F32) + gb_ref[...]) / GLA_TAU
        hi = la.astype(BF16)
        lo = (la - hi.astype(F32)).astype(BF16)
        bcum = (jnp.dot(tri, hi, preferred_element_type=F32)
                + jnp.dot(tri, lo, preferred_element_type=F32))
        q = p_ref[bi, :, 0:QK_P].astype(F32)
        k = p_ref[bi, :, QK_P:2 * QK_P].astype(F32)
        vb = p_ref[bi, :, 2 * QK_P:2 * QK_P + V_P]
        qtb = (q * jnp.exp(bcum)).astype(BF16)
        ktb = (k * jnp.exp(-bcum)).astype(BF16)
        for h in range(HEADS):
            pr = slice(h // 2 * LANE, (h // 2 + 1) * LANE)
            att = lax.dot_general(qtb[:, pr] * hm_ref[h % 2], ktb[:, pr], _NT,
                                  preferred_element_type=F32)
            att = jnp.where(mask, att.astype(BF16), jnp.zeros((), BF16))
            sl = slice(h * LANE, (h + 1) * LANE)
            oc_ref[bi, :, sl] = jnp.dot(att, vb[:, sl], preferred_element_type=F32)
        return bcum, k, qtb, vb

    parts = [intra(bi) for bi in range(MIX_NB)]

    n_sub = GLA_R // GLA_C
    n_pair = HEADS // 2
    order = range(n_sub - 1, -1, -1) if backward else range(n_sub)
    last_row = 0 if backward else GLA_C - 1
    ss = [[s_ref[bi, pi] for pi in range(n_pair)] for bi in range(MIX_NB)]
    for cc in order:
        r0 = cc * GLA_C
        rows = slice(r0, r0 + GLA_C)
        for bi, (bcum, k, qtb, vb) in enumerate(parts):
            bl = bcum[r0 + last_row:r0 + last_row + 1, :]
            kd = (k[rows] * jnp.exp(bl - bcum[rows])).astype(BF16)
            ebl = jnp.exp(bl)
            for pi in range(n_pair):
                pr = slice(pi * LANE, (pi + 1) * LANE)
                vs = slice(2 * pi * LANE, 2 * (pi + 1) * LANE)
                s = ss[bi][pi]
                oc_ref[bi, rows, vs] += lax.dot_general(qtb[rows, pr], s.astype(BF16) * bm_ref[...],
                                                        _NT, preferred_element_type=F32)
                kdv = lax.dot_general(vb[rows, vs], kd[:, pr], _TN, preferred_element_type=F32)
                ss[bi][pi] = s * ebl[:, pr] + kdv
    for bi in range(MIX_NB):
        for pi in range(n_pair):
            s_ref[bi, pi] = ss[bi][pi]

    if backward:
        for bi in range(MIX_NB):
            for h in range(HEADS):
                sl = slice(h * LANE, (h + 1) * LANE)
                o = oc_ref[bi, :, sl] + of_ref[bi, :, sl]
                ms = jnp.sum(o * o, axis=-1, keepdims=True) * (1.0 / DV)
                y = o * lax.rsqrt(ms + EPS) * gn_ref[:, sl] * _silu(g_ref[bi, :, sl].astype(F32))
                y_ref[bi, :, sl] = y.astype(BF16)


def _gla_call(p, gw, gb, tri, gn, ctx_len):
    b, lt, _ = p.shape
    tc = GLA_R
    grid, tok, qkv, gsp, out, _ = _mixer_specs(b, lt, ctx_len, tc, OFF_GQKV, OFF_GG)
    ins = lambda d: [qkv(d), tok(d, LANE, OFF_Z // LANE),
                     _const_spec((LANE, QK_P)), _const_spec((1, QK_P)), _const_spec((tc, tc)),
                     _const_spec((2, tc, LANE)), _const_spec((2 * LANE, LANE))]
    hm = jnp.asarray(np.broadcast_to(TAB["gla_hm"][:, None, :], (2, tc, LANE)), dtype=BF16)
    bm = jnp.asarray(TAB["gla_bm"], dtype=BF16)
    state = pltpu.VMEM((MIX_NB, HEADS // 2, 2 * LANE, LANE), F32)
    of = pl.pallas_call(
        functools.partial(_gla_kernel, backward=False),
        grid=grid, in_specs=ins(0), out_specs=out(0),
        out_shape=jax.ShapeDtypeStruct((b, lt, V_P), F32),
        scratch_shapes=[state], compiler_params=_cparams(2),
    )(p, p, gw[0], gb[0], tri[0], hm, bm)
    return pl.pallas_call(
        functools.partial(_gla_kernel, backward=True),
        grid=grid,
        in_specs=ins(1) + [out(1), gsp(1), _const_spec((1, V_P))],
        out_specs=out(1),
        out_shape=jax.ShapeDtypeStruct((b, lt, V_P), BF16),
        scratch_shapes=[state, pltpu.VMEM((MIX_NB, tc, V_P), F32)],
        compiler_params=_cparams(2),
    )(p, p, gw[1], gb[1], tri[1], hm, bm, of, p, gn)


FFN_T = 256


def _merge_ffn_kernel(*refs, final, split):
    (mod_ref, u_ref, sf_ref, sb_ref, yr_ref, yg_ref, gate_ref,
     dsk_ref, gluw_ref, glub_ref, wbs_ref, wbr_ref, wbg_ref, wo_ref,
     gain_ref, wfi_ref, wfo_ref, fin_ref, o_ref, act_ref) = refs[1 + split:]

    def branch(y, w_ref, g0):
        gate = _sigmoid(gate_ref[:, g0:g0 + D].astype(F32))
        return gate * jnp.dot(y, w_ref[...], preferred_element_type=F32)

    ys = jax.nn.gelu(sf_ref[...] + sb_ref[...] + dsk_ref[...] * u_ref[...])
    glu = jnp.dot(ys.astype(BF16), gluw_ref[...], preferred_element_type=F32) + glub_ref[...]
    ys = (ys * _sigmoid(glu)).astype(BF16)
    m = (branch(ys, wbs_ref, 0) + branch(yr_ref[...], wbr_ref, D)
         + branch(yg_ref[...], wbg_ref, 2 * D))
    x = _residual_rows(refs, split) + mod_ref[2:3, :] * jnp.dot(m.astype(BF16), wo_ref[...],
                                                                  preferred_element_type=F32)
    h = _rms(x, gain_ref[...])
    hb = (h * (1.0 + mod_ref[4:5, :]) + mod_ref[3:4, :]).astype(BF16)
    for t in range(FFN_H // FFN_T):
        c0 = t * FFN_T
        a = jnp.dot(hb, wfi_ref[:, c0:c0 + FFN_T], preferred_element_type=F32)
        bq = jnp.dot(hb, wfi_ref[:, FFN_H + c0:FFN_H + c0 + FFN_T], preferred_element_type=F32)
        act_ref[:, t * FFN_T:(t + 1) * FFN_T] = (_silu(a) * bq).astype(BF16)
    y = x + mod_ref[5:6, :] * jnp.dot(act_ref[...], wfo_ref[...], preferred_element_type=F32)
    if final:
        y = _rms(y, fin_ref[...])
    o_ref[...] = y


def _merge_ffn_call(xs, modsel, u, s5, yret, ygla, p, dsk, gluw, glub, wbs, wbr, wbg, wo,
                    gain, wfi, wfo, fin, final, lat_only, ctx_len):
    split = isinstance(xs, tuple)
    xs = xs if split else (xs,)
    b = xs[0].shape[0]
    lt = sum(t.shape[1] for t in xs)
    j0 = ctx_len // ROW_T if lat_only else 0
    nt = lt // ROW_T - j0
    tok = lambda w: pl.BlockSpec((None, ROW_T, w), lambda i, j: (i, j + j0, 0))
    return pl.pallas_call(
        functools.partial(_merge_ffn_kernel, final=final, split=split),
        grid=(b, nt),
        in_specs=_residual_specs(split, j0) + [
                  pl.BlockSpec((None, None, SUB, D), lambda i, j: (i, jnp.minimum(j + j0, 1), 0, 0)),
                  tok(S5_W), tok(S5_W), tok(S5_W), tok(V_P), tok(V_P), tok(GATE_W),
                  _const_spec((1, S5_W)), _const_spec((S5_W, S5_W)), _const_spec((1, S5_W)),
                  _const_spec((S5_W, D)), _const_spec((V_P, D)), _const_spec((V_P, D)),
                  _const_spec((D, D)),
                  _const_spec((1, D)),
                  _const_spec((D, 2 * FFN_H)),
                  _const_spec((FFN_H, D)), _const_spec((1, D))],
        out_specs=pl.BlockSpec((None, ROW_T, D), lambda i, j: (i, j, 0)),
        out_shape=jax.ShapeDtypeStruct((b, nt * ROW_T, D), F32),
        scratch_shapes=[pltpu.VMEM((ROW_T, FFN_H), BF16)],
        compiler_params=_cparams(2),
    )(*xs, modsel, u, *s5, yret, ygla, p, dsk, gluw, glub, wbs, wbr, wbg, wo,
      gain.reshape(1, D), wfi, wfo, fin.reshape(1, D))


def _s5_params(lam_re, lam_im, log_dt, b_re, b_im, c_re, c_im):
    dt = jnp.exp(log_dt)[..., None]
    mag = jnp.exp(lam_re * dt)
    a_re, a_im = mag * jnp.cos(lam_im * dt), mag * jnp.sin(lam_im * dt)
    den = lam_re * lam_re + lam_im * lam_im
    f_re = ((a_re - 1.0) * lam_re + a_im * lam_im) / den
    f_im = (a_im * lam_re - (a_re - 1.0) * lam_im) / den
    bb_re = f_re[..., None] * b_re - f_im[..., None] * b_im
    bb_im = f_re[..., None] * b_im + f_im[..., None] * b_re
    eye = jnp.eye(S5_GROUPS, dtype=F32)

    def blk_in(bb):
        t = jnp.einsum("dgpc,gh->dgchp", bb, eye)
        return t.reshape(2, S5_W, S5_NS)

    bblk = jnp.concatenate([blk_in(bb_re), blk_in(bb_im)], axis=-1).astype(BF16)

    def blk_out(cc):
        t = jnp.einsum("gcp,gh->gphc", cc, eye)
        return t.reshape(S5_NS, S5_W)

    cblk = jnp.concatenate([blk_out(c_re), -blk_out(c_im)], axis=0).astype(BF16)
    bc = lambda a: jnp.broadcast_to(a.reshape(2, 1, S5_NS), (2, SUB, S5_NS))
    return bblk, bc(a_re), bc(a_im), cblk


def _ret_tables(log_decay):
    tc = RET_C
    pos = jnp.arange(tc, dtype=F32)
    w = jnp.stack([pos, tc - 1.0 - pos])
    rel = w[:, :, None] - w[:, None, :]
    lg = log_decay[:, :, None, None]
    dmask = jnp.where(rel[:, None] >= 0, jnp.exp(jnp.maximum(rel[:, None], 0.0) * lg), 0.0)
    qk_head = TAB["qk_head"]
    lane_lg = jnp.where(jnp.asarray(qk_head >= 0),
                        jnp.take(log_decay, jnp.asarray(np.maximum(qk_head, 0)), axis=1), 0.0)
    qdec = jnp.exp((w[:, :, None] + 1.0) * lane_lg[:, None, :])
    kdec = jnp.exp((tc - 1.0 - w[:, :, None]) * lane_lg[:, None, :])
    cdec = jnp.exp(tc * lane_lg)[:, None, :]
    return dmask, qdec, kdec, cdec


def _rope_tables(seq, ctx_len):
    rows = seq // GRID_W
    nf = DK // 4
    inv = 1.0 / (ROPE_BASE ** (np.arange(nf, dtype=np.float32) / nf))
    r = np.repeat(np.arange(rows, dtype=np.float32), GRID_W)
    col = np.tile(np.arange(GRID_W, dtype=np.float32), rows)
    ang = np.concatenate([r[:, None] * inv, col[:, None] * inv], axis=-1)
    cos_t = np.ones((ctx_len + seq, LANE), np.float32)
    sin_t = np.zeros((ctx_len + seq, LANE), np.float32)
    for h in range(HEADS):
        cos_t[ctx_len:, h * 24:(h + 1) * 24] = np.cos(ang)
        sin_t[ctx_len:, h * 24:(h + 1) * 24] = np.sin(ang)
    return jnp.asarray(cos_t), jnp.asarray(sin_t)


def _gla_tables(gate_w, gate_b):
    gw = _gather_cols(gate_w, TAB["gla_src"])
    gw_full = jnp.zeros((2, LANE, QK_P), F32)
    gw_full = gw_full.at[0, 0:GLA_RANK].set(gw[0]).at[1, GLA_RANK:2 * GLA_RANK].set(gw[1])
    gb = _gather_cols(gate_b, TAB["gla_src"])[:, None, :]
    i = np.arange(GLA_R)
    same = (i[:, None] // GLA_C) == (i[None, :] // GLA_C)
    tri = np.stack([same & (i[:, None] >= i[None, :]), same & (i[:, None] <= i[None, :])]).astype(np.float32)
    return gw_full.astype(BF16), gb, jnp.asarray(tri, dtype=BF16)


def _pad_rows(w, src):
    return _gather_cols(w.T, src).T


def kernel(x, c, ctx, c_ctx, w_mod, b_mod, norm_mix, norm_ffn, w_in, s5_lam_re, s5_lam_im, s5_log_dt, s5_b_re, s5_b_im, s5_c_re, s5_c_im, s5_d, s5_glu_w, s5_glu_b, ret_log_decay, ret_gn, gla_gate_w, gla_gate_b, gla_norm, w_br_s5, w_br_ret, w_br_gla, w_out, w_ffn_in, w_ffn_out, norm_final):
    b, seq, _ = x.shape
    ctx_len = ctx.shape[1]
    depth = w_mod.shape[0]
    assert b % S5_NB == 0 and b % MIX_NB == 0 and seq % RET_C == 0 and seq % GRID_W == 0
    assert ctx_len == ROW_T == RET_C

    xs = (ctx, x)
    rows = ((b + 1 + SUB - 1) // SUB) * SUB
    c_all = jnp.zeros((rows, D), F32).at[:b].set(c).at[b].set(c_ctx)
    cos_t, sin_t = _rope_tables(seq, ctx_len)
    nh = FFN_H // FFN_T

    for l in range(depth):
        last = l == depth - 1
        mods = _mod_call(c_all, w_mod[l], b_mod[l])
        m_lat = mods[:b].reshape(b, N_MOD, D)
        m_ctx = jnp.broadcast_to(mods[b].reshape(1, N_MOD, D), (b, N_MOD, D))
        modsel = jnp.stack([m_ctx, m_lat], axis=1)
        modsel = jnp.pad(modsel, ((0, 0), (0, 0), (0, SUB - N_MOD), (0, 0)))

        w_all = _gather_cols(w_in[l], TAB["src"]).astype(BF16)
        p, u = _inproj_call(xs, modsel, norm_mix[l], cos_t, sin_t, w_all)

        bblk, a_re, a_im, cblk = _s5_params(s5_lam_re[l], s5_lam_im[l], s5_log_dt[l],
                                            s5_b_re[l], s5_b_im[l], s5_c_re[l], s5_c_im[l])
        s5 = _s5_call(u, bblk, a_re, a_im, cblk, ctx_len)

        dmask, qdec, kdec, cdec = _ret_tables(ret_log_decay[l])
        yret = _ret_call(p, dmask, qdec, kdec, cdec,
                         _gather_cols(ret_gn[l], TAB["vsrc"]).reshape(1, V_P), ctx_len)

        gw, gb, tri = _gla_tables(gla_gate_w[l], gla_gate_b[l])
        ygla = _gla_call(p, gw, gb, tri, _gather_cols(gla_norm[l], TAB["vsrc"]).reshape(1, V_P),
                         ctx_len)

        xs = _merge_ffn_call(xs, modsel, u, s5, yret, ygla, p,
                             s5_d[l].reshape(1, S5_W), s5_glu_w[l].astype(BF16),
                             s5_glu_b[l].reshape(1, S5_W), w_br_s5[l].astype(BF16),
                             _pad_rows(w_br_ret[l], TAB["vsrc"]).astype(BF16),
                             _pad_rows(w_br_gla[l], TAB["vsrc"]).astype(BF16),
                             w_out[l].astype(BF16), norm_ffn[l], w_ffn_in[l].astype(BF16),
                             w_ffn_out[l].astype(BF16), norm_final, last, last, ctx_len)
    return xs
```

```python
import functools

import numpy as np
import jax
import jax.numpy as jnp
from jax import lax
from jax.experimental import pallas as pl
from jax.experimental.pallas import tpu as pltpu

F32 = jnp.float32
BF16 = jnp.bfloat16

D = 1024
EPS = 1e-6
N_MOD = 6
GRID_W = 64
S5_W = 256
S5_GROUPS = 16
S5_STATE = 64
S5_NS = S5_GROUPS * S5_STATE
HEADS = 4
DK = 48
DV = 96
QK_W = HEADS * DK
V_W = HEADS * DV
GLA_RANK = 16
GLA_TAU = 16.0
ROPE_BASE = 10000.0
FFN_H = 2816

LANE = 128
SUB = 8
QK_P = 2 * LANE
V_P = HEADS * LANE
QKV_P = 2 * QK_P + V_P
GATE_W = 3 * D

OFF_GATE = 0
OFF_RQKV = GATE_W
OFF_GQKV = OFF_RQKV + QKV_P
OFF_RG = OFF_GQKV + QKV_P
OFF_GG = OFF_RG + V_P
NP = OFF_GG + V_P
OFF_S5 = NP
NW = NP + S5_W

ROW_T = 256
RET_C = 256
GLA_C = 64
GLA_R = 256
MIX_NB = 4
S5_C = 64
V7X_VMEM_BYTES = 64 * 1024 * 1024
VMEM_LIMIT = V7X_VMEM_BYTES * 7 // 8

IN_SIZES = (S5_W, QK_W, QK_W, V_W, V_W, QK_W, QK_W, V_W, V_W, GLA_RANK, GLA_RANK, D, D, D)
IN_OFFS = np.concatenate([[0], np.cumsum(IN_SIZES)]).astype(np.int64)


def _head_lane():
    m = np.zeros((HEADS, DK), np.int64)
    for h in range(HEADS):
        for i in range(DK):
            m[h, i] = (i % 2) * LANE + h * (DK // 2) + i // 2
    return m


HEAD_LANE = _head_lane()
HALF = LANE // 2
GLA_LANE = np.array([[(h // 2) * LANE + (h % 2) * HALF + i for i in range(DK)] for h in range(HEADS)])
GLA_Z_LANE = (DK, HALF + DK)
assert HALF - DK == GLA_RANK


def _static_tables():
    src = np.full((NW,), -1, np.int64)
    src[OFF_GATE:OFF_GATE + GATE_W] = IN_OFFS[11] + np.arange(GATE_W)
    for qkv, gg, lane, iq, ik, iv, ig in ((OFF_RQKV, OFF_RG, HEAD_LANE, 1, 2, 3, 4),
                                          (OFF_GQKV, OFF_GG, GLA_LANE, 5, 6, 7, 8)):
        for h in range(HEADS):
            for i in range(DK):
                src[qkv + lane[h, i]] = IN_OFFS[iq] + h * DK + i
                src[qkv + QK_P + lane[h, i]] = IN_OFFS[ik] + h * DK + i
            for j in range(DV):
                src[qkv + 2 * QK_P + h * LANE + j] = IN_OFFS[iv] + h * DV + j
                src[gg + h * LANE + j] = IN_OFFS[ig] + h * DV + j
    for d in range(2):
        src[OFF_GQKV + GLA_Z_LANE[d]:OFF_GQKV + GLA_Z_LANE[d] + GLA_RANK] = (
            IN_OFFS[9 + d] + np.arange(GLA_RANK))
    src[OFF_S5:OFF_S5 + S5_W] = np.arange(S5_W)
    vsrc = np.full((V_P,), -1, np.int64)
    for h in range(HEADS):
        vsrc[h * LANE:h * LANE + DV] = h * DV + np.arange(DV)
    qk_head = np.full((QK_P,), -1, np.int64)
    qk_src = np.full((QK_P,), -1, np.int64)
    for h in range(HEADS):
        for i in range(DK):
            qk_head[HEAD_LANE[h, i]] = h
            qk_src[HEAD_LANE[h, i]] = h * DK + i
    v_head = np.repeat(np.arange(HEADS), LANE)
    v_real = (np.arange(V_P) % LANE) < DV
    head_mask = np.zeros((SUB, QK_P), np.float32)
    for h in range(HEADS):
        head_mask[h] = (qk_head == h)
    bm_t = (v_head[:, None] == qk_head[None, :]).astype(np.float32)
    gla_src = np.full((QK_P,), -1, np.int64)
    for h in range(HEADS):
        gla_src[GLA_LANE[h]] = h * DK + np.arange(DK)
    lane_in_pair = np.arange(LANE)
    gla_hm = np.stack([(lane_in_pair // HALF == r) & (lane_in_pair % HALF < DK) for r in range(2)])
    gla_bm = (np.arange(2 * LANE)[:, None] // LANE) == (lane_in_pair[None, :] // HALF)
    return dict(src=src, vsrc=vsrc, qk_head=qk_head, qk_src=qk_src, head_mask=head_mask,
                bm_t=bm_t, v_real=v_real.astype(np.float32), gla_src=gla_src,
                gla_hm=gla_hm.astype(np.float32), gla_bm=gla_bm.astype(np.float32))


TAB = _static_tables()


def _gather_cols(w, src):
    valid = jnp.asarray(src >= 0)
    out = jnp.take(w, jnp.asarray(np.maximum(src, 0)), axis=-1)
    return jnp.where(valid, out, jnp.zeros((), w.dtype))


def _cparams(n_axes):
    return pltpu.CompilerParams(dimension_semantics=("arbitrary",) * n_axes,
                                vmem_limit_bytes=VMEM_LIMIT)


def _const_spec(shape):
    nd = len(shape)
    return pl.BlockSpec(shape, lambda *_: (0,) * nd, pipeline_mode=pl.Buffered(1))


def _dir_chunk(d, c, n_ctx, n_all):
    bwd = jnp.where(c < n_ctx, n_ctx - 1 - c, n_all + n_ctx - 1 - c)
    return jnp.where(d == 0, c, bwd)


def _sigmoid(x):
    return 0.5 * jnp.tanh(0.5 * x) + 0.5


def _silu(x):
    return x * _sigmoid(x)


def _mod_kernel(c_ref, w_ref, b_ref, o_ref):
    o_ref[...] = jnp.dot(_silu(c_ref[...]), w_ref[...], preferred_element_type=F32,
                         precision=lax.Precision.HIGHEST) + b_ref[...]


def _mod_call(c_all, w_mod_l, b_mod_l):
    rows = c_all.shape[0]
    tn = D
    return pl.pallas_call(
        _mod_kernel,
        grid=(N_MOD * D // tn,),
        in_specs=[pl.BlockSpec((rows, D), lambda j: (0, 0)),
                  pl.BlockSpec((D, tn), lambda j: (0, j)),
                  pl.BlockSpec((1, tn), lambda j: (0, j))],
        out_specs=pl.BlockSpec((rows, tn), lambda j: (0, j)),
        out_shape=jax.ShapeDtypeStruct((rows, N_MOD * D), F32),
        compiler_params=_cparams(1),
    )(c_all, w_mod_l, b_mod_l.reshape(1, -1))


def _rms(x, gain):
    return x * lax.rsqrt(jnp.mean(x * x, axis=-1, keepdims=True) + EPS) * gain


def _residual_rows(refs, split):
    if not split:
        return refs[0][...]
    return jnp.where(pl.program_id(1) == 0, refs[0][...], refs[1][...])


def _residual_specs(split, j0):
    if not split:
        return [pl.BlockSpec((None, ROW_T, D), lambda i, j: (i, j + j0, 0))]
    return [pl.BlockSpec((None, ROW_T, D), lambda i, j: (i, 0, 0)),
            pl.BlockSpec((None, ROW_T, D), lambda i, j: (i, jnp.maximum(j - 1, 0), 0))]


def _inproj_kernel(*refs, split):
    mod_ref, gain_ref, cos_ref, sin_ref, w_ref, p_ref, u_ref = refs[1 + split:]
    h = _rms(_residual_rows(refs, split), gain_ref[...])
    h = h * (1.0 + mod_ref[1:2, :]) + mod_ref[0:1, :]
    hb = h.astype(BF16)

    def proj(c0, c1):
        return jnp.dot(hb, w_ref[:, c0:c1], preferred_element_type=F32)

    for c0 in range(0, GATE_W, D):
        p_ref[:, c0:c0 + D] = proj(c0, c0 + D).astype(BF16)
    cs = cos_ref[...]
    sn = sin_ref[...]
    qk = proj(OFF_RQKV, OFF_RQKV + 2 * QK_P)
    for o, scale in ((0, 1.0), (QK_P, DK ** -0.5)):
        a = qk[:, o:o + LANE] * scale
        b = qk[:, o + LANE:o + 2 * LANE] * scale
        p_ref[:, OFF_RQKV + o:OFF_RQKV + o + LANE] = (a * cs - b * sn).astype(BF16)
        p_ref[:, OFF_RQKV + o + LANE:OFF_RQKV + o + 2 * LANE] = (a * sn + b * cs).astype(BF16)
    c0 = OFF_RQKV + 2 * QK_P
    p_ref[:, c0:OFF_GQKV] = proj(c0, OFF_GQKV).astype(BF16)
    p_ref[:, OFF_GQKV:OFF_GQKV + QK_P] = (proj(OFF_GQKV, OFF_GQKV + QK_P) * (DK ** -0.5)).astype(BF16)
    c0 = OFF_GQKV + QK_P
    p_ref[:, c0:OFF_RG] = proj(c0, OFF_RG).astype(BF16)
    p_ref[:, OFF_RG:NP] = proj(OFF_RG, NP).astype(BF16)
    u_ref[...] = proj(OFF_S5, OFF_S5 + S5_W)


def _inproj_call(xs, modsel, gain, cos_t, sin_t, w_all):
    split = isinstance(xs, tuple)
    xs = xs if split else (xs,)
    b = xs[0].shape[0]
    lt = sum(t.shape[1] for t in xs)
    nt = lt // ROW_T
    return pl.pallas_call(
        functools.partial(_inproj_kernel, split=split),
        grid=(b, nt),
        in_specs=_residual_specs(split, 0) + [
                  pl.BlockSpec((None, None, SUB, D), lambda i, j: (i, jnp.minimum(j, 1), 0, 0)),
                  _const_spec((1, D)),
                  pl.BlockSpec((ROW_T, LANE), lambda i, j: (j, 0)),
                  pl.BlockSpec((ROW_T, LANE), lambda i, j: (j, 0)),
                  _const_spec((D, NW))],
        out_specs=[pl.BlockSpec((None, ROW_T, NP), lambda i, j: (i, j, 0)),
                   pl.BlockSpec((None, ROW_T, S5_W), lambda i, j: (i, j, 0))],
        out_shape=[jax.ShapeDtypeStruct((b, lt, NP), BF16),
                   jax.ShapeDtypeStruct((b, lt, S5_W), F32)],
        compiler_params=_cparams(2),
    )(*xs, modsel, gain.reshape(1, D), cos_t, sin_t, w_all)


S5_NB = 2 * SUB
S5_RB = 256


def _s5_kernel(uf_ref, ub_ref, bb_ref, are_ref, aim_ref, cb_ref, yf_ref, yb_ref,
               x_ref, hb_ref, h_ref, *, tc):
    rows = tc * S5_NB
    steps = S5_RB // S5_NB
    n_blk = rows // S5_RB
    orders = (list(range(n_blk)), list(range(n_blk - 1, -1, -1)))
    u_refs, y_refs = (uf_ref, ub_ref), (yf_ref, yb_ref)

    @pl.when(pl.program_id(1) == 0)
    def _():
        h_ref[...] = jnp.zeros_like(h_ref)

    for d in (0, 1):
        ub = pltpu.einshape("bts->(tb)s", u_refs[d][...]).astype(BF16)
        for k in orders[d]:
            rs = slice(k * S5_RB, (k + 1) * S5_RB)
            x_ref[d, rs, :] = jnp.dot(ub[rs], bb_ref[d], preferred_element_type=F32)

    half = S5_NS // 2
    lanes = [(slice(lo, lo + half), slice(S5_NS + lo, S5_NS + lo + half)) for lo in (0, half)]
    coef = [[(are_ref[d, :, re], aim_ref[d, :, re]) for re, _ in lanes] for d in (0, 1)]
    carry = [[[(h_ref[d, g * SUB:(g + 1) * SUB, re], h_ref[d, g * SUB:(g + 1) * SUB, im])
               for g in range(S5_NB // SUB)] for re, im in lanes] for d in (0, 1)]
    for kk in range(n_blk):
        for d in (0, 1):
            k = orders[d][kk]
            ts = range(k * steps, (k + 1) * steps)
            for t in (reversed(ts) if d else ts):
                r0 = t * S5_NB
                for li, (re, im) in enumerate(lanes):
                    ar, ai = coef[d][li]
                    new = []
                    for g, (hr, hi) in enumerate(carry[d][li]):
                        rw = slice(r0 + g * SUB, r0 + (g + 1) * SUB)
                        new.append((ar * hr - ai * hi + x_ref[d, rw, re],
                                    ar * hi + ai * hr + x_ref[d, rw, im]))
                    carry[d][li] = new
                    hb_ref[d, r0:r0 + S5_NB, re] = jnp.concatenate(
                        [n[0] for n in new], axis=0).astype(BF16)
                    hb_ref[d, r0:r0 + S5_NB, im] = jnp.concatenate(
                        [n[1] for n in new], axis=0).astype(BF16)
            rs = slice(k * S5_RB, (k + 1) * S5_RB)
            y = jnp.dot(hb_ref[d, rs, :], cb_ref[...], preferred_element_type=F32)
            y_refs[d][:, k * steps:(k + 1) * steps, :] = pltpu.einshape("(tb)s->bts", y, b=S5_NB)
    for d in (0, 1):
        for li, (re, im) in enumerate(lanes):
            for g, (hr, hi) in enumerate(carry[d][li]):
                h_ref[d, g * SUB:(g + 1) * SUB, re] = hr
                h_ref[d, g * SUB:(g + 1) * SUB, im] = hi


def _s5_call(u, bblk, a_re, a_im, cblk, ctx_len):
    b, lt, _ = u.shape
    tc = S5_C
    n_all, n_ctx = lt // tc, ctx_len // tc
    tok = lambda d: pl.BlockSpec((S5_NB, tc, S5_W), lambda g, c: (g, _dir_chunk(d, c, n_ctx, n_all), 0))
    out = jax.ShapeDtypeStruct((b, lt, S5_W), F32)
    return pl.pallas_call(
        functools.partial(_s5_kernel, tc=tc),
        grid=(b // S5_NB, n_all),
        in_specs=[tok(0), tok(1), _const_spec((2, S5_W, 2 * S5_NS)), _const_spec((2, SUB, S5_NS)),
                  _const_spec((2, SUB, S5_NS)), _const_spec((2 * S5_NS, S5_W))],
        out_specs=[tok(0), tok(1)],
        out_shape=[out, out],
        scratch_shapes=[pltpu.VMEM((2, tc * S5_NB, 2 * S5_NS), F32),
                        pltpu.VMEM((2, tc * S5_NB, 2 * S5_NS), BF16),
                        pltpu.VMEM((2, S5_NB, 2 * S5_NS), F32)],
        compiler_params=_cparams(2),
    )(u, u, bblk, a_re, a_im, cblk)


_NT = (((1,), (1,)), ((), ()))
_TN = (((0,), (0,)), ((), ()))


def _ret_kernel(*refs, backward):
    if backward:
        (p_ref, dm_ref, qd_ref, kd_ref, cd_ref, hm_ref, bm_ref, of_ref, g_ref, real_ref, gn_ref,
         y_ref, s_ref) = refs
    else:
        p_ref, dm_ref, qd_ref, kd_ref, cd_ref, hm_ref, bm_ref, o_ref, s_ref = refs

    @pl.when(pl.program_id(1) == 0)
    def _():
        s_ref[...] = jnp.zeros_like(s_ref)

    def chunk(bi):
        qb = p_ref[bi, :, 0:QK_P]
        kb = p_ref[bi, :, QK_P:2 * QK_P]
        vb = p_ref[bi, :, 2 * QK_P:2 * QK_P + V_P]
        st = s_ref[bi]
        inter = lax.dot_general(qb * qd_ref[...], st.astype(BF16) * bm_ref[...], _NT,
                                preferred_element_type=F32)
        for h in range(HEADS):
            att = lax.dot_general(qb * hm_ref[h], kb, _NT, preferred_element_type=F32)
            att = att.astype(BF16) * dm_ref[h]
            sl = slice(h * LANE, (h + 1) * LANE)
            o = jnp.dot(att, vb[:, sl], preferred_element_type=F32) + inter[:, sl]
            if backward:
                o = o + of_ref[bi, :, sl]
                mu = jnp.sum(o, axis=-1, keepdims=True) * (1.0 / DV)
                dlt = (o - mu) * real_ref[:, sl]
                var = jnp.sum(dlt * dlt, axis=-1, keepdims=True) * (1.0 / DV)
                y = dlt * lax.rsqrt(var + EPS) * gn_ref[:, sl] * _silu(g_ref[bi, :, sl].astype(F32))
                y_ref[bi, :, sl] = y.astype(BF16)
            else:
                o_ref[bi, :, sl] = o
        kdv = lax.dot_general(vb, kb * kd_ref[...], _TN, preferred_element_type=F32)
        s_ref[bi] = st * cd_ref[...] + kdv

    for bi in range(MIX_NB):
        chunk(bi)


def _mixer_specs(b, lt, ctx_len, tc, off_qkv, off_g):
    n_all, n_ctx = lt // tc, ctx_len // tc
    tok = lambda d, w, blk: pl.BlockSpec(
        (MIX_NB, tc, w), lambda i, c: (i, _dir_chunk(d, c, n_ctx, n_all), blk))
    qkv = lambda d: tok(d, QKV_P, off_qkv // QKV_P)
    gsp = lambda d: tok(d, V_P, off_g // V_P)
    out = lambda d: tok(d, V_P, 0)
    state = pltpu.VMEM((MIX_NB, V_P, QK_P), F32)
    return (b // MIX_NB, n_all), tok, qkv, gsp, out, state


def _head_mask_rows(rows):
    return jnp.asarray(np.broadcast_to(TAB["head_mask"][:HEADS, None, :], (HEADS, rows, QK_P)), dtype=BF16)


def _ret_call(p, dmask, qdec, kdec, cdec, gn, ctx_len):
    b, lt, _ = p.shape
    tc = RET_C
    grid, _, qkv, gsp, out, state = _mixer_specs(b, lt, ctx_len, tc, OFF_RQKV, OFF_RG)
    ins = lambda d: [qkv(d), _const_spec((HEADS, tc, tc)), _const_spec((tc, QK_P)),
                     _const_spec((tc, QK_P)), _const_spec((1, QK_P)),
                     _const_spec((HEADS, tc, QK_P)), _const_spec((V_P, QK_P))]
    tabs = lambda d: (dmask[d].astype(BF16), qdec[d].astype(BF16), kdec[d].astype(BF16), cdec[d],
                      _head_mask_rows(tc), jnp.asarray(TAB["bm_t"], dtype=BF16))
    of = pl.pallas_call(
        functools.partial(_ret_kernel, backward=False),
        grid=grid, in_specs=ins(0), out_specs=out(0),
        out_shape=jax.ShapeDtypeStruct((b, lt, V_P), F32),
        scratch_shapes=[state], compiler_params=_cparams(2),
    )(p, *tabs(0))
    return pl.pallas_call(
        functools.partial(_ret_kernel, backward=True),
        grid=grid,
        in_specs=ins(1) + [out(1), gsp(1), _const_spec((1, V_P)), _const_spec((1, V_P))],
        out_specs=out(1),
        out_shape=jax.ShapeDtypeStruct((b, lt, V_P), BF16),
        scratch_shapes=[state], compiler_params=_cparams(2),
    )(p, *tabs(1), of, p, jnp.asarray(TAB["v_real"]).reshape(1, V_P), gn)


def _gla_kernel(*refs, backward):
    if backward:
        (p_ref, gw_ref, gb_ref, tri_ref, hm_ref, bm_ref, of_ref, g_ref, gn_ref,
         y_ref, s_ref, oc_ref) = refs
    else:
        p_ref, gw_ref, gb_ref, tri_ref, hm_ref, bm_ref, oc_ref, s_ref = refs

    @pl.when(pl.program_id(1) == 0)
    def _():
        s_ref[...] = jnp.zeros_like(s_ref)

    tri = tri_ref[...]
    mask = tri > 0

    def intra(bi):
        la = jax.nn.log_sigmoid(
            jnp.dot(p_ref[bi, :, 0:LANE], gw_ref[...], preferred_element_type=F32)
            + gb_ref[...]) / GLA_TAU
        hi = la.astype(BF16)
        lo = (la - hi.astype(F32)).astype(BF16)
        bcum = (jnp.dot(tri, hi, preferred_element_type=F32)
                + jnp.dot(tri, lo, preferred_element_type=F32))
        q = p_ref[bi, :, 0:QK_P].astype(F32)
        k = p_ref[bi, :, QK_P:2 * QK_P].astype(F32)
        vb = p_ref[bi, :, 2 * QK_P:2 * QK_P + V_P]
        qtb = (q * jnp.exp(bcum)).astype(BF16)
        ktb = (k * jnp.exp(-bcum)).astype(BF16)
        for h in range(HEADS):
            pr = slice(h // 2 * LANE, (h // 2 + 1) * LANE)
            att = lax.dot_general(qtb[:, pr] * hm_ref[h % 2], ktb[:, pr], _NT,
                                  preferred_element_type=F32)
            att = jnp.where(mask, att.astype(BF16), jnp.zeros((), BF16))
            sl = slice(h * LANE, (h + 1) * LANE)
            oc_ref[bi, :, sl] = jnp.dot(att, vb[:, sl], preferred_element_type=F32)
        return bcum, k, qtb, vb

    parts = [intra(bi) for bi in range(MIX_NB)]

    n_sub = GLA_R // GLA_C
    n_pair = HEADS // 2
    order = range(n_sub - 1, -1, -1) if backward else range(n_sub)
    last_row = 0 if backward else GLA_C - 1
    ss = [[s_ref[bi, pi] for pi in range(n_pair)] for bi in range(MIX_NB)]
    for cc in order:
        r0 = cc * GLA_C
        rows = slice(r0, r0 + GLA_C)
        for bi, (bcum, k, qtb, vb) in enumerate(parts):
            bl = bcum[r0 + last_row:r0 + last_row + 1, :]
            kd = (k[rows] * jnp.exp(bl - bcum[rows])).astype(BF16)
            ebl = jnp.exp(bl)
            for pi in range(n_pair):
                pr = slice(pi * LANE, (pi + 1) * LANE)
                vs = slice(2 * pi * LANE, 2 * (pi + 1) * LANE)
                s = ss[bi][pi]
                oc_ref[bi, rows, vs] += lax.dot_general(qtb[rows, pr], s.astype(BF16) * bm_ref[...],
                                                        _NT, preferred_element_type=F32)
                kdv = lax.dot_general(vb[rows, vs], kd[:, pr], _TN, preferred_element_type=F32)
                ss[bi][pi] = s * ebl[:, pr] + kdv
    for bi in range(MIX_NB):
        for pi in range(n_pair):
            s_ref[bi, pi] = ss[bi][pi]

    if backward:
        for bi in range(MIX_NB):
            for h in range(HEADS):
                sl = slice(h * LANE, (h + 1) * LANE)
                o = oc_ref[bi, :, sl] + of_ref[bi, :, sl]
                ms = jnp.sum(o * o, axis=-1, keepdims=True) * (1.0 / DV)
                y = o * lax.rsqrt(ms + EPS) * gn_ref[:, sl] * _silu(g_ref[bi, :, sl].astype(F32))
                y_ref[bi, :, sl] = y.astype(BF16)


def _gla_call(p, gw, gb, tri, gn, ctx_len):
    b, lt, _ = p.shape
    tc = GLA_R
    grid, _, qkv, gsp, out, _ = _mixer_specs(b, lt, ctx_len, tc, OFF_GQKV, OFF_GG)
    ins = lambda d: [qkv(d),
                     _const_spec((LANE, QK_P)), _const_spec((1, QK_P)), _const_spec((tc, tc)),
                     _const_spec((2, tc, LANE)), _const_spec((2 * LANE, LANE))]
    hm = jnp.asarray(np.broadcast_to(TAB["gla_hm"][:, None, :], (2, tc, LANE)), dtype=BF16)
    bm = jnp.asarray(TAB["gla_bm"], dtype=BF16)
    state = pltpu.VMEM((MIX_NB, HEADS // 2, 2 * LANE, LANE), F32)
    of = pl.pallas_call(
        functools.partial(_gla_kernel, backward=False),
        grid=grid, in_specs=ins(0), out_specs=out(0),
        out_shape=jax.ShapeDtypeStruct((b, lt, V_P), F32),
        scratch_shapes=[state], compiler_params=_cparams(2),
    )(p, gw[0], gb[0], tri[0], hm, bm)
    return pl.pallas_call(
        functools.partial(_gla_kernel, backward=True),
        grid=grid,
        in_specs=ins(1) + [out(1), gsp(1), _const_spec((1, V_P))],
        out_specs=out(1),
        out_shape=jax.ShapeDtypeStruct((b, lt, V_P), BF16),
        scratch_shapes=[state, pltpu.VMEM((MIX_NB, tc, V_P), F32)],
        compiler_params=_cparams(2),
    )(p, gw[1], gb[1], tri[1], hm, bm, of, p, gn)


FFN_T = 256


def _merge_ffn_kernel(*refs, final, split):
    (mod_ref, u_ref, sf_ref, sb_ref, yr_ref, yg_ref, gate_ref,
     dsk_ref, gluw_ref, glub_ref, wbs_ref, wbr_ref, wbg_ref, wo_ref,
     gain_ref, wfi_ref, wfo_ref, fin_ref, o_ref, act_ref) = refs[1 + split:]

    def branch(y, w_ref, g0):
        gate = _sigmoid(gate_ref[:, g0:g0 + D].astype(F32))
        return gate * jnp.dot(y, w_ref[...], preferred_element_type=F32)

    ys = jax.nn.gelu(sf_ref[...] + sb_ref[...] + dsk_ref[...] * u_ref[...])
    glu = jnp.dot(ys.astype(BF16), gluw_ref[...], preferred_element_type=F32) + glub_ref[...]
    ys = (ys * _sigmoid(glu)).astype(BF16)
    m = (branch(ys, wbs_ref, 0) + branch(yr_ref[...], wbr_ref, D)
         + branch(yg_ref[...], wbg_ref, 2 * D))
    x = _residual_rows(refs, split) + mod_ref[2:3, :] * jnp.dot(m.astype(BF16), wo_ref[...],
                                                                  preferred_element_type=F32)
    h = _rms(x, gain_ref[...])
    hb = (h * (1.0 + mod_ref[4:5, :]) + mod_ref[3:4, :]).astype(BF16)
    for t in range(FFN_H // FFN_T):
        c0 = t * FFN_T
        a = jnp.dot(hb, wfi_ref[:, c0:c0 + FFN_T], preferred_element_type=F32)
        bq = jnp.dot(hb, wfi_ref[:, FFN_H + c0:FFN_H + c0 + FFN_T], preferred_element_type=F32)
        act_ref[:, t * FFN_T:(t + 1) * FFN_T] = (_silu(a) * bq).astype(BF16)
    y = x + mod_ref[5:6, :] * jnp.dot(act_ref[...], wfo_ref[...], preferred_element_type=F32)
    if final:
        y = _rms(y, fin_ref[...])
    o_ref[...] = y


def _merge_ffn_call(xs, modsel, u, s5, yret, ygla, p, dsk, gluw, glub, wbs, wbr, wbg, wo,
                    gain, wfi, wfo, fin, final, lat_only, ctx_len):
    split = isinstance(xs, tuple)
    xs = xs if split else (xs,)
    b = xs[0].shape[0]
    lt = sum(t.shape[1] for t in xs)
    j0 = ctx_len // ROW_T if lat_only else 0
    nt = lt // ROW_T - j0
    tok = lambda w: pl.BlockSpec((None, ROW_T, w), lambda i, j: (i, j + j0, 0))
    return pl.pallas_call(
        functools.partial(_merge_ffn_kernel, final=final, split=split),
        grid=(b, nt),
        in_specs=_residual_specs(split, j0) + [
                  pl.BlockSpec((None, None, SUB, D), lambda i, j: (i, jnp.minimum(j + j0, 1), 0, 0)),
                  tok(S5_W), tok(S5_W), tok(S5_W), tok(V_P), tok(V_P), tok(GATE_W),
                  _const_spec((1, S5_W)), _const_spec((S5_W, S5_W)), _const_spec((1, S5_W)),
                  _const_spec((S5_W, D)), _const_spec((V_P, D)), _const_spec((V_P, D)),
                  _const_spec((D, D)),
                  _const_spec((1, D)),
                  _const_spec((D, 2 * FFN_H)),
                  _const_spec((FFN_H, D)), _const_spec((1, D))],
        out_specs=pl.BlockSpec((None, ROW_T, D), lambda i, j: (i, j, 0)),
        out_shape=jax.ShapeDtypeStruct((b, nt * ROW_T, D), F32),
        scratch_shapes=[pltpu.VMEM((ROW_T, FFN_H), BF16)],
        compiler_params=_cparams(2),
    )(*xs, modsel, u, *s5, yret, ygla, p, dsk, gluw, glub, wbs, wbr, wbg, wo,
      gain.reshape(1, D), wfi, wfo, fin.reshape(1, D))


def _s5_params(lam_re, lam_im, log_dt, b_re, b_im, c_re, c_im):
    dt = jnp.exp(log_dt)[..., None]
    mag = jnp.exp(lam_re * dt)
    a_re, a_im = mag * jnp.cos(lam_im * dt), mag * jnp.sin(lam_im * dt)
    den = lam_re * lam_re + lam_im * lam_im
    f_re = ((a_re - 1.0) * lam_re + a_im * lam_im) / den
    f_im = (a_im * lam_re - (a_re - 1.0) * lam_im) / den
    bb_re = f_re[..., None] * b_re - f_im[..., None] * b_im
    bb_im = f_re[..., None] * b_im + f_im[..., None] * b_re
    eye = jnp.eye(S5_GROUPS, dtype=F32)

    def blk_in(bb):
        t = jnp.einsum("dgpc,gh->dgchp", bb, eye)
        return t.reshape(2, S5_W, S5_NS)

    bblk = jnp.concatenate([blk_in(bb_re), blk_in(bb_im)], axis=-1).astype(BF16)

    def blk_out(cc):
        t = jnp.einsum("gcp,gh->gphc", cc, eye)
        return t.reshape(S5_NS, S5_W)

    cblk = jnp.concatenate([blk_out(c_re), -blk_out(c_im)], axis=0).astype(BF16)
    bc = lambda a: jnp.broadcast_to(a.reshape(2, 1, S5_NS), (2, SUB, S5_NS))
    return bblk, bc(a_re), bc(a_im), cblk


def _ret_tables(log_decay):
    tc = RET_C
    pos = jnp.arange(tc, dtype=F32)
    w = jnp.stack([pos, tc - 1.0 - pos])
    rel = w[:, :, None] - w[:, None, :]
    lg = log_decay[:, :, None, None]
    dmask = jnp.where(rel[:, None] >= 0, jnp.exp(jnp.maximum(rel[:, None], 0.0) * lg), 0.0)
    qk_head = TAB["qk_head"]
    lane_lg = jnp.where(jnp.asarray(qk_head >= 0),
                        jnp.take(log_decay, jnp.asarray(np.maximum(qk_head, 0)), axis=1), 0.0)
    qdec = jnp.exp((w[:, :, None] + 1.0) * lane_lg[:, None, :])
    kdec = jnp.exp((tc - 1.0 - w[:, :, None]) * lane_lg[:, None, :])
    cdec = jnp.exp(tc * lane_lg)[:, None, :]
    return dmask, qdec, kdec, cdec


def _rope_tables(seq, ctx_len):
    rows = seq // GRID_W
    nf = DK // 4
    inv = 1.0 / (ROPE_BASE ** (np.arange(nf, dtype=np.float32) / nf))
    r = np.repeat(np.arange(rows, dtype=np.float32), GRID_W)
    col = np.tile(np.arange(GRID_W, dtype=np.float32), rows)
    ang = np.concatenate([r[:, None] * inv, col[:, None] * inv], axis=-1)
    cos_t = np.ones((ctx_len + seq, LANE), np.float32)
    sin_t = np.zeros((ctx_len + seq, LANE), np.float32)
    for h in range(HEADS):
        cos_t[ctx_len:, h * 24:(h + 1) * 24] = np.cos(ang)
        sin_t[ctx_len:, h * 24:(h + 1) * 24] = np.sin(ang)
    return jnp.asarray(cos_t), jnp.asarray(sin_t)


def _gla_tables(gate_w, gate_b):
    gw = _gather_cols(gate_w, TAB["gla_src"])
    gw_full = jnp.zeros((2, LANE, QK_P), F32)
    for d in range(2):
        gw_full = gw_full.at[d, GLA_Z_LANE[d]:GLA_Z_LANE[d] + GLA_RANK].set(gw[d] * DK ** 0.5)
    gb = _gather_cols(gate_b, TAB["gla_src"])[:, None, :]
    i = np.arange(GLA_R)
    same = (i[:, None] // GLA_C) == (i[None, :] // GLA_C)
    tri = np.stack([same & (i[:, None] >= i[None, :]), same & (i[:, None] <= i[None, :])]).astype(np.float32)
    return gw_full.astype(BF16), gb, jnp.asarray(tri, dtype=BF16)


def _pad_rows(w, src):
    return _gather_cols(w.T, src).T


def kernel(x, c, ctx, c_ctx, w_mod, b_mod, norm_mix, norm_ffn, w_in, s5_lam_re, s5_lam_im, s5_log_dt, s5_b_re, s5_b_im, s5_c_re, s5_c_im, s5_d, s5_glu_w, s5_glu_b, ret_log_decay, ret_gn, gla_gate_w, gla_gate_b, gla_norm, w_br_s5, w_br_ret, w_br_gla, w_out, w_ffn_in, w_ffn_out, norm_final):
    b, seq, _ = x.shape
    ctx_len = ctx.shape[1]
    depth = w_mod.shape[0]
    assert b % S5_NB == 0 and b % MIX_NB == 0 and seq % RET_C == 0 and seq % GRID_W == 0
    assert ctx_len == ROW_T == RET_C

    xs = (ctx, x)
    rows = ((b + 1 + SUB - 1) // SUB) * SUB
    c_all = jnp.zeros((rows, D), F32).at[:b].set(c).at[b].set(c_ctx)
    cos_t, sin_t = _rope_tables(seq, ctx_len)
    nh = FFN_H // FFN_T

    for l in range(depth):
        last = l == depth - 1
        mods = _mod_call(c_all, w_mod[l], b_mod[l])
        m_lat = mods[:b].reshape(b, N_MOD, D)
        m_ctx = jnp.broadcast_to(mods[b].reshape(1, N_MOD, D), (b, N_MOD, D))
        modsel = jnp.stack([m_ctx, m_lat], axis=1)
        modsel = jnp.pad(modsel, ((0, 0), (0, 0), (0, SUB - N_MOD), (0, 0)))

        w_all = _gather_cols(w_in[l], TAB["src"]).astype(BF16)
        p, u = _inproj_call(xs, modsel, norm_mix[l], cos_t, sin_t, w_all)

        bblk, a_re, a_im, cblk = _s5_params(s5_lam_re[l], s5_lam_im[l], s5_log_dt[l],
                                            s5_b_re[l], s5_b_im[l], s5_c_re[l], s5_c_im[l])
        s5 = _s5_call(u, bblk, a_re, a_im, cblk, ctx_len)

        dmask, qdec, kdec, cdec = _ret_tables(ret_log_decay[l])
        yret = _ret_call(p, dmask, qdec, kdec, cdec,
                         _gather_cols(ret_gn[l], TAB["vsrc"]).reshape(1, V_P), ctx_len)

        gw, gb, tri = _gla_tables(gla_gate_w[l], gla_gate_b[l])
        ygla = _gla_call(p, gw, gb, tri, _gather_cols(gla_norm[l], TAB["vsrc"]).reshape(1, V_P),
                         ctx_len)

        xs = _merge_ffn_call(xs, modsel, u, s5, yret, ygla, p,
                             s5_d[l].reshape(1, S5_W), s5_glu_w[l].astype(BF16),
                             s5_glu_b[l].reshape(1, S5_W), w_br_s5[l].astype(BF16),
                             _pad_rows(w_br_ret[l], TAB["vsrc"]).astype(BF16),
                             _pad_rows(w_br_gla[l], TAB["vsrc"]).astype(BF16),
                             w_out[l].astype(BF16), norm_ffn[l], w_ffn_in[l].astype(BF16),
                             w_ffn_out[l].astype(BF16), norm_final, last, last, ctx_len)
    return xs
```

```python
import functools

import numpy as np
import jax
import jax.numpy as jnp
from jax import lax
from jax.experimental import pallas as pl
from jax.experimental.pallas import tpu as pltpu

F32 = jnp.float32
BF16 = jnp.bfloat16

D = 1024
EPS = 1e-6
N_MOD = 6
GRID_W = 64
S5_W = 256
S5_GROUPS = 16
S5_STATE = 64
S5_NS = S5_GROUPS * S5_STATE
HEADS = 4
DK = 48
DV = 96
QK_W = HEADS * DK
V_W = HEADS * DV
GLA_RANK = 16
GLA_TAU = 16.0
ROPE_BASE = 10000.0
FFN_H = 2816

LANE = 128
SUB = 8
QK_P = 2 * LANE
V_P = HEADS * LANE
QKV_P = 2 * QK_P + V_P
GATE_W = 3 * D

OFF_GATE = 0
OFF_RQKV = GATE_W
OFF_GQKV = OFF_RQKV + QKV_P
OFF_RG = OFF_GQKV + QKV_P
OFF_GG = OFF_RG + V_P
NP = OFF_GG + V_P
OFF_S5 = NP
NW = NP + S5_W

ROW_T = 256
RET_C = 256
GLA_C = 64
GLA_R = 256
MIX_NB = 4
S5_C = 64
V7X_VMEM_BYTES = 64 * 1024 * 1024
VMEM_LIMIT = V7X_VMEM_BYTES * 7 // 8

IN_SIZES = (S5_W, QK_W, QK_W, V_W, V_W, QK_W, QK_W, V_W, V_W, GLA_RANK, GLA_RANK, D, D, D)
IN_OFFS = np.concatenate([[0], np.cumsum(IN_SIZES)]).astype(np.int64)


def _head_lane():
    m = np.zeros((HEADS, DK), np.int64)
    for h in range(HEADS):
        for i in range(DK):
            m[h, i] = (i % 2) * LANE + h * (DK // 2) + i // 2
    return m


HEAD_LANE = _head_lane()
HALF = LANE // 2
GLA_LANE = np.array([[(h // 2) * LANE + (h % 2) * HALF + i for i in range(DK)] for h in range(HEADS)])
GLA_Z_LANE = (DK, HALF + DK)
assert HALF - DK == GLA_RANK


def _static_tables():
    src = np.full((NW,), -1, np.int64)
    src[OFF_GATE:OFF_GATE + GATE_W] = IN_OFFS[11] + np.arange(GATE_W)
    for qkv, gg, lane, iq, ik, iv, ig in ((OFF_RQKV, OFF_RG, HEAD_LANE, 1, 2, 3, 4),
                                          (OFF_GQKV, OFF_GG, GLA_LANE, 5, 6, 7, 8)):
        for h in range(HEADS):
            for i in range(DK):
                src[qkv + lane[h, i]] = IN_OFFS[iq] + h * DK + i
                src[qkv + QK_P + lane[h, i]] = IN_OFFS[ik] + h * DK + i
            for j in range(DV):
                src[qkv + 2 * QK_P + h * LANE + j] = IN_OFFS[iv] + h * DV + j
                src[gg + h * LANE + j] = IN_OFFS[ig] + h * DV + j
    for d in range(2):
        src[OFF_GQKV + GLA_Z_LANE[d]:OFF_GQKV + GLA_Z_LANE[d] + GLA_RANK] = (
            IN_OFFS[9 + d] + np.arange(GLA_RANK))
    src[OFF_S5:OFF_S5 + S5_W] = np.arange(S5_W)
    vsrc = np.full((V_P,), -1, np.int64)
    for h in range(HEADS):
        vsrc[h * LANE:h * LANE + DV] = h * DV + np.arange(DV)
    qk_head = np.full((QK_P,), -1, np.int64)
    qk_src = np.full((QK_P,), -1, np.int64)
    for h in range(HEADS):
        for i in range(DK):
            qk_head[HEAD_LANE[h, i]] = h
            qk_src[HEAD_LANE[h, i]] = h * DK + i
    v_head = np.repeat(np.arange(HEADS), LANE)
    v_real = (np.arange(V_P) % LANE) < DV
    head_mask = np.zeros((SUB, QK_P), np.float32)
    for h in range(HEADS):
        head_mask[h] = (qk_head == h)
    bm_t = (v_head[:, None] == qk_head[None, :]).astype(np.float32)
    gla_src = np.full((QK_P,), -1, np.int64)
    for h in range(HEADS):
        gla_src[GLA_LANE[h]] = h * DK + np.arange(DK)
    lane_in_pair = np.arange(LANE)
    gla_hm = np.stack([(lane_in_pair // HALF == r) & (lane_in_pair % HALF < DK) for r in range(2)])
    gla_bm = (np.arange(2 * LANE)[:, None] // LANE) == (lane_in_pair[None, :] // HALF)
    return dict(src=src, vsrc=vsrc, qk_head=qk_head, qk_src=qk_src, head_mask=head_mask,
                bm_t=bm_t, v_real=v_real.astype(np.float32), gla_src=gla_src,
                gla_hm=gla_hm.astype(np.float32), gla_bm=gla_bm.astype(np.float32))


TAB = _static_tables()


def _gather_cols(w, src):
    valid = jnp.asarray(src >= 0)
    out = jnp.take(w, jnp.asarray(np.maximum(src, 0)), axis=-1)
    return jnp.where(valid, out, jnp.zeros((), w.dtype))


def _relayout_w_in(w):
    r = w.shape[0]
    s5, rq, rk, rv, rg, gq, gk, gv, gg, zf, zb = (
        w[:, IN_OFFS[i]:IN_OFFS[i + 1]] for i in range(11))
    gate = w[:, IN_OFFS[11]:]
    pad = lambda t, n: jnp.pad(t, ((0, 0),) * (t.ndim - 1) + ((0, n),))

    def rotary_halves(t):
        t4 = t.reshape(r, HEADS, DK // 2, 2)
        halves = [pad(t4[..., par].reshape(r, QK_W // 2), LANE - QK_W // 2) for par in (0, 1)]
        return jnp.concatenate(halves, axis=1)

    def slots(t):
        return pad(t.reshape(r, HEADS, DV), LANE - DV).reshape(r, V_P)

    gq3, gk3 = gq.reshape(r, HEADS, DK), gk.reshape(r, HEADS, DK)
    zero = jnp.zeros((r, GLA_RANK), w.dtype)
    gla_q = jnp.concatenate([gq3[:, 0], zf, gq3[:, 1], zb, gq3[:, 2], zero, gq3[:, 3], zero], axis=1)
    gla_k = pad(gk3, HALF - DK).reshape(r, QK_P)
    return jnp.concatenate([gate, rotary_halves(rq), rotary_halves(rk), slots(rv),
                            gla_q, gla_k, slots(gv), slots(rg), slots(gg), s5], axis=1)


def _cparams(n_axes):
    return pltpu.CompilerParams(dimension_semantics=("arbitrary",) * n_axes,
                                vmem_limit_bytes=VMEM_LIMIT)


def _const_spec(shape):
    nd = len(shape)
    return pl.BlockSpec(shape, lambda *_: (0,) * nd, pipeline_mode=pl.Buffered(1))


def _dir_chunk(d, c, n_ctx, n_all):
    bwd = jnp.where(c < n_ctx, n_ctx - 1 - c, n_all + n_ctx - 1 - c)
    return jnp.where(d == 0, c, bwd)


def _sigmoid(x):
    return 0.5 * jnp.tanh(0.5 * x) + 0.5


def _silu(x):
    return x * _sigmoid(x)


def _mod_kernel(c_ref, w_ref, b_ref, o_ref):
    o_ref[...] = jnp.dot(_silu(c_ref[...]), w_ref[...], preferred_element_type=F32,
                         precision=lax.Precision.HIGHEST) + b_ref[...]


def _mod_call(c_all, w_mod_l, b_mod_l):
    rows = c_all.shape[0]
    tn = D
    return pl.pallas_call(
        _mod_kernel,
        grid=(N_MOD * D // tn,),
        in_specs=[pl.BlockSpec((rows, D), lambda j: (0, 0)),
                  pl.BlockSpec((D, tn), lambda j: (0, j)),
                  pl.BlockSpec((1, tn), lambda j: (0, j))],
        out_specs=pl.BlockSpec((rows, tn), lambda j: (0, j)),
        out_shape=jax.ShapeDtypeStruct((rows, N_MOD * D), F32),
        compiler_params=_cparams(1),
    )(c_all, w_mod_l, b_mod_l.reshape(1, -1))


def _rms(x, gain):
    return x * lax.rsqrt(jnp.mean(x * x, axis=-1, keepdims=True) + EPS) * gain


def _residual_rows(refs, split):
    if not split:
        return refs[0][...]
    return jnp.where(pl.program_id(1) == 0, refs[0][...], refs[1][...])


def _residual_specs(split, j0):
    if not split:
        return [pl.BlockSpec((None, ROW_T, D), lambda i, j: (i, j + j0, 0))]
    return [pl.BlockSpec((None, ROW_T, D), lambda i, j: (i, 0, 0)),
            pl.BlockSpec((None, ROW_T, D), lambda i, j: (i, jnp.maximum(j - 1, 0), 0))]


def _inproj_kernel(*refs, split, ctx_readout):
    mod_ref, gain_ref, cos_ref, sin_ref, w_ref, p_ref, u_ref = refs[1 + split:]
    h = _rms(_residual_rows(refs, split), gain_ref[...])
    h = h * (1.0 + mod_ref[1:2, :]) + mod_ref[0:1, :]
    hb = h.astype(BF16)

    def proj(c0, c1):
        return jnp.dot(hb, w_ref[:, c0:c1], preferred_element_type=F32)

    def readout_cols():
        for c0 in list(range(0, GATE_W, D)) + [OFF_RG]:
            p_ref[:, c0:c0 + D] = proj(c0, c0 + D).astype(BF16)

    if ctx_readout:
        readout_cols()
    else:
        is_ctx = pl.program_id(1) == 0
        pl.when(jnp.logical_not(is_ctx))(readout_cols)

        @pl.when(is_ctx)
        def _():
            p_ref[:, 0:GATE_W] = jnp.zeros((ROW_T, GATE_W), BF16)
            p_ref[:, OFF_RG:NP] = jnp.zeros((ROW_T, NP - OFF_RG), BF16)
    cs = cos_ref[...]
    sn = sin_ref[...]
    qk = proj(OFF_RQKV, OFF_RQKV + 2 * QK_P)
    for o, scale in ((0, 1.0), (QK_P, DK ** -0.5)):
        a = qk[:, o:o + LANE] * scale
        b = qk[:, o + LANE:o + 2 * LANE] * scale
        p_ref[:, OFF_RQKV + o:OFF_RQKV + o + LANE] = (a * cs - b * sn).astype(BF16)
        p_ref[:, OFF_RQKV + o + LANE:OFF_RQKV + o + 2 * LANE] = (a * sn + b * cs).astype(BF16)
    c0 = OFF_RQKV + 2 * QK_P
    p_ref[:, c0:OFF_GQKV] = proj(c0, OFF_GQKV).astype(BF16)
    p_ref[:, OFF_GQKV:OFF_GQKV + QK_P] = (proj(OFF_GQKV, OFF_GQKV + QK_P) * (DK ** -0.5)).astype(BF16)
    c0 = OFF_GQKV + QK_P
    p_ref[:, c0:OFF_RG] = proj(c0, OFF_RG).astype(BF16)
    u_ref[...] = proj(OFF_S5, OFF_S5 + S5_W)


def _inproj_call(xs, modsel, gain, cos_t, sin_t, w_all, ctx_readout):
    split = isinstance(xs, tuple)
    xs = xs if split else (xs,)
    b = xs[0].shape[0]
    lt = sum(t.shape[1] for t in xs)
    nt = lt // ROW_T
    return pl.pallas_call(
        functools.partial(_inproj_kernel, split=split, ctx_readout=ctx_readout),
        grid=(b, nt),
        in_specs=_residual_specs(split, 0) + [
                  pl.BlockSpec((None, None, SUB, D), lambda i, j: (i, jnp.minimum(j, 1), 0, 0)),
                  _const_spec((1, D)),
                  pl.BlockSpec((ROW_T, LANE), lambda i, j: (j, 0)),
                  pl.BlockSpec((ROW_T, LANE), lambda i, j: (j, 0)),
                  _const_spec((D, NW))],
        out_specs=[pl.BlockSpec((None, ROW_T, NP), lambda i, j: (i, j, 0)),
                   pl.BlockSpec((None, ROW_T, S5_W), lambda i, j: (i, j, 0))],
        out_shape=[jax.ShapeDtypeStruct((b, lt, NP), BF16),
                   jax.ShapeDtypeStruct((b, lt, S5_W), F32)],
        compiler_params=_cparams(2),
    )(*xs, modsel, gain.reshape(1, D), cos_t, sin_t, w_all)


S5_NB = 2 * SUB
S5_RB = 256


def _s5_kernel(uf_ref, ub_ref, bb_ref, are_ref, aim_ref, cb_ref, yf_ref, yb_ref,
               x_ref, hb_ref, h_ref, *, tc):
    rows = tc * S5_NB
    steps = S5_RB // S5_NB
    n_blk = rows // S5_RB
    orders = (list(range(n_blk)), list(range(n_blk - 1, -1, -1)))
    u_refs, y_refs = (uf_ref, ub_ref), (yf_ref, yb_ref)

    @pl.when(pl.program_id(1) == 0)
    def _():
        h_ref[...] = jnp.zeros_like(h_ref)

    for d in (0, 1):
        ub = pltpu.einshape("bts->(tb)s", u_refs[d][...]).astype(BF16)
        for k in orders[d]:
            rs = slice(k * S5_RB, (k + 1) * S5_RB)
            x_ref[d, rs, :] = jnp.dot(ub[rs], bb_ref[d], preferred_element_type=F32)

    half = S5_NS // 2
    lanes = [(slice(lo, lo + half), slice(S5_NS + lo, S5_NS + lo + half)) for lo in (0, half)]
    coef = [[(are_ref[d, :, re], aim_ref[d, :, re]) for re, _ in lanes] for d in (0, 1)]
    carry = [[[(h_ref[d, g * SUB:(g + 1) * SUB, re], h_ref[d, g * SUB:(g + 1) * SUB, im])
               for g in range(S5_NB // SUB)] for re, im in lanes] for d in (0, 1)]
    for kk in range(n_blk):
        for d in (0, 1):
            k = orders[d][kk]
            ts = range(k * steps, (k + 1) * steps)
            for t in (reversed(ts) if d else ts):
                r0 = t * S5_NB
                for li, (re, im) in enumerate(lanes):
                    ar, ai = coef[d][li]
                    new = []
                    for g, (hr, hi) in enumerate(carry[d][li]):
                        rw = slice(r0 + g * SUB, r0 + (g + 1) * SUB)
                        new.append((ar * hr - ai * hi + x_ref[d, rw, re],
                                    ar * hi + ai * hr + x_ref[d, rw, im]))
                    carry[d][li] = new
                    hb_ref[d, r0:r0 + S5_NB, re] = jnp.concatenate(
                        [n[0] for n in new], axis=0).astype(BF16)
                    hb_ref[d, r0:r0 + S5_NB, im] = jnp.concatenate(
                        [n[1] for n in new], axis=0).astype(BF16)
            rs = slice(k * S5_RB, (k + 1) * S5_RB)
            y = jnp.dot(hb_ref[d, rs, :], cb_ref[...], preferred_element_type=F32)
            y_refs[d][:, k * steps:(k + 1) * steps, :] = pltpu.einshape("(tb)s->bts", y, b=S5_NB)
    for d in (0, 1):
        for li, (re, im) in enumerate(lanes):
            for g, (hr, hi) in enumerate(carry[d][li]):
                h_ref[d, g * SUB:(g + 1) * SUB, re] = hr
                h_ref[d, g * SUB:(g + 1) * SUB, im] = hi


def _s5_call(u, bblk, a_re, a_im, cblk, ctx_len):
    b, lt, _ = u.shape
    tc = S5_C
    n_all, n_ctx = lt // tc, ctx_len // tc
    tok = lambda d: pl.BlockSpec((S5_NB, tc, S5_W), lambda g, c: (g, _dir_chunk(d, c, n_ctx, n_all), 0))
    out = jax.ShapeDtypeStruct((b, lt, S5_W), F32)
    return pl.pallas_call(
        functools.partial(_s5_kernel, tc=tc),
        grid=(b // S5_NB, n_all),
        in_specs=[tok(0), tok(1), _const_spec((2, S5_W, 2 * S5_NS)), _const_spec((2, SUB, S5_NS)),
                  _const_spec((2, SUB, S5_NS)), _const_spec((2 * S5_NS, S5_W))],
        out_specs=[tok(0), tok(1)],
        out_shape=[out, out],
        scratch_shapes=[pltpu.VMEM((2, tc * S5_NB, 2 * S5_NS), F32),
                        pltpu.VMEM((2, tc * S5_NB, 2 * S5_NS), BF16),
                        pltpu.VMEM((2, S5_NB, 2 * S5_NS), F32)],
        compiler_params=_cparams(2),
    )(u, u, bblk, a_re, a_im, cblk)


_NT = (((1,), (1,)), ((), ()))
_TN = (((0,), (0,)), ((), ()))


def _ret_kernel(*refs, backward):
    if backward:
        (p_ref, dm_ref, qd_ref, kd_ref, cd_ref, hm_ref, bm_ref, of_ref, g_ref, real_ref, gn_ref,
         y_ref, s_ref) = refs
    else:
        p_ref, dm_ref, qd_ref, kd_ref, cd_ref, hm_ref, bm_ref, o_ref, s_ref = refs

    @pl.when(pl.program_id(1) == 0)
    def _():
        s_ref[...] = jnp.zeros_like(s_ref)

    def chunk(bi):
        qb = p_ref[bi, :, 0:QK_P]
        kb = p_ref[bi, :, QK_P:2 * QK_P]
        vb = p_ref[bi, :, 2 * QK_P:2 * QK_P + V_P]
        st = s_ref[bi]
        inter = lax.dot_general(qb * qd_ref[...], st.astype(BF16) * bm_ref[...], _NT,
                                preferred_element_type=F32)
        for h in range(HEADS):
            att = lax.dot_general(qb * hm_ref[h], kb, _NT, preferred_element_type=F32)
            att = att.astype(BF16) * dm_ref[h]
            sl = slice(h * LANE, (h + 1) * LANE)
            o = jnp.dot(att, vb[:, sl], preferred_element_type=F32) + inter[:, sl]
            if backward:
                o = o + of_ref[bi, :, sl]
                mu = jnp.sum(o, axis=-1, keepdims=True) * (1.0 / DV)
                dlt = (o - mu) * real_ref[:, sl]
                var = jnp.sum(dlt * dlt, axis=-1, keepdims=True) * (1.0 / DV)
                y = dlt * lax.rsqrt(var + EPS) * gn_ref[:, sl] * _silu(g_ref[bi, :, sl].astype(F32))
                y_ref[bi, :, sl] = y.astype(BF16)
            else:
                o_ref[bi, :, sl] = o
        kdv = lax.dot_general(vb, kb * kd_ref[...], _TN, preferred_element_type=F32)
        s_ref[bi] = st * cd_ref[...] + kdv

    for bi in range(MIX_NB):
        chunk(bi)


def _mixer_specs(b, lt, ctx_len, tc, off_qkv, off_g):
    n_all, n_ctx = lt // tc, ctx_len // tc
    tok = lambda d, w, blk: pl.BlockSpec(
        (MIX_NB, tc, w), lambda i, c: (i, _dir_chunk(d, c, n_ctx, n_all), blk))
    qkv = lambda d: tok(d, QKV_P, off_qkv // QKV_P)
    gsp = lambda d: tok(d, V_P, off_g // V_P)
    out = lambda d: tok(d, V_P, 0)
    state = pltpu.VMEM((MIX_NB, V_P, QK_P), F32)
    return (b // MIX_NB, n_all), tok, qkv, gsp, out, state


def _head_mask_rows(rows):
    return jnp.asarray(np.broadcast_to(TAB["head_mask"][:HEADS, None, :], (HEADS, rows, QK_P)), dtype=BF16)


def _ret_call(p, dmask, qdec, kdec, cdec, gn, ctx_len):
    b, lt, _ = p.shape
    tc = RET_C
    grid, _, qkv, gsp, out, state = _mixer_specs(b, lt, ctx_len, tc, OFF_RQKV, OFF_RG)
    ins = lambda d: [qkv(d), _const_spec((HEADS, tc, tc)), _const_spec((tc, QK_P)),
                     _const_spec((tc, QK_P)), _const_spec((1, QK_P)),
                     _const_spec((HEADS, tc, QK_P)), _const_spec((V_P, QK_P))]
    tabs = lambda d: (dmask[d].astype(BF16), qdec[d].astype(BF16), kdec[d].astype(BF16), cdec[d],
                      _head_mask_rows(tc), jnp.asarray(TAB["bm_t"], dtype=BF16))
    of = pl.pallas_call(
        functools.partial(_ret_kernel, backward=False),
        grid=grid, in_specs=ins(0), out_specs=out(0),
        out_shape=jax.ShapeDtypeStruct((b, lt, V_P), F32),
        scratch_shapes=[state], compiler_params=_cparams(2),
    )(p, *tabs(0))
    return pl.pallas_call(
        functools.partial(_ret_kernel, backward=True),
        grid=grid,
        in_specs=ins(1) + [out(1), gsp(1), _const_spec((1, V_P)), _const_spec((1, V_P))],
        out_specs=out(1),
        out_shape=jax.ShapeDtypeStruct((b, lt, V_P), BF16),
        scratch_shapes=[state], compiler_params=_cparams(2),
    )(p, *tabs(1), of, p, jnp.asarray(TAB["v_real"]).reshape(1, V_P), gn)


def _gla_kernel(*refs, backward):
    if backward:
        (p_ref, gw_ref, gb_ref, tri_ref, hm_ref, bm_ref, of_ref, g_ref, gn_ref,
         y_ref, s_ref, oc_ref) = refs
    else:
        p_ref, gw_ref, gb_ref, tri_ref, hm_ref, bm_ref, oc_ref, s_ref = refs

    @pl.when(pl.program_id(1) == 0)
    def _():
        s_ref[...] = jnp.zeros_like(s_ref)

    tri = tri_ref[...]
    mask = tri > 0

    def intra(bi):
        la = jax.nn.log_sigmoid(
            jnp.dot(p_ref[bi, :, 0:LANE], gw_ref[...], preferred_element_type=F32)
            + gb_ref[...]) / GLA_TAU
        hi = la.astype(BF16)
        lo = (la - hi.astype(F32)).astype(BF16)
        bcum = (jnp.dot(tri, hi, preferred_element_type=F32)
                + jnp.dot(tri, lo, preferred_element_type=F32))
        q = p_ref[bi, :, 0:QK_P].astype(F32)
        k = p_ref[bi, :, QK_P:2 * QK_P].astype(F32)
        vb = p_ref[bi, :, 2 * QK_P:2 * QK_P + V_P]
        qtb = (q * jnp.exp(bcum)).astype(BF16)
        ktb = (k * jnp.exp(-bcum)).astype(BF16)
        for h in range(HEADS):
            pr = slice(h // 2 * LANE, (h // 2 + 1) * LANE)
            att = lax.dot_general(qtb[:, pr] * hm_ref[h % 2], ktb[:, pr], _NT,
                                  preferred_element_type=F32)
            att = jnp.where(mask, att.astype(BF16), jnp.zeros((), BF16))
            sl = slice(h * LANE, (h + 1) * LANE)
            oc_ref[bi, :, sl] = jnp.dot(att, vb[:, sl], preferred_element_type=F32)
        return bcum, k, qtb, vb

    parts = [intra(bi) for bi in range(MIX_NB)]

    n_sub = GLA_R // GLA_C
    n_pair = HEADS // 2
    order = range(n_sub - 1, -1, -1) if backward else range(n_sub)
    last_row = 0 if backward else GLA_C - 1
    ss = [[s_ref[bi, pi] for pi in range(n_pair)] for bi in range(MIX_NB)]
    for cc in order:
        r0 = cc * GLA_C
        rows = slice(r0, r0 + GLA_C)
        for bi, (bcum, k, qtb, vb) in enumerate(parts):
            bl = bcum[r0 + last_row:r0 + last_row + 1, :]
            kd = (k[rows] * jnp.exp(bl - bcum[rows])).astype(BF16)
            ebl = jnp.exp(bl)
            for pi in range(n_pair):
                pr = slice(pi * LANE, (pi + 1) * LANE)
                vs = slice(2 * pi * LANE, 2 * (pi + 1) * LANE)
                s = ss[bi][pi]
                oc_ref[bi, rows, vs] += lax.dot_general(qtb[rows, pr], s.astype(BF16) * bm_ref[...],
                                                        _NT, preferred_element_type=F32)
                kdv = lax.dot_general(vb[rows, vs], kd[:, pr], _TN, preferred_element_type=F32)
                ss[bi][pi] = s * ebl[:, pr] + kdv
    for bi in range(MIX_NB):
        for pi in range(n_pair):
            s_ref[bi, pi] = ss[bi][pi]

    if backward:
        for bi in range(MIX_NB):
            for h in range(HEADS):
                sl = slice(h * LANE, (h + 1) * LANE)
                o = oc_ref[bi, :, sl] + of_ref[bi, :, sl]
                ms = jnp.sum(o * o, axis=-1, keepdims=True) * (1.0 / DV)
                y = o * lax.rsqrt(ms + EPS) * gn_ref[:, sl] * _silu(g_ref[bi, :, sl].astype(F32))
                y_ref[bi, :, sl] = y.astype(BF16)


def _gla_call(p, gw, gb, tri, gn, ctx_len):
    b, lt, _ = p.shape
    tc = GLA_R
    grid, _, qkv, gsp, out, _ = _mixer_specs(b, lt, ctx_len, tc, OFF_GQKV, OFF_GG)
    ins = lambda d: [qkv(d),
                     _const_spec((LANE, QK_P)), _const_spec((1, QK_P)), _const_spec((tc, tc)),
                     _const_spec((2, tc, LANE)), _const_spec((2 * LANE, LANE))]
    hm = jnp.asarray(np.broadcast_to(TAB["gla_hm"][:, None, :], (2, tc, LANE)), dtype=BF16)
    bm = jnp.asarray(TAB["gla_bm"], dtype=BF16)
    state = pltpu.VMEM((MIX_NB, HEADS // 2, 2 * LANE, LANE), F32)
    of = pl.pallas_call(
        functools.partial(_gla_kernel, backward=False),
        grid=grid, in_specs=ins(0), out_specs=out(0),
        out_shape=jax.ShapeDtypeStruct((b, lt, V_P), F32),
        scratch_shapes=[state], compiler_params=_cparams(2),
    )(p, gw[0], gb[0], tri[0], hm, bm)
    return pl.pallas_call(
        functools.partial(_gla_kernel, backward=True),
        grid=grid,
        in_specs=ins(1) + [out(1), gsp(1), _const_spec((1, V_P))],
        out_specs=out(1),
        out_shape=jax.ShapeDtypeStruct((b, lt, V_P), BF16),
        scratch_shapes=[state, pltpu.VMEM((MIX_NB, tc, V_P), F32)],
        compiler_params=_cparams(2),
    )(p, gw[1], gb[1], tri[1], hm, bm, of, p, gn)


FFN_T = 256


def _merge_ffn_kernel(*refs, final, split):
    (mod_ref, u_ref, sf_ref, sb_ref, yr_ref, yg_ref, gate_ref,
     dsk_ref, gluw_ref, glub_ref, wbs_ref, wbr_ref, wbg_ref, wo_ref,
     gain_ref, wfi_ref, wfo_ref, fin_ref, o_ref, act_ref) = refs[1 + split:]

    def branch(y, w_ref, g0):
        gate = _sigmoid(gate_ref[:, g0:g0 + D].astype(F32))
        return gate * jnp.dot(y, w_ref[...], preferred_element_type=F32)

    ys = jax.nn.gelu(sf_ref[...] + sb_ref[...] + dsk_ref[...] * u_ref[...])
    glu = jnp.dot(ys.astype(BF16), gluw_ref[...], preferred_element_type=F32) + glub_ref[...]
    ys = (ys * _sigmoid(glu)).astype(BF16)
    m = (branch(ys, wbs_ref, 0) + branch(yr_ref[...], wbr_ref, D)
         + branch(yg_ref[...], wbg_ref, 2 * D))
    x = _residual_rows(refs, split) + mod_ref[2:3, :] * jnp.dot(m.astype(BF16), wo_ref[...],
                                                                  preferred_element_type=F32)
    h = _rms(x, gain_ref[...])
    hb = (h * (1.0 + mod_ref[4:5, :]) + mod_ref[3:4, :]).astype(BF16)
    for t in range(FFN_H // FFN_T):
        c0 = t * FFN_T
        a = jnp.dot(hb, wfi_ref[:, c0:c0 + FFN_T], preferred_element_type=F32)
        bq = jnp.dot(hb, wfi_ref[:, FFN_H + c0:FFN_H + c0 + FFN_T], preferred_element_type=F32)
        act_ref[:, t * FFN_T:(t + 1) * FFN_T] = (_silu(a) * bq).astype(BF16)
    y = x + mod_ref[5:6, :] * jnp.dot(act_ref[...], wfo_ref[...], preferred_element_type=F32)
    if final:
        y = _rms(y, fin_ref[...])
    o_ref[...] = y


def _merge_ffn_call(xs, modsel, u, s5, yret, ygla, p, dsk, gluw, glub, wbs, wbr, wbg, wo,
                    gain, wfi, wfo, fin, final, lat_only, ctx_len):
    split = isinstance(xs, tuple)
    xs = xs if split else (xs,)
    b = xs[0].shape[0]
    lt = sum(t.shape[1] for t in xs)
    j0 = ctx_len // ROW_T if lat_only else 0
    nt = lt // ROW_T - j0
    tok = lambda w: pl.BlockSpec((None, ROW_T, w), lambda i, j: (i, j + j0, 0))
    return pl.pallas_call(
        functools.partial(_merge_ffn_kernel, final=final, split=split),
        grid=(b, nt),
        in_specs=_residual_specs(split, j0) + [
                  pl.BlockSpec((None, None, SUB, D), lambda i, j: (i, jnp.minimum(j + j0, 1), 0, 0)),
                  tok(S5_W), tok(S5_W), tok(S5_W), tok(V_P), tok(V_P), tok(GATE_W),
                  _const_spec((1, S5_W)), _const_spec((S5_W, S5_W)), _const_spec((1, S5_W)),
                  _const_spec((S5_W, D)), _const_spec((V_P, D)), _const_spec((V_P, D)),
                  _const_spec((D, D)),
                  _const_spec((1, D)),
                  _const_spec((D, 2 * FFN_H)),
                  _const_spec((FFN_H, D)), _const_spec((1, D))],
        out_specs=pl.BlockSpec((None, ROW_T, D), lambda i, j: (i, j, 0)),
        out_shape=jax.ShapeDtypeStruct((b, nt * ROW_T, D), F32),
        scratch_shapes=[pltpu.VMEM((ROW_T, FFN_H), BF16)],
        compiler_params=_cparams(2),
    )(*xs, modsel, u, *s5, yret, ygla, p, dsk, gluw, glub, wbs, wbr, wbg, wo,
      gain.reshape(1, D), wfi, wfo, fin.reshape(1, D))


def _s5_params(lam_re, lam_im, log_dt, b_re, b_im, c_re, c_im):
    dt = jnp.exp(log_dt)[..., None]
    mag = jnp.exp(lam_re * dt)
    a_re, a_im = mag * jnp.cos(lam_im * dt), mag * jnp.sin(lam_im * dt)
    den = lam_re * lam_re + lam_im * lam_im
    f_re = ((a_re - 1.0) * lam_re + a_im * lam_im) / den
    f_im = (a_im * lam_re - (a_re - 1.0) * lam_im) / den
    bb_re = f_re[..., None] * b_re - f_im[..., None] * b_im
    bb_im = f_re[..., None] * b_im + f_im[..., None] * b_re
    eye = jnp.eye(S5_GROUPS, dtype=F32)

    def blk_in(bb):
        t = jnp.einsum("dgpc,gh->dgchp", bb, eye)
        return t.reshape(2, S5_W, S5_NS)

    bblk = jnp.concatenate([blk_in(bb_re), blk_in(bb_im)], axis=-1).astype(BF16)

    def blk_out(cc):
        t = jnp.einsum("gcp,gh->gphc", cc, eye)
        return t.reshape(S5_NS, S5_W)

    cblk = jnp.concatenate([blk_out(c_re), -blk_out(c_im)], axis=0).astype(BF16)
    bc = lambda a: jnp.broadcast_to(a.reshape(2, 1, S5_NS), (2, SUB, S5_NS))
    return bblk, bc(a_re), bc(a_im), cblk


def _ret_tables(log_decay):
    tc = RET_C
    pos = jnp.arange(tc, dtype=F32)
    w = jnp.stack([pos, tc - 1.0 - pos])
    rel = w[:, :, None] - w[:, None, :]
    lg = log_decay[:, :, None, None]
    dmask = jnp.where(rel[:, None] >= 0, jnp.exp(jnp.maximum(rel[:, None], 0.0) * lg), 0.0)
    qk_head = TAB["qk_head"]
    lane_lg = jnp.where(jnp.asarray(qk_head >= 0),
                        jnp.take(log_decay, jnp.asarray(np.maximum(qk_head, 0)), axis=1), 0.0)
    qdec = jnp.exp((w[:, :, None] + 1.0) * lane_lg[:, None, :])
    kdec = jnp.exp((tc - 1.0 - w[:, :, None]) * lane_lg[:, None, :])
    cdec = jnp.exp(tc * lane_lg)[:, None, :]
    return dmask, qdec, kdec, cdec


def _rope_tables(seq, ctx_len):
    rows = seq // GRID_W
    nf = DK // 4
    inv = 1.0 / (ROPE_BASE ** (np.arange(nf, dtype=np.float32) / nf))
    r = np.repeat(np.arange(rows, dtype=np.float32), GRID_W)
    col = np.tile(np.arange(GRID_W, dtype=np.float32), rows)
    ang = np.concatenate([r[:, None] * inv, col[:, None] * inv], axis=-1)
    cos_t = np.ones((ctx_len + seq, LANE), np.float32)
    sin_t = np.zeros((ctx_len + seq, LANE), np.float32)
    for h in range(HEADS):
        cos_t[ctx_len:, h * 24:(h + 1) * 24] = np.cos(ang)
        sin_t[ctx_len:, h * 24:(h + 1) * 24] = np.sin(ang)
    return jnp.asarray(cos_t), jnp.asarray(sin_t)


def _gla_tables(gate_w, gate_b):
    gw = _gather_cols(gate_w, TAB["gla_src"])
    gw_full = jnp.zeros((2, LANE, QK_P), F32)
    for d in range(2):
        gw_full = gw_full.at[d, GLA_Z_LANE[d]:GLA_Z_LANE[d] + GLA_RANK].set(gw[d] * DK ** 0.5)
    gb = _gather_cols(gate_b, TAB["gla_src"])[:, None, :]
    i = np.arange(GLA_R)
    same = (i[:, None] // GLA_C) == (i[None, :] // GLA_C)
    tri = np.stack([same & (i[:, None] >= i[None, :]), same & (i[:, None] <= i[None, :])]).astype(np.float32)
    return gw_full.astype(BF16), gb, jnp.asarray(tri, dtype=BF16)


def _pad_rows(w, src):
    return _gather_cols(w.T, src).T


def kernel(x, c, ctx, c_ctx, w_mod, b_mod, norm_mix, norm_ffn, w_in, s5_lam_re, s5_lam_im, s5_log_dt, s5_b_re, s5_b_im, s5_c_re, s5_c_im, s5_d, s5_glu_w, s5_glu_b, ret_log_decay, ret_gn, gla_gate_w, gla_gate_b, gla_norm, w_br_s5, w_br_ret, w_br_gla, w_out, w_ffn_in, w_ffn_out, norm_final):
    b, seq, _ = x.shape
    ctx_len = ctx.shape[1]
    depth = w_mod.shape[0]
    assert b % S5_NB == 0 and b % MIX_NB == 0 and seq % RET_C == 0 and seq % GRID_W == 0
    assert ctx_len == ROW_T == RET_C

    xs = (ctx, x)
    rows = ((b + 1 + SUB - 1) // SUB) * SUB
    c_all = jnp.zeros((rows, D), F32).at[:b].set(c).at[b].set(c_ctx)
    cos_t, sin_t = _rope_tables(seq, ctx_len)
    nh = FFN_H // FFN_T

    for l in range(depth):
        last = l == depth - 1
        mods = _mod_call(c_all, w_mod[l], b_mod[l])
        m_lat = mods[:b].reshape(b, N_MOD, D)
        m_ctx = jnp.broadcast_to(mods[b].reshape(1, N_MOD, D), (b, N_MOD, D))
        modsel = jnp.stack([m_ctx, m_lat], axis=1)
        modsel = jnp.pad(modsel, ((0, 0), (0, 0), (0, SUB - N_MOD), (0, 0)))

        w_all = _relayout_w_in(w_in[l].astype(BF16))
        p, u = _inproj_call(xs, modsel, norm_mix[l], cos_t, sin_t, w_all, ctx_readout=not last)

        bblk, a_re, a_im, cblk = _s5_params(s5_lam_re[l], s5_lam_im[l], s5_log_dt[l],
                                            s5_b_re[l], s5_b_im[l], s5_c_re[l], s5_c_im[l])
        s5 = _s5_call(u, bblk, a_re, a_im, cblk, ctx_len)

        dmask, qdec, kdec, cdec = _ret_tables(ret_log_decay[l])
        yret = _ret_call(p, dmask, qdec, kdec, cdec,
                         _gather_cols(ret_gn[l], TAB["vsrc"]).reshape(1, V_P), ctx_len)

        gw, gb, tri = _gla_tables(gla_gate_w[l], gla_gate_b[l])
        ygla = _gla_call(p, gw, gb, tri, _gather_cols(gla_norm[l], TAB["vsrc"]).reshape(1, V_P),
                         ctx_len)

        xs = _merge_ffn_call(xs, modsel, u, s5, yret, ygla, p,
                             s5_d[l].reshape(1, S5_W), s5_glu_w[l].astype(BF16),
                             s5_glu_b[l].reshape(1, S5_W), w_br_s5[l].astype(BF16),
                             _pad_rows(w_br_ret[l], TAB["vsrc"]).astype(BF16),
                             _pad_rows(w_br_gla[l], TAB["vsrc"]).astype(BF16),
                             w_out[l].astype(BF16), norm_ffn[l], w_ffn_in[l].astype(BF16),
                             w_ffn_out[l].astype(BF16), norm_final, last, last, ctx_len)
    return xs
```

```python
import functools

import numpy as np
import jax
import jax.numpy as jnp
from jax import lax
from jax.experimental import pallas as pl
from jax.experimental.pallas import tpu as pltpu

F32 = jnp.float32
BF16 = jnp.bfloat16

D = 1024
EPS = 1e-6
N_MOD = 6
GRID_W = 64
S5_W = 256
S5_GROUPS = 16
S5_STATE = 64
S5_NS = S5_GROUPS * S5_STATE
HEADS = 4
DK = 48
DV = 96
QK_W = HEADS * DK
V_W = HEADS * DV
GLA_RANK = 16
GLA_TAU = 16.0
ROPE_BASE = 10000.0
FFN_H = 2816

LANE = 128
SUB = 8
QK_P = 2 * LANE
V_P = HEADS * LANE
QKV_P = 2 * QK_P + V_P
GATE_W = 3 * D

OFF_GATE = 0
OFF_RQKV = GATE_W
OFF_GQKV = OFF_RQKV + QKV_P
OFF_RG = OFF_GQKV + QKV_P
OFF_GG = OFF_RG + V_P
NP = OFF_GG + V_P
OFF_S5 = NP
NW = NP + S5_W

ROW_T = 256
RET_C = 256
GLA_C = 64
GLA_R = 256
MIX_NB = 4
S5_C = 64
V7X_VMEM_BYTES = 64 * 1024 * 1024
VMEM_LIMIT = V7X_VMEM_BYTES * 7 // 8

IN_SIZES = (S5_W, QK_W, QK_W, V_W, V_W, QK_W, QK_W, V_W, V_W, GLA_RANK, GLA_RANK, D, D, D)
IN_OFFS = np.concatenate([[0], np.cumsum(IN_SIZES)]).astype(np.int64)


def _head_lane():
    m = np.zeros((HEADS, DK), np.int64)
    for h in range(HEADS):
        for i in range(DK):
            m[h, i] = (i % 2) * LANE + h * (DK // 2) + i // 2
    return m


HEAD_LANE = _head_lane()
HALF = LANE // 2
GLA_LANE = np.array([[(h // 2) * LANE + (h % 2) * HALF + i for i in range(DK)] for h in range(HEADS)])
GLA_Z_LANE = (DK, HALF + DK)
assert HALF - DK == GLA_RANK


def _static_tables():
    src = np.full((NW,), -1, np.int64)
    src[OFF_GATE:OFF_GATE + GATE_W] = IN_OFFS[11] + np.arange(GATE_W)
    for qkv, gg, lane, iq, ik, iv, ig in ((OFF_RQKV, OFF_RG, HEAD_LANE, 1, 2, 3, 4),
                                          (OFF_GQKV, OFF_GG, GLA_LANE, 5, 6, 7, 8)):
        for h in range(HEADS):
            for i in range(DK):
                src[qkv + lane[h, i]] = IN_OFFS[iq] + h * DK + i
                src[qkv + QK_P + lane[h, i]] = IN_OFFS[ik] + h * DK + i
            for j in range(DV):
                src[qkv + 2 * QK_P + h * LANE + j] = IN_OFFS[iv] + h * DV + j
                src[gg + h * LANE + j] = IN_OFFS[ig] + h * DV + j
    for d in range(2):
        src[OFF_GQKV + GLA_Z_LANE[d]:OFF_GQKV + GLA_Z_LANE[d] + GLA_RANK] = (
            IN_OFFS[9 + d] + np.arange(GLA_RANK))
    src[OFF_S5:OFF_S5 + S5_W] = np.arange(S5_W)
    vsrc = np.full((V_P,), -1, np.int64)
    for h in range(HEADS):
        vsrc[h * LANE:h * LANE + DV] = h * DV + np.arange(DV)
    qk_head = np.full((QK_P,), -1, np.int64)
    qk_src = np.full((QK_P,), -1, np.int64)
    for h in range(HEADS):
        for i in range(DK):
            qk_head[HEAD_LANE[h, i]] = h
            qk_src[HEAD_LANE[h, i]] = h * DK + i
    v_head = np.repeat(np.arange(HEADS), LANE)
    v_real = (np.arange(V_P) % LANE) < DV
    head_mask = np.zeros((SUB, QK_P), np.float32)
    for h in range(HEADS):
        head_mask[h] = (qk_head == h)
    bm_t = (v_head[:, None] == qk_head[None, :]).astype(np.float32)
    gla_src = np.full((QK_P,), -1, np.int64)
    for h in range(HEADS):
        gla_src[GLA_LANE[h]] = h * DK + np.arange(DK)
    lane_in_pair = np.arange(LANE)
    gla_hm = np.stack([(lane_in_pair // HALF == r) & (lane_in_pair % HALF < DK) for r in range(2)])
    gla_bm = (np.arange(2 * LANE)[:, None] // LANE) == (lane_in_pair[None, :] // HALF)
    return dict(src=src, vsrc=vsrc, qk_head=qk_head, qk_src=qk_src, head_mask=head_mask,
                bm_t=bm_t, v_real=v_real.astype(np.float32), gla_src=gla_src,
                gla_hm=gla_hm.astype(np.float32), gla_bm=gla_bm.astype(np.float32))


TAB = _static_tables()


def _gather_cols(w, src):
    valid = jnp.asarray(src >= 0)
    out = jnp.take(w, jnp.asarray(np.maximum(src, 0)), axis=-1)
    return jnp.where(valid, out, jnp.zeros((), w.dtype))


def _cparams(n_axes):
    return pltpu.CompilerParams(dimension_semantics=("arbitrary",) * n_axes,
                                vmem_limit_bytes=VMEM_LIMIT)


def _const_spec(shape):
    nd = len(shape)
    return pl.BlockSpec(shape, lambda *_: (0,) * nd, pipeline_mode=pl.Buffered(1))


def _dir_chunk(d, c, n_ctx, n_all):
    bwd = jnp.where(c < n_ctx, n_ctx - 1 - c, n_all + n_ctx - 1 - c)
    return jnp.where(d == 0, c, bwd)


def _sigmoid(x):
    return 0.5 * jnp.tanh(0.5 * x) + 0.5


def _silu(x):
    return x * _sigmoid(x)


def _mod_kernel(c_ref, w_ref, b_ref, o_ref):
    o_ref[...] = jnp.dot(_silu(c_ref[...]), w_ref[...], preferred_element_type=F32,
                         precision=lax.Precision.HIGHEST) + b_ref[...]


def _mod_call(c_all, w_mod_l, b_mod_l):
    rows = c_all.shape[0]
    tn = D
    return pl.pallas_call(
        _mod_kernel,
        grid=(N_MOD * D // tn,),
        in_specs=[pl.BlockSpec((rows, D), lambda j: (0, 0)),
                  pl.BlockSpec((D, tn), lambda j: (0, j)),
                  pl.BlockSpec((1, tn), lambda j: (0, j))],
        out_specs=pl.BlockSpec((rows, tn), lambda j: (0, j)),
        out_shape=jax.ShapeDtypeStruct((rows, N_MOD * D), F32),
        compiler_params=_cparams(1),
    )(c_all, w_mod_l, b_mod_l.reshape(1, -1))


def _rms(x, gain):
    return x * lax.rsqrt(jnp.mean(x * x, axis=-1, keepdims=True) + EPS) * gain


def _residual_rows(refs, split):
    if not split:
        return refs[0][...]
    return jnp.where(pl.program_id(1) == 0, refs[0][...], refs[1][...])


def _residual_specs(split, j0):
    if not split:
        return [pl.BlockSpec((None, ROW_T, D), lambda i, j: (i, j + j0, 0))]
    return [pl.BlockSpec((None, ROW_T, D), lambda i, j: (i, 0, 0)),
            pl.BlockSpec((None, ROW_T, D), lambda i, j: (i, jnp.maximum(j - 1, 0), 0))]


def _inproj_kernel(*refs, split):
    mod_ref, gain_ref, cos_ref, sin_ref, w_ref, p_ref, u_ref = refs[1 + split:]
    h = _rms(_residual_rows(refs, split), gain_ref[...])
    h = h * (1.0 + mod_ref[1:2, :]) + mod_ref[0:1, :]
    hb = h.astype(BF16)

    def proj(c0, c1):
        return jnp.dot(hb, w_ref[:, c0:c1], preferred_element_type=F32)

    for c0 in range(0, GATE_W, D):
        p_ref[:, c0:c0 + D] = proj(c0, c0 + D).astype(BF16)
    cs = cos_ref[...]
    sn = sin_ref[...]
    qk = proj(OFF_RQKV, OFF_RQKV + 2 * QK_P)
    for o, scale in ((0, 1.0), (QK_P, DK ** -0.5)):
        a = qk[:, o:o + LANE] * scale
        b = qk[:, o + LANE:o + 2 * LANE] * scale
        p_ref[:, OFF_RQKV + o:OFF_RQKV + o + LANE] = (a * cs - b * sn).astype(BF16)
        p_ref[:, OFF_RQKV + o + LANE:OFF_RQKV + o + 2 * LANE] = (a * sn + b * cs).astype(BF16)
    c0 = OFF_RQKV + 2 * QK_P
    p_ref[:, c0:OFF_GQKV] = proj(c0, OFF_GQKV).astype(BF16)
    p_ref[:, OFF_GQKV:OFF_GQKV + QK_P] = (proj(OFF_GQKV, OFF_GQKV + QK_P) * (DK ** -0.5)).astype(BF16)
    c0 = OFF_GQKV + QK_P
    p_ref[:, c0:OFF_RG] = proj(c0, OFF_RG).astype(BF16)
    p_ref[:, OFF_RG:NP] = proj(OFF_RG, NP).astype(BF16)
    u_ref[...] = proj(OFF_S5, OFF_S5 + S5_W)


def _inproj_call(xs, modsel, gain, cos_t, sin_t, w_all):
    split = isinstance(xs, tuple)
    xs = xs if split else (xs,)
    b = xs[0].shape[0]
    lt = sum(t.shape[1] for t in xs)
    nt = lt // ROW_T
    return pl.pallas_call(
        functools.partial(_inproj_kernel, split=split),
        grid=(b, nt),
        in_specs=_residual_specs(split, 0) + [
                  pl.BlockSpec((None, None, SUB, D), lambda i, j: (i, jnp.minimum(j, 1), 0, 0)),
                  _const_spec((1, D)),
                  pl.BlockSpec((ROW_T, LANE), lambda i, j: (j, 0)),
                  pl.BlockSpec((ROW_T, LANE), lambda i, j: (j, 0)),
                  _const_spec((D, NW))],
        out_specs=[pl.BlockSpec((None, ROW_T, NP), lambda i, j: (i, j, 0)),
                   pl.BlockSpec((None, ROW_T, S5_W), lambda i, j: (i, j, 0))],
        out_shape=[jax.ShapeDtypeStruct((b, lt, NP), BF16),
                   jax.ShapeDtypeStruct((b, lt, S5_W), F32)],
        compiler_params=_cparams(2),
    )(*xs, modsel, gain.reshape(1, D), cos_t, sin_t, w_all)


S5_NB = 2 * SUB
S5_RB = 256


def _s5_kernel(uf_ref, ub_ref, bb_ref, are_ref, aim_ref, cb_ref, yf_ref, yb_ref,
               x_ref, hb_ref, h_ref, *, tc):
    rows = tc * S5_NB
    steps = S5_RB // S5_NB
    n_blk = rows // S5_RB
    orders = (list(range(n_blk)), list(range(n_blk - 1, -1, -1)))
    u_refs, y_refs = (uf_ref, ub_ref), (yf_ref, yb_ref)

    @pl.when(pl.program_id(1) == 0)
    def _():
        h_ref[...] = jnp.zeros_like(h_ref)

    for d in (0, 1):
        ub = pltpu.einshape("bts->(tb)s", u_refs[d][...]).astype(BF16)
        for k in orders[d]:
            rs = slice(k * S5_RB, (k + 1) * S5_RB)
            x_ref[d, rs, :] = jnp.dot(ub[rs], bb_ref[d], preferred_element_type=F32)

    half = S5_NS // 2
    lanes = [(slice(lo, lo + half), slice(S5_NS + lo, S5_NS + lo + half)) for lo in (0, half)]
    coef = [[(are_ref[d, :, re], aim_ref[d, :, re]) for re, _ in lanes] for d in (0, 1)]
    carry = [[[(h_ref[d, g * SUB:(g + 1) * SUB, re], h_ref[d, g * SUB:(g + 1) * SUB, im])
               for g in range(S5_NB // SUB)] for re, im in lanes] for d in (0, 1)]
    for kk in range(n_blk):
        for d in (0, 1):
            k = orders[d][kk]
            ts = range(k * steps, (k + 1) * steps)
            for t in (reversed(ts) if d else ts):
                r0 = t * S5_NB
                for li, (re, im) in enumerate(lanes):
                    ar, ai = coef[d][li]
                    new = []
                    for g, (hr, hi) in enumerate(carry[d][li]):
                        rw = slice(r0 + g * SUB, r0 + (g + 1) * SUB)
                        new.append((ar * hr - ai * hi + x_ref[d, rw, re],
                                    ar * hi + ai * hr + x_ref[d, rw, im]))
                    carry[d][li] = new
                    hb_ref[d, r0:r0 + S5_NB, re] = jnp.concatenate(
                        [n[0] for n in new], axis=0).astype(BF16)
                    hb_ref[d, r0:r0 + S5_NB, im] = jnp.concatenate(
                        [n[1] for n in new], axis=0).astype(BF16)
            rs = slice(k * S5_RB, (k + 1) * S5_RB)
            y = jnp.dot(hb_ref[d, rs, :], cb_ref[...], preferred_element_type=F32)
            y_refs[d][:, k * steps:(k + 1) * steps, :] = pltpu.einshape("(tb)s->bts", y, b=S5_NB)
    for d in (0, 1):
        for li, (re, im) in enumerate(lanes):
            for g, (hr, hi) in enumerate(carry[d][li]):
                h_ref[d, g * SUB:(g + 1) * SUB, re] = hr
                h_ref[d, g * SUB:(g + 1) * SUB, im] = hi


def _s5_call(u, bblk, a_re, a_im, cblk, ctx_len):
    b, lt, _ = u.shape
    tc = S5_C
    n_all, n_ctx = lt // tc, ctx_len // tc
    tok = lambda d: pl.BlockSpec((S5_NB, tc, S5_W), lambda g, c: (g, _dir_chunk(d, c, n_ctx, n_all), 0))
    out = jax.ShapeDtypeStruct((b, lt, S5_W), F32)
    return pl.pallas_call(
        functools.partial(_s5_kernel, tc=tc),
        grid=(b // S5_NB, n_all),
        in_specs=[tok(0), tok(1), _const_spec((2, S5_W, 2 * S5_NS)), _const_spec((2, SUB, S5_NS)),
                  _const_spec((2, SUB, S5_NS)), _const_spec((2 * S5_NS, S5_W))],
        out_specs=[tok(0), tok(1)],
        out_shape=[out, out],
        scratch_shapes=[pltpu.VMEM((2, tc * S5_NB, 2 * S5_NS), F32),
                        pltpu.VMEM((2, tc * S5_NB, 2 * S5_NS), BF16),
                        pltpu.VMEM((2, S5_NB, 2 * S5_NS), F32)],
        compiler_params=_cparams(2),
    )(u, u, bblk, a_re, a_im, cblk)


_NT = (((1,), (1,)), ((), ()))
_TN = (((0,), (0,)), ((), ()))


def _ret_kernel(*refs, backward):
    if backward:
        (p_ref, dm_ref, qd_ref, kd_ref, cd_ref, hm_ref, bm_ref, of_ref, g_ref, real_ref, gn_ref,
         y_ref, s_ref) = refs
    else:
        p_ref, dm_ref, qd_ref, kd_ref, cd_ref, hm_ref, bm_ref, o_ref, s_ref = refs

    @pl.when(pl.program_id(1) == 0)
    def _():
        s_ref[...] = jnp.zeros_like(s_ref)

    def chunk(bi):
        qb = p_ref[bi, :, 0:QK_P]
        kb = p_ref[bi, :, QK_P:2 * QK_P]
        vb = p_ref[bi, :, 2 * QK_P:2 * QK_P + V_P]
        st = s_ref[bi]
        inter = lax.dot_general(qb * qd_ref[...], st.astype(BF16) * bm_ref[...], _NT,
                                preferred_element_type=F32)
        for h in range(HEADS):
            att = lax.dot_general(qb * hm_ref[h], kb, _NT, preferred_element_type=F32)
            att = att.astype(BF16) * dm_ref[h]
            sl = slice(h * LANE, (h + 1) * LANE)
            o = jnp.dot(att, vb[:, sl], preferred_element_type=F32) + inter[:, sl]
            if backward:
                o = o + of_ref[bi, :, sl]
                mu = jnp.sum(o, axis=-1, keepdims=True) * (1.0 / DV)
                dlt = (o - mu) * real_ref[:, sl]
                var = jnp.sum(dlt * dlt, axis=-1, keepdims=True) * (1.0 / DV)
                y = dlt * lax.rsqrt(var + EPS) * gn_ref[:, sl] * _silu(g_ref[bi, :, sl].astype(F32))
                y_ref[bi, :, sl] = y.astype(BF16)
            else:
                o_ref[bi, :, sl] = o
        kdv = lax.dot_general(vb, kb * kd_ref[...], _TN, preferred_element_type=F32)
        s_ref[bi] = st * cd_ref[...] + kdv

    for bi in range(MIX_NB):
        chunk(bi)


def _mixer_specs(b, lt, ctx_len, tc, off_qkv, off_g):
    n_all, n_ctx = lt // tc, ctx_len // tc
    tok = lambda d, w, blk: pl.BlockSpec(
        (MIX_NB, tc, w), lambda i, c: (i, _dir_chunk(d, c, n_ctx, n_all), blk))
    qkv = lambda d: tok(d, QKV_P, off_qkv // QKV_P)
    gsp = lambda d: tok(d, V_P, off_g // V_P)
    out = lambda d: tok(d, V_P, 0)
    state = pltpu.VMEM((MIX_NB, V_P, QK_P), F32)
    return (b // MIX_NB, n_all), tok, qkv, gsp, out, state


def _head_mask_rows(rows):
    return jnp.asarray(np.broadcast_to(TAB["head_mask"][:HEADS, None, :], (HEADS, rows, QK_P)), dtype=BF16)


def _ret_call(p, dmask, qdec, kdec, cdec, gn, ctx_len):
    b, lt, _ = p.shape
    tc = RET_C
    grid, _, qkv, gsp, out, state = _mixer_specs(b, lt, ctx_len, tc, OFF_RQKV, OFF_RG)
    ins = lambda d: [qkv(d), _const_spec((HEADS, tc, tc)), _const_spec((tc, QK_P)),
                     _const_spec((tc, QK_P)), _const_spec((1, QK_P)),
                     _const_spec((HEADS, tc, QK_P)), _const_spec((V_P, QK_P))]
    tabs = lambda d: (dmask[d].astype(BF16), qdec[d].astype(BF16), kdec[d].astype(BF16), cdec[d],
                      _head_mask_rows(tc), jnp.asarray(TAB["bm_t"], dtype=BF16))
    of = pl.pallas_call(
        functools.partial(_ret_kernel, backward=False),
        grid=grid, in_specs=ins(0), out_specs=out(0),
        out_shape=jax.ShapeDtypeStruct((b, lt, V_P), F32),
        scratch_shapes=[state], compiler_params=_cparams(2),
    )(p, *tabs(0))
    return pl.pallas_call(
        functools.partial(_ret_kernel, backward=True),
        grid=grid,
        in_specs=ins(1) + [out(1), gsp(1), _const_spec((1, V_P)), _const_spec((1, V_P))],
        out_specs=out(1),
        out_shape=jax.ShapeDtypeStruct((b, lt, V_P), BF16),
        scratch_shapes=[state], compiler_params=_cparams(2),
    )(p, *tabs(1), of, p, jnp.asarray(TAB["v_real"]).reshape(1, V_P), gn)


def _gla_kernel(*refs, backward):
    if backward:
        (p_ref, gw_ref, gb_ref, tri_ref, hm_ref, bm_ref, of_ref, g_ref, gn_ref,
         y_ref, s_ref, oc_ref) = refs
    else:
        p_ref, gw_ref, gb_ref, tri_ref, hm_ref, bm_ref, oc_ref, s_ref = refs

    @pl.when(pl.program_id(1) == 0)
    def _():
        s_ref[...] = jnp.zeros_like(s_ref)

    tri = tri_ref[...]
    mask = tri > 0

    def intra(bi):
        la = jax.nn.log_sigmoid(
            jnp.dot(p_ref[bi, :, 0:LANE], gw_ref[...], preferred_element_type=F32)
            + gb_ref[...]) / GLA_TAU
        hi = la.astype(BF16)
        lo = (la - hi.astype(F32)).astype(BF16)
        bcum = (jnp.dot(tri, hi, preferred_element_type=F32)
                + jnp.dot(tri, lo, preferred_element_type=F32))
        q = p_ref[bi, :, 0:QK_P].astype(F32)
        k = p_ref[bi, :, QK_P:2 * QK_P].astype(F32)
        vb = p_ref[bi, :, 2 * QK_P:2 * QK_P + V_P]
        qtb = (q * jnp.exp(bcum)).astype(BF16)
        ktb = (k * jnp.exp(-bcum)).astype(BF16)
        for h in range(HEADS):
            pr = slice(h // 2 * LANE, (h // 2 + 1) * LANE)
            att = lax.dot_general(qtb[:, pr] * hm_ref[h % 2], ktb[:, pr], _NT,
                                  preferred_element_type=F32)
            att = jnp.where(mask, att.astype(BF16), jnp.zeros((), BF16))
            sl = slice(h * LANE, (h + 1) * LANE)
            oc_ref[bi, :, sl] = jnp.dot(att, vb[:, sl], preferred_element_type=F32)
        return bcum, k, qtb, vb

    parts = [intra(bi) for bi in range(MIX_NB)]

    n_sub = GLA_R // GLA_C
    n_pair = HEADS // 2
    order = range(n_sub - 1, -1, -1) if backward else range(n_sub)
    last_row = 0 if backward else GLA_C - 1
    ss = [[s_ref[bi, pi] for pi in range(n_pair)] for bi in range(MIX_NB)]
    for cc in order:
        r0 = cc * GLA_C
        rows = slice(r0, r0 + GLA_C)
        for bi, (bcum, k, qtb, vb) in enumerate(parts):
            bl = bcum[r0 + last_row:r0 + last_row + 1, :]
            kd = (k[rows] * jnp.exp(bl - bcum[rows])).astype(BF16)
            ebl = jnp.exp(bl)
            for pi in range(n_pair):
                pr = slice(pi * LANE, (pi + 1) * LANE)
                vs = slice(2 * pi * LANE, 2 * (pi + 1) * LANE)
                s = ss[bi][pi]
                oc_ref[bi, rows, vs] += lax.dot_general(qtb[rows, pr], s.astype(BF16) * bm_ref[...],
                                                        _NT, preferred_element_type=F32)
                kdv = lax.dot_general(vb[rows, vs], kd[:, pr], _TN, preferred_element_type=F32)
                ss[bi][pi] = s * ebl[:, pr] + kdv
    for bi in range(MIX_NB):
        for pi in range(n_pair):
            s_ref[bi, pi] = ss[bi][pi]

    if backward:
        for bi in range(MIX_NB):
            for h in range(HEADS):
                sl = slice(h * LANE, (h + 1) * LANE)
                o = oc_ref[bi, :, sl] + of_ref[bi, :, sl]
                ms = jnp.sum(o * o, axis=-1, keepdims=True) * (1.0 / DV)
                y = o * lax.rsqrt(ms + EPS) * gn_ref[:, sl] * _silu(g_ref[bi, :, sl].astype(F32))
                y_ref[bi, :, sl] = y.astype(BF16)


def _gla_call(p, gw, gb, tri, gn, ctx_len):
    b, lt, _ = p.shape
    tc = GLA_R
    grid, _, qkv, gsp, out, _ = _mixer_specs(b, lt, ctx_len, tc, OFF_GQKV, OFF_GG)
    ins = lambda d: [qkv(d),
                     _const_spec((LANE, QK_P)), _const_spec((1, QK_P)), _const_spec((tc, tc)),
                     _const_spec((2, tc, LANE)), _const_spec((2 * LANE, LANE))]
    hm = jnp.asarray(np.broadcast_to(TAB["gla_hm"][:, None, :], (2, tc, LANE)), dtype=BF16)
    bm = jnp.asarray(TAB["gla_bm"], dtype=BF16)
    state = pltpu.VMEM((MIX_NB, HEADS // 2, 2 * LANE, LANE), F32)
    of = pl.pallas_call(
        functools.partial(_gla_kernel, backward=False),
        grid=grid, in_specs=ins(0), out_specs=out(0),
        out_shape=jax.ShapeDtypeStruct((b, lt, V_P), F32),
        scratch_shapes=[state], compiler_params=_cparams(2),
    )(p, gw[0], gb[0], tri[0], hm, bm)
    return pl.pallas_call(
        functools.partial(_gla_kernel, backward=True),
        grid=grid,
        in_specs=ins(1) + [out(1), gsp(1), _const_spec((1, V_P))],
        out_specs=out(1),
        out_shape=jax.ShapeDtypeStruct((b, lt, V_P), BF16),
        scratch_shapes=[state, pltpu.VMEM((MIX_NB, tc, V_P), F32)],
        compiler_params=_cparams(2),
    )(p, gw[1], gb[1], tri[1], hm, bm, of, p, gn)


FFN_T = 256


def _merge_ffn_kernel(*refs, final, split):
    (mod_ref, u_ref, sf_ref, sb_ref, yr_ref, yg_ref, gate_ref,
     dsk_ref, gluw_ref, glub_ref, wbs_ref, wbr_ref, wbg_ref, wo_ref,
     gain_ref, wfi_ref, wfo_ref, fin_ref, o_ref, act_ref) = refs[1 + split:]

    def branch(y, w_ref, g0):
        gate = _sigmoid(gate_ref[:, g0:g0 + D].astype(F32))
        return gate * jnp.dot(y, w_ref[...], preferred_element_type=F32)

    ys = jax.nn.gelu(sf_ref[...] + sb_ref[...] + dsk_ref[...] * u_ref[...])
    glu = jnp.dot(ys.astype(BF16), gluw_ref[...], preferred_element_type=F32) + glub_ref[...]
    ys = (ys * _sigmoid(glu)).astype(BF16)
    m = (branch(ys, wbs_ref, 0) + branch(yr_ref[...], wbr_ref, D)
         + branch(yg_ref[...], wbg_ref, 2 * D))
    x = _residual_rows(refs, split) + mod_ref[2:3, :] * jnp.dot(m.astype(BF16), wo_ref[...],
                                                                  preferred_element_type=F32)
    h = _rms(x, gain_ref[...])
    hb = (h * (1.0 + mod_ref[4:5, :]) + mod_ref[3:4, :]).astype(BF16)
    for t in range(FFN_H // FFN_T):
        c0 = t * FFN_T
        a = jnp.dot(hb, wfi_ref[:, c0:c0 + FFN_T], preferred_element_type=F32)
        bq = jnp.dot(hb, wfi_ref[:, FFN_H + c0:FFN_H + c0 + FFN_T], preferred_element_type=F32)
        act_ref[:, t * FFN_T:(t + 1) * FFN_T] = (_silu(a) * bq).astype(BF16)
    y = x + mod_ref[5:6, :] * jnp.dot(act_ref[...], wfo_ref[...], preferred_element_type=F32)
    if final:
        y = _rms(y, fin_ref[...])
    o_ref[...] = y


def _merge_ffn_call(xs, modsel, u, s5, yret, ygla, p, dsk, gluw, glub, wbs, wbr, wbg, wo,
                    gain, wfi, wfo, fin, final, lat_only, ctx_len):
    split = isinstance(xs, tuple)
    xs = xs if split else (xs,)
    b = xs[0].shape[0]
    lt = sum(t.shape[1] for t in xs)
    j0 = ctx_len // ROW_T if lat_only else 0
    nt = lt // ROW_T - j0
    tok = lambda w: pl.BlockSpec((None, ROW_T, w), lambda i, j: (i, j + j0, 0))
    return pl.pallas_call(
        functools.partial(_merge_ffn_kernel, final=final, split=split),
        grid=(b, nt),
        in_specs=_residual_specs(split, j0) + [
                  pl.BlockSpec((None, None, SUB, D), lambda i, j: (i, jnp.minimum(j + j0, 1), 0, 0)),
                  tok(S5_W), tok(S5_W), tok(S5_W), tok(V_P), tok(V_P), tok(GATE_W),
                  _const_spec((1, S5_W)), _const_spec((S5_W, S5_W)), _const_spec((1, S5_W)),
                  _const_spec((S5_W, D)), _const_spec((V_P, D)), _const_spec((V_P, D)),
                  _const_spec((D, D)),
                  _const_spec((1, D)),
                  _const_spec((D, 2 * FFN_H)),
                  _const_spec((FFN_H, D)), _const_spec((1, D))],
        out_specs=pl.BlockSpec((None, ROW_T, D), lambda i, j: (i, j, 0)),
        out_shape=jax.ShapeDtypeStruct((b, nt * ROW_T, D), F32),
        scratch_shapes=[pltpu.VMEM((ROW_T, FFN_H), BF16)],
        compiler_params=_cparams(2),
    )(*xs, modsel, u, *s5, yret, ygla, p, dsk, gluw, glub, wbs, wbr, wbg, wo,
      gain.reshape(1, D), wfi, wfo, fin.reshape(1, D))


def _s5_params(lam_re, lam_im, log_dt, b_re, b_im, c_re, c_im):
    dt = jnp.exp(log_dt)[..., None]
    mag = jnp.exp(lam_re * dt)
    a_re, a_im = mag * jnp.cos(lam_im * dt), mag * jnp.sin(lam_im * dt)
    den = lam_re * lam_re + lam_im * lam_im
    f_re = ((a_re - 1.0) * lam_re + a_im * lam_im) / den
    f_im = (a_im * lam_re - (a_re - 1.0) * lam_im) / den
    bb_re = f_re[..., None] * b_re - f_im[..., None] * b_im
    bb_im = f_re[..., None] * b_im + f_im[..., None] * b_re
    eye = jnp.eye(S5_GROUPS, dtype=F32)

    def blk_in(bb):
        t = jnp.einsum("dgpc,gh->dgchp", bb, eye)
        return t.reshape(2, S5_W, S5_NS)

    bblk = jnp.concatenate([blk_in(bb_re), blk_in(bb_im)], axis=-1).astype(BF16)

    def blk_out(cc):
        t = jnp.einsum("gcp,gh->gphc", cc, eye)
        return t.reshape(S5_NS, S5_W)

    cblk = jnp.concatenate([blk_out(c_re), -blk_out(c_im)], axis=0).astype(BF16)
    bc = lambda a: jnp.broadcast_to(a.reshape(2, 1, S5_NS), (2, SUB, S5_NS))
    return bblk, bc(a_re), bc(a_im), cblk


def _ret_tables(log_decay):
    tc = RET_C
    pos = jnp.arange(tc, dtype=F32)
    w = jnp.stack([pos, tc - 1.0 - pos])
    rel = w[:, :, None] - w[:, None, :]
    lg = log_decay[:, :, None, None]
    dmask = jnp.where(rel[:, None] >= 0, jnp.exp(jnp.maximum(rel[:, None], 0.0) * lg), 0.0)
    qk_head = TAB["qk_head"]
    lane_lg = jnp.where(jnp.asarray(qk_head >= 0),
                        jnp.take(log_decay, jnp.asarray(np.maximum(qk_head, 0)), axis=1), 0.0)
    qdec = jnp.exp((w[:, :, None] + 1.0) * lane_lg[:, None, :])
    kdec = jnp.exp((tc - 1.0 - w[:, :, None]) * lane_lg[:, None, :])
    cdec = jnp.exp(tc * lane_lg)[:, None, :]
    return dmask, qdec, kdec, cdec


def _rope_tables(seq, ctx_len):
    rows = seq // GRID_W
    nf = DK // 4
    inv = 1.0 / (ROPE_BASE ** (np.arange(nf, dtype=np.float32) / nf))
    r = np.repeat(np.arange(rows, dtype=np.float32), GRID_W)
    col = np.tile(np.arange(GRID_W, dtype=np.float32), rows)
    ang = np.concatenate([r[:, None] * inv, col[:, None] * inv], axis=-1)
    cos_t = np.ones((ctx_len + seq, LANE), np.float32)
    sin_t = np.zeros((ctx_len + seq, LANE), np.float32)
    npair = DK // 2
    for h in range(HEADS):
        cos_t[ctx_len:, h * npair:(h + 1) * npair] = np.cos(ang)
        sin_t[ctx_len:, h * npair:(h + 1) * npair] = np.sin(ang)
    return jnp.asarray(cos_t), jnp.asarray(sin_t)


def _gla_tables(gate_w, gate_b):
    gw = _gather_cols(gate_w, TAB["gla_src"])
    gw_full = jnp.zeros((2, LANE, QK_P), F32)
    for d in range(2):
        gw_full = gw_full.at[d, GLA_Z_LANE[d]:GLA_Z_LANE[d] + GLA_RANK].set(gw[d] * DK ** 0.5)
    gb = _gather_cols(gate_b, TAB["gla_src"])[:, None, :]
    i = np.arange(GLA_R)
    same = (i[:, None] // GLA_C) == (i[None, :] // GLA_C)
    tri = np.stack([same & (i[:, None] >= i[None, :]), same & (i[:, None] <= i[None, :])]).astype(np.float32)
    return gw_full.astype(BF16), gb, jnp.asarray(tri, dtype=BF16)


def _pad_rows(w, src):
    return _gather_cols(w.T, src).T


def kernel(x, c, ctx, c_ctx, w_mod, b_mod, norm_mix, norm_ffn, w_in, s5_lam_re, s5_lam_im, s5_log_dt, s5_b_re, s5_b_im, s5_c_re, s5_c_im, s5_d, s5_glu_w, s5_glu_b, ret_log_decay, ret_gn, gla_gate_w, gla_gate_b, gla_norm, w_br_s5, w_br_ret, w_br_gla, w_out, w_ffn_in, w_ffn_out, norm_final):
    b, seq, _ = x.shape
    ctx_len = ctx.shape[1]
    depth = w_mod.shape[0]
    assert b % S5_NB == 0 and b % MIX_NB == 0 and seq % RET_C == 0 and seq % GRID_W == 0
    assert ctx_len == ROW_T == RET_C

    xs = (ctx, x)
    rows = ((b + 1 + SUB - 1) // SUB) * SUB
    c_all = jnp.zeros((rows, D), F32).at[:b].set(c).at[b].set(c_ctx)
    cos_t, sin_t = _rope_tables(seq, ctx_len)

    for l in range(depth):
        last = l == depth - 1
        mods = _mod_call(c_all, w_mod[l], b_mod[l])
        m_lat = mods[:b].reshape(b, N_MOD, D)
        m_ctx = jnp.broadcast_to(mods[b].reshape(1, N_MOD, D), (b, N_MOD, D))
        modsel = jnp.stack([m_ctx, m_lat], axis=1)
        modsel = jnp.pad(modsel, ((0, 0), (0, 0), (0, SUB - N_MOD), (0, 0)))

        w_all = _gather_cols(w_in[l], TAB["src"]).astype(BF16)
        p, u = _inproj_call(xs, modsel, norm_mix[l], cos_t, sin_t, w_all)

        bblk, a_re, a_im, cblk = _s5_params(s5_lam_re[l], s5_lam_im[l], s5_log_dt[l],
                                            s5_b_re[l], s5_b_im[l], s5_c_re[l], s5_c_im[l])
        s5 = _s5_call(u, bblk, a_re, a_im, cblk, ctx_len)

        dmask, qdec, kdec, cdec = _ret_tables(ret_log_decay[l])
        yret = _ret_call(p, dmask, qdec, kdec, cdec,
                         _gather_cols(ret_gn[l], TAB["vsrc"]).reshape(1, V_P), ctx_len)

        gw, gb, tri = _gla_tables(gla_gate_w[l], gla_gate_b[l])
        ygla = _gla_call(p, gw, gb, tri, _gather_cols(gla_norm[l], TAB["vsrc"]).reshape(1, V_P),
                         ctx_len)

        xs = _merge_ffn_call(xs, modsel, u, s5, yret, ygla, p,
                             s5_d[l].reshape(1, S5_W), s5_glu_w[l].astype(BF16),
                             s5_glu_b[l].reshape(1, S5_W), w_br_s5[l].astype(BF16),
                             _pad_rows(w_br_ret[l], TAB["vsrc"]).astype(BF16),
                             _pad_rows(w_br_gla[l], TAB["vsrc"]).astype(BF16),
                             w_out[l].astype(BF16), norm_ffn[l], w_ffn_in[l].astype(BF16),
                             w_ffn_out[l].astype(BF16), norm_final, last, last, ctx_len)
    return xs
```

```python
import functools

import numpy as np
import jax
import jax.numpy as jnp
from jax import lax
from jax.experimental import pallas as pl
from jax.experimental.pallas import tpu as pltpu

F32 = jnp.float32
BF16 = jnp.bfloat16

D = 1024
EPS = 1e-6
N_MOD = 6
GRID_W = 64
S5_W = 256
S5_GROUPS = 16
S5_STATE = 64
S5_NS = S5_GROUPS * S5_STATE
HEADS = 4
DK = 48
DV = 96
QK_W = HEADS * DK
V_W = HEADS * DV
GLA_RANK = 16
GLA_TAU = 16.0
ROPE_BASE = 10000.0
FFN_H = 2816

LANE = 128
SUB = 8
QK_P = 2 * LANE
V_P = HEADS * LANE
QKV_P = 2 * QK_P + V_P
GATE_W = 3 * D

OFF_GATE = 0
OFF_RQKV = GATE_W
OFF_GQKV = OFF_RQKV + QKV_P
OFF_RG = OFF_GQKV + QKV_P
OFF_GG = OFF_RG + V_P
NP = OFF_GG + V_P
OFF_S5 = NP
NW = NP + S5_W

ROW_T = 256
RET_C = 256
GLA_C = 64
GLA_R = 256
MIX_NB = 4
S5_C = 64
V7X_VMEM_BYTES = 64 * 1024 * 1024
VMEM_LIMIT = V7X_VMEM_BYTES * 7 // 8

IN_SIZES = (S5_W, QK_W, QK_W, V_W, V_W, QK_W, QK_W, V_W, V_W, GLA_RANK, GLA_RANK, D, D, D)
IN_OFFS = np.concatenate([[0], np.cumsum(IN_SIZES)]).astype(np.int64)


def _head_lane():
    m = np.zeros((HEADS, DK), np.int64)
    for h in range(HEADS):
        for i in range(DK):
            m[h, i] = (i % 2) * LANE + h * (DK // 2) + i // 2
    return m


HEAD_LANE = _head_lane()
HALF = LANE // 2
GLA_LANE = np.array([[(h // 2) * LANE + (h % 2) * HALF + i for i in range(DK)] for h in range(HEADS)])
GLA_Z_LANE = (DK, HALF + DK)
assert HALF - DK == GLA_RANK


def _static_tables():
    src = np.full((NW,), -1, np.int64)
    src[OFF_GATE:OFF_GATE + GATE_W] = IN_OFFS[11] + np.arange(GATE_W)
    for qkv, gg, lane, iq, ik, iv, ig in ((OFF_RQKV, OFF_RG, HEAD_LANE, 1, 2, 3, 4),
                                          (OFF_GQKV, OFF_GG, GLA_LANE, 5, 6, 7, 8)):
        for h in range(HEADS):
            for i in range(DK):
                src[qkv + lane[h, i]] = IN_OFFS[iq] + h * DK + i
                src[qkv + QK_P + lane[h, i]] = IN_OFFS[ik] + h * DK + i
            for j in range(DV):
                src[qkv + 2 * QK_P + h * LANE + j] = IN_OFFS[iv] + h * DV + j
                src[gg + h * LANE + j] = IN_OFFS[ig] + h * DV + j
    for d in range(2):
        src[OFF_GQKV + GLA_Z_LANE[d]:OFF_GQKV + GLA_Z_LANE[d] + GLA_RANK] = (
            IN_OFFS[9 + d] + np.arange(GLA_RANK))
    src[OFF_S5:OFF_S5 + S5_W] = np.arange(S5_W)
    vsrc = np.full((V_P,), -1, np.int64)
    for h in range(HEADS):
        vsrc[h * LANE:h * LANE + DV] = h * DV + np.arange(DV)
    qk_head = np.full((QK_P,), -1, np.int64)
    qk_src = np.full((QK_P,), -1, np.int64)
    for h in range(HEADS):
        for i in range(DK):
            qk_head[HEAD_LANE[h, i]] = h
            qk_src[HEAD_LANE[h, i]] = h * DK + i
    v_head = np.repeat(np.arange(HEADS), LANE)
    v_real = (np.arange(V_P) % LANE) < DV
    head_mask = np.zeros((SUB, QK_P), np.float32)
    for h in range(HEADS):
        head_mask[h] = (qk_head == h)
    bm_t = (v_head[:, None] == qk_head[None, :]).astype(np.float32)
    gla_src = np.full((QK_P,), -1, np.int64)
    for h in range(HEADS):
        gla_src[GLA_LANE[h]] = h * DK + np.arange(DK)
    lane_in_pair = np.arange(LANE)
    gla_hm = np.stack([(lane_in_pair // HALF == r) & (lane_in_pair % HALF < DK) for r in range(2)])
    gla_bm = (np.arange(2 * LANE)[:, None] // LANE) == (lane_in_pair[None, :] // HALF)
    return dict(src=src, vsrc=vsrc, qk_head=qk_head, qk_src=qk_src, head_mask=head_mask,
                bm_t=bm_t, v_real=v_real.astype(np.float32), gla_src=gla_src,
                gla_hm=gla_hm.astype(np.float32), gla_bm=gla_bm.astype(np.float32))


TAB = _static_tables()


def _gather_cols(w, src):
    valid = jnp.asarray(src >= 0)
    out = jnp.take(w, jnp.asarray(np.maximum(src, 0)), axis=-1)
    return jnp.where(valid, out, jnp.zeros((), w.dtype))


def _cparams(n_axes):
    return pltpu.CompilerParams(dimension_semantics=("arbitrary",) * n_axes,
                                vmem_limit_bytes=VMEM_LIMIT)


def _const_spec(shape):
    nd = len(shape)
    return pl.BlockSpec(shape, lambda *_: (0,) * nd, pipeline_mode=pl.Buffered(1))


def _dir_chunk(d, c, n_ctx, n_all):
    bwd = jnp.where(c < n_ctx, n_ctx - 1 - c, n_all + n_ctx - 1 - c)
    return jnp.where(d == 0, c, bwd)


def _sigmoid(x):
    return 0.5 * jnp.tanh(0.5 * x) + 0.5


def _silu(x):
    return x * _sigmoid(x)


def _mod_kernel(c_ref, w_ref, b_ref, o_ref):
    o_ref[...] = jnp.dot(_silu(c_ref[...]), w_ref[...], preferred_element_type=F32,
                         precision=lax.Precision.HIGHEST) + b_ref[...]


def _mod_call(c_all, w_mod_l, b_mod_l):
    rows = c_all.shape[0]
    tn = D
    return pl.pallas_call(
        _mod_kernel,
        grid=(N_MOD * D // tn,),
        in_specs=[pl.BlockSpec((rows, D), lambda j: (0, 0)),
                  pl.BlockSpec((D, tn), lambda j: (0, j)),
                  pl.BlockSpec((1, tn), lambda j: (0, j))],
        out_specs=pl.BlockSpec((rows, tn), lambda j: (0, j)),
        out_shape=jax.ShapeDtypeStruct((rows, N_MOD * D), F32),
        compiler_params=_cparams(1),
    )(c_all, w_mod_l, b_mod_l.reshape(1, -1))


def _rms(x, gain):
    return x * lax.rsqrt(jnp.mean(x * x, axis=-1, keepdims=True) + EPS) * gain


def _residual_rows(refs, split):
    if not split:
        return refs[0][...]
    return jnp.where(pl.program_id(1) == 0, refs[0][...], refs[1][...])


def _residual_specs(split, j0):
    if not split:
        return [pl.BlockSpec((None, ROW_T, D), lambda i, j: (i, j + j0, 0))]
    return [pl.BlockSpec((None, ROW_T, D), lambda i, j: (i, 0, 0)),
            pl.BlockSpec((None, ROW_T, D), lambda i, j: (i, jnp.maximum(j - 1, 0), 0))]


def _inproj_kernel(*refs, split):
    mod_ref, gain_ref, cos_ref, sin_ref, w_ref, p_ref, u_ref = refs[1 + split:]
    h = _rms(_residual_rows(refs, split), gain_ref[...])
    h = h * (1.0 + mod_ref[1:2, :]) + mod_ref[0:1, :]
    hb = h.astype(BF16)

    def proj(c0, c1):
        return jnp.dot(hb, w_ref[:, c0:c1], preferred_element_type=F32)

    for c0 in range(0, GATE_W, D):
        p_ref[:, c0:c0 + D] = proj(c0, c0 + D).astype(BF16)
    cs = cos_ref[...]
    sn = sin_ref[...]
    qk = proj(OFF_RQKV, OFF_RQKV + 2 * QK_P)
    for o, scale in ((0, 1.0), (QK_P, DK ** -0.5)):
        a = qk[:, o:o + LANE] * scale
        b = qk[:, o + LANE:o + 2 * LANE] * scale
        p_ref[:, OFF_RQKV + o:OFF_RQKV + o + LANE] = (a * cs - b * sn).astype(BF16)
        p_ref[:, OFF_RQKV + o + LANE:OFF_RQKV + o + 2 * LANE] = (a * sn + b * cs).astype(BF16)
    c0 = OFF_RQKV + 2 * QK_P
    p_ref[:, c0:OFF_GQKV] = proj(c0, OFF_GQKV).astype(BF16)
    p_ref[:, OFF_GQKV:OFF_GQKV + QK_P] = (proj(OFF_GQKV, OFF_GQKV + QK_P) * (DK ** -0.5)).astype(BF16)
    c0 = OFF_GQKV + QK_P
    p_ref[:, c0:OFF_RG] = proj(c0, OFF_RG).astype(BF16)
    p_ref[:, OFF_RG:NP] = proj(OFF_RG, NP).astype(BF16)
    u_ref[...] = proj(OFF_S5, OFF_S5 + S5_W)


def _inproj_call(xs, modsel, gain, cos_t, sin_t, w_all):
    split = isinstance(xs, tuple)
    xs = xs if split else (xs,)
    b = xs[0].shape[0]
    lt = sum(t.shape[1] for t in xs)
    nt = lt // ROW_T
    return pl.pallas_call(
        functools.partial(_inproj_kernel, split=split),
        grid=(b, nt),
        in_specs=_residual_specs(split, 0) + [
                  pl.BlockSpec((None, None, SUB, D), lambda i, j: (i, jnp.minimum(j, 1), 0, 0)),
                  _const_spec((1, D)),
                  pl.BlockSpec((ROW_T, LANE), lambda i, j: (j, 0)),
                  pl.BlockSpec((ROW_T, LANE), lambda i, j: (j, 0)),
                  _const_spec((D, NW))],
        out_specs=[pl.BlockSpec((None, ROW_T, NP), lambda i, j: (i, j, 0)),
                   pl.BlockSpec((None, ROW_T, S5_W), lambda i, j: (i, j, 0))],
        out_shape=[jax.ShapeDtypeStruct((b, lt, NP), BF16),
                   jax.ShapeDtypeStruct((b, lt, S5_W), F32)],
        compiler_params=_cparams(2),
    )(*xs, modsel, gain.reshape(1, D), cos_t, sin_t, w_all)


S5_NB = 2 * SUB
S5_RB = 256


def _s5_kernel(uf_ref, ub_ref, bb_ref, are_ref, aim_ref, cb_ref, yf_ref, yb_ref,
               x_ref, hb_ref, h_ref, *, tc):
    rows = tc * S5_NB
    steps = S5_RB // S5_NB
    n_blk = rows // S5_RB
    orders = (list(range(n_blk)), list(range(n_blk - 1, -1, -1)))
    u_refs, y_refs = (uf_ref, ub_ref), (yf_ref, yb_ref)

    @pl.when(pl.program_id(1) == 0)
    def _():
        h_ref[...] = jnp.zeros_like(h_ref)

    ubs = [pltpu.einshape("bts->(tb)s", u_refs[d][...]).astype(BF16) for d in (0, 1)]

    def x_block(d, k):
        rs = slice(k * S5_RB, (k + 1) * S5_RB)
        x_ref[d, rs, :] = jnp.dot(ubs[d][rs], bb_ref[d], preferred_element_type=F32)

    half = S5_NS // 2
    lanes = [(slice(lo, lo + half), slice(S5_NS + lo, S5_NS + lo + half)) for lo in (0, half)]
    coef = [[(are_ref[d, :, re], aim_ref[d, :, re]) for re, _ in lanes] for d in (0, 1)]
    carry = [[[(h_ref[d, g * SUB:(g + 1) * SUB, re], h_ref[d, g * SUB:(g + 1) * SUB, im])
               for g in range(S5_NB // SUB)] for re, im in lanes] for d in (0, 1)]

    def scan_block(d, k):
        ts = range(k * steps, (k + 1) * steps)
        for t in (reversed(ts) if d else ts):
            r0 = t * S5_NB
            for li, (re, im) in enumerate(lanes):
                ar, ai = coef[d][li]
                new = []
                for g, (hr, hi) in enumerate(carry[d][li]):
                    rw = slice(r0 + g * SUB, r0 + (g + 1) * SUB)
                    new.append((ar * hr - ai * hi + x_ref[d, rw, re],
                                ar * hi + ai * hr + x_ref[d, rw, im]))
                carry[d][li] = new
                hb_ref[d, r0:r0 + S5_NB, re] = jnp.concatenate(
                    [n[0] for n in new], axis=0).astype(BF16)
                hb_ref[d, r0:r0 + S5_NB, im] = jnp.concatenate(
                    [n[1] for n in new], axis=0).astype(BF16)

    def y_block(d, k):
        rs = slice(k * S5_RB, (k + 1) * S5_RB)
        y = jnp.dot(hb_ref[d, rs, :], cb_ref[...], preferred_element_type=F32)
        y_refs[d][:, k * steps:(k + 1) * steps, :] = pltpu.einshape("(tb)s->bts", y, b=S5_NB)

    for i in range(-1, n_blk + 1):
        for d in (0, 1):
            if i + 1 < n_blk:
                x_block(d, orders[d][i + 1])
            if 0 <= i - 1:
                y_block(d, orders[d][i - 1])
            if 0 <= i < n_blk:
                scan_block(d, orders[d][i])
    for d in (0, 1):
        for li, (re, im) in enumerate(lanes):
            for g, (hr, hi) in enumerate(carry[d][li]):
                h_ref[d, g * SUB:(g + 1) * SUB, re] = hr
                h_ref[d, g * SUB:(g + 1) * SUB, im] = hi


def _s5_call(u, bblk, a_re, a_im, cblk, ctx_len):
    b, lt, _ = u.shape
    tc = S5_C
    n_all, n_ctx = lt // tc, ctx_len // tc
    tok = lambda d: pl.BlockSpec((S5_NB, tc, S5_W), lambda g, c: (g, _dir_chunk(d, c, n_ctx, n_all), 0))
    out = jax.ShapeDtypeStruct((b, lt, S5_W), F32)
    return pl.pallas_call(
        functools.partial(_s5_kernel, tc=tc),
        grid=(b // S5_NB, n_all),
        in_specs=[tok(0), tok(1), _const_spec((2, S5_W, 2 * S5_NS)), _const_spec((2, SUB, S5_NS)),
                  _const_spec((2, SUB, S5_NS)), _const_spec((2 * S5_NS, S5_W))],
        out_specs=[tok(0), tok(1)],
        out_shape=[out, out],
        scratch_shapes=[pltpu.VMEM((2, tc * S5_NB, 2 * S5_NS), F32),
                        pltpu.VMEM((2, tc * S5_NB, 2 * S5_NS), BF16),
                        pltpu.VMEM((2, S5_NB, 2 * S5_NS), F32)],
        compiler_params=_cparams(2),
    )(u, u, bblk, a_re, a_im, cblk)


_NT = (((1,), (1,)), ((), ()))
_TN = (((0,), (0,)), ((), ()))


def _ret_kernel(*refs, backward):
    if backward:
        (p_ref, dm_ref, qd_ref, kd_ref, cd_ref, hm_ref, bm_ref, of_ref, g_ref, real_ref, gn_ref,
         y_ref, s_ref) = refs
    else:
        p_ref, dm_ref, qd_ref, kd_ref, cd_ref, hm_ref, bm_ref, o_ref, s_ref = refs

    @pl.when(pl.program_id(1) == 0)
    def _():
        s_ref[...] = jnp.zeros_like(s_ref)

    def chunk(bi):
        qb = p_ref[bi, :, 0:QK_P]
        kb = p_ref[bi, :, QK_P:2 * QK_P]
        vb = p_ref[bi, :, 2 * QK_P:2 * QK_P + V_P]
        st = s_ref[bi]
        inter = lax.dot_general(qb * qd_ref[...], st.astype(BF16) * bm_ref[...], _NT,
                                preferred_element_type=F32)
        for h in range(HEADS):
            att = lax.dot_general(qb * hm_ref[h], kb, _NT, preferred_element_type=F32)
            att = att.astype(BF16) * dm_ref[h]
            sl = slice(h * LANE, (h + 1) * LANE)
            o = jnp.dot(att, vb[:, sl], preferred_element_type=F32) + inter[:, sl]
            if backward:
                o = o + of_ref[bi, :, sl]
                mu = jnp.sum(o, axis=-1, keepdims=True) * (1.0 / DV)
                dlt = (o - mu) * real_ref[:, sl]
                var = jnp.sum(dlt * dlt, axis=-1, keepdims=True) * (1.0 / DV)
                y = dlt * lax.rsqrt(var + EPS) * gn_ref[:, sl] * _silu(g_ref[bi, :, sl].astype(F32))
                y_ref[bi, :, sl] = y.astype(BF16)
            else:
                o_ref[bi, :, sl] = o
        kdv = lax.dot_general(vb, kb * kd_ref[...], _TN, preferred_element_type=F32)
        s_ref[bi] = st * cd_ref[...] + kdv

    for bi in range(MIX_NB):
        chunk(bi)


def _mixer_specs(b, lt, ctx_len, tc, off_qkv, off_g):
    n_all, n_ctx = lt // tc, ctx_len // tc
    tok = lambda d, w, blk: pl.BlockSpec(
        (MIX_NB, tc, w), lambda i, c: (i, _dir_chunk(d, c, n_ctx, n_all), blk))
    qkv = lambda d: tok(d, QKV_P, off_qkv // QKV_P)
    gsp = lambda d: tok(d, V_P, off_g // V_P)
    out = lambda d: tok(d, V_P, 0)
    state = pltpu.VMEM((MIX_NB, V_P, QK_P), F32)
    return (b // MIX_NB, n_all), tok, qkv, gsp, out, state


def _head_mask_rows(rows):
    return jnp.asarray(np.broadcast_to(TAB["head_mask"][:HEADS, None, :], (HEADS, rows, QK_P)), dtype=BF16)


def _ret_call(p, dmask, qdec, kdec, cdec, gn, ctx_len):
    b, lt, _ = p.shape
    tc = RET_C
    grid, _, qkv, gsp, out, state = _mixer_specs(b, lt, ctx_len, tc, OFF_RQKV, OFF_RG)
    ins = lambda d: [qkv(d), _const_spec((HEADS, tc, tc)), _const_spec((tc, QK_P)),
                     _const_spec((tc, QK_P)), _const_spec((1, QK_P)),
                     _const_spec((HEADS, tc, QK_P)), _const_spec((V_P, QK_P))]
    tabs = lambda d: (dmask[d].astype(BF16), qdec[d].astype(BF16), kdec[d].astype(BF16), cdec[d],
                      _head_mask_rows(tc), jnp.asarray(TAB["bm_t"], dtype=BF16))
    of = pl.pallas_call(
        functools.partial(_ret_kernel, backward=False),
        grid=grid, in_specs=ins(0), out_specs=out(0),
        out_shape=jax.ShapeDtypeStruct((b, lt, V_P), F32),
        scratch_shapes=[state], compiler_params=_cparams(2),
    )(p, *tabs(0))
    return pl.pallas_call(
        functools.partial(_ret_kernel, backward=True),
        grid=grid,
        in_specs=ins(1) + [out(1), gsp(1), _const_spec((1, V_P)), _const_spec((1, V_P))],
        out_specs=out(1),
        out_shape=jax.ShapeDtypeStruct((b, lt, V_P), BF16),
        scratch_shapes=[state], compiler_params=_cparams(2),
    )(p, *tabs(1), of, p, jnp.asarray(TAB["v_real"]).reshape(1, V_P), gn)


def _gla_kernel(*refs, backward):
    if backward:
        (p_ref, gw_ref, gb_ref, tri_ref, hm_ref, bm_ref, of_ref, g_ref, gn_ref,
         y_ref, s_ref, oc_ref) = refs
    else:
        p_ref, gw_ref, gb_ref, tri_ref, hm_ref, bm_ref, oc_ref, s_ref = refs

    @pl.when(pl.program_id(1) == 0)
    def _():
        s_ref[...] = jnp.zeros_like(s_ref)

    tri = tri_ref[...]
    mask = tri > 0

    def gates(bi):
        la = jax.nn.log_sigmoid(
            jnp.dot(p_ref[bi, :, 0:LANE], gw_ref[...], preferred_element_type=F32)
            + gb_ref[...]) / GLA_TAU
        hi = la.astype(BF16)
        lo = (la - hi.astype(F32)).astype(BF16)
        bcum = (jnp.dot(tri, hi, preferred_element_type=F32)
                + jnp.dot(tri, lo, preferred_element_type=F32))
        q = p_ref[bi, :, 0:QK_P].astype(F32)
        k = p_ref[bi, :, QK_P:2 * QK_P].astype(F32)
        vb = p_ref[bi, :, 2 * QK_P:2 * QK_P + V_P]
        qtb = (q * jnp.exp(bcum)).astype(BF16)
        ktb = (k * jnp.exp(-bcum)).astype(BF16)
        return bcum, k, qtb, ktb, vb

    def attention(bi, part):
        _, _, qtb, ktb, vb = part
        for h in range(HEADS):
            pr = slice(h // 2 * LANE, (h // 2 + 1) * LANE)
            att = lax.dot_general(qtb[:, pr] * hm_ref[h % 2], ktb[:, pr], _NT,
                                  preferred_element_type=F32)
            att = jnp.where(mask, att.astype(BF16), jnp.zeros((), BF16))
            sl = slice(h * LANE, (h + 1) * LANE)
            oc_ref[bi, :, sl] = jnp.dot(att, vb[:, sl], preferred_element_type=F32)

    n_sub = GLA_R // GLA_C
    n_pair = HEADS // 2
    order = range(n_sub - 1, -1, -1) if backward else range(n_sub)
    last_row = 0 if backward else GLA_C - 1

    def chain(bi, part):
        bcum, k, qtb, _, vb = part
        ss = [s_ref[bi, pi] for pi in range(n_pair)]
        for cc in order:
            r0 = cc * GLA_C
            rows = slice(r0, r0 + GLA_C)
            bl = bcum[r0 + last_row:r0 + last_row + 1, :]
            kd = (k[rows] * jnp.exp(bl - bcum[rows])).astype(BF16)
            ebl = jnp.exp(bl)
            for pi in range(n_pair):
                pr = slice(pi * LANE, (pi + 1) * LANE)
                vs = slice(2 * pi * LANE, 2 * (pi + 1) * LANE)
                oc_ref[bi, rows, vs] += lax.dot_general(
                    qtb[rows, pr], ss[pi].astype(BF16) * bm_ref[...], _NT,
                    preferred_element_type=F32)
                kdv = lax.dot_general(vb[rows, vs], kd[:, pr], _TN, preferred_element_type=F32)
                ss[pi] = ss[pi] * ebl[:, pr] + kdv
        for pi in range(n_pair):
            s_ref[bi, pi] = ss[pi]

    def readout(bi):
        for h in range(HEADS):
            sl = slice(h * LANE, (h + 1) * LANE)
            o = oc_ref[bi, :, sl] + of_ref[bi, :, sl]
            ms = jnp.sum(o * o, axis=-1, keepdims=True) * (1.0 / DV)
            y = o * lax.rsqrt(ms + EPS) * gn_ref[:, sl] * _silu(g_ref[bi, :, sl].astype(F32))
            y_ref[bi, :, sl] = y.astype(BF16)

    parts = {}
    for step in range(MIX_NB + 3):
        if step < MIX_NB:
            parts[step] = gates(step)
        if 0 <= step - 1 < MIX_NB:
            attention(step - 1, parts[step - 1])
        if 0 <= step - 2 < MIX_NB:
            chain(step - 2, parts[step - 2])
        if backward and 0 <= step - 3 < MIX_NB:
            readout(step - 3)


def _gla_call(p, gw, gb, tri, gn, ctx_len):
    b, lt, _ = p.shape
    tc = GLA_R
    grid, _, qkv, gsp, out, _ = _mixer_specs(b, lt, ctx_len, tc, OFF_GQKV, OFF_GG)
    ins = lambda d: [qkv(d),
                     _const_spec((LANE, QK_P)), _const_spec((1, QK_P)), _const_spec((tc, tc)),
                     _const_spec((2, tc, LANE)), _const_spec((2 * LANE, LANE))]
    hm = jnp.asarray(np.broadcast_to(TAB["gla_hm"][:, None, :], (2, tc, LANE)), dtype=BF16)
    bm = jnp.asarray(TAB["gla_bm"], dtype=BF16)
    state = pltpu.VMEM((MIX_NB, HEADS // 2, 2 * LANE, LANE), F32)
    of = pl.pallas_call(
        functools.partial(_gla_kernel, backward=False),
        grid=grid, in_specs=ins(0), out_specs=out(0),
        out_shape=jax.ShapeDtypeStruct((b, lt, V_P), F32),
        scratch_shapes=[state], compiler_params=_cparams(2),
    )(p, gw[0], gb[0], tri[0], hm, bm)
    return pl.pallas_call(
        functools.partial(_gla_kernel, backward=True),
        grid=grid,
        in_specs=ins(1) + [out(1), gsp(1), _const_spec((1, V_P))],
        out_specs=out(1),
        out_shape=jax.ShapeDtypeStruct((b, lt, V_P), BF16),
        scratch_shapes=[state, pltpu.VMEM((MIX_NB, tc, V_P), F32)],
        compiler_params=_cparams(2),
    )(p, gw[1], gb[1], tri[1], hm, bm, of, p, gn)


FFN_T = 256


def _merge_ffn_kernel(*refs, final, split):
    (mod_ref, u_ref, sf_ref, sb_ref, yr_ref, yg_ref, gate_ref,
     dsk_ref, gluw_ref, glub_ref, wbs_ref, wbr_ref, wbg_ref, wo_ref,
     gain_ref, wfi_ref, wfo_ref, fin_ref, o_ref, act_ref) = refs[1 + split:]

    def branch(y, w_ref, g0):
        gate = _sigmoid(gate_ref[:, g0:g0 + D].astype(F32))
        return gate * jnp.dot(y, w_ref[...], preferred_element_type=F32)

    ys = jax.nn.gelu(sf_ref[...] + sb_ref[...] + dsk_ref[...] * u_ref[...])
    glu = jnp.dot(ys.astype(BF16), gluw_ref[...], preferred_element_type=F32) + glub_ref[...]
    ys = (ys * _sigmoid(glu)).astype(BF16)
    m = (branch(ys, wbs_ref, 0) + branch(yr_ref[...], wbr_ref, D)
         + branch(yg_ref[...], wbg_ref, 2 * D))
    x = _residual_rows(refs, split) + mod_ref[2:3, :] * jnp.dot(m.astype(BF16), wo_ref[...],
                                                                  preferred_element_type=F32)
    h = _rms(x, gain_ref[...])
    hb = (h * (1.0 + mod_ref[4:5, :]) + mod_ref[3:4, :]).astype(BF16)
    for t in range(FFN_H // FFN_T):
        c0 = t * FFN_T
        a = jnp.dot(hb, wfi_ref[:, c0:c0 + FFN_T], preferred_element_type=F32)
        bq = jnp.dot(hb, wfi_ref[:, FFN_H + c0:FFN_H + c0 + FFN_T], preferred_element_type=F32)
        act_ref[:, t * FFN_T:(t + 1) * FFN_T] = (_silu(a) * bq).astype(BF16)
    y = x + mod_ref[5:6, :] * jnp.dot(act_ref[...], wfo_ref[...], preferred_element_type=F32)
    if final:
        y = _rms(y, fin_ref[...])
    o_ref[...] = y


def _merge_ffn_call(xs, modsel, u, s5, yret, ygla, p, dsk, gluw, glub, wbs, wbr, wbg, wo,
                    gain, wfi, wfo, fin, final, lat_only, ctx_len):
    split = isinstance(xs, tuple)
    xs = xs if split else (xs,)
    b = xs[0].shape[0]
    lt = sum(t.shape[1] for t in xs)
    j0 = ctx_len // ROW_T if lat_only else 0
    nt = lt // ROW_T - j0
    tok = lambda w: pl.BlockSpec((None, ROW_T, w), lambda i, j: (i, j + j0, 0))
    return pl.pallas_call(
        functools.partial(_merge_ffn_kernel, final=final, split=split),
        grid=(b, nt),
        in_specs=_residual_specs(split, j0) + [
                  pl.BlockSpec((None, None, SUB, D), lambda i, j: (i, jnp.minimum(j + j0, 1), 0, 0)),
                  tok(S5_W), tok(S5_W), tok(S5_W), tok(V_P), tok(V_P), tok(GATE_W),
                  _const_spec((1, S5_W)), _const_spec((S5_W, S5_W)), _const_spec((1, S5_W)),
                  _const_spec((S5_W, D)), _const_spec((V_P, D)), _const_spec((V_P, D)),
                  _const_spec((D, D)),
                  _const_spec((1, D)),
                  _const_spec((D, 2 * FFN_H)),
                  _const_spec((FFN_H, D)), _const_spec((1, D))],
        out_specs=pl.BlockSpec((None, ROW_T, D), lambda i, j: (i, j, 0)),
        out_shape=jax.ShapeDtypeStruct((b, nt * ROW_T, D), F32),
        scratch_shapes=[pltpu.VMEM((ROW_T, FFN_H), BF16)],
        compiler_params=_cparams(2),
    )(*xs, modsel, u, *s5, yret, ygla, p, dsk, gluw, glub, wbs, wbr, wbg, wo,
      gain.reshape(1, D), wfi, wfo, fin.reshape(1, D))


def _s5_params(lam_re, lam_im, log_dt, b_re, b_im, c_re, c_im):
    dt = jnp.exp(log_dt)[..., None]
    mag = jnp.exp(lam_re * dt)
    a_re, a_im = mag * jnp.cos(lam_im * dt), mag * jnp.sin(lam_im * dt)
    den = lam_re * lam_re + lam_im * lam_im
    f_re = ((a_re - 1.0) * lam_re + a_im * lam_im) / den
    f_im = (a_im * lam_re - (a_re - 1.0) * lam_im) / den
    bb_re = f_re[..., None] * b_re - f_im[..., None] * b_im
    bb_im = f_re[..., None] * b_im + f_im[..., None] * b_re
    eye = jnp.eye(S5_GROUPS, dtype=F32)

    def blk_in(bb):
        t = jnp.einsum("dgpc,gh->dgchp", bb, eye)
        return t.reshape(2, S5_W, S5_NS)

    bblk = jnp.concatenate([blk_in(bb_re), blk_in(bb_im)], axis=-1).astype(BF16)

    def blk_out(cc):
        t = jnp.einsum("gcp,gh->gphc", cc, eye)
        return t.reshape(S5_NS, S5_W)

    cblk = jnp.concatenate([blk_out(c_re), -blk_out(c_im)], axis=0).astype(BF16)
    bc = lambda a: jnp.broadcast_to(a.reshape(2, 1, S5_NS), (2, SUB, S5_NS))
    return bblk, bc(a_re), bc(a_im), cblk


def _ret_tables(log_decay):
    tc = RET_C
    pos = jnp.arange(tc, dtype=F32)
    w = jnp.stack([pos, tc - 1.0 - pos])
    rel = w[:, :, None] - w[:, None, :]
    lg = log_decay[:, :, None, None]
    dmask = jnp.where(rel[:, None] >= 0, jnp.exp(jnp.maximum(rel[:, None], 0.0) * lg), 0.0)
    qk_head = TAB["qk_head"]
    lane_lg = jnp.where(jnp.asarray(qk_head >= 0),
                        jnp.take(log_decay, jnp.asarray(np.maximum(qk_head, 0)), axis=1), 0.0)
    qdec = jnp.exp((w[:, :, None] + 1.0) * lane_lg[:, None, :])
    kdec = jnp.exp((tc - 1.0 - w[:, :, None]) * lane_lg[:, None, :])
    cdec = jnp.exp(tc * lane_lg)[:, None, :]
    return dmask, qdec, kdec, cdec


def _rope_tables(seq, ctx_len):
    rows = seq // GRID_W
    nf = DK // 4
    inv = 1.0 / (ROPE_BASE ** (np.arange(nf, dtype=np.float32) / nf))
    r = np.repeat(np.arange(rows, dtype=np.float32), GRID_W)
    col = np.tile(np.arange(GRID_W, dtype=np.float32), rows)
    ang = np.concatenate([r[:, None] * inv, col[:, None] * inv], axis=-1)
    cos_t = np.ones((ctx_len + seq, LANE), np.float32)
    sin_t = np.zeros((ctx_len + seq, LANE), np.float32)
    npair = DK // 2
    for h in range(HEADS):
        cos_t[ctx_len:, h * npair:(h + 1) * npair] = np.cos(ang)
        sin_t[ctx_len:, h * npair:(h + 1) * npair] = np.sin(ang)
    return jnp.asarray(cos_t), jnp.asarray(sin_t)


def _gla_tables(gate_w, gate_b):
    gw = _gather_cols(gate_w, TAB["gla_src"])
    gw_full = jnp.zeros((2, LANE, QK_P), F32)
    for d in range(2):
        gw_full = gw_full.at[d, GLA_Z_LANE[d]:GLA_Z_LANE[d] + GLA_RANK].set(gw[d] * DK ** 0.5)
    gb = _gather_cols(gate_b, TAB["gla_src"])[:, None, :]
    i = np.arange(GLA_R)
    same = (i[:, None] // GLA_C) == (i[None, :] // GLA_C)
    tri = np.stack([same & (i[:, None] >= i[None, :]), same & (i[:, None] <= i[None, :])]).astype(np.float32)
    return gw_full.astype(BF16), gb, jnp.asarray(tri, dtype=BF16)


def _pad_rows(w, src):
    return _gather_cols(w.T, src).T


def kernel(x, c, ctx, c_ctx, w_mod, b_mod, norm_mix, norm_ffn, w_in, s5_lam_re, s5_lam_im, s5_log_dt, s5_b_re, s5_b_im, s5_c_re, s5_c_im, s5_d, s5_glu_w, s5_glu_b, ret_log_decay, ret_gn, gla_gate_w, gla_gate_b, gla_norm, w_br_s5, w_br_ret, w_br_gla, w_out, w_ffn_in, w_ffn_out, norm_final):
    b, seq, _ = x.shape
    ctx_len = ctx.shape[1]
    depth = w_mod.shape[0]
    assert b % S5_NB == 0 and b % MIX_NB == 0 and seq % RET_C == 0 and seq % GRID_W == 0
    assert ctx_len == ROW_T == RET_C

    xs = (ctx, x)
    rows = ((b + 1 + SUB - 1) // SUB) * SUB
    c_all = jnp.zeros((rows, D), F32).at[:b].set(c).at[b].set(c_ctx)
    cos_t, sin_t = _rope_tables(seq, ctx_len)

    for l in range(depth):
        last = l == depth - 1
        mods = _mod_call(c_all, w_mod[l], b_mod[l])
        m_lat = mods[:b].reshape(b, N_MOD, D)
        m_ctx = jnp.broadcast_to(mods[b].reshape(1, N_MOD, D), (b, N_MOD, D))
        modsel = jnp.stack([m_ctx, m_lat], axis=1)
        modsel = jnp.pad(modsel, ((0, 0), (0, 0), (0, SUB - N_MOD), (0, 0)))

        w_all = _gather_cols(w_in[l], TAB["src"]).astype(BF16)
        p, u = _inproj_call(xs, modsel, norm_mix[l], cos_t, sin_t, w_all)

        bblk, a_re, a_im, cblk = _s5_params(s5_lam_re[l], s5_lam_im[l], s5_log_dt[l],
                                            s5_b_re[l], s5_b_im[l], s5_c_re[l], s5_c_im[l])
        s5 = _s5_call(u, bblk, a_re, a_im, cblk, ctx_len)

        dmask, qdec, kdec, cdec = _ret_tables(ret_log_decay[l])
        yret = _ret_call(p, dmask, qdec, kdec, cdec,
                         _gather_cols(ret_gn[l], TAB["vsrc"]).reshape(1, V_P), ctx_len)

        gw, gb, tri = _gla_tables(gla_gate_w[l], gla_gate_b[l])
        ygla = _gla_call(p, gw, gb, tri, _gather_cols(gla_norm[l], TAB["vsrc"]).reshape(1, V_P),
                         ctx_len)

        xs = _merge_ffn_call(xs, modsel, u, s5, yret, ygla, p,
                             s5_d[l].reshape(1, S5_W), s5_glu_w[l].astype(BF16),
                             s5_glu_b[l].reshape(1, S5_W), w_br_s5[l].astype(BF16),
                             _pad_rows(w_br_ret[l], TAB["vsrc"]).astype(BF16),
                             _pad_rows(w_br_gla[l], TAB["vsrc"]).astype(BF16),
                             w_out[l].astype(BF16), norm_ffn[l], w_ffn_in[l].astype(BF16),
                             w_ffn_out[l].astype(BF16), norm_final, last, last, ctx_len)
    return xs
```

```python
import functools

import numpy as np
import jax
import jax.numpy as jnp
from jax import lax
from jax.experimental import pallas as pl
from jax.experimental.pallas import tpu as pltpu

F32 = jnp.float32
BF16 = jnp.bfloat16

D = 1024
EPS = 1e-6
N_MOD = 6
GRID_W = 64
S5_W = 256
S5_GROUPS = 16
S5_STATE = 64
S5_NS = S5_GROUPS * S5_STATE
HEADS = 4
DK = 48
DV = 96
QK_W = HEADS * DK
V_W = HEADS * DV
GLA_RANK = 16
GLA_TAU = 16.0
ROPE_BASE = 10000.0
FFN_H = 2816

LANE = 128
SUB = 8
QK_P = 2 * LANE
V_P = HEADS * LANE
QKV_P = 2 * QK_P + V_P
GATE_W = 3 * D

OFF_GATE = 0
OFF_RQKV = GATE_W
OFF_GQKV = OFF_RQKV + QKV_P
OFF_RG = OFF_GQKV + QKV_P
OFF_GG = OFF_RG + V_P
NP = OFF_GG + V_P
OFF_S5 = NP
NW = NP + S5_W

ROW_T = 256
RET_C = 256
GLA_C = 64
GLA_R = 256
MIX_NB = 4
S5_C = 128
V7X_VMEM_BYTES = 64 * 1024 * 1024
VMEM_LIMIT = V7X_VMEM_BYTES * 7 // 8

IN_SIZES = (S5_W, QK_W, QK_W, V_W, V_W, QK_W, QK_W, V_W, V_W, GLA_RANK, GLA_RANK, D, D, D)
IN_OFFS = np.concatenate([[0], np.cumsum(IN_SIZES)]).astype(np.int64)


def _head_lane():
    m = np.zeros((HEADS, DK), np.int64)
    for h in range(HEADS):
        for i in range(DK):
            m[h, i] = (i % 2) * LANE + h * (DK // 2) + i // 2
    return m


HEAD_LANE = _head_lane()
HALF = LANE // 2
GLA_LANE = np.array([[(h // 2) * LANE + (h % 2) * HALF + i for i in range(DK)] for h in range(HEADS)])
GLA_Z_LANE = (DK, HALF + DK)
assert HALF - DK == GLA_RANK


def _static_tables():
    src = np.full((NW,), -1, np.int64)
    src[OFF_GATE:OFF_GATE + GATE_W] = IN_OFFS[11] + np.arange(GATE_W)
    for qkv, gg, lane, iq, ik, iv, ig in ((OFF_RQKV, OFF_RG, HEAD_LANE, 1, 2, 3, 4),
                                          (OFF_GQKV, OFF_GG, GLA_LANE, 5, 6, 7, 8)):
        for h in range(HEADS):
            for i in range(DK):
                src[qkv + lane[h, i]] = IN_OFFS[iq] + h * DK + i
                src[qkv + QK_P + lane[h, i]] = IN_OFFS[ik] + h * DK + i
            for j in range(DV):
                src[qkv + 2 * QK_P + h * LANE + j] = IN_OFFS[iv] + h * DV + j
                src[gg + h * LANE + j] = IN_OFFS[ig] + h * DV + j
    for d in range(2):
        src[OFF_GQKV + GLA_Z_LANE[d]:OFF_GQKV + GLA_Z_LANE[d] + GLA_RANK] = (
            IN_OFFS[9 + d] + np.arange(GLA_RANK))
    src[OFF_S5:OFF_S5 + S5_W] = np.arange(S5_W)
    vsrc = np.full((V_P,), -1, np.int64)
    for h in range(HEADS):
        vsrc[h * LANE:h * LANE + DV] = h * DV + np.arange(DV)
    qk_head = np.full((QK_P,), -1, np.int64)
    qk_src = np.full((QK_P,), -1, np.int64)
    for h in range(HEADS):
        for i in range(DK):
            qk_head[HEAD_LANE[h, i]] = h
            qk_src[HEAD_LANE[h, i]] = h * DK + i
    v_head = np.repeat(np.arange(HEADS), LANE)
    v_real = (np.arange(V_P) % LANE) < DV
    head_mask = np.zeros((SUB, QK_P), np.float32)
    for h in range(HEADS):
        head_mask[h] = (qk_head == h)
    bm_t = (v_head[:, None] == qk_head[None, :]).astype(np.float32)
    gla_src = np.full((QK_P,), -1, np.int64)
    for h in range(HEADS):
        gla_src[GLA_LANE[h]] = h * DK + np.arange(DK)
    lane_in_pair = np.arange(LANE)
    gla_hm = np.stack([(lane_in_pair // HALF == r) & (lane_in_pair % HALF < DK) for r in range(2)])
    gla_bm = (np.arange(2 * LANE)[:, None] // LANE) == (lane_in_pair[None, :] // HALF)
    return dict(src=src, vsrc=vsrc, qk_head=qk_head, qk_src=qk_src, head_mask=head_mask,
                bm_t=bm_t, v_real=v_real.astype(np.float32), gla_src=gla_src,
                gla_hm=gla_hm.astype(np.float32), gla_bm=gla_bm.astype(np.float32))


TAB = _static_tables()


def _gather_cols(w, src):
    valid = jnp.asarray(src >= 0)
    out = jnp.take(w, jnp.asarray(np.maximum(src, 0)), axis=-1)
    return jnp.where(valid, out, jnp.zeros((), w.dtype))


def _cparams(n_axes):
    return pltpu.CompilerParams(dimension_semantics=("arbitrary",) * n_axes,
                                vmem_limit_bytes=VMEM_LIMIT)


def _const_spec(shape):
    nd = len(shape)
    return pl.BlockSpec(shape, lambda *_: (0,) * nd, pipeline_mode=pl.Buffered(1))


def _dir_chunk(d, c, n_ctx, n_all):
    bwd = jnp.where(c < n_ctx, n_ctx - 1 - c, n_all + n_ctx - 1 - c)
    return jnp.where(d == 0, c, bwd)


def _sigmoid(x):
    return 0.5 * jnp.tanh(0.5 * x) + 0.5


def _silu(x):
    return x * _sigmoid(x)


def _mod_kernel(c_ref, w_ref, b_ref, o_ref):
    o_ref[...] = jnp.dot(_silu(c_ref[...]), w_ref[...], preferred_element_type=F32,
                         precision=lax.Precision.HIGHEST) + b_ref[...]


def _mod_call(c_all, w_mod_l, b_mod_l):
    rows = c_all.shape[0]
    tn = D
    return pl.pallas_call(
        _mod_kernel,
        grid=(N_MOD * D // tn,),
        in_specs=[pl.BlockSpec((rows, D), lambda j: (0, 0)),
                  pl.BlockSpec((D, tn), lambda j: (0, j)),
                  pl.BlockSpec((1, tn), lambda j: (0, j))],
        out_specs=pl.BlockSpec((rows, tn), lambda j: (0, j)),
        out_shape=jax.ShapeDtypeStruct((rows, N_MOD * D), F32),
        compiler_params=_cparams(1),
    )(c_all, w_mod_l, b_mod_l.reshape(1, -1))


def _rms(x, gain):
    return x * lax.rsqrt(jnp.mean(x * x, axis=-1, keepdims=True) + EPS) * gain


def _residual_rows(refs, split):
    if not split:
        return refs[0][...]
    return jnp.where(pl.program_id(1) == 0, refs[0][...], refs[1][...])


def _residual_specs(split, j0):
    if not split:
        return [pl.BlockSpec((None, ROW_T, D), lambda i, j: (i, j + j0, 0))]
    return [pl.BlockSpec((None, ROW_T, D), lambda i, j: (i, 0, 0)),
            pl.BlockSpec((None, ROW_T, D), lambda i, j: (i, jnp.maximum(j - 1, 0), 0))]


def _inproj_kernel(*refs, split):
    mod_ref, gain_ref, cos_ref, sin_ref, w_ref, p_ref, u_ref = refs[1 + split:]
    h = _rms(_residual_rows(refs, split), gain_ref[...])
    h = h * (1.0 + mod_ref[1:2, :]) + mod_ref[0:1, :]
    hb = h.astype(BF16)

    def proj(c0, c1):
        return jnp.dot(hb, w_ref[:, c0:c1], preferred_element_type=F32)

    for c0 in range(0, GATE_W, D):
        p_ref[:, c0:c0 + D] = proj(c0, c0 + D).astype(BF16)
    cs = cos_ref[...]
    sn = sin_ref[...]
    qk = proj(OFF_RQKV, OFF_RQKV + 2 * QK_P)
    for o, scale in ((0, 1.0), (QK_P, DK ** -0.5)):
        a = qk[:, o:o + LANE] * scale
        b = qk[:, o + LANE:o + 2 * LANE] * scale
        p_ref[:, OFF_RQKV + o:OFF_RQKV + o + LANE] = (a * cs - b * sn).astype(BF16)
        p_ref[:, OFF_RQKV + o + LANE:OFF_RQKV + o + 2 * LANE] = (a * sn + b * cs).astype(BF16)
    c0 = OFF_RQKV + 2 * QK_P
    p_ref[:, c0:OFF_GQKV] = proj(c0, OFF_GQKV).astype(BF16)
    p_ref[:, OFF_GQKV:OFF_GQKV + QK_P] = (proj(OFF_GQKV, OFF_GQKV + QK_P) * (DK ** -0.5)).astype(BF16)
    c0 = OFF_GQKV + QK_P
    p_ref[:, c0:OFF_RG] = proj(c0, OFF_RG).astype(BF16)
    p_ref[:, OFF_RG:NP] = proj(OFF_RG, NP).astype(BF16)
    u_ref[...] = proj(OFF_S5, OFF_S5 + S5_W)


def _inproj_call(xs, modsel, gain, cos_t, sin_t, w_all):
    split = isinstance(xs, tuple)
    xs = xs if split else (xs,)
    b = xs[0].shape[0]
    lt = sum(t.shape[1] for t in xs)
    nt = lt // ROW_T
    return pl.pallas_call(
        functools.partial(_inproj_kernel, split=split),
        grid=(b, nt),
        in_specs=_residual_specs(split, 0) + [
                  pl.BlockSpec((None, None, SUB, D), lambda i, j: (i, jnp.minimum(j, 1), 0, 0)),
                  _const_spec((1, D)),
                  pl.BlockSpec((ROW_T, LANE), lambda i, j: (j, 0)),
                  pl.BlockSpec((ROW_T, LANE), lambda i, j: (j, 0)),
                  _const_spec((D, NW))],
        out_specs=[pl.BlockSpec((None, ROW_T, NP), lambda i, j: (i, j, 0)),
                   pl.BlockSpec((None, ROW_T, S5_W), lambda i, j: (i, j, 0))],
        out_shape=[jax.ShapeDtypeStruct((b, lt, NP), BF16),
                   jax.ShapeDtypeStruct((b, lt, S5_W), F32)],
        compiler_params=_cparams(2),
    )(*xs, modsel, gain.reshape(1, D), cos_t, sin_t, w_all)


S5_NB = 2 * SUB
S5_RB = 256


def _s5_kernel(uf_ref, ub_ref, bb_ref, are_ref, aim_ref, cb_ref, yf_ref, yb_ref,
               x_ref, hb_ref, h_ref, *, tc):
    rows = tc * S5_NB
    steps = S5_RB // S5_NB
    n_blk = rows // S5_RB
    orders = (list(range(n_blk)), list(range(n_blk - 1, -1, -1)))
    u_refs, y_refs = (uf_ref, ub_ref), (yf_ref, yb_ref)

    @pl.when(pl.program_id(1) == 0)
    def _():
        h_ref[...] = jnp.zeros_like(h_ref)

    ubs = [pltpu.einshape("bts->(tb)s", u_refs[d][...]).astype(BF16) for d in (0, 1)]

    def x_block(d, i):
        k = orders[d][i]
        rs = slice(k * S5_RB, (k + 1) * S5_RB)
        x_ref[d, i % 2] = jnp.dot(ubs[d][rs], bb_ref[d], preferred_element_type=F32)

    half = S5_NS // 2
    lanes = [(slice(lo, lo + half), slice(S5_NS + lo, S5_NS + lo + half)) for lo in (0, half)]
    coef = [[(are_ref[d, :, re], aim_ref[d, :, re]) for re, _ in lanes] for d in (0, 1)]
    carry = [[[(h_ref[d, g * SUB:(g + 1) * SUB, re], h_ref[d, g * SUB:(g + 1) * SUB, im])
               for g in range(S5_NB // SUB)] for re, im in lanes] for d in (0, 1)]

    def scan_block(d, i):
        ts = range(steps)
        for t in (reversed(ts) if d else ts):
            r0 = t * S5_NB
            for li, (re, im) in enumerate(lanes):
                ar, ai = coef[d][li]
                new = []
                for g, (hr, hi) in enumerate(carry[d][li]):
                    rw = slice(r0 + g * SUB, r0 + (g + 1) * SUB)
                    new.append((ar * hr - ai * hi + x_ref[d, i % 2, rw, re],
                                ar * hi + ai * hr + x_ref[d, i % 2, rw, im]))
                carry[d][li] = new
                hb_ref[d, i % 2, r0:r0 + S5_NB, re] = jnp.concatenate(
                    [n[0] for n in new], axis=0).astype(BF16)
                hb_ref[d, i % 2, r0:r0 + S5_NB, im] = jnp.concatenate(
                    [n[1] for n in new], axis=0).astype(BF16)

    def y_block(d, i):
        k = orders[d][i]
        y = jnp.dot(hb_ref[d, i % 2], cb_ref[...], preferred_element_type=F32)
        y_refs[d][:, k * steps:(k + 1) * steps, :] = pltpu.einshape("(tb)s->bts", y, b=S5_NB)

    for i in range(-1, n_blk + 1):
        for d in (0, 1):
            if i + 1 < n_blk:
                x_block(d, i + 1)
            if 0 <= i - 1:
                y_block(d, i - 1)
            if 0 <= i < n_blk:
                scan_block(d, i)
    for d in (0, 1):
        for li, (re, im) in enumerate(lanes):
            for g, (hr, hi) in enumerate(carry[d][li]):
                h_ref[d, g * SUB:(g + 1) * SUB, re] = hr
                h_ref[d, g * SUB:(g + 1) * SUB, im] = hi


def _s5_call(u, bblk, a_re, a_im, cblk, ctx_len):
    b, lt, _ = u.shape
    tc = S5_C
    n_all, n_ctx = lt // tc, ctx_len // tc
    tok = lambda d: pl.BlockSpec((S5_NB, tc, S5_W), lambda g, c: (g, _dir_chunk(d, c, n_ctx, n_all), 0))
    out = jax.ShapeDtypeStruct((b, lt, S5_W), F32)
    return pl.pallas_call(
        functools.partial(_s5_kernel, tc=tc),
        grid=(b // S5_NB, n_all),
        in_specs=[tok(0), tok(1), _const_spec((2, S5_W, 2 * S5_NS)), _const_spec((2, SUB, S5_NS)),
                  _const_spec((2, SUB, S5_NS)), _const_spec((2 * S5_NS, S5_W))],
        out_specs=[tok(0), tok(1)],
        out_shape=[out, out],
        scratch_shapes=[pltpu.VMEM((2, 2, S5_RB, 2 * S5_NS), F32),
                        pltpu.VMEM((2, 2, S5_RB, 2 * S5_NS), BF16),
                        pltpu.VMEM((2, S5_NB, 2 * S5_NS), F32)],
        compiler_params=_cparams(2),
    )(u, u, bblk, a_re, a_im, cblk)


_NT = (((1,), (1,)), ((), ()))
_TN = (((0,), (0,)), ((), ()))


def _ret_kernel(*refs, backward):
    if backward:
        (p_ref, dm_ref, qd_ref, kd_ref, cd_ref, hm_ref, bm_ref, of_ref, g_ref, real_ref, gn_ref,
         y_ref, s_ref) = refs
    else:
        p_ref, dm_ref, qd_ref, kd_ref, cd_ref, hm_ref, bm_ref, o_ref, s_ref = refs

    @pl.when(pl.program_id(1) == 0)
    def _():
        s_ref[...] = jnp.zeros_like(s_ref)

    def chunk(bi):
        qb = p_ref[bi, :, 0:QK_P]
        kb = p_ref[bi, :, QK_P:2 * QK_P]
        vb = p_ref[bi, :, 2 * QK_P:2 * QK_P + V_P]
        st = s_ref[bi]
        inter = lax.dot_general(qb * qd_ref[...], st.astype(BF16) * bm_ref[...], _NT,
                                preferred_element_type=F32)
        for h in range(HEADS):
            att = lax.dot_general(qb * hm_ref[h], kb, _NT, preferred_element_type=F32)
            att = att.astype(BF16) * dm_ref[h]
            sl = slice(h * LANE, (h + 1) * LANE)
            o = jnp.dot(att, vb[:, sl], preferred_element_type=F32) + inter[:, sl]
            if backward:
                o = o + of_ref[bi, :, sl]
                mu = jnp.sum(o, axis=-1, keepdims=True) * (1.0 / DV)
                dlt = (o - mu) * real_ref[:, sl]
                var = jnp.sum(dlt * dlt, axis=-1, keepdims=True) * (1.0 / DV)
                y = dlt * lax.rsqrt(var + EPS) * gn_ref[:, sl] * _silu(g_ref[bi, :, sl].astype(F32))
                y_ref[bi, :, sl] = y.astype(BF16)
            else:
                o_ref[bi, :, sl] = o
        kdv = lax.dot_general(vb, kb * kd_ref[...], _TN, preferred_element_type=F32)
        s_ref[bi] = st * cd_ref[...] + kdv

    for bi in range(MIX_NB):
        chunk(bi)


def _mixer_specs(b, lt, ctx_len, tc, off_qkv, off_g):
    n_all, n_ctx = lt // tc, ctx_len // tc
    tok = lambda d, w, blk: pl.BlockSpec(
        (MIX_NB, tc, w), lambda i, c: (i, _dir_chunk(d, c, n_ctx, n_all), blk))
    qkv = lambda d: tok(d, QKV_P, off_qkv // QKV_P)
    gsp = lambda d: tok(d, V_P, off_g // V_P)
    out = lambda d: tok(d, V_P, 0)
    state = pltpu.VMEM((MIX_NB, V_P, QK_P), F32)
    return (b // MIX_NB, n_all), tok, qkv, gsp, out, state


def _head_mask_rows(rows):
    return jnp.asarray(np.broadcast_to(TAB["head_mask"][:HEADS, None, :], (HEADS, rows, QK_P)), dtype=BF16)


def _ret_call(p, dmask, qdec, kdec, cdec, gn, ctx_len):
    b, lt, _ = p.shape
    tc = RET_C
    grid, _, qkv, gsp, out, state = _mixer_specs(b, lt, ctx_len, tc, OFF_RQKV, OFF_RG)
    ins = lambda d: [qkv(d), _const_spec((HEADS, tc, tc)), _const_spec((tc, QK_P)),
                     _const_spec((tc, QK_P)), _const_spec((1, QK_P)),
                     _const_spec((HEADS, tc, QK_P)), _const_spec((V_P, QK_P))]
    tabs = lambda d: (dmask[d].astype(BF16), qdec[d].astype(BF16), kdec[d].astype(BF16), cdec[d],
                      _head_mask_rows(tc), jnp.asarray(TAB["bm_t"], dtype=BF16))
    of = pl.pallas_call(
        functools.partial(_ret_kernel, backward=False),
        grid=grid, in_specs=ins(0), out_specs=out(0),
        out_shape=jax.ShapeDtypeStruct((b, lt, V_P), F32),
        scratch_shapes=[state], compiler_params=_cparams(2),
    )(p, *tabs(0))
    return pl.pallas_call(
        functools.partial(_ret_kernel, backward=True),
        grid=grid,
        in_specs=ins(1) + [out(1), gsp(1), _const_spec((1, V_P)), _const_spec((1, V_P))],
        out_specs=out(1),
        out_shape=jax.ShapeDtypeStruct((b, lt, V_P), BF16),
        scratch_shapes=[state], compiler_params=_cparams(2),
    )(p, *tabs(1), of, p, jnp.asarray(TAB["v_real"]).reshape(1, V_P), gn)


def _gla_kernel(*refs, backward):
    if backward:
        (p_ref, gw_ref, gb_ref, tri_ref, hm_ref, bm_ref, of_ref, g_ref, gn_ref,
         y_ref, s_ref, oc_ref) = refs
    else:
        p_ref, gw_ref, gb_ref, tri_ref, hm_ref, bm_ref, oc_ref, s_ref = refs

    @pl.when(pl.program_id(1) == 0)
    def _():
        s_ref[...] = jnp.zeros_like(s_ref)

    tri = tri_ref[...]
    mask = tri > 0

    def gates(bi):
        la = jax.nn.log_sigmoid(
            jnp.dot(p_ref[bi, :, 0:LANE], gw_ref[...], preferred_element_type=F32)
            + gb_ref[...]) / GLA_TAU
        hi = la.astype(BF16)
        lo = (la - hi.astype(F32)).astype(BF16)
        bcum = (jnp.dot(tri, hi, preferred_element_type=F32)
                + jnp.dot(tri, lo, preferred_element_type=F32))
        q = p_ref[bi, :, 0:QK_P].astype(F32)
        k = p_ref[bi, :, QK_P:2 * QK_P].astype(F32)
        vb = p_ref[bi, :, 2 * QK_P:2 * QK_P + V_P]
        qtb = (q * jnp.exp(bcum)).astype(BF16)
        ktb = (k * jnp.exp(-bcum)).astype(BF16)
        return bcum, k, qtb, ktb, vb

    def attention(bi, part):
        _, _, qtb, ktb, vb = part
        for h in range(HEADS):
            pr = slice(h // 2 * LANE, (h // 2 + 1) * LANE)
            att = lax.dot_general(qtb[:, pr] * hm_ref[h % 2], ktb[:, pr], _NT,
                                  preferred_element_type=F32)
            att = jnp.where(mask, att.astype(BF16), jnp.zeros((), BF16))
            sl = slice(h * LANE, (h + 1) * LANE)
            oc_ref[bi, :, sl] = jnp.dot(att, vb[:, sl], preferred_element_type=F32)

    n_sub = GLA_R // GLA_C
    n_pair = HEADS // 2
    order = range(n_sub - 1, -1, -1) if backward else range(n_sub)
    last_row = 0 if backward else GLA_C - 1

    def chain(bi, part):
        bcum, k, qtb, _, vb = part
        ss = [s_ref[bi, pi] for pi in range(n_pair)]
        for cc in order:
            r0 = cc * GLA_C
            rows = slice(r0, r0 + GLA_C)
            bl = bcum[r0 + last_row:r0 + last_row + 1, :]
            kd = (k[rows] * jnp.exp(bl - bcum[rows])).astype(BF16)
            ebl = jnp.exp(bl)
            for pi in range(n_pair):
                pr = slice(pi * LANE, (pi + 1) * LANE)
                vs = slice(2 * pi * LANE, 2 * (pi + 1) * LANE)
                oc_ref[bi, rows, vs] += lax.dot_general(
                    qtb[rows, pr], ss[pi].astype(BF16) * bm_ref[...], _NT,
                    preferred_element_type=F32)
                kdv = lax.dot_general(vb[rows, vs], kd[:, pr], _TN, preferred_element_type=F32)
                ss[pi] = ss[pi] * ebl[:, pr] + kdv
        for pi in range(n_pair):
            s_ref[bi, pi] = ss[pi]

    def readout(bi):
        for h in range(HEADS):
            sl = slice(h * LANE, (h + 1) * LANE)
            o = oc_ref[bi, :, sl] + of_ref[bi, :, sl]
            ms = jnp.sum(o * o, axis=-1, keepdims=True) * (1.0 / DV)
            y = o * lax.rsqrt(ms + EPS) * gn_ref[:, sl] * _silu(g_ref[bi, :, sl].astype(F32))
            y_ref[bi, :, sl] = y.astype(BF16)

    parts = {}
    for step in range(MIX_NB + 3):
        if step < MIX_NB:
            parts[step] = gates(step)
        if 0 <= step - 1 < MIX_NB:
            attention(step - 1, parts[step - 1])
        if 0 <= step - 2 < MIX_NB:
            chain(step - 2, parts[step - 2])
        if backward and 0 <= step - 3 < MIX_NB:
            readout(step - 3)


def _gla_call(p, gw, gb, tri, gn, ctx_len):
    b, lt, _ = p.shape
    tc = GLA_R
    grid, _, qkv, gsp, out, _ = _mixer_specs(b, lt, ctx_len, tc, OFF_GQKV, OFF_GG)
    ins = lambda d: [qkv(d),
                     _const_spec((LANE, QK_P)), _const_spec((1, QK_P)), _const_spec((tc, tc)),
                     _const_spec((2, tc, LANE)), _const_spec((2 * LANE, LANE))]
    hm = jnp.asarray(np.broadcast_to(TAB["gla_hm"][:, None, :], (2, tc, LANE)), dtype=BF16)
    bm = jnp.asarray(TAB["gla_bm"], dtype=BF16)
    state = pltpu.VMEM((MIX_NB, HEADS // 2, 2 * LANE, LANE), F32)
    of = pl.pallas_call(
        functools.partial(_gla_kernel, backward=False),
        grid=grid, in_specs=ins(0), out_specs=out(0),
        out_shape=jax.ShapeDtypeStruct((b, lt, V_P), F32),
        scratch_shapes=[state], compiler_params=_cparams(2),
    )(p, gw[0], gb[0], tri[0], hm, bm)
    return pl.pallas_call(
        functools.partial(_gla_kernel, backward=True),
        grid=grid,
        in_specs=ins(1) + [out(1), gsp(1), _const_spec((1, V_P))],
        out_specs=out(1),
        out_shape=jax.ShapeDtypeStruct((b, lt, V_P), BF16),
        scratch_shapes=[state, pltpu.VMEM((MIX_NB, tc, V_P), F32)],
        compiler_params=_cparams(2),
    )(p, gw[1], gb[1], tri[1], hm, bm, of, p, gn)


FFN_T = 256


def _merge_ffn_kernel(*refs, final, split):
    (mod_ref, u_ref, sf_ref, sb_ref, yr_ref, yg_ref, gate_ref,
     dsk_ref, gluw_ref, glub_ref, wbs_ref, wbr_ref, wbg_ref, wo_ref,
     gain_ref, wfi_ref, wfo_ref, fin_ref, o_ref, act_ref) = refs[1 + split:]

    def branch(y, w_ref, g0):
        gate = _sigmoid(gate_ref[:, g0:g0 + D].astype(F32))
        return gate * jnp.dot(y, w_ref[...], preferred_element_type=F32)

    ys = jax.nn.gelu(sf_ref[...] + sb_ref[...] + dsk_ref[...] * u_ref[...])
    glu = jnp.dot(ys.astype(BF16), gluw_ref[...], preferred_element_type=F32) + glub_ref[...]
    ys = (ys * _sigmoid(glu)).astype(BF16)
    m = (branch(ys, wbs_ref, 0) + branch(yr_ref[...], wbr_ref, D)
         + branch(yg_ref[...], wbg_ref, 2 * D))
    x = _residual_rows(refs, split) + mod_ref[2:3, :] * jnp.dot(m.astype(BF16), wo_ref[...],
                                                                  preferred_element_type=F32)
    h = _rms(x, gain_ref[...])
    hb = (h * (1.0 + mod_ref[4:5, :]) + mod_ref[3:4, :]).astype(BF16)
    for t in range(FFN_H // FFN_T):
        c0 = t * FFN_T
        a = jnp.dot(hb, wfi_ref[:, c0:c0 + FFN_T], preferred_element_type=F32)
        bq = jnp.dot(hb, wfi_ref[:, FFN_H + c0:FFN_H + c0 + FFN_T], preferred_element_type=F32)
        act_ref[:, t * FFN_T:(t + 1) * FFN_T] = (_silu(a) * bq).astype(BF16)
    y = x + mod_ref[5:6, :] * jnp.dot(act_ref[...], wfo_ref[...], preferred_element_type=F32)
    if final:
        y = _rms(y, fin_ref[...])
    o_ref[...] = y


def _merge_ffn_call(xs, modsel, u, s5, yret, ygla, p, dsk, gluw, glub, wbs, wbr, wbg, wo,
                    gain, wfi, wfo, fin, final, lat_only, ctx_len):
    split = isinstance(xs, tuple)
    xs = xs if split else (xs,)
    b = xs[0].shape[0]
    lt = sum(t.shape[1] for t in xs)
    j0 = ctx_len // ROW_T if lat_only else 0
    nt = lt // ROW_T - j0
    tok = lambda w: pl.BlockSpec((None, ROW_T, w), lambda i, j: (i, j + j0, 0))
    return pl.pallas_call(
        functools.partial(_merge_ffn_kernel, final=final, split=split),
        grid=(b, nt),
        in_specs=_residual_specs(split, j0) + [
                  pl.BlockSpec((None, None, SUB, D), lambda i, j: (i, jnp.minimum(j + j0, 1), 0, 0)),
                  tok(S5_W), tok(S5_W), tok(S5_W), tok(V_P), tok(V_P), tok(GATE_W),
                  _const_spec((1, S5_W)), _const_spec((S5_W, S5_W)), _const_spec((1, S5_W)),
                  _const_spec((S5_W, D)), _const_spec((V_P, D)), _const_spec((V_P, D)),
                  _const_spec((D, D)),
                  _const_spec((1, D)),
                  _const_spec((D, 2 * FFN_H)),
                  _const_spec((FFN_H, D)), _const_spec((1, D))],
        out_specs=pl.BlockSpec((None, ROW_T, D), lambda i, j: (i, j, 0)),
        out_shape=jax.ShapeDtypeStruct((b, nt * ROW_T, D), F32),
        scratch_shapes=[pltpu.VMEM((ROW_T, FFN_H), BF16)],
        compiler_params=_cparams(2),
    )(*xs, modsel, u, *s5, yret, ygla, p, dsk, gluw, glub, wbs, wbr, wbg, wo,
      gain.reshape(1, D), wfi, wfo, fin.reshape(1, D))


def _s5_params(lam_re, lam_im, log_dt, b_re, b_im, c_re, c_im):
    dt = jnp.exp(log_dt)[..., None]
    mag = jnp.exp(lam_re * dt)
    a_re, a_im = mag * jnp.cos(lam_im * dt), mag * jnp.sin(lam_im * dt)
    den = lam_re * lam_re + lam_im * lam_im
    f_re = ((a_re - 1.0) * lam_re + a_im * lam_im) / den
    f_im = (a_im * lam_re - (a_re - 1.0) * lam_im) / den
    bb_re = f_re[..., None] * b_re - f_im[..., None] * b_im
    bb_im = f_re[..., None] * b_im + f_im[..., None] * b_re
    eye = jnp.eye(S5_GROUPS, dtype=F32)

    def blk_in(bb):
        t = jnp.einsum("dgpc,gh->dgchp", bb, eye)
        return t.reshape(2, S5_W, S5_NS)

    bblk = jnp.concatenate([blk_in(bb_re), blk_in(bb_im)], axis=-1).astype(BF16)

    def blk_out(cc):
        t = jnp.einsum("gcp,gh->gphc", cc, eye)
        return t.reshape(S5_NS, S5_W)

    cblk = jnp.concatenate([blk_out(c_re), -blk_out(c_im)], axis=0).astype(BF16)
    bc = lambda a: jnp.broadcast_to(a.reshape(2, 1, S5_NS), (2, SUB, S5_NS))
    return bblk, bc(a_re), bc(a_im), cblk


def _ret_tables(log_decay):
    tc = RET_C
    pos = jnp.arange(tc, dtype=F32)
    w = jnp.stack([pos, tc - 1.0 - pos])
    rel = w[:, :, None] - w[:, None, :]
    lg = log_decay[:, :, None, None]
    dmask = jnp.where(rel[:, None] >= 0, jnp.exp(jnp.maximum(rel[:, None], 0.0) * lg), 0.0)
    qk_head = TAB["qk_head"]
    lane_lg = jnp.where(jnp.asarray(qk_head >= 0),
                        jnp.take(log_decay, jnp.asarray(np.maximum(qk_head, 0)), axis=1), 0.0)
    qdec = jnp.exp((w[:, :, None] + 1.0) * lane_lg[:, None, :])
    kdec = jnp.exp((tc - 1.0 - w[:, :, None]) * lane_lg[:, None, :])
    cdec = jnp.exp(tc * lane_lg)[:, None, :]
    return dmask, qdec, kdec, cdec


def _rope_tables(seq, ctx_len):
    rows = seq // GRID_W
    nf = DK // 4
    inv = 1.0 / (ROPE_BASE ** (np.arange(nf, dtype=np.float32) / nf))
    r = np.repeat(np.arange(rows, dtype=np.float32), GRID_W)
    col = np.tile(np.arange(GRID_W, dtype=np.float32), rows)
    ang = np.concatenate([r[:, None] * inv, col[:, None] * inv], axis=-1)
    cos_t = np.ones((ctx_len + seq, LANE), np.float32)
    sin_t = np.zeros((ctx_len + seq, LANE), np.float32)
    npair = DK // 2
    for h in range(HEADS):
        cos_t[ctx_len:, h * npair:(h + 1) * npair] = np.cos(ang)
        sin_t[ctx_len:, h * npair:(h + 1) * npair] = np.sin(ang)
    return jnp.asarray(cos_t), jnp.asarray(sin_t)


def _gla_tables(gate_w, gate_b):
    gw = _gather_cols(gate_w, TAB["gla_src"])
    gw_full = jnp.zeros((2, LANE, QK_P), F32)
    for d in range(2):
        gw_full = gw_full.at[d, GLA_Z_LANE[d]:GLA_Z_LANE[d] + GLA_RANK].set(gw[d] * DK ** 0.5)
    gb = _gather_cols(gate_b, TAB["gla_src"])[:, None, :]
    i = np.arange(GLA_R)
    same = (i[:, None] // GLA_C) == (i[None, :] // GLA_C)
    tri = np.stack([same & (i[:, None] >= i[None, :]), same & (i[:, None] <= i[None, :])]).astype(np.float32)
    return gw_full.astype(BF16), gb, jnp.asarray(tri, dtype=BF16)


def _pad_rows(w, src):
    return _gather_cols(w.T, src).T


def kernel(x, c, ctx, c_ctx, w_mod, b_mod, norm_mix, norm_ffn, w_in, s5_lam_re, s5_lam_im, s5_log_dt, s5_b_re, s5_b_im, s5_c_re, s5_c_im, s5_d, s5_glu_w, s5_glu_b, ret_log_decay, ret_gn, gla_gate_w, gla_gate_b, gla_norm, w_br_s5, w_br_ret, w_br_gla, w_out, w_ffn_in, w_ffn_out, norm_final):
    b, seq, _ = x.shape
    ctx_len = ctx.shape[1]
    depth = w_mod.shape[0]
    assert b % S5_NB == 0 and b % MIX_NB == 0 and seq % RET_C == 0 and seq % GRID_W == 0
    assert ctx_len == ROW_T == RET_C

    xs = (ctx, x)
    rows = ((b + 1 + SUB - 1) // SUB) * SUB
    c_all = jnp.zeros((rows, D), F32).at[:b].set(c).at[b].set(c_ctx)
    cos_t, sin_t = _rope_tables(seq, ctx_len)

    for l in range(depth):
        last = l == depth - 1
        mods = _mod_call(c_all, w_mod[l], b_mod[l])
        m_lat = mods[:b].reshape(b, N_MOD, D)
        m_ctx = jnp.broadcast_to(mods[b].reshape(1, N_MOD, D), (b, N_MOD, D))
        modsel = jnp.stack([m_ctx, m_lat], axis=1)
        modsel = jnp.pad(modsel, ((0, 0), (0, 0), (0, SUB - N_MOD), (0, 0)))

        w_all = _gather_cols(w_in[l], TAB["src"]).astype(BF16)
        p, u = _inproj_call(xs, modsel, norm_mix[l], cos_t, sin_t, w_all)

        bblk, a_re, a_im, cblk = _s5_params(s5_lam_re[l], s5_lam_im[l], s5_log_dt[l],
                                            s5_b_re[l], s5_b_im[l], s5_c_re[l], s5_c_im[l])
        s5 = _s5_call(u, bblk, a_re, a_im, cblk, ctx_len)

        dmask, qdec, kdec, cdec = _ret_tables(ret_log_decay[l])
        yret = _ret_call(p, dmask, qdec, kdec, cdec,
                         _gather_cols(ret_gn[l], TAB["vsrc"]).reshape(1, V_P), ctx_len)

        gw, gb, tri = _gla_tables(gla_gate_w[l], gla_gate_b[l])
        ygla = _gla_call(p, gw, gb, tri, _gather_cols(gla_norm[l], TAB["vsrc"]).reshape(1, V_P),
                         ctx_len)

        xs = _merge_ffn_call(xs, modsel, u, s5, yret, ygla, p,
                             s5_d[l].reshape(1, S5_W), s5_glu_w[l].astype(BF16),
                             s5_glu_b[l].reshape(1, S5_W), w_br_s5[l].astype(BF16),
                             _pad_rows(w_br_ret[l], TAB["vsrc"]).astype(BF16),
                             _pad_rows(w_br_gla[l], TAB["vsrc"]).astype(BF16),
                             w_out[l].astype(BF16), norm_ffn[l], w_ffn_in[l].astype(BF16),
                             w_ffn_out[l].astype(BF16), norm_final, last, last, ctx_len)
    return xs
```

```python
import functools

import numpy as np
import jax
import jax.numpy as jnp
from jax import lax
from jax.experimental import pallas as pl
from jax.experimental.pallas import tpu as pltpu

F32 = jnp.float32
BF16 = jnp.bfloat16

D = 1024
EPS = 1e-6
N_MOD = 6
GRID_W = 64
S5_W = 256
S5_GROUPS = 16
S5_STATE = 64
S5_NS = S5_GROUPS * S5_STATE
HEADS = 4
DK = 48
DV = 96
QK_W = HEADS * DK
V_W = HEADS * DV
GLA_RANK = 16
GLA_TAU = 16.0
ROPE_BASE = 10000.0
FFN_H = 2816

LANE = 128
SUB = 8
QK_P = 2 * LANE
V_P = HEADS * LANE
QKV_P = 2 * QK_P + V_P
GATE_W = 3 * D

OFF_GATE = 0
OFF_RQKV = GATE_W
OFF_GQKV = OFF_RQKV + QKV_P
OFF_RG = OFF_GQKV + QKV_P
OFF_GG = OFF_RG + V_P
NP = OFF_GG + V_P
OFF_S5 = NP
NW = NP + S5_W

ROW_T = 256
INPROJ_WIDE_T = 3 * ROW_T
RET_C = 256
GLA_C = 64
GLA_R = 256
MIX_NB = 4
S5_C = 128
V7X_VMEM_BYTES = 64 * 1024 * 1024
VMEM_LIMIT = V7X_VMEM_BYTES * 7 // 8

IN_SIZES = (S5_W, QK_W, QK_W, V_W, V_W, QK_W, QK_W, V_W, V_W, GLA_RANK, GLA_RANK, D, D, D)
IN_OFFS = np.concatenate([[0], np.cumsum(IN_SIZES)]).astype(np.int64)


def _head_lane():
    m = np.zeros((HEADS, DK), np.int64)
    for h in range(HEADS):
        for i in range(DK):
            m[h, i] = (i % 2) * LANE + h * (DK // 2) + i // 2
    return m


HEAD_LANE = _head_lane()
HALF = LANE // 2
GLA_LANE = np.array([[(h // 2) * LANE + (h % 2) * HALF + i for i in range(DK)] for h in range(HEADS)])
GLA_Z_LANE = (DK, HALF + DK)
assert HALF - DK == GLA_RANK


def _static_tables():
    src = np.full((NW,), -1, np.int64)
    src[OFF_GATE:OFF_GATE + GATE_W] = IN_OFFS[11] + np.arange(GATE_W)
    for qkv, gg, lane, iq, ik, iv, ig in ((OFF_RQKV, OFF_RG, HEAD_LANE, 1, 2, 3, 4),
                                          (OFF_GQKV, OFF_GG, GLA_LANE, 5, 6, 7, 8)):
        for h in range(HEADS):
            for i in range(DK):
                src[qkv + lane[h, i]] = IN_OFFS[iq] + h * DK + i
                src[qkv + QK_P + lane[h, i]] = IN_OFFS[ik] + h * DK + i
            for j in range(DV):
                src[qkv + 2 * QK_P + h * LANE + j] = IN_OFFS[iv] + h * DV + j
                src[gg + h * LANE + j] = IN_OFFS[ig] + h * DV + j
    for d in range(2):
        src[OFF_GQKV + GLA_Z_LANE[d]:OFF_GQKV + GLA_Z_LANE[d] + GLA_RANK] = (
            IN_OFFS[9 + d] + np.arange(GLA_RANK))
    src[OFF_S5:OFF_S5 + S5_W] = np.arange(S5_W)
    vsrc = np.full((V_P,), -1, np.int64)
    for h in range(HEADS):
        vsrc[h * LANE:h * LANE + DV] = h * DV + np.arange(DV)
    qk_head = np.full((QK_P,), -1, np.int64)
    qk_src = np.full((QK_P,), -1, np.int64)
    for h in range(HEADS):
        for i in range(DK):
            qk_head[HEAD_LANE[h, i]] = h
            qk_src[HEAD_LANE[h, i]] = h * DK + i
    v_head = np.repeat(np.arange(HEADS), LANE)
    v_real = (np.arange(V_P) % LANE) < DV
    head_mask = np.zeros((SUB, QK_P), np.float32)
    for h in range(HEADS):
        head_mask[h] = (qk_head == h)
    bm_t = (v_head[:, None] == qk_head[None, :]).astype(np.float32)
    gla_src = np.full((QK_P,), -1, np.int64)
    for h in range(HEADS):
        gla_src[GLA_LANE[h]] = h * DK + np.arange(DK)
    lane_in_pair = np.arange(LANE)
    gla_hm = np.stack([(lane_in_pair // HALF == r) & (lane_in_pair % HALF < DK) for r in range(2)])
    gla_bm = (np.arange(2 * LANE)[:, None] // LANE) == (lane_in_pair[None, :] // HALF)
    return dict(src=src, vsrc=vsrc, qk_head=qk_head, qk_src=qk_src, head_mask=head_mask,
                bm_t=bm_t, v_real=v_real.astype(np.float32), gla_src=gla_src,
                gla_hm=gla_hm.astype(np.float32), gla_bm=gla_bm.astype(np.float32))


TAB = _static_tables()


def _gather_cols(w, src):
    valid = jnp.asarray(src >= 0)
    out = jnp.take(w, jnp.asarray(np.maximum(src, 0)), axis=-1)
    return jnp.where(valid, out, jnp.zeros((), w.dtype))


def _cparams(n_axes):
    return pltpu.CompilerParams(dimension_semantics=("arbitrary",) * n_axes,
                                vmem_limit_bytes=VMEM_LIMIT)


def _const_spec(shape):
    nd = len(shape)
    return pl.BlockSpec(shape, lambda *_: (0,) * nd, pipeline_mode=pl.Buffered(1))


def _dir_chunk(d, c, n_ctx, n_all):
    bwd = jnp.where(c < n_ctx, n_ctx - 1 - c, n_all + n_ctx - 1 - c)
    return jnp.where(d == 0, c, bwd)


def _sigmoid(x):
    return 0.5 * jnp.tanh(0.5 * x) + 0.5


def _silu(x):
    return x * _sigmoid(x)


def _mod_kernel(c_ref, w_ref, b_ref, o_ref):
    o_ref[...] = jnp.dot(_silu(c_ref[...]), w_ref[...], preferred_element_type=F32,
                         precision=lax.Precision.HIGHEST) + b_ref[...]


def _mod_call(c_all, w_mod_l, b_mod_l):
    rows = c_all.shape[0]
    tn = D
    return pl.pallas_call(
        _mod_kernel,
        grid=(N_MOD * D // tn,),
        in_specs=[pl.BlockSpec((rows, D), lambda j: (0, 0)),
                  pl.BlockSpec((D, tn), lambda j: (0, j)),
                  pl.BlockSpec((1, tn), lambda j: (0, j))],
        out_specs=pl.BlockSpec((rows, tn), lambda j: (0, j)),
        out_shape=jax.ShapeDtypeStruct((rows, N_MOD * D), F32),
        compiler_params=_cparams(1),
    )(c_all, w_mod_l, b_mod_l.reshape(1, -1))


def _rms(x, gain):
    return x * lax.rsqrt(jnp.mean(x * x, axis=-1, keepdims=True) + EPS) * gain


def _residual_rows(refs, split):
    if not split:
        return refs[0][...]
    return jnp.where(pl.program_id(1) == 0, refs[0][...], refs[1][...])


def _residual_specs(split, j0):
    if not split:
        return [pl.BlockSpec((None, ROW_T, D), lambda i, j: (i, j + j0, 0))]
    return [pl.BlockSpec((None, ROW_T, D), lambda i, j: (i, 0, 0)),
            pl.BlockSpec((None, ROW_T, D), lambda i, j: (i, jnp.maximum(j - 1, 0), 0))]


def _inproj_kernel(*refs, split, ctx_rows):
    mod_ref, gain_ref, cos_ref, sin_ref, w_ref, p_ref, u_ref = refs[1 + split:]
    h = _rms(_residual_rows(refs, split), gain_ref[...])
    if ctx_rows is None:
        h = h * (1.0 + mod_ref[1:2, :]) + mod_ref[0:1, :]
    else:
        row = lax.broadcasted_iota(jnp.int32, (h.shape[0], 1), 0)
        is_ctx = jnp.logical_and(row < ctx_rows, pl.program_id(1) == 0)
        scale = jnp.where(is_ctx, mod_ref[0, 1:2, :], mod_ref[1, 1:2, :])
        shift = jnp.where(is_ctx, mod_ref[0, 0:1, :], mod_ref[1, 0:1, :])
        h = h * (1.0 + scale) + shift
    hb = h.astype(BF16)

    def proj(c0, c1):
        return jnp.dot(hb, w_ref[:, c0:c1], preferred_element_type=F32)

    for c0 in range(0, GATE_W, D):
        p_ref[:, c0:c0 + D] = proj(c0, c0 + D).astype(BF16)
    cs = cos_ref[...]
    sn = sin_ref[...]
    qk = proj(OFF_RQKV, OFF_RQKV + 2 * QK_P)
    for o, scale in ((0, 1.0), (QK_P, DK ** -0.5)):
        a = qk[:, o:o + LANE] * scale
        b = qk[:, o + LANE:o + 2 * LANE] * scale
        p_ref[:, OFF_RQKV + o:OFF_RQKV + o + LANE] = (a * cs - b * sn).astype(BF16)
        p_ref[:, OFF_RQKV + o + LANE:OFF_RQKV + o + 2 * LANE] = (a * sn + b * cs).astype(BF16)
    c0 = OFF_RQKV + 2 * QK_P
    p_ref[:, c0:OFF_GQKV] = proj(c0, OFF_GQKV).astype(BF16)
    p_ref[:, OFF_GQKV:OFF_GQKV + QK_P] = (proj(OFF_GQKV, OFF_GQKV + QK_P) * (DK ** -0.5)).astype(BF16)
    c0 = OFF_GQKV + QK_P
    p_ref[:, c0:OFF_RG] = proj(c0, OFF_RG).astype(BF16)
    p_ref[:, OFF_RG:NP] = proj(OFF_RG, NP).astype(BF16)
    u_ref[...] = proj(OFF_S5, OFF_S5 + S5_W)


def _inproj_call(xs, modsel, gain, cos_t, sin_t, w_all):
    split = isinstance(xs, tuple)
    xs = xs if split else (xs,)
    b = xs[0].shape[0]
    lt = sum(t.shape[1] for t in xs)
    rows = ROW_T if split else INPROJ_WIDE_T
    nt = lt // rows
    if split:
        res = _residual_specs(split, 0)
        mod = pl.BlockSpec((None, None, SUB, D), lambda i, j: (i, jnp.minimum(j, 1), 0, 0))
    else:
        res = [pl.BlockSpec((None, rows, D), lambda i, j: (i, j, 0))]
        mod = pl.BlockSpec((None, 2, SUB, D), lambda i, j: (i, 0, 0, 0))
    return pl.pallas_call(
        functools.partial(_inproj_kernel, split=split, ctx_rows=None if split else ROW_T),
        grid=(b, nt),
        in_specs=res + [
                  mod,
                  _const_spec((1, D)),
                  pl.BlockSpec((rows, LANE), lambda i, j: (j, 0)),
                  pl.BlockSpec((rows, LANE), lambda i, j: (j, 0)),
                  _const_spec((D, NW))],
        out_specs=[pl.BlockSpec((None, rows, NP), lambda i, j: (i, j, 0)),
                   pl.BlockSpec((None, rows, S5_W), lambda i, j: (i, j, 0))],
        out_shape=[jax.ShapeDtypeStruct((b, lt, NP), BF16),
                   jax.ShapeDtypeStruct((b, lt, S5_W), F32)],
        compiler_params=_cparams(2),
    )(*xs, modsel, gain.reshape(1, D), cos_t, sin_t, w_all)


S5_NB = 2 * SUB
S5_RB = 256


def _s5_kernel(uf_ref, ub_ref, bb_ref, are_ref, aim_ref, cb_ref, yf_ref, yb_ref,
               x_ref, hb_ref, h_ref, *, tc):
    rows = tc * S5_NB
    steps = S5_RB // S5_NB
    n_blk = rows // S5_RB
    orders = (list(range(n_blk)), list(range(n_blk - 1, -1, -1)))
    u_refs, y_refs = (uf_ref, ub_ref), (yf_ref, yb_ref)

    @pl.when(pl.program_id(1) == 0)
    def _():
        h_ref[...] = jnp.zeros_like(h_ref)

    ubs = [pltpu.einshape("bts->(tb)s", u_refs[d][...]).astype(BF16) for d in (0, 1)]

    def x_block(d, i):
        k = orders[d][i]
        rs = slice(k * S5_RB, (k + 1) * S5_RB)
        x_ref[d, i % 2] = jnp.dot(ubs[d][rs], bb_ref[d], preferred_element_type=F32)

    half = S5_NS // 2
    lanes = [(slice(lo, lo + half), slice(S5_NS + lo, S5_NS + lo + half)) for lo in (0, half)]
    coef = [[(are_ref[d, :, re], aim_ref[d, :, re]) for re, _ in lanes] for d in (0, 1)]
    carry = [[[(h_ref[d, g * SUB:(g + 1) * SUB, re], h_ref[d, g * SUB:(g + 1) * SUB, im])
               for g in range(S5_NB // SUB)] for re, im in lanes] for d in (0, 1)]

    def scan_block(d, i):
        ts = range(steps)
        for t in (reversed(ts) if d else ts):
            r0 = t * S5_NB
            for li, (re, im) in enumerate(lanes):
                ar, ai = coef[d][li]
                new = []
                for g, (hr, hi) in enumerate(carry[d][li]):
                    rw = slice(r0 + g * SUB, r0 + (g + 1) * SUB)
                    new.append((ar * hr - ai * hi + x_ref[d, i % 2, rw, re],
                                ar * hi + ai * hr + x_ref[d, i % 2, rw, im]))
                carry[d][li] = new
                hb_ref[d, i % 2, r0:r0 + S5_NB, re] = jnp.concatenate(
                    [n[0] for n in new], axis=0).astype(BF16)
                hb_ref[d, i % 2, r0:r0 + S5_NB, im] = jnp.concatenate(
                    [n[1] for n in new], axis=0).astype(BF16)

    def y_block(d, i):
        k = orders[d][i]
        y = jnp.dot(hb_ref[d, i % 2], cb_ref[...], preferred_element_type=F32)
        y_refs[d][:, k * steps:(k + 1) * steps, :] = pltpu.einshape("(tb)s->bts", y, b=S5_NB)

    for i in range(-1, n_blk + 1):
        for d in (0, 1):
            if i + 1 < n_blk:
                x_block(d, i + 1)
            if 0 <= i - 1:
                y_block(d, i - 1)
            if 0 <= i < n_blk:
                scan_block(d, i)
    for d in (0, 1):
        for li, (re, im) in enumerate(lanes):
            for g, (hr, hi) in enumerate(carry[d][li]):
                h_ref[d, g * SUB:(g + 1) * SUB, re] = hr
                h_ref[d, g * SUB:(g + 1) * SUB, im] = hi


def _s5_call(u, bblk, a_re, a_im, cblk, ctx_len):
    b, lt, _ = u.shape
    tc = S5_C
    n_all, n_ctx = lt // tc, ctx_len // tc
    tok = lambda d: pl.BlockSpec((S5_NB, tc, S5_W), lambda g, c: (g, _dir_chunk(d, c, n_ctx, n_all), 0))
    out = jax.ShapeDtypeStruct((b, lt, S5_W), F32)
    return pl.pallas_call(
        functools.partial(_s5_kernel, tc=tc),
        grid=(b // S5_NB, n_all),
        in_specs=[tok(0), tok(1), _const_spec((2, S5_W, 2 * S5_NS)), _const_spec((2, SUB, S5_NS)),
                  _const_spec((2, SUB, S5_NS)), _const_spec((2 * S5_NS, S5_W))],
        out_specs=[tok(0), tok(1)],
        out_shape=[out, out],
        scratch_shapes=[pltpu.VMEM((2, 2, S5_RB, 2 * S5_NS), F32),
                        pltpu.VMEM((2, 2, S5_RB, 2 * S5_NS), BF16),
                        pltpu.VMEM((2, S5_NB, 2 * S5_NS), F32)],
        compiler_params=_cparams(2),
    )(u, u, bblk, a_re, a_im, cblk)


_NT = (((1,), (1,)), ((), ()))
_TN = (((0,), (0,)), ((), ()))


def _ret_kernel(*refs, backward):
    if backward:
        (p_ref, dm_ref, qd_ref, kd_ref, cd_ref, hm_ref, bm_ref, of_ref, g_ref, real_ref, gn_ref,
         y_ref, s_ref) = refs
    else:
        p_ref, dm_ref, qd_ref, kd_ref, cd_ref, hm_ref, bm_ref, o_ref, s_ref = refs

    @pl.when(pl.program_id(1) == 0)
    def _():
        s_ref[...] = jnp.zeros_like(s_ref)

    def chunk(bi):
        qb = p_ref[bi, :, 0:QK_P]
        kb = p_ref[bi, :, QK_P:2 * QK_P]
        vb = p_ref[bi, :, 2 * QK_P:2 * QK_P + V_P]
        st = s_ref[bi]
        inter = lax.dot_general(qb * qd_ref[...], st.astype(BF16) * bm_ref[...], _NT,
                                preferred_element_type=F32)
        for h in range(HEADS):
            att = lax.dot_general(qb * hm_ref[h], kb, _NT, preferred_element_type=F32)
            att = att.astype(BF16) * dm_ref[h]
            sl = slice(h * LANE, (h + 1) * LANE)
            o = jnp.dot(att, vb[:, sl], preferred_element_type=F32) + inter[:, sl]
            if backward:
                o = o + of_ref[bi, :, sl]
                mu = jnp.sum(o, axis=-1, keepdims=True) * (1.0 / DV)
                dlt = (o - mu) * real_ref[:, sl]
                var = jnp.sum(dlt * dlt, axis=-1, keepdims=True) * (1.0 / DV)
                y = dlt * lax.rsqrt(var + EPS) * gn_ref[:, sl] * _silu(g_ref[bi, :, sl].astype(F32))
                y_ref[bi, :, sl] = y.astype(BF16)
            else:
                o_ref[bi, :, sl] = o
        kdv = lax.dot_general(vb, kb * kd_ref[...], _TN, preferred_element_type=F32)
        s_ref[bi] = st * cd_ref[...] + kdv

    for bi in range(MIX_NB):
        chunk(bi)


def _mixer_specs(b, lt, ctx_len, tc, off_qkv, off_g):
    n_all, n_ctx = lt // tc, ctx_len // tc
    tok = lambda d, w, blk: pl.BlockSpec(
        (MIX_NB, tc, w), lambda i, c: (i, _dir_chunk(d, c, n_ctx, n_all), blk))
    qkv = lambda d: tok(d, QKV_P, off_qkv // QKV_P)
    gsp = lambda d: tok(d, V_P, off_g // V_P)
    out = lambda d: tok(d, V_P, 0)
    state = pltpu.VMEM((MIX_NB, V_P, QK_P), F32)
    return (b // MIX_NB, n_all), tok, qkv, gsp, out, state


def _head_mask_rows(rows):
    return jnp.asarray(np.broadcast_to(TAB["head_mask"][:HEADS, None, :], (HEADS, rows, QK_P)), dtype=BF16)


def _ret_call(p, dmask, qdec, kdec, cdec, gn, ctx_len):
    b, lt, _ = p.shape
    tc = RET_C
    grid, _, qkv, gsp, out, state = _mixer_specs(b, lt, ctx_len, tc, OFF_RQKV, OFF_RG)
    ins = lambda d: [qkv(d), _const_spec((HEADS, tc, tc)), _const_spec((tc, QK_P)),
                     _const_spec((tc, QK_P)), _const_spec((1, QK_P)),
                     _const_spec((HEADS, tc, QK_P)), _const_spec((V_P, QK_P))]
    tabs = lambda d: (dmask[d].astype(BF16), qdec[d].astype(BF16), kdec[d].astype(BF16), cdec[d],
                      _head_mask_rows(tc), jnp.asarray(TAB["bm_t"], dtype=BF16))
    of = pl.pallas_call(
        functools.partial(_ret_kernel, backward=False),
        grid=grid, in_specs=ins(0), out_specs=out(0),
        out_shape=jax.ShapeDtypeStruct((b, lt, V_P), F32),
        scratch_shapes=[state], compiler_params=_cparams(2),
    )(p, *tabs(0))
    return pl.pallas_call(
        functools.partial(_ret_kernel, backward=True),
        grid=grid,
        in_specs=ins(1) + [out(1), gsp(1), _const_spec((1, V_P)), _const_spec((1, V_P))],
        out_specs=out(1),
        out_shape=jax.ShapeDtypeStruct((b, lt, V_P), BF16),
        scratch_shapes=[state], compiler_params=_cparams(2),
    )(p, *tabs(1), of, p, jnp.asarray(TAB["v_real"]).reshape(1, V_P), gn)


def _gla_kernel(*refs, backward):
    if backward:
        (p_ref, gw_ref, gb_ref, tri_ref, hm_ref, bm_ref, of_ref, g_ref, gn_ref,
         y_ref, s_ref, oc_ref) = refs
    else:
        p_ref, gw_ref, gb_ref, tri_ref, hm_ref, bm_ref, oc_ref, s_ref = refs

    @pl.when(pl.program_id(1) == 0)
    def _():
        s_ref[...] = jnp.zeros_like(s_ref)

    tri = tri_ref[...]
    mask = tri > 0

    def gates(bi):
        la = jax.nn.log_sigmoid(
            jnp.dot(p_ref[bi, :, 0:LANE], gw_ref[...], preferred_element_type=F32)
            + gb_ref[...]) / GLA_TAU
        hi = la.astype(BF16)
        lo = (la - hi.astype(F32)).astype(BF16)
        bcum = (jnp.dot(tri, hi, preferred_element_type=F32)
                + jnp.dot(tri, lo, preferred_element_type=F32))
        q = p_ref[bi, :, 0:QK_P].astype(F32)
        k = p_ref[bi, :, QK_P:2 * QK_P].astype(F32)
        vb = p_ref[bi, :, 2 * QK_P:2 * QK_P + V_P]
        qtb = (q * jnp.exp(bcum)).astype(BF16)
        ktb = (k * jnp.exp(-bcum)).astype(BF16)
        return bcum, k, qtb, ktb, vb

    def attention(bi, part):
        _, _, qtb, ktb, vb = part
        for h in range(HEADS):
            pr = slice(h // 2 * LANE, (h // 2 + 1) * LANE)
            att = lax.dot_general(qtb[:, pr] * hm_ref[h % 2], ktb[:, pr], _NT,
                                  preferred_element_type=F32)
            att = jnp.where(mask, att.astype(BF16), jnp.zeros((), BF16))
            sl = slice(h * LANE, (h + 1) * LANE)
            oc_ref[bi, :, sl] = jnp.dot(att, vb[:, sl], preferred_element_type=F32)

    n_sub = GLA_R // GLA_C
    n_pair = HEADS // 2
    order = range(n_sub - 1, -1, -1) if backward else range(n_sub)
    last_row = 0 if backward else GLA_C - 1

    def chain(bi, part):
        bcum, k, qtb, _, vb = part
        ss = [s_ref[bi, pi] for pi in range(n_pair)]
        for cc in order:
            r0 = cc * GLA_C
            rows = slice(r0, r0 + GLA_C)
            bl = bcum[r0 + last_row:r0 + last_row + 1, :]
            kd = (k[rows] * jnp.exp(bl - bcum[rows])).astype(BF16)
            ebl = jnp.exp(bl)
            for pi in range(n_pair):
                pr = slice(pi * LANE, (pi + 1) * LANE)
                vs = slice(2 * pi * LANE, 2 * (pi + 1) * LANE)
                oc_ref[bi, rows, vs] += lax.dot_general(
                    qtb[rows, pr], ss[pi].astype(BF16) * bm_ref[...], _NT,
                    preferred_element_type=F32)
                kdv = lax.dot_general(vb[rows, vs], kd[:, pr], _TN, preferred_element_type=F32)
                ss[pi] = ss[pi] * ebl[:, pr] + kdv
        for pi in range(n_pair):
            s_ref[bi, pi] = ss[pi]

    def readout(bi):
        for h in range(HEADS):
            sl = slice(h * LANE, (h + 1) * LANE)
            o = oc_ref[bi, :, sl] + of_ref[bi, :, sl]
            ms = jnp.sum(o * o, axis=-1, keepdims=True) * (1.0 / DV)
            y = o * lax.rsqrt(ms + EPS) * gn_ref[:, sl] * _silu(g_ref[bi, :, sl].astype(F32))
            y_ref[bi, :, sl] = y.astype(BF16)

    parts = {}
    for step in range(MIX_NB + 3):
        if step < MIX_NB:
            parts[step] = gates(step)
        if 0 <= step - 1 < MIX_NB:
            attention(step - 1, parts[step - 1])
        if 0 <= step - 2 < MIX_NB:
            chain(step - 2, parts[step - 2])
        if backward and 0 <= step - 3 < MIX_NB:
            readout(step - 3)


def _gla_call(p, gw, gb, tri, gn, ctx_len):
    b, lt, _ = p.shape
    tc = GLA_R
    grid, _, qkv, gsp, out, _ = _mixer_specs(b, lt, ctx_len, tc, OFF_GQKV, OFF_GG)
    ins = lambda d: [qkv(d),
                     _const_spec((LANE, QK_P)), _const_spec((1, QK_P)), _const_spec((tc, tc)),
                     _const_spec((2, tc, LANE)), _const_spec((2 * LANE, LANE))]
    hm = jnp.asarray(np.broadcast_to(TAB["gla_hm"][:, None, :], (2, tc, LANE)), dtype=BF16)
    bm = jnp.asarray(TAB["gla_bm"], dtype=BF16)
    state = pltpu.VMEM((MIX_NB, HEADS // 2, 2 * LANE, LANE), F32)
    of = pl.pallas_call(
        functools.partial(_gla_kernel, backward=False),
        grid=grid, in_specs=ins(0), out_specs=out(0),
        out_shape=jax.ShapeDtypeStruct((b, lt, V_P), F32),
        scratch_shapes=[state], compiler_params=_cparams(2),
    )(p, gw[0], gb[0], tri[0], hm, bm)
    return pl.pallas_call(
        functools.partial(_gla_kernel, backward=True),
        grid=grid,
        in_specs=ins(1) + [out(1), gsp(1), _const_spec((1, V_P))],
        out_specs=out(1),
        out_shape=jax.ShapeDtypeStruct((b, lt, V_P), BF16),
        scratch_shapes=[state, pltpu.VMEM((MIX_NB, tc, V_P), F32)],
        compiler_params=_cparams(2),
    )(p, gw[1], gb[1], tri[1], hm, bm, of, p, gn)


FFN_T = 256


def _merge_ffn_kernel(*refs, final, split):
    (mod_ref, u_ref, sf_ref, sb_ref, yr_ref, yg_ref, gate_ref,
     dsk_ref, gluw_ref, glub_ref, wbs_ref, wbr_ref, wbg_ref, wo_ref,
     gain_ref, wfi_ref, wfo_ref, fin_ref, o_ref, act_ref) = refs[1 + split:]

    def branch(y, w_ref, g0):
        gate = _sigmoid(gate_ref[:, g0:g0 + D].astype(F32))
        return gate * jnp.dot(y, w_ref[...], preferred_element_type=F32)

    ys = jax.nn.gelu(sf_ref[...] + sb_ref[...] + dsk_ref[...] * u_ref[...])
    glu = jnp.dot(ys.astype(BF16), gluw_ref[...], preferred_element_type=F32) + glub_ref[...]
    ys = (ys * _sigmoid(glu)).astype(BF16)
    m = (branch(ys, wbs_ref, 0) + branch(yr_ref[...], wbr_ref, D)
         + branch(yg_ref[...], wbg_ref, 2 * D))
    x = _residual_rows(refs, split) + mod_ref[2:3, :] * jnp.dot(m.astype(BF16), wo_ref[...],
                                                                  preferred_element_type=F32)
    h = _rms(x, gain_ref[...])
    hb = (h * (1.0 + mod_ref[4:5, :]) + mod_ref[3:4, :]).astype(BF16)
    for t in range(FFN_H // FFN_T):
        c0 = t * FFN_T
        a = jnp.dot(hb, wfi_ref[:, c0:c0 + FFN_T], preferred_element_type=F32)
        bq = jnp.dot(hb, wfi_ref[:, FFN_H + c0:FFN_H + c0 + FFN_T], preferred_element_type=F32)
        act_ref[:, t * FFN_T:(t + 1) * FFN_T] = (_silu(a) * bq).astype(BF16)
    y = x + mod_ref[5:6, :] * jnp.dot(act_ref[...], wfo_ref[...], preferred_element_type=F32)
    if final:
        y = _rms(y, fin_ref[...])
    o_ref[...] = y


def _merge_ffn_call(xs, modsel, u, s5, yret, ygla, p, dsk, gluw, glub, wbs, wbr, wbg, wo,
                    gain, wfi, wfo, fin, final, lat_only, ctx_len):
    split = isinstance(xs, tuple)
    xs = xs if split else (xs,)
    b = xs[0].shape[0]
    lt = sum(t.shape[1] for t in xs)
    j0 = ctx_len // ROW_T if lat_only else 0
    nt = lt // ROW_T - j0
    tok = lambda w: pl.BlockSpec((None, ROW_T, w), lambda i, j: (i, j + j0, 0))
    return pl.pallas_call(
        functools.partial(_merge_ffn_kernel, final=final, split=split),
        grid=(b, nt),
        in_specs=_residual_specs(split, j0) + [
                  pl.BlockSpec((None, None, SUB, D), lambda i, j: (i, jnp.minimum(j + j0, 1), 0, 0)),
                  tok(S5_W), tok(S5_W), tok(S5_W), tok(V_P), tok(V_P), tok(GATE_W),
                  _const_spec((1, S5_W)), _const_spec((S5_W, S5_W)), _const_spec((1, S5_W)),
                  _const_spec((S5_W, D)), _const_spec((V_P, D)), _const_spec((V_P, D)),
                  _const_spec((D, D)),
                  _const_spec((1, D)),
                  _const_spec((D, 2 * FFN_H)),
                  _const_spec((FFN_H, D)), _const_spec((1, D))],
        out_specs=pl.BlockSpec((None, ROW_T, D), lambda i, j: (i, j, 0)),
        out_shape=jax.ShapeDtypeStruct((b, nt * ROW_T, D), F32),
        scratch_shapes=[pltpu.VMEM((ROW_T, FFN_H), BF16)],
        compiler_params=_cparams(2),
    )(*xs, modsel, u, *s5, yret, ygla, p, dsk, gluw, glub, wbs, wbr, wbg, wo,
      gain.reshape(1, D), wfi, wfo, fin.reshape(1, D))


def _s5_params(lam_re, lam_im, log_dt, b_re, b_im, c_re, c_im):
    dt = jnp.exp(log_dt)[..., None]
    mag = jnp.exp(lam_re * dt)
    a_re, a_im = mag * jnp.cos(lam_im * dt), mag * jnp.sin(lam_im * dt)
    den = lam_re * lam_re + lam_im * lam_im
    f_re = ((a_re - 1.0) * lam_re + a_im * lam_im) / den
    f_im = (a_im * lam_re - (a_re - 1.0) * lam_im) / den
    bb_re = f_re[..., None] * b_re - f_im[..., None] * b_im
    bb_im = f_re[..., None] * b_im + f_im[..., None] * b_re
    eye = jnp.eye(S5_GROUPS, dtype=F32)

    def blk_in(bb):
        t = jnp.einsum("dgpc,gh->dgchp", bb, eye)
        return t.reshape(2, S5_W, S5_NS)

    bblk = jnp.concatenate([blk_in(bb_re), blk_in(bb_im)], axis=-1).astype(BF16)

    def blk_out(cc):
        t = jnp.einsum("gcp,gh->gphc", cc, eye)
        return t.reshape(S5_NS, S5_W)

    cblk = jnp.concatenate([blk_out(c_re), -blk_out(c_im)], axis=0).astype(BF16)
    bc = lambda a: jnp.broadcast_to(a.reshape(2, 1, S5_NS), (2, SUB, S5_NS))
    return bblk, bc(a_re), bc(a_im), cblk


def _ret_tables(log_decay):
    tc = RET_C
    pos = jnp.arange(tc, dtype=F32)
    w = jnp.stack([pos, tc - 1.0 - pos])
    rel = w[:, :, None] - w[:, None, :]
    lg = log_decay[:, :, None, None]
    dmask = jnp.where(rel[:, None] >= 0, jnp.exp(jnp.maximum(rel[:, None], 0.0) * lg), 0.0)
    qk_head = TAB["qk_head"]
    lane_lg = jnp.where(jnp.asarray(qk_head >= 0),
                        jnp.take(log_decay, jnp.asarray(np.maximum(qk_head, 0)), axis=1), 0.0)
    qdec = jnp.exp((w[:, :, None] + 1.0) * lane_lg[:, None, :])
    kdec = jnp.exp((tc - 1.0 - w[:, :, None]) * lane_lg[:, None, :])
    cdec = jnp.exp(tc * lane_lg)[:, None, :]
    return dmask, qdec, kdec, cdec


def _rope_tables(seq, ctx_len):
    rows = seq // GRID_W
    nf = DK // 4
    inv = 1.0 / (ROPE_BASE ** (np.arange(nf, dtype=np.float32) / nf))
    r = np.repeat(np.arange(rows, dtype=np.float32), GRID_W)
    col = np.tile(np.arange(GRID_W, dtype=np.float32), rows)
    ang = np.concatenate([r[:, None] * inv, col[:, None] * inv], axis=-1)
    cos_t = np.ones((ctx_len + seq, LANE), np.float32)
    sin_t = np.zeros((ctx_len + seq, LANE), np.float32)
    npair = DK // 2
    for h in range(HEADS):
        cos_t[ctx_len:, h * npair:(h + 1) * npair] = np.cos(ang)
        sin_t[ctx_len:, h * npair:(h + 1) * npair] = np.sin(ang)
    return jnp.asarray(cos_t), jnp.asarray(sin_t)


def _gla_tables(gate_w, gate_b):
    gw = _gather_cols(gate_w, TAB["gla_src"])
    gw_full = jnp.zeros((2, LANE, QK_P), F32)
    for d in range(2):
        gw_full = gw_full.at[d, GLA_Z_LANE[d]:GLA_Z_LANE[d] + GLA_RANK].set(gw[d] * DK ** 0.5)
    gb = _gather_cols(gate_b, TAB["gla_src"])[:, None, :]
    i = np.arange(GLA_R)
    same = (i[:, None] // GLA_C) == (i[None, :] // GLA_C)
    tri = np.stack([same & (i[:, None] >= i[None, :]), same & (i[:, None] <= i[None, :])]).astype(np.float32)
    return gw_full.astype(BF16), gb, jnp.asarray(tri, dtype=BF16)


def _pad_rows(w, src):
    return _gather_cols(w.T, src).T


def kernel(x, c, ctx, c_ctx, w_mod, b_mod, norm_mix, norm_ffn, w_in, s5_lam_re, s5_lam_im, s5_log_dt, s5_b_re, s5_b_im, s5_c_re, s5_c_im, s5_d, s5_glu_w, s5_glu_b, ret_log_decay, ret_gn, gla_gate_w, gla_gate_b, gla_norm, w_br_s5, w_br_ret, w_br_gla, w_out, w_ffn_in, w_ffn_out, norm_final):
    b, seq, _ = x.shape
    ctx_len = ctx.shape[1]
    depth = w_mod.shape[0]
    assert b % S5_NB == 0 and b % MIX_NB == 0 and seq % RET_C == 0 and seq % GRID_W == 0
    assert ctx_len == ROW_T == RET_C

    xs = (ctx, x)
    rows = ((b + 1 + SUB - 1) // SUB) * SUB
    c_all = jnp.zeros((rows, D), F32).at[:b].set(c).at[b].set(c_ctx)
    cos_t, sin_t = _rope_tables(seq, ctx_len)

    for l in range(depth):
        last = l == depth - 1
        mods = _mod_call(c_all, w_mod[l], b_mod[l])
        m_lat = mods[:b].reshape(b, N_MOD, D)
        m_ctx = jnp.broadcast_to(mods[b].reshape(1, N_MOD, D), (b, N_MOD, D))
        modsel = jnp.stack([m_ctx, m_lat], axis=1)
        modsel = jnp.pad(modsel, ((0, 0), (0, 0), (0, SUB - N_MOD), (0, 0)))

        w_all = _gather_cols(w_in[l], TAB["src"]).astype(BF16)
        p, u = _inproj_call(xs, modsel, norm_mix[l], cos_t, sin_t, w_all)

        bblk, a_re, a_im, cblk = _s5_params(s5_lam_re[l], s5_lam_im[l], s5_log_dt[l],
                                            s5_b_re[l], s5_b_im[l], s5_c_re[l], s5_c_im[l])
        s5 = _s5_call(u, bblk, a_re, a_im, cblk, ctx_len)

        dmask, qdec, kdec, cdec = _ret_tables(ret_log_decay[l])
        yret = _ret_call(p, dmask, qdec, kdec, cdec,
                         _gather_cols(ret_gn[l], TAB["vsrc"]).reshape(1, V_P), ctx_len)

        gw, gb, tri = _gla_tables(gla_gate_w[l], gla_gate_b[l])
        ygla = _gla_call(p, gw, gb, tri, _gather_cols(gla_norm[l], TAB["vsrc"]).reshape(1, V_P),
                         ctx_len)

        xs = _merge_ffn_call(xs, modsel, u, s5, yret, ygla, p,
                             s5_d[l].reshape(1, S5_W), s5_glu_w[l].astype(BF16),
                             s5_glu_b[l].reshape(1, S5_W), w_br_s5[l].astype(BF16),
                             _pad_rows(w_br_ret[l], TAB["vsrc"]).astype(BF16),
                             _pad_rows(w_br_gla[l], TAB["vsrc"]).astype(BF16),
                             w_out[l].astype(BF16), norm_ffn[l], w_ffn_in[l].astype(BF16),
                             w_ffn_out[l].astype(BF16), norm_final, last, last, ctx_len)
    return xs
```
